```python
import math
import jax
import jax.numpy as jnp
from jax import lax
import numpy as np

D_MODEL = 2048
BATCH = 1
SEQ = 8192
DEPTH = 4

GRID_W = 64
CTX_LEN = 256
EPS = 1e-6

SSD_HEADS = 16
SSD_HEAD_DIM = 64
SSD_D_INNER = SSD_HEADS * SSD_HEAD_DIM
SSD_STATE = 128
SSD_GROUPS = 2
SSD_CONV = 3
SSD_CHUNK = 128
SSD_XBC = SSD_D_INNER + 2 * SSD_GROUPS * SSD_STATE

NA_HEADS = 16
NA_HEAD_DIM = 64
NA_D = NA_HEADS * NA_HEAD_DIM
NA_WIN_ROWS = 8
NA_WIN_COLS = 16

MIX_WIDTH = SSD_D_INNER + NA_D
_S0 = SSD_D_INNER
_S1 = _S0 + SSD_XBC
_S2 = _S1 + 2 * SSD_HEADS
_S3 = _S2 + NA_D
_S4 = _S3 + NA_D
EVEN_IN = _S4 + NA_D

CONV_WIDTH = D_MODEL
CONV_K = 3

N_GROUPS = 8
EXPERTS_PER_GROUP = 8
N_EXPERTS = N_GROUPS * EXPERTS_PER_GROUP
TOP_K = 2
D_FF_EXPERT = 384
MOE_BLOCK = 128

N_EVEN = (DEPTH + 1) // 2
N_ODD = DEPTH // 2

kernel_name = "hybrid_ssd_natten_shortconv_hmoe_dit"


def _rmsnorm(x, w):
    xf = x.astype(jnp.float32)
    y = xf * lax.rsqrt(jnp.mean(xf * xf, axis=-1, keepdims=True) + EPS)
    return (y * w.astype(jnp.float32)).astype(x.dtype)


def _modulate(n, shift, scale):
    return n * (1 + scale) + shift


def _flip(t):
    return jnp.flip(t, axis=1)


def _dwconv(u, w, b=None):
    k = w.shape[0]
    y = lax.conv_general_dilated(u, w[:, None, :].astype(u.dtype), window_strides=(1,),
                                 padding=[(k // 2, k // 2)],
                                 dimension_numbers=('NWC', 'WIO', 'NWC'),
                                 feature_group_count=u.shape[-1])
    return y if b is None else y + b.astype(y.dtype)


def _segsum(a):
    q = a.shape[-1]
    cs = jnp.cumsum(a, axis=-1)
    seg = cs[..., :, None] - cs[..., None, :]
    return jnp.where(jnp.tril(jnp.ones((q, q), bool)), seg, -jnp.inf)


def _ssd_states(xs, dt, a_neg, bh, h0):
    bsz, L, H, P = xs.shape
    nc = L // SSD_CHUNK
    xdt = (xs * dt[..., None]).reshape(bsz, nc, SSD_CHUNK, H, P)
    bc = bh.reshape(bsz, nc, SSD_CHUNK, H, SSD_STATE)
    a = (dt * a_neg).reshape(bsz, nc, SSD_CHUNK, H).transpose(0, 3, 1, 2)
    a_cs = jnp.cumsum(a, axis=-1)
    decay_to_end = jnp.exp(a_cs[..., -1:] - a_cs)
    chunk_states = jnp.einsum('bclhn,bhcl,bclhp->bchpn', bc, decay_to_end, xdt)
    chunk_decay = jnp.exp(a_cs[..., -1])

    def step(h, inp):
        s, d = inp
        return h * d[..., None, None] + s, h

    h_final, h_enter = lax.scan(step, h0, (jnp.moveaxis(chunk_states, 1, 0), jnp.moveaxis(chunk_decay, 2, 0)))
    return xdt, a, a_cs, bc, jnp.moveaxis(h_enter, 0, 1), h_final


def _ssd_scan(xs, dt, a_neg, bh, ch, h0):
    bsz, L, H, P = xs.shape
    xdt, a, a_cs, bc, h_enter, h_final = _ssd_states(xs, dt, a_neg, bh, h0)
    cc = ch.reshape(bsz, -1, SSD_CHUNK, H, SSD_STATE)
    l_mat = jnp.exp(_segsum(a))
    y_diag = jnp.einsum('bclhn,bcshn,bhcls,bcshp->bclhp', cc, bc, l_mat, xdt)
    y_off = jnp.einsum('bclhn,bchpn,bhcl->bclhp', cc, h_enter, jnp.exp(a_cs))
    return (y_diag + y_off).reshape(bsz, L, H, P), h_final


def _ssd_inputs(xbc, dt_raw, conv_w, conv_b, dt_bias):
    bsz, L, _ = xbc.shape
    xbc = jax.nn.silu(_dwconv(xbc, conv_w, conv_b))
    xs, bm, cm = jnp.split(xbc, [SSD_D_INNER, SSD_D_INNER + SSD_GROUPS * SSD_STATE], axis=-1)
    rep = SSD_HEADS // SSD_GROUPS
    xs = xs.reshape(bsz, L, SSD_HEADS, SSD_HEAD_DIM).astype(jnp.float32)
    bh = jnp.repeat(bm.reshape(bsz, L, SSD_GROUPS, SSD_STATE), rep, axis=2).astype(jnp.float32)
    ch = jnp.repeat(cm.reshape(bsz, L, SSD_GROUPS, SSD_STATE), rep, axis=2).astype(jnp.float32)
    dt = jax.nn.softplus(dt_raw.reshape(bsz, L, 2, SSD_HEADS).astype(jnp.float32) + dt_bias.astype(jnp.float32))
    return xs, bh, ch, dt[:, :, 0], dt[:, :, 1]


def _ssd_output(y, xs, z, d_skip, norm_w):
    y = y + d_skip.astype(jnp.float32)[:, None] * xs
    y = y.reshape(y.shape[0], y.shape[1], SSD_D_INNER)
    return _rmsnorm(y * jax.nn.silu(z.astype(jnp.float32)), norm_w).astype(z.dtype)


def _heads(t, w=None):
    t = t.reshape(t.shape[0], t.shape[1], NA_HEADS, NA_HEAD_DIM)
    return t if w is None else _rmsnorm(t, w)


def _dense_attention(q, k, v):
    s = jnp.einsum('bqhd,bkhd->bhqk', q, k).astype(jnp.float32) * (q.shape[-1] ** -0.5)
    p = jax.nn.softmax(s, axis=-1).astype(v.dtype)
    out = jnp.einsum('bhqk,bkhd->bqhd', p, v)
    return out.reshape(q.shape[0], q.shape[1], -1)


def _neighbourhood_attention(q, k, v, k_ctx, v_ctx, rpb):
    bsz, L, H, Dh = q.shape
    rows = L // GRID_W
    kr = min(NA_WIN_ROWS, rows)
    kc = NA_WIN_COLS
    scale = Dh ** -0.5
    qg = q.reshape(bsz, rows, GRID_W, H, Dh)
    kg = k.reshape(bsz, rows, GRID_W, H, Dh)
    vg = v.reshape(bsz, rows, GRID_W, H, Dh)
    r = jnp.arange(rows)
    row_idx = jnp.clip(r - kr // 2, 0, rows - kr)[:, None] + jnp.arange(kr)[None, :]
    k_band = kg[:, row_idx]
    v_band = vg[:, row_idx]
    s_loc = jnp.einsum('brqhd,brkwhd->brhqkw', qg, k_band).astype(jnp.float32) * scale
    col = jnp.arange(GRID_W)
    c0 = jnp.clip(col - kc // 2, 0, GRID_W - kc)
    col_in = (col[None, :] >= c0[:, None]) & (col[None, :] < c0[:, None] + kc)
    dr = row_idx - r[:, None] + (NA_WIN_ROWS - 1)
    dc = jnp.clip(col[None, :] - col[:, None], 1 - kc, kc - 1) + (kc - 1)
    bias = jnp.take(rpb[:, dr].astype(jnp.float32), dc, axis=-1)
    s_loc = jnp.where(col_in[:, None, :], s_loc + bias.transpose(1, 0, 3, 2, 4), -jnp.inf)
    s_ctx = jnp.einsum('brqhd,bchd->brhqc', qg, k_ctx).astype(jnp.float32) * scale
    n_loc = kr * GRID_W
    s = jnp.concatenate([s_loc.reshape(bsz, rows, H, GRID_W, n_loc), s_ctx], axis=-1)
    p = jax.nn.softmax(s, axis=-1).astype(v.dtype)
    p_loc = p[..., :n_loc].reshape(bsz, rows, H, GRID_W, kr, GRID_W)
    out = (jnp.einsum('brhqkw,brkwhd->brqhd', p_loc, v_band)
           + jnp.einsum('brhqc,bchd->brqhd', p[..., n_loc:], v_ctx))
    return out.reshape(bsz, L, H * Dh)


def _even_mixer(n_lat, n_ctx, w_in, conv_w, conv_b, dt_bias, a_log, d_skip, ssd_norm_w,
                q_norm_w, k_norm_w, rpb, w_out, ctx_out):
    a_neg = -jnp.exp(a_log.astype(jnp.float32))

    def project(n):
        z, xbc, dt_raw, q, k, v = jnp.split(n @ w_in, [_S0, _S1, _S2, _S3, _S4], axis=-1)
        xs, bh, ch, dt_f, dt_b = _ssd_inputs(xbc, dt_raw, conv_w, conv_b, dt_bias)
        return z, xs, bh, ch, dt_f, dt_b, q, _heads(k, k_norm_w), _heads(v)

    zc, xc, bc, cc, dfc, dbc, qc, kc, vc = project(n_ctx)
    h0 = jnp.zeros((xc.shape[0], SSD_HEADS, SSD_HEAD_DIM, SSD_STATE), jnp.float32)
    if ctx_out:
        yc_f, hc_f = _ssd_scan(xc, dfc, a_neg[0], bc, cc, h0)
        yc_b, hc_b = _ssd_scan(_flip(xc), _flip(dbc), a_neg[1], _flip(bc), _flip(cc), h0)
        y_ssd_c = _ssd_output(yc_f + _flip(yc_b), xc, zc, d_skip, ssd_norm_w)
        y_att_c = _dense_attention(_heads(qc, q_norm_w), kc, vc)
        y_ctx = jnp.concatenate([y_ssd_c, y_att_c], axis=-1) @ w_out
    else:
        hc_f = _ssd_states(xc, dfc, a_neg[0], bc, h0)[-1]
        hc_b = _ssd_states(_flip(xc), _flip(dbc), a_neg[1], _flip(bc), h0)[-1]
        y_ctx = None

    zl, xl, bl, cl, dfl, dbl, ql, kl, vl = project(n_lat)
    yl_f, _ = _ssd_scan(xl, dfl, a_neg[0], bl, cl, hc_f)
    yl_b, _ = _ssd_scan(_flip(xl), _flip(dbl), a_neg[1], _flip(bl), _flip(cl), hc_b)
    y_ssd_l = _ssd_output(yl_f + _flip(yl_b), xl, zl, d_skip, ssd_norm_w)
    y_att_l = _neighbourhood_attention(_heads(ql, q_norm_w), kl, vl, kc, vc, rpb)
    y_lat = jnp.concatenate([y_ssd_l, y_att_l], axis=-1) @ w_out
    return y_lat, y_ctx


def _short_conv_mixer(n, w_in, conv_w, w_out):
    bg, cg, xv = jnp.split(n @ w_in, 3, axis=-1)
    return (bg * _dwconv(cg * xv, conv_w)) @ w_out


def _expert_dispatch(x, expert, gates, w_gate, w_up, w_down):
    T, D = x.shape
    n = T * TOP_K
    flat_e = expert.reshape(-1)
    flat_tok = jnp.repeat(jnp.arange(T, dtype=jnp.int32), TOP_K)
    flat_g = gates.reshape(-1)
    n_blocks = -(-n // MOE_BLOCK) + N_EXPERTS
    n_rows = n_blocks * MOE_BLOCK
    order = jnp.argsort(flat_e)
    e_sorted = flat_e[order]
    counts = jnp.bincount(flat_e, length=N_EXPERTS)
    padded = (counts + MOE_BLOCK - 1) // MOE_BLOCK * MOE_BLOCK
    pad_end = jnp.cumsum(padded)
    pad_start = pad_end - padded
    raw_start = jnp.cumsum(counts) - counts
    dest = pad_start[e_sorted] + jnp.arange(n) - raw_start[e_sorted]
    slot_tok = jnp.full((n_rows,), T, jnp.int32).at[dest].set(flat_tok[order])
    slot_gate = jnp.zeros((n_rows,), jnp.float32).at[dest].set(flat_g[order])
    block_expert = jnp.minimum(jnp.searchsorted(pad_end, jnp.arange(n_blocks) * MOE_BLOCK, side='right'),
                               N_EXPERTS - 1)
    x_pad = jnp.concatenate([x, jnp.zeros((1, D), x.dtype)], axis=0)
    xb = x_pad[slot_tok].reshape(n_blocks, MOE_BLOCK, D)

    def run_block(args):
        xblk, e = args
        h = jax.nn.silu(xblk @ w_gate[e]) * (xblk @ w_up[e])
        return h @ w_down[e]

    yb = lax.map(run_block, (xb, block_expert)).reshape(n_rows, D)
    y = jnp.zeros((T + 1, D), x.dtype).at[slot_tok].add(yb * slot_gate[:, None].astype(x.dtype))
    return y[:T]


def _hier_moe(x, w_group, w_router, w_gate, w_up, w_down):
    T = x.shape[0]
    xf = x.astype(jnp.float32)
    g_logits = xf @ w_group.astype(jnp.float32)
    grp = jnp.argmax(g_logits, axis=-1)
    p_grp = jnp.take_along_axis(jax.nn.softmax(g_logits, axis=-1), grp[:, None], axis=-1)
    e_logits = (xf @ w_router.astype(jnp.float32)).reshape(T, N_GROUPS, EXPERTS_PER_GROUP)
    e_logits = jnp.take_along_axis(e_logits, grp[:, None, None], axis=1)[:, 0]
    top_val, top_idx = lax.top_k(e_logits, TOP_K)
    gates = p_grp * jax.nn.softmax(top_val, axis=-1)
    expert = grp[:, None].astype(jnp.int32) * EXPERTS_PER_GROUP + top_idx.astype(jnp.int32)
    return _expert_dispatch(x, expert, gates, w_gate, w_up, w_down)


def setup_inputs(seed: int = 0) -> dict:
    key = jax.random.key(seed)
    k = jax.random.split(key, 27)
    f32 = jnp.float32
    D = D_MODEL

    def nrm(i, shape, scale):
        return scale * jax.random.normal(k[i], shape, f32)

    dt0 = jnp.exp(jax.random.uniform(k[11], (N_EVEN, 2, SSD_HEADS), f32, math.log(1e-3), math.log(1e-1)))
    return {
        "x": nrm(0, (BATCH, SEQ, D), 1.0),
        "c": nrm(1, (BATCH, D), 1.0),
        "ctx": nrm(2, (BATCH, CTX_LEN, D), 1.0),
        "c_ctx": nrm(3, (D,), 1.0),
        "w_mod": nrm(4, (DEPTH, D, 6 * D), 0.5 * D ** -0.5),
        "b_mod": nrm(5, (DEPTH, 6 * D), 0.02),
        "norm_mix_w": 1.0 + nrm(6, (DEPTH, D), 0.02),
        "norm_ffn_w": 1.0 + nrm(7, (DEPTH, D), 0.02),
        "ev_w_in": nrm(8, (N_EVEN, D, EVEN_IN), D ** -0.5),
        "ev_conv_w": nrm(9, (N_EVEN, SSD_CONV, SSD_XBC), SSD_CONV ** -0.5),
        "ev_conv_b": nrm(10, (N_EVEN, SSD_XBC), 0.02),
        "ev_dt_bias": dt0 + jnp.log(-jnp.expm1(-dt0)),
        "ev_a_log": jnp.log(jax.random.uniform(k[12], (N_EVEN, 2, SSD_HEADS), f32, 1.0, 16.0)),
        "ev_d_skip": 1.0 + nrm(13, (N_EVEN, SSD_HEADS), 0.02),
        "ev_ssd_norm_w": 1.0 + nrm(14, (N_EVEN, SSD_D_INNER), 0.02),
        "ev_q_norm_w": 1.0 + nrm(15, (N_EVEN, NA_HEAD_DIM), 0.02),
        "ev_k_norm_w": 1.0 + nrm(16, (N_EVEN, NA_HEAD_DIM), 0.02),
        "ev_rpb": nrm(17, (N_EVEN, NA_HEADS, 2 * NA_WIN_ROWS - 1, 2 * NA_WIN_COLS - 1), 0.02),
        "ev_w_out": nrm(18, (N_EVEN, MIX_WIDTH, D), MIX_WIDTH ** -0.5),
        "od_w_in": nrm(19, (N_ODD, D, 3 * CONV_WIDTH), D ** -0.5),
        "od_conv_w": nrm(20, (N_ODD, CONV_K, CONV_WIDTH), CONV_K ** -0.5),
        "od_w_out": nrm(21, (N_ODD, CONV_WIDTH, D), CONV_WIDTH ** -0.5),
        "moe_w_group": nrm(22, (DEPTH, D, N_GROUPS), D ** -0.5),
        "moe_w_router": nrm(23, (DEPTH, D, N_EXPERTS), D ** -0.5),
        "moe_w_gate": nrm(24, (DEPTH, N_EXPERTS, D, D_FF_EXPERT), D ** -0.5),
        "moe_w_up": nrm(25, (DEPTH, N_EXPERTS, D, D_FF_EXPERT), D ** -0.5),
        "moe_w_down": nrm(26, (DEPTH, N_EXPERTS, D_FF_EXPERT, D), D_FF_EXPERT ** -0.5),
    }


def reference(x, c, ctx, c_ctx, w_mod, b_mod, norm_mix_w, norm_ffn_w,
              ev_w_in, ev_conv_w, ev_conv_b, ev_dt_bias, ev_a_log, ev_d_skip, ev_ssd_norm_w,
              ev_q_norm_w, ev_k_norm_w, ev_rpb, ev_w_out,
              od_w_in, od_conv_w, od_w_out,
              moe_w_group, moe_w_router, moe_w_gate, moe_w_up, moe_w_down):
    silu_c = jax.nn.silu(c)
    silu_cc = jax.nn.silu(c_ctx)
    lat, cx = x, ctx
    bsz, L, D = x.shape
    ctx_len = ctx.shape[1]
    for l in range(DEPTH):
        even = l % 2 == 0
        ctx_out = any(j % 2 == 0 for j in range(l + 1, DEPTH))
        ctx_in = even or ctx_out
        sh1, sc1, g1, sh2, sc2, g2 = jnp.split((silu_c @ w_mod[l] + b_mod[l])[:, None, :], 6, axis=-1)
        n_lat = _modulate(_rmsnorm(lat, norm_mix_w[l]), sh1, sc1)
        n_ctx = None
        if ctx_in:
            csh1, csc1, cg1, csh2, csc2, cg2 = jnp.split(silu_cc @ w_mod[l] + b_mod[l], 6, axis=-1)
            n_ctx = _modulate(_rmsnorm(cx, norm_mix_w[l]), csh1, csc1)
        if even:
            e = l // 2
            y_lat, y_ctx = _even_mixer(n_lat, n_ctx, ev_w_in[e], ev_conv_w[e], ev_conv_b[e], ev_dt_bias[e],
                                       ev_a_log[e], ev_d_skip[e], ev_ssd_norm_w[e], ev_q_norm_w[e],
                                       ev_k_norm_w[e], ev_rpb[e], ev_w_out[e], ctx_out)
        else:
            o = l // 2
            y_lat = _short_conv_mixer(n_lat, od_w_in[o], od_conv_w[o], od_w_out[o])
            y_ctx = _short_conv_mixer(n_ctx, od_w_in[o], od_conv_w[o], od_w_out[o]) if ctx_out else None
        lat = lat + g1 * y_lat
        m_lat = _modulate(_rmsnorm(lat, norm_ffn_w[l]), sh2, sc2)
        if ctx_out:
            cx = cx + cg1 * y_ctx
            m_ctx = _modulate(_rmsnorm(cx, norm_ffn_w[l]), csh2, csc2)
            tokens = jnp.concatenate([m_lat.reshape(-1, D), m_ctx.reshape(-1, D)], axis=0)
        else:
            tokens = m_lat.reshape(-1, D)
        f = _hier_moe(tokens, moe_w_group[l], moe_w_router[l], moe_w_gate[l], moe_w_up[l], moe_w_down[l])
        lat = lat + g2 * f[:bsz * L].reshape(bsz, L, D)
        if ctx_out:
            cx = cx + cg2 * f[bsz * L:].reshape(bsz, ctx_len, D)
    return lat
```

```python
import functools

import jax
import jax.numpy as jnp
from jax import lax
from jax.experimental import pallas as pl
from jax.experimental.pallas import tpu as pltpu

F32 = jnp.float32
MXU_DTYPE = jnp.bfloat16
HI = lax.Precision.HIGHEST
EPS = 1e-6
NEG = -1e30

LANES = 128
SUBLANES = 8
VMEM_LIMIT = 48 * 1024 * 1024

SSD_HEADS = 16
SSD_HEAD_DIM = 64
SSD_D_INNER = SSD_HEADS * SSD_HEAD_DIM
SSD_STATE = 128
SSD_GROUPS = 2
SSD_CHUNK = 128
SSD_BC = SSD_GROUPS * SSD_STATE
SSD_XBC = SSD_D_INNER + 2 * SSD_BC
SSD_PAIRS = SSD_HEADS // 2
NA_HEADS = 16
NA_HEAD_DIM = 64
NA_D = NA_HEADS * NA_HEAD_DIM
NA_WIN_ROWS = 8
NA_WIN_COLS = 16
GRID_W = 64
N_GROUPS = 8
EXPERTS_PER_GROUP = 8
N_EXPERTS = N_GROUPS * EXPERTS_PER_GROUP
TOP_K = 2
D_FF_EXPERT = 384
MOE_BLOCK = 128
S_ZX = SSD_D_INNER + SSD_XBC
S_DT = S_ZX + 2 * SSD_HEADS
QKVD = 3 * NA_D + LANES


def _pick(n, prefs):
    for p in prefs:
        if n % p == 0:
            return p
    return n


def _params(*sem):
    return pltpu.CompilerParams(dimension_semantics=sem, vmem_limit_bytes=VMEM_LIMIT)


def _silu(x):
    return x * (1.0 / (1.0 + jnp.exp(-x)))


def _softplus(x):
    return jnp.maximum(x, 0.0) + jnp.log1p(jnp.exp(-jnp.abs(x)))


def _dot(a, b, precision=None):
    return jnp.dot(a, b, preferred_element_type=F32, precision=precision)


def _dot_nt(a, b):
    return lax.dot_general(a, b, (((1,), (1,)), ((), ())), preferred_element_type=F32)


def _norm_mod(x, nw, sh, sc):
    ms = jnp.mean(x * x, axis=-1, keepdims=True)
    n = x * lax.rsqrt(ms + EPS) * nw
    return n * (1.0 + sc) + sh


def _mod_kernel(ct_ref, w_ref, b_ref, o_ref):
    ct = ct_ref[...]
    s = _silu(ct)
    w = w_ref[...]
    r0 = jnp.sum(s[:, 0:1] * w, axis=0, keepdims=True)
    r1 = jnp.sum(s[:, 1:2] * w, axis=0, keepdims=True)
    o_ref[...] = jnp.concatenate([r0, r1], axis=0) + b_ref[...]


def _modulation(ct, w_mod, b_mod):
    depth, d, n = w_mod.shape
    tn = _pick(n, (512, 256, 128))
    return pl.pallas_call(
        _mod_kernel,
        grid=(depth, n // tn),
        in_specs=[pl.BlockSpec((d, 2), lambda l, j: (0, 0)),
                  pl.BlockSpec((None, d, tn), lambda l, j: (l, 0, j)),
                  pl.BlockSpec((None, 1, tn), lambda l, j: (l, 0, j))],
        out_specs=pl.BlockSpec((None, 2, tn), lambda l, j: (l, 0, j)),
        out_shape=jax.ShapeDtypeStruct((depth, 2, n), F32),
        compiler_params=_params("parallel", "parallel"),
    )(ct, w_mod, b_mod.reshape(depth, 1, n))


def _nm_matmul_kernel(x_ref, nw_ref, sh_ref, sc_ref, w_ref, o_ref, a_ref):
    @pl.when(pl.program_id(1) == 0)
    def _():
        a_ref[...] = _norm_mod(x_ref[...], nw_ref[...], sh_ref[...], sc_ref[...]).astype(a_ref.dtype)

    o_ref[...] = _dot(a_ref[...], w_ref[...].astype(a_ref.dtype)).astype(o_ref.dtype)


def _nm_matmul(x, nw, sh, sc, w, n_cols):
    m, d = x.shape
    tm = _pick(m, (1024, 512, 256, 128))
    tn = _pick(n_cols, (512, 640, 256, 128))
    return pl.pallas_call(
        _nm_matmul_kernel,
        grid=(m // tm, n_cols // tn),
        in_specs=[pl.BlockSpec((tm, d), lambda i, j: (i, 0)),
                  pl.BlockSpec((1, d), lambda i, j: (0, 0)),
                  pl.BlockSpec((1, d), lambda i, j: (0, 0)),
                  pl.BlockSpec((1, d), lambda i, j: (0, 0)),
                  pl.BlockSpec((d, tn), lambda i, j: (0, j))],
        out_specs=pl.BlockSpec((tm, tn), lambda i, j: (i, j)),
        out_shape=jax.ShapeDtypeStruct((m, n_cols), F32),
        scratch_shapes=[pltpu.VMEM((tm, d), MXU_DTYPE)],
        compiler_params=_params("parallel", "arbitrary"),
    )(x, nw, sh, sc, w)


def _matmul_res_kernel(*refs, n_a):
    a_refs, w_refs = refs[:n_a], refs[n_a:2 * n_a]
    res_ref, g_ref, o_ref = refs[2 * n_a:]
    acc = None
    for a_ref, w_ref in zip(a_refs, w_refs):
        t = _dot(a_ref[...].astype(MXU_DTYPE), w_ref[...].astype(MXU_DTYPE))
        acc = t if acc is None else acc + t
    o_ref[...] = res_ref[...] + g_ref[...] * acc


def _matmul_res(a_list, w, res, g):
    n_a = len(a_list)
    m, n = res.shape
    kp = w.shape[0] // n_a
    tm = _pick(m, (1024, 512, 256, 128))
    tn = _pick(n, (512, 256, 128))
    in_specs = [pl.BlockSpec((tm, kp), lambda i, j: (i, 0)) for _ in range(n_a)]
    in_specs += [pl.BlockSpec((kp, tn), functools.partial(lambda i, j, p: (p, j), p=p)) for p in range(n_a)]
    in_specs += [pl.BlockSpec((tm, tn), lambda i, j: (i, j)), pl.BlockSpec((1, tn), lambda i, j: (0, j))]
    return pl.pallas_call(
        functools.partial(_matmul_res_kernel, n_a=n_a),
        grid=(m // tm, n // tn),
        in_specs=in_specs,
        out_specs=pl.BlockSpec((tm, tn), lambda i, j: (i, j)),
        out_shape=jax.ShapeDtypeStruct((m, n), F32),
        compiler_params=_params("parallel", "parallel"),
    )(*a_list, *([w] * n_a), res, g)


def _shifted(x, prev_ref, next_ref):
    i, n_i = pl.program_id(0), pl.num_programs(0)
    tl = x.shape[0]
    row = lax.broadcasted_iota(jnp.int32, x.shape, 0)
    prev_row = jnp.where(i == 0, 0.0, prev_ref[SUBLANES - 1:SUBLANES, :])
    next_row = jnp.where(i == n_i - 1, 0.0, next_ref[0:1, :])
    up = jnp.where(row == 0, prev_row, pltpu.roll(x, 1, 0))
    down = jnp.where(row == tl - 1, next_row, pltpu.roll(x, tl - 1, 0))
    return up, down


def _halo_specs(tl, tc, n_rows, col_off):
    nb = tl // SUBLANES
    last = n_rows // SUBLANES - 1
    return [pl.BlockSpec((tl, tc), lambda i, j: (i, j + col_off)),
            pl.BlockSpec((SUBLANES, tc), lambda i, j: (jnp.maximum(i * nb - 1, 0), j + col_off)),
            pl.BlockSpec((SUBLANES, tc), lambda i, j: (jnp.minimum((i + 1) * nb, last), j + col_off))]


def _xbc_conv_kernel(x_ref, prev_ref, next_ref, w_ref, b_ref, o_ref):
    x = x_ref[...]
    up, down = _shifted(x, prev_ref, next_ref)
    w = w_ref[...]
    o_ref[...] = _silu(w[0:1] * up + w[1:2] * x + w[2:3] * down + b_ref[...])


def _xbc_conv(zx, conv_w, conv_b):
    n_rows = zx.shape[0]
    tl = _pick(n_rows, (512, 256, 128))
    tc = 512
    off = SSD_D_INNER // tc
    return pl.pallas_call(
        _xbc_conv_kernel,
        grid=(n_rows // tl, SSD_XBC // tc),
        in_specs=_halo_specs(tl, tc, n_rows, off) + [pl.BlockSpec((3, tc), lambda i, j: (0, j)),
                                                     pl.BlockSpec((1, tc), lambda i, j: (0, j))],
        out_specs=pl.BlockSpec((tl, tc), lambda i, j: (i, j)),
        out_shape=jax.ShapeDtypeStruct((n_rows, SSD_XBC), F32),
        compiler_params=_params("parallel", "parallel"),
    )(zx, zx, zx, conv_w, conv_b.reshape(1, -1))


def _gate_conv_kernel(bg_ref, cg_ref, cgp_ref, cgn_ref, xv_ref, xvp_ref, xvn_ref, w_ref, o_ref):
    i, n_i = pl.program_id(0), pl.num_programs(0)
    u = cg_ref[...] * xv_ref[...]
    tl = u.shape[0]
    row = lax.broadcasted_iota(jnp.int32, u.shape, 0)
    prev_row = jnp.where(i == 0, 0.0, cgp_ref[SUBLANES - 1:SUBLANES, :] * xvp_ref[SUBLANES - 1:SUBLANES, :])
    next_row = jnp.where(i == n_i - 1, 0.0, cgn_ref[0:1, :] * xvn_ref[0:1, :])
    up = jnp.where(row == 0, prev_row, pltpu.roll(u, 1, 0))
    down = jnp.where(row == tl - 1, next_row, pltpu.roll(u, tl - 1, 0))
    w = w_ref[...]
    o_ref[...] = (bg_ref[...] * (w[0:1] * up + w[1:2] * u + w[2:3] * down)).astype(o_ref.dtype)


def _gate_conv(h3, conv_w):
    n_rows, c3 = h3.shape
    c = c3 // 3
    tl = _pick(n_rows, (512, 256, 128))
    tc = _pick(c, (512, 256, 128))
    nb = c // tc
    return pl.pallas_call(
        _gate_conv_kernel,
        grid=(n_rows // tl, nb),
        in_specs=[pl.BlockSpec((tl, tc), lambda i, j: (i, j))] + _halo_specs(tl, tc, n_rows, nb)
        + _halo_specs(tl, tc, n_rows, 2 * nb) + [pl.BlockSpec((3, tc), lambda i, j: (0, j))],
        out_specs=pl.BlockSpec((tl, tc), lambda i, j: (i, j)),
        out_shape=jax.ShapeDtypeStruct((n_rows, c), MXU_DTYPE),
        compiler_params=_params("parallel", "parallel"),
    )(h3, h3, h3, h3, h3, h3, h3, conv_w)


def _ssd_kernel(xbc_ref, dt_ref, dtt_ref, dtb_ref, dtbt_ref, alog_ref, alogt_ref, h0_ref,
                y_ref, hout_ref, h_ref):
    d, c, n_c = pl.program_id(0), pl.program_id(1), pl.num_programs(1)
    q = SSD_CHUNK

    @pl.when(c == 0)
    def _():
        h_ref[...] = h0_ref[...]

    dt = _softplus(dt_ref[...] + dtb_ref[...])
    dtt = _softplus(dtt_ref[...] + dtbt_ref[...])
    a = dt * -jnp.exp(alog_ref[...])
    at = dtt * -jnp.exp(alogt_ref[...])
    row = lax.broadcasted_iota(jnp.int32, (q, q), 0)
    col = lax.broadcasted_iota(jnp.int32, (q, q), 1)
    sign = jnp.where(d == 0, 1, -1)
    mask = (row - col) * sign >= 0
    tri = mask.astype(F32)
    tri_t = ((col - row) * sign >= 0).astype(F32)
    cs = _dot(tri, a, HI)
    cst = _dot(at, tri_t, HI)
    tot = jnp.broadcast_to(jnp.sum(a, axis=0, keepdims=True), (SUBLANES, SSD_HEADS))
    expand = (lax.broadcasted_iota(jnp.int32, (SSD_HEADS, SSD_D_INNER), 1) // SSD_HEAD_DIM
              == lax.broadcasted_iota(jnp.int32, (SSD_HEADS, SSD_D_INNER), 0)).astype(F32)
    dt_e = _dot(dt, expand, HI)
    cs_e = _dot(cs, expand, HI)
    tot_e = _dot(tot, expand, HI)[0:1]

    xs = xbc_ref[:, 0:SSD_D_INNER]
    xdt = xs * dt_e
    xdt_m = xdt.astype(MXU_DTYPE)
    xw_m = (xdt * jnp.exp(tot_e - cs_e)).astype(MXU_DTYPE)
    e_e = jnp.exp(cs_e)
    dec_e = jnp.exp(tot_e)
    first = lax.broadcasted_iota(jnp.int32, (q, LANES), 1) < SSD_HEAD_DIM

    ppg = SSD_PAIRS // SSD_GROUPS
    for g in range(SSD_GROUPS):
        bg = xbc_ref[:, SSD_D_INNER + g * SSD_STATE:SSD_D_INNER + (g + 1) * SSD_STATE]
        cg = xbc_ref[:, SSD_D_INNER + SSD_BC + g * SSD_STATE:SSD_D_INNER + SSD_BC + (g + 1) * SSD_STATE]
        bg_m, cg_m = bg.astype(MXU_DTYPE), cg.astype(MXU_DTYPE)
        bgt_m = bg.T.astype(MXU_DTYPE)
        scores = _dot_nt(cg_m, bg_m)
        for pp in range(ppg):
            p = g * ppg + pp
            sl = slice(p * LANES, (p + 1) * LANES)
            ys = []
            for hh in range(2):
                h = 2 * p + hh
                diff = cs[:, h:h + 1] - cst[h:h + 1, :]
                decay = jnp.exp(jnp.where(mask, diff, -jnp.inf))
                ys.append(_dot((scores * decay).astype(MXU_DTYPE), xdt_m[:, sl]))
            hp = h_ref[p]
            y_off = _dot(cg_m, hp.astype(MXU_DTYPE)) * e_e[:, sl]
            y_ref[:, sl] = jnp.where(first, ys[0], ys[1]) + y_off
            h_ref[p] = dec_e[:, sl] * hp + _dot(bgt_m, xw_m[:, sl])

    @pl.when(c == n_c - 1)
    def _():
        hout_ref[...] = h_ref[...]


def _ssd(xbc, dt_raw, dt_bias, a_log, h0):
    n_rows = xbc.shape[0]
    q = SSD_CHUNK
    n_c = n_rows // q
    dt = dt_raw.reshape(n_rows, 2, SSD_HEADS).transpose(1, 0, 2)
    dtt = dt.transpose(0, 2, 1)

    def chunk(d, c):
        return jnp.where(d == 0, c, n_c - 1 - c)

    small = lambda shape: pl.BlockSpec((None,) + shape, lambda d, c: (d, 0, 0))
    return pl.pallas_call(
        _ssd_kernel,
        grid=(2, n_c),
        in_specs=[pl.BlockSpec((q, SSD_XBC), lambda d, c: (chunk(d, c), 0)),
                  pl.BlockSpec((None, q, SSD_HEADS), lambda d, c: (d, chunk(d, c), 0)),
                  pl.BlockSpec((None, SSD_HEADS, q), lambda d, c: (d, 0, chunk(d, c))),
                  small((1, SSD_HEADS)), small((SSD_HEADS, 1)), small((1, SSD_HEADS)), small((SSD_HEADS, 1)),
                  pl.BlockSpec((None, SSD_PAIRS, SSD_STATE, LANES), lambda d, c: (d, 0, 0, 0))],
        out_specs=[pl.BlockSpec((None, q, SSD_D_INNER), lambda d, c: (d, chunk(d, c), 0)),
                   pl.BlockSpec((None, SSD_PAIRS, SSD_STATE, LANES), lambda d, c: (d, 0, 0, 0))],
        out_shape=[jax.ShapeDtypeStruct((2, n_rows, SSD_D_INNER), F32),
                   jax.ShapeDtypeStruct((2, SSD_PAIRS, SSD_STATE, LANES), F32)],
        scratch_shapes=[pltpu.VMEM((SSD_PAIRS, SSD_STATE, LANES), F32)],
        compiler_params=_params("arbitrary", "arbitrary"),
    )(xbc, dt, dtt, dt_bias[:, None, :], dt_bias[:, :, None], a_log[:, None, :], a_log[:, :, None], h0)


def _ssd_out_kernel(y_ref, xbc_ref, z_ref, dsk_ref, nw_ref, o_ref):
    y = y_ref[0] + y_ref[1] + dsk_ref[...] * xbc_ref[...]
    g = y * _silu(z_ref[...])
    ms = jnp.mean(g * g, axis=-1, keepdims=True)
    o_ref[...] = (g * lax.rsqrt(ms + EPS) * nw_ref[...]).astype(o_ref.dtype)


def _ssd_out(y2, xbc, zx, d_skip, norm_w):
    n_rows = xbc.shape[0]
    tl = _pick(n_rows, (512, 256, 128))
    w = SSD_D_INNER
    return pl.pallas_call(
        _ssd_out_kernel,
        grid=(n_rows // tl,),
        in_specs=[pl.BlockSpec((2, tl, w), lambda i: (0, i, 0)),
                  pl.BlockSpec((tl, w), lambda i: (i, 0)),
                  pl.BlockSpec((tl, w), lambda i: (i, 0)),
                  pl.BlockSpec((1, w), lambda i: (0, 0)),
                  pl.BlockSpec((1, w), lambda i: (0, 0))],
        out_specs=pl.BlockSpec((tl, w), lambda i: (i, 0)),
        out_shape=jax.ShapeDtypeStruct((n_rows, w), MXU_DTYPE),
        compiler_params=_params("parallel"),
    )(y2, xbc, zx, jnp.repeat(d_skip, SSD_HEAD_DIM)[None, :], norm_w[None, :])


def _head_norm(x, w, scale):
    blk = (lax.broadcasted_iota(jnp.int32, (LANES, LANES), 0) // NA_HEAD_DIM
           == lax.broadcasted_iota(jnp.int32, (LANES, LANES), 1) // NA_HEAD_DIM).astype(F32)
    ms = _dot(x * x, blk, HI) * (1.0 / NA_HEAD_DIM)
    return x * lax.rsqrt(ms + EPS) * (w * scale)


def _qkv_prep_kernel(q_ref, k_ref, v_ref, qw_ref, kw_ref, qo_ref, ko_ref, vo_ref):
    qo_ref[...] = _head_norm(q_ref[...], qw_ref[...], NA_HEAD_DIM ** -0.5).astype(qo_ref.dtype)
    ko_ref[...] = _head_norm(k_ref[...], kw_ref[...], 1.0).astype(ko_ref.dtype)
    vo_ref[...] = v_ref[...].astype(vo_ref.dtype)


def _qkv_prep(qkvd, q_norm_w, k_norm_w):
    n_rows = qkvd.shape[0]
    tl = _pick(n_rows, (512, 256, 128))
    nb = NA_D // LANES
    tile = lambda off: pl.BlockSpec((tl, LANES), lambda i, j: (i, j + off))
    wspec = pl.BlockSpec((1, LANES), lambda i, j: (0, 0))
    out = jax.ShapeDtypeStruct((n_rows, NA_D), MXU_DTYPE)
    return pl.pallas_call(
        _qkv_prep_kernel,
        grid=(n_rows // tl, nb),
        in_specs=[tile(0), tile(nb), tile(2 * nb), wspec, wspec],
        out_specs=[tile(0), tile(0), tile(0)],
        out_shape=[out, out, out],
        compiler_params=_params("parallel", "parallel"),
    )(qkvd, qkvd, qkvd, jnp.tile(q_norm_w, 2)[None, :], jnp.tile(k_norm_w, 2)[None, :])


def _attend(q2, parts):
    first = lax.broadcasted_iota(jnp.int32, q2.shape, 1) < NA_HEAD_DIM
    outs = []
    for head_mask in (first, jnp.logical_not(first)):
        qa = jnp.where(head_mask, q2, jnp.zeros_like(q2))
        scores = []
        for k, _, bias in parts:
            s = _dot_nt(qa, k)
            scores.append(s if bias is None else s + bias)
        m = functools.reduce(jnp.maximum, [jnp.max(s, axis=-1, keepdims=True) for s in scores])
        probs = [jnp.exp(s - m) for s in scores]
        denom = functools.reduce(jnp.add, [jnp.sum(p, axis=-1, keepdims=True) for p in probs])
        acc = functools.reduce(jnp.add, [_dot(p.astype(MXU_DTYPE), v) for p, (_, v, _) in zip(probs, parts)])
        outs.append(acc / denom)
    return jnp.where(first, outs[0], outs[1])


def _natten_kernel(q_ref, k_ref, v_ref, kc_ref, vc_ref, tbl_ref, o_ref, *, rb, rows):
    i = pl.program_id(1)
    kc, vc = kc_ref[...], vc_ref[...]
    n_win = NA_WIN_ROWS * GRID_W

    def body(t, carry):
        r = i * rb + t
        start = jnp.clip(r - NA_WIN_ROWS // 2, 0, rows - NA_WIN_ROWS)
        dr0 = start - r + (NA_WIN_ROWS - 1)
        qs = pl.ds(pl.multiple_of(t * GRID_W, GRID_W), GRID_W)
        ks = pl.ds(pl.multiple_of(start * GRID_W, GRID_W), n_win)
        q2 = q_ref[qs, :]
        kw, vw = k_ref[ks, :], v_ref[ks, :]
        first = lax.broadcasted_iota(jnp.int32, q2.shape, 1) < NA_HEAD_DIM
        outs = []
        for a in range(2):
            qa = jnp.where(first if a == 0 else jnp.logical_not(first), q2, jnp.zeros_like(q2))
            s_loc = _dot_nt(qa, kw) + tbl_ref[a, dr0]
            s_ctx = _dot_nt(qa, kc)
            m = jnp.maximum(jnp.max(s_loc, axis=-1, keepdims=True), jnp.max(s_ctx, axis=-1, keepdims=True))
            p_loc, p_ctx = jnp.exp(s_loc - m), jnp.exp(s_ctx - m)
            denom = jnp.sum(p_loc, axis=-1, keepdims=True) + jnp.sum(p_ctx, axis=-1, keepdims=True)
            acc = _dot(p_loc.astype(MXU_DTYPE), vw) + _dot(p_ctx.astype(MXU_DTYPE), vc)
            outs.append(acc / denom)
        o_ref[qs, :] = jnp.where(first, outs[0], outs[1]).astype(o_ref.dtype)
        return carry

    lax.fori_loop(0, rb, body, 0)


def _bias_table(rpb):
    col = jnp.arange(GRID_W)
    c0 = jnp.clip(col - NA_WIN_COLS // 2, 0, GRID_W - NA_WIN_COLS)
    col_in = (col[None, :] >= c0[:, None]) & (col[None, :] < c0[:, None] + NA_WIN_COLS)
    dc = jnp.clip(col[None, :] - col[:, None], 1 - NA_WIN_COLS, NA_WIN_COLS - 1) + (NA_WIN_COLS - 1)
    t = jnp.where(col_in, rpb.astype(F32)[:, :, dc], NEG)
    dr = jnp.arange(NA_WIN_ROWS)[:, None] + jnp.arange(NA_WIN_ROWS)[None, :]
    t = t[:, dr]
    return t.transpose(0, 1, 3, 2, 4).reshape(NA_HEADS, NA_WIN_ROWS, GRID_W, NA_WIN_ROWS * GRID_W)


def _natten(qn, kn, vb, kc, vc, rpb):
    n_rows = qn.shape[0]
    rows = n_rows // GRID_W
    n_ctx = kc.shape[0]
    rb = _pick(rows, (4, 2, 1))
    n_win = NA_WIN_ROWS * GRID_W
    seq = lambda n: pl.BlockSpec((n, LANES), lambda p, i: (0, p))
    tile = pl.BlockSpec((rb * GRID_W, LANES), lambda p, i: (i, p))
    return pl.pallas_call(
        functools.partial(_natten_kernel, rb=rb, rows=rows),
        grid=(NA_HEADS // 2, rows // rb),
        in_specs=[tile, seq(n_rows), seq(n_rows), seq(n_ctx), seq(n_ctx),
                  pl.BlockSpec((2, NA_WIN_ROWS, GRID_W, n_win), lambda p, i: (p, 0, 0, 0))],
        out_specs=tile,
        out_shape=jax.ShapeDtypeStruct((n_rows, NA_D), MXU_DTYPE),
        compiler_params=_params("parallel", "arbitrary"),
    )(qn, kn, vb, kc, vc, _bias_table(rpb))


def _ctx_attn_kernel(q_ref, k_ref, v_ref, o_ref):
    o_ref[...] = _attend(q_ref[...], [(k_ref[...], v_ref[...], None)]).astype(o_ref.dtype)


def _ctx_attn(qn, kn, vb):
    n_ctx = qn.shape[0]
    spec = pl.BlockSpec((n_ctx, LANES), lambda p: (0, p))
    return pl.pallas_call(
        _ctx_attn_kernel,
        grid=(NA_HEADS // 2,),
        in_specs=[spec, spec, spec],
        out_specs=spec,
        out_shape=jax.ShapeDtypeStruct((n_ctx, NA_D), MXU_DTYPE),
        compiler_params=_params("parallel"),
    )(qn, kn, vb)


def _router_kernel(x_ref, nw_ref, sh_ref, sc_ref, wr_ref, m_ref, e_ref, g_ref):
    m = _norm_mod(x_ref[...], nw_ref[...], sh_ref[...], sc_ref[...])
    m_ref[...] = m.astype(m_ref.dtype)
    logits = _dot(m, wr_ref[...], HI)
    lane = lax.broadcasted_iota(jnp.int32, logits.shape, 1)
    big = jnp.int32(LANES)

    def top(vals):
        v = jnp.max(vals, axis=-1, keepdims=True)
        idx = jnp.min(jnp.where(vals == v, lane, big), axis=-1, keepdims=True)
        return v, idx

    gl = jnp.where(lane < N_GROUPS, logits, -jnp.inf)
    g_max, grp = top(gl)
    p_grp = 1.0 / jnp.sum(jnp.exp(gl - g_max), axis=-1, keepdims=True)
    e_lane = lane - N_GROUPS
    in_grp = (e_lane >= grp * EXPERTS_PER_GROUP) & (e_lane < (grp + 1) * EXPERTS_PER_GROUP)
    el = jnp.where(in_grp, logits, -jnp.inf)
    v1, i1 = top(el)
    v2, i2 = top(jnp.where(lane == i1, -jnp.inf, el))
    t = jnp.exp(v2 - v1)
    g1 = p_grp / (1.0 + t)
    g2 = p_grp * t / (1.0 + t)
    e_ref[...] = jnp.where(lane == 0, i1 - N_GROUPS, jnp.where(lane == 1, i2 - N_GROUPS, 0))
    g_ref[...] = jnp.where(lane == 0, g1, jnp.where(lane == 1, g2, 0.0))


def _router(x, nw, sh, sc, w_route):
    m_rows, d = x.shape
    tm = _pick(m_rows, (512, 256, 128))
    vec = pl.BlockSpec((1, d), lambda i: (0, 0))
    wide = pl.BlockSpec((tm, LANES), lambda i: (i, 0))
    m, e, g = pl.pallas_call(
        _router_kernel,
        grid=(m_rows // tm,),
        in_specs=[pl.BlockSpec((tm, d), lambda i: (i, 0)), vec, vec, vec,
                  pl.BlockSpec((d, LANES), lambda i: (0, 0))],
        out_specs=[pl.BlockSpec((tm, d), lambda i: (i, 0)), wide, wide],
        out_shape=[jax.ShapeDtypeStruct((m_rows, d), MXU_DTYPE),
                   jax.ShapeDtypeStruct((m_rows, LANES), jnp.int32),
                   jax.ShapeDtypeStruct((m_rows, LANES), F32)],
        compiler_params=_params("parallel"),
    )(x, nw, sh, sc, w_route)
    return m, e[:, :TOP_K], g[:, :TOP_K]


def _moe_kernel(be_ref, x_ref, gt_ref, wg_ref, wu_ref, wd_ref, o_ref, wgb_ref, wub_ref, wdb_ref):
    b = pl.program_id(0)
    changed = jnp.logical_or(b == 0, be_ref[b] != be_ref[jnp.maximum(b - 1, 0)])

    @pl.when(changed)
    def _():
        wgb_ref[...] = wg_ref[...].astype(wgb_ref.dtype)
        wub_ref[...] = wu_ref[...].astype(wub_ref.dtype)
        wdb_ref[...] = wd_ref[...].astype(wdb_ref.dtype)

    x = x_ref[...]
    h = _silu(_dot(x, wgb_ref[...])) * _dot(x, wub_ref[...])
    o_ref[...] = _dot(h.astype(MXU_DTYPE), wdb_ref[...]) * gt_ref[...]


def _moe_ffn(xb, slot_gate, block_expert, w_gate, w_up, w_down):
    n_rows, d = xb.shape
    n_blocks = n_rows // MOE_BLOCK
    f = w_gate.shape[-1]
    grid_spec = pltpu.PrefetchScalarGridSpec(
        num_scalar_prefetch=1,
        grid=(n_blocks,),
        in_specs=[pl.BlockSpec((MOE_BLOCK, d), lambda b, be: (b, 0)),
                  pl.BlockSpec((MOE_BLOCK, 1), lambda b, be: (b, 0)),
                  pl.BlockSpec((None, d, f), lambda b, be: (be[b], 0, 0)),
                  pl.BlockSpec((None, d, f), lambda b, be: (be[b], 0, 0)),
                  pl.BlockSpec((None, f, d), lambda b, be: (be[b], 0, 0))],
        out_specs=pl.BlockSpec((MOE_BLOCK, d), lambda b, be: (b, 0)),
        scratch_shapes=[pltpu.VMEM((d, f), MXU_DTYPE), pltpu.VMEM((d, f), MXU_DTYPE), pltpu.VMEM((f, d), MXU_DTYPE)],
    )
    return pl.pallas_call(
        _moe_kernel,
        grid_spec=grid_spec,
        out_shape=jax.ShapeDtypeStruct((n_rows, d), F32),
        compiler_params=_params("arbitrary"),
    )(block_expert, xb, slot_gate[:, None], w_gate, w_up, w_down)


def _dispatch(expert, gates):
    n_tok = expert.shape[0]
    n = n_tok * TOP_K
    n_blocks = -(-n // MOE_BLOCK) + N_EXPERTS
    n_rows = n_blocks * MOE_BLOCK
    flat_e = expert.reshape(-1)
    onehot = (flat_e[:, None] == jnp.arange(N_EXPERTS, dtype=jnp.int32)[None, :]).astype(jnp.int32)
    csum = jnp.cumsum(onehot, axis=0)
    rank = jnp.take_along_axis(csum, flat_e[:, None], axis=1)[:, 0] - 1
    counts = csum[-1]
    padded = (counts + MOE_BLOCK - 1) // MOE_BLOCK * MOE_BLOCK
    pad_end = jnp.cumsum(padded)
    dest = (pad_end - padded)[flat_e] + rank
    slot_tok = jnp.full((n_rows,), n_tok, jnp.int32).at[dest].set(jnp.arange(n, dtype=jnp.int32) // TOP_K)
    slot_gate = jnp.zeros((n_rows,), F32).at[dest].set(gates.reshape(-1))
    block_expert = jnp.minimum(jnp.searchsorted(pad_end, jnp.arange(n_blocks) * MOE_BLOCK, side='right'),
                               N_EXPERTS - 1).astype(jnp.int32)
    return slot_tok, slot_gate, block_expert


def _moe(m_tok, expert, gates, w_gate, w_up, w_down):
    n_tok, d = m_tok.shape
    slot_tok, slot_gate, block_expert = _dispatch(expert, gates)
    m_pad = jnp.concatenate([m_tok, jnp.zeros((1, d), m_tok.dtype)], axis=0)
    yb = _moe_ffn(m_pad[slot_tok], slot_gate, block_expert, w_gate, w_up, w_down)
    return jnp.zeros((n_tok + 1, d), F32).at[slot_tok].add(yb)[:n_tok]


def _res_kernel(x_ref, g_ref, f_ref, o_ref):
    o_ref[...] = x_ref[...] + g_ref[...] * f_ref[...]


def _residual(x, g, f):
    m, d = x.shape
    tm = _pick(m, (512, 256, 128))
    tile = pl.BlockSpec((tm, d), lambda i: (i, 0))
    return pl.pallas_call(
        _res_kernel,
        grid=(m // tm,),
        in_specs=[tile, pl.BlockSpec((1, d), lambda i: (0, 0)), tile],
        out_specs=tile,
        out_shape=jax.ShapeDtypeStruct((m, d), F32),
        compiler_params=_params("parallel"),
    )(x, g, f)


def _even_mixer(lat, cx, nw, mod_lat, mod_ctx, w_in, conv_w, conv_b, dt_bias, a_log, d_skip, ssd_norm_w,
                q_norm_w, k_norm_w, rpb, w_out, ctx_out):
    w_qkvd = jnp.concatenate([w_in[:, S_DT:], w_in[:, S_ZX:S_DT],
                              jnp.zeros((w_in.shape[0], LANES - 2 * SSD_HEADS), w_in.dtype)], axis=1)

    def project(x, mod):
        sh, sc = mod[0], mod[1]
        zx = _nm_matmul(x, nw, sh, sc, w_in, S_ZX)
        qkvd = _nm_matmul(x, nw, sh, sc, w_qkvd, QKVD)
        xbc = _xbc_conv(zx, conv_w, conv_b)
        qn, kn, vb = _qkv_prep(qkvd, q_norm_w, k_norm_w)
        return zx, xbc, qkvd[:, 3 * NA_D:3 * NA_D + 2 * SSD_HEADS], qn, kn, vb

    zx_c, xbc_c, dt_c, qn_c, kn_c, vb_c = project(cx, mod_ctx)
    zx_l, xbc_l, dt_l, qn_l, kn_l, vb_l = project(lat, mod_lat)
    h0 = jnp.zeros((2, SSD_PAIRS, SSD_STATE, LANES), F32)
    y_c, h_ctx = _ssd(xbc_c, dt_c, dt_bias, a_log, h0)
    y_l, _ = _ssd(xbc_l, dt_l, dt_bias, a_log, h_ctx)
    yssd_l = _ssd_out(y_l, xbc_l, zx_l, d_skip, ssd_norm_w)
    yatt_l = _natten(qn_l, kn_l, vb_l, kn_c, vb_c, rpb)
    lat = _matmul_res([yssd_l, yatt_l], w_out, lat, mod_lat[2])
    if ctx_out:
        yssd_c = _ssd_out(y_c, xbc_c, zx_c, d_skip, ssd_norm_w)
        yatt_c = _ctx_attn(qn_c, kn_c, vb_c)
        cx = _matmul_res([yssd_c, yatt_c], w_out, cx, mod_ctx[2])
    return lat, cx


def _odd_mixer(lat, cx, nw, mod_lat, mod_ctx, w_in, conv_w, w_out, ctx_out):
    def mix(x, mod):
        h3 = _nm_matmul(x, nw, mod[0], mod[1], w_in, w_in.shape[1])
        return _matmul_res([_gate_conv(h3, conv_w)], w_out, x, mod[2])

    lat = mix(lat, mod_lat)
    if ctx_out:
        cx = mix(cx, mod_ctx)
    return lat, cx


def kernel(x, c, ctx, c_ctx, w_mod, b_mod, norm_mix_w, norm_ffn_w, ev_w_in, ev_conv_w, ev_conv_b, ev_dt_bias,
           ev_a_log, ev_d_skip, ev_ssd_norm_w, ev_q_norm_w, ev_k_norm_w, ev_rpb, ev_w_out, od_w_in, od_conv_w,
           od_w_out, moe_w_group, moe_w_router, moe_w_gate, moe_w_up, moe_w_down):
    bsz, n_lat, d = x.shape
    assert bsz == 1 and ctx.shape[0] == 1
    assert n_lat % GRID_W == 0 and n_lat // GRID_W >= NA_WIN_ROWS and n_lat % SSD_CHUNK == 0
    assert ctx.shape[1] % SSD_CHUNK == 0 and d % LANES == 0
    depth = w_mod.shape[0]
    lat, cx = x[0], ctx[0]
    mods = _modulation(jnp.stack([c[0], c_ctx], axis=1), w_mod, b_mod)

    for l in range(depth):
        even = l % 2 == 0
        ctx_out = any(j % 2 == 0 for j in range(l + 1, depth))
        mod_lat = [mods[l, 0:1, i * d:(i + 1) * d] for i in range(6)]
        mod_ctx = [mods[l, 1:2, i * d:(i + 1) * d] for i in range(6)]
        nw = norm_mix_w[l][None, :]
        if even:
            e = l // 2
            lat, cx = _even_mixer(lat, cx, nw, mod_lat, mod_ctx, ev_w_in[e], ev_conv_w[e], ev_conv_b[e],
                                  ev_dt_bias[e], ev_a_log[e], ev_d_skip[e], ev_ssd_norm_w[e], ev_q_norm_w[e],
                                  ev_k_norm_w[e], ev_rpb[e], ev_w_out[e], ctx_out)
        else:
            o = l // 2
            lat, cx = _odd_mixer(lat, cx, nw, mod_lat, mod_ctx, od_w_in[o], od_conv_w[o], od_w_out[o], ctx_out)

        nfw = norm_ffn_w[l][None, :]
        w_route = jnp.concatenate([moe_w_group[l], moe_w_router[l],
                                   jnp.zeros((d, LANES - N_GROUPS - N_EXPERTS), F32)], axis=1)
        m_tok, expert, gates = _router(lat, nfw, mod_lat[3], mod_lat[4], w_route)
        if ctx_out:
            m_c, e_c, g_c = _router(cx, nfw, mod_ctx[3], mod_ctx[4], w_route)
            m_tok = jnp.concatenate([m_tok, m_c], axis=0)
            expert = jnp.concatenate([expert, e_c], axis=0)
            gates = jnp.concatenate([gates, g_c], axis=0)
        f = _moe(m_tok, expert, gates, moe_w_gate[l], moe_w_up[l], moe_w_down[l])
        lat = _residual(lat, mod_lat[5], f[:n_lat])
        if ctx_out:
            cx = _residual(cx, mod_ctx[5], f[n_lat:])
    return lat[None]
```

```python
import functools

import jax
import jax.numpy as jnp
from jax import lax
from jax.experimental import pallas as pl
from jax.experimental.pallas import tpu as pltpu

F32 = jnp.float32
MXU_DTYPE = jnp.bfloat16
HI = lax.Precision.HIGHEST
EPS = 1e-6
NEG = -1e30

LANES = 128
SUBLANES = 8
VMEM_LIMIT = 48 * 1024 * 1024

SSD_HEADS = 16
SSD_HEAD_DIM = 64
SSD_D_INNER = SSD_HEADS * SSD_HEAD_DIM
SSD_STATE = 128
SSD_GROUPS = 2
SSD_CHUNK = 128
SSD_BC = SSD_GROUPS * SSD_STATE
SSD_XBC = SSD_D_INNER + 2 * SSD_BC
SSD_PAIRS = SSD_HEADS // 2
NA_HEADS = 16
NA_HEAD_DIM = 64
NA_D = NA_HEADS * NA_HEAD_DIM
NA_WIN_ROWS = 8
NA_WIN_COLS = 16
GRID_W = 64
N_GROUPS = 8
EXPERTS_PER_GROUP = 8
N_EXPERTS = N_GROUPS * EXPERTS_PER_GROUP
TOP_K = 2
D_FF_EXPERT = 384
MOE_BLOCK = 128
S_ZX = SSD_D_INNER + SSD_XBC
S_DT = S_ZX + 2 * SSD_HEADS
QKVD = 3 * NA_D + LANES


def _pick(n, prefs):
    for p in prefs:
        if n % p == 0:
            return p
    return n


def _params(*sem):
    return pltpu.CompilerParams(dimension_semantics=sem, vmem_limit_bytes=VMEM_LIMIT)


def _vec(arr, idx):
    return arr.reshape(arr.shape[0], 1, arr.shape[-1]), idx


def _vec_spec(vec, width, col):
    idx = vec[1]
    return pl.BlockSpec((None, 1, width), lambda *g: (idx, 0, col(*g)))


def _mat_spec(layer, block, idx):
    return pl.BlockSpec((None,) + block, lambda *g: (layer,) + idx(*g))


def _zero(*g):
    return 0


def _silu(x):
    return x * (1.0 / (1.0 + jnp.exp(-x)))


def _softplus(x):
    return jnp.maximum(x, 0.0) + jnp.log1p(jnp.exp(-jnp.abs(x)))


def _dot(a, b, precision=None):
    return jnp.dot(a, b, preferred_element_type=F32, precision=precision)


def _dot_nt(a, b):
    return lax.dot_general(a, b, (((1,), (1,)), ((), ())), preferred_element_type=F32)


def _norm_mod(x, nw, sh, sc):
    ms = jnp.mean(x * x, axis=-1, keepdims=True)
    n = x * lax.rsqrt(ms + EPS) * nw
    return n * (1.0 + sc) + sh


def _mod_kernel(ct_ref, w_ref, b_ref, o_ref):
    ct = ct_ref[...]
    s = _silu(ct)
    w = w_ref[...]
    r0 = jnp.sum(s[:, 0:1] * w, axis=0, keepdims=True)
    r1 = jnp.sum(s[:, 1:2] * w, axis=0, keepdims=True)
    o_ref[...] = jnp.concatenate([r0, r1], axis=0) + b_ref[...]


def _modulation(ct, w_mod, b_mod):
    depth, d, n = w_mod.shape
    tn = _pick(n, (512, 256, 128))
    return pl.pallas_call(
        _mod_kernel,
        grid=(depth, n // tn),
        in_specs=[pl.BlockSpec((d, 2), lambda l, j: (0, 0)),
                  pl.BlockSpec((None, d, tn), lambda l, j: (l, 0, j)),
                  pl.BlockSpec((None, 1, tn), lambda l, j: (l, 0, j))],
        out_specs=pl.BlockSpec((None, 2, tn), lambda l, j: (l, 0, j)),
        out_shape=jax.ShapeDtypeStruct((depth, 2, n), F32),
        compiler_params=_params("parallel", "parallel"),
        name="modulation",
    )(ct, w_mod, b_mod.reshape(depth, 1, n))


def _nm_matmul_kernel(x_ref, nw_ref, sh_ref, sc_ref, w_ref, o_ref, a_ref):
    @pl.when(pl.program_id(1) == 0)
    def _():
        a_ref[...] = _norm_mod(x_ref[...], nw_ref[...], sh_ref[...], sc_ref[...]).astype(a_ref.dtype)

    o_ref[...] = _dot(a_ref[...], w_ref[...].astype(a_ref.dtype)).astype(o_ref.dtype)


def _nm_matmul(x, nw, sh, sc, w, layer, n_cols, name):
    m, d = x.shape
    tm = _pick(m, (1024, 512, 256, 128))
    tn = _pick(n_cols, (512, 640, 256, 128))
    return pl.pallas_call(
        _nm_matmul_kernel,
        grid=(m // tm, n_cols // tn),
        in_specs=[pl.BlockSpec((tm, d), lambda i, j: (i, 0)),
                  _vec_spec(nw, d, _zero), _vec_spec(sh, d, _zero), _vec_spec(sc, d, _zero),
                  _mat_spec(layer, (d, tn), lambda i, j: (0, j))],
        out_specs=pl.BlockSpec((tm, tn), lambda i, j: (i, j)),
        out_shape=jax.ShapeDtypeStruct((m, n_cols), F32),
        scratch_shapes=[pltpu.VMEM((tm, d), MXU_DTYPE)],
        compiler_params=_params("parallel", "arbitrary"),
        name=name,
    )(x, nw[0], sh[0], sc[0], w)


def _matmul_res_kernel(*refs, n_a):
    a_refs, w_refs = refs[:n_a], refs[n_a:2 * n_a]
    res_ref, g_ref, o_ref = refs[2 * n_a:]
    acc = None
    for a_ref, w_ref in zip(a_refs, w_refs):
        t = _dot(a_ref[...].astype(MXU_DTYPE), w_ref[...].astype(MXU_DTYPE))
        acc = t if acc is None else acc + t
    o_ref[...] = res_ref[...] + g_ref[...] * acc


def _matmul_res(a_list, w, layer, res, g, name):
    n_a = len(a_list)
    m, n = res.shape
    kp = w.shape[1] // n_a
    tm = _pick(m, (1024, 512, 256, 128))
    tn = _pick(n, (512, 256, 128))
    in_specs = [pl.BlockSpec((tm, kp), lambda i, j: (i, 0)) for _ in range(n_a)]
    in_specs += [_mat_spec(layer, (kp, tn), functools.partial(lambda i, j, p: (p, j), p=p)) for p in range(n_a)]
    in_specs += [pl.BlockSpec((tm, tn), lambda i, j: (i, j)), _vec_spec(g, tn, lambda i, j: j)]
    return pl.pallas_call(
        functools.partial(_matmul_res_kernel, n_a=n_a),
        grid=(m // tm, n // tn),
        in_specs=in_specs,
        out_specs=pl.BlockSpec((tm, tn), lambda i, j: (i, j)),
        out_shape=jax.ShapeDtypeStruct((m, n), F32),
        compiler_params=_params("parallel", "parallel"),
        name=name,
    )(*a_list, *([w] * n_a), res, g[0])


def _shifted(x, prev_ref, next_ref):
    i, n_i = pl.program_id(0), pl.num_programs(0)
    tl = x.shape[0]
    row = lax.broadcasted_iota(jnp.int32, x.shape, 0)
    prev_row = jnp.where(i == 0, 0.0, prev_ref[SUBLANES - 1:SUBLANES, :])
    next_row = jnp.where(i == n_i - 1, 0.0, next_ref[0:1, :])
    up = jnp.where(row == 0, prev_row, pltpu.roll(x, 1, 0))
    down = jnp.where(row == tl - 1, next_row, pltpu.roll(x, tl - 1, 0))
    return up, down


def _halo_specs(tl, tc, n_rows, col_off):
    nb = tl // SUBLANES
    last = n_rows // SUBLANES - 1
    return [pl.BlockSpec((tl, tc), lambda i, j: (i, j + col_off)),
            pl.BlockSpec((SUBLANES, tc), lambda i, j: (jnp.maximum(i * nb - 1, 0), j + col_off)),
            pl.BlockSpec((SUBLANES, tc), lambda i, j: (jnp.minimum((i + 1) * nb, last), j + col_off))]


def _xbc_conv_kernel(x_ref, prev_ref, next_ref, w_ref, b_ref, o_ref):
    x = x_ref[...]
    up, down = _shifted(x, prev_ref, next_ref)
    w = w_ref[...]
    o_ref[...] = _silu(w[0:1] * up + w[1:2] * x + w[2:3] * down + b_ref[...])


def _xbc_conv(zx, conv_w, conv_b):
    n_rows = zx.shape[0]
    tl = _pick(n_rows, (512, 256, 128))
    tc = 512
    off = SSD_D_INNER // tc
    return pl.pallas_call(
        _xbc_conv_kernel,
        grid=(n_rows // tl, SSD_XBC // tc),
        in_specs=_halo_specs(tl, tc, n_rows, off) + [pl.BlockSpec((3, tc), lambda i, j: (0, j)),
                                                     pl.BlockSpec((1, tc), lambda i, j: (0, j))],
        out_specs=pl.BlockSpec((tl, tc), lambda i, j: (i, j)),
        out_shape=jax.ShapeDtypeStruct((n_rows, SSD_XBC), F32),
        compiler_params=_params("parallel", "parallel"),
        name="xbc_conv",
    )(zx, zx, zx, conv_w, conv_b.reshape(1, -1))


def _gate_conv_kernel(bg_ref, cg_ref, cgp_ref, cgn_ref, xv_ref, xvp_ref, xvn_ref, w_ref, o_ref):
    i, n_i = pl.program_id(0), pl.num_programs(0)
    u = cg_ref[...] * xv_ref[...]
    tl = u.shape[0]
    row = lax.broadcasted_iota(jnp.int32, u.shape, 0)
    prev_row = jnp.where(i == 0, 0.0, cgp_ref[SUBLANES - 1:SUBLANES, :] * xvp_ref[SUBLANES - 1:SUBLANES, :])
    next_row = jnp.where(i == n_i - 1, 0.0, cgn_ref[0:1, :] * xvn_ref[0:1, :])
    up = jnp.where(row == 0, prev_row, pltpu.roll(u, 1, 0))
    down = jnp.where(row == tl - 1, next_row, pltpu.roll(u, tl - 1, 0))
    w = w_ref[...]
    o_ref[...] = (bg_ref[...] * (w[0:1] * up + w[1:2] * u + w[2:3] * down)).astype(o_ref.dtype)


def _gate_conv(h3, conv_w):
    n_rows, c3 = h3.shape
    c = c3 // 3
    tl = _pick(n_rows, (512, 256, 128))
    tc = _pick(c, (512, 256, 128))
    nb = c // tc
    return pl.pallas_call(
        _gate_conv_kernel,
        grid=(n_rows // tl, nb),
        in_specs=[pl.BlockSpec((tl, tc), lambda i, j: (i, j))] + _halo_specs(tl, tc, n_rows, nb)
        + _halo_specs(tl, tc, n_rows, 2 * nb) + [pl.BlockSpec((3, tc), lambda i, j: (0, j))],
        out_specs=pl.BlockSpec((tl, tc), lambda i, j: (i, j)),
        out_shape=jax.ShapeDtypeStruct((n_rows, c), MXU_DTYPE),
        compiler_params=_params("parallel", "parallel"),
        name="gate_conv",
    )(h3, h3, h3, h3, h3, h3, h3, conv_w)


def _ssd_kernel(xbc_ref, dt_ref, dtt_ref, dtb_ref, dtbt_ref, alog_ref, alogt_ref, h0_ref,
                y_ref, hout_ref, h_ref):
    d, c, n_c = pl.program_id(0), pl.program_id(1), pl.num_programs(1)
    q = SSD_CHUNK

    @pl.when(c == 0)
    def _():
        h_ref[...] = h0_ref[...]

    dt = _softplus(dt_ref[...] + dtb_ref[...])
    dtt = _softplus(dtt_ref[...] + dtbt_ref[...])
    a = dt * -jnp.exp(alog_ref[...])
    at = dtt * -jnp.exp(alogt_ref[...])
    row = lax.broadcasted_iota(jnp.int32, (q, q), 0)
    col = lax.broadcasted_iota(jnp.int32, (q, q), 1)
    sign = jnp.where(d == 0, 1, -1)
    mask = (row - col) * sign >= 0
    tri = mask.astype(F32)
    tri_t = ((col - row) * sign >= 0).astype(F32)
    cs = _dot(tri, a, HI)
    cst = _dot(at, tri_t, HI)
    tot = jnp.broadcast_to(jnp.sum(a, axis=0, keepdims=True), (SUBLANES, SSD_HEADS))
    expand = (lax.broadcasted_iota(jnp.int32, (SSD_HEADS, SSD_D_INNER), 1) // SSD_HEAD_DIM
              == lax.broadcasted_iota(jnp.int32, (SSD_HEADS, SSD_D_INNER), 0)).astype(F32)
    dt_e = _dot(dt, expand, HI)
    cs_e = _dot(cs, expand, HI)
    tot_e = _dot(tot, expand, HI)[0:1]

    xs = xbc_ref[:, 0:SSD_D_INNER]
    xdt = xs * dt_e
    xdt_m = xdt.astype(MXU_DTYPE)
    xw_m = (xdt * jnp.exp(tot_e - cs_e)).astype(MXU_DTYPE)
    e_e = jnp.exp(cs_e)
    dec_e = jnp.exp(tot_e)
    first = lax.broadcasted_iota(jnp.int32, (q, LANES), 1) < SSD_HEAD_DIM

    ppg = SSD_PAIRS // SSD_GROUPS
    for g in range(SSD_GROUPS):
        bg = xbc_ref[:, SSD_D_INNER + g * SSD_STATE:SSD_D_INNER + (g + 1) * SSD_STATE]
        cg = xbc_ref[:, SSD_D_INNER + SSD_BC + g * SSD_STATE:SSD_D_INNER + SSD_BC + (g + 1) * SSD_STATE]
        bg_m, cg_m = bg.astype(MXU_DTYPE), cg.astype(MXU_DTYPE)
        bgt_m = bg.T.astype(MXU_DTYPE)
        scores = _dot_nt(cg_m, bg_m)
        for pp in range(ppg):
            p = g * ppg + pp
            sl = slice(p * LANES, (p + 1) * LANES)
            ys = []
            for hh in range(2):
                h = 2 * p + hh
                diff = cs[:, h:h + 1] - cst[h:h + 1, :]
                decay = jnp.exp(jnp.where(mask, diff, -jnp.inf))
                ys.append(_dot((scores * decay).astype(MXU_DTYPE), xdt_m[:, sl]))
            hp = h_ref[p]
            y_off = _dot(cg_m, hp.astype(MXU_DTYPE)) * e_e[:, sl]
            y_ref[:, sl] = jnp.where(first, ys[0], ys[1]) + y_off
            h_ref[p] = dec_e[:, sl] * hp + _dot(bgt_m, xw_m[:, sl])

    @pl.when(c == n_c - 1)
    def _():
        hout_ref[...] = h_ref[...]


def _ssd(xbc, dt_raw, dt_bias, a_log, h0):
    n_rows = xbc.shape[0]
    q = SSD_CHUNK
    n_c = n_rows // q
    dt = dt_raw.reshape(n_rows, 2, SSD_HEADS).transpose(1, 0, 2)
    dtt = dt.transpose(0, 2, 1)

    def chunk(d, c):
        return jnp.where(d == 0, c, n_c - 1 - c)

    small = lambda shape: pl.BlockSpec((None,) + shape, lambda d, c: (d, 0, 0))
    return pl.pallas_call(
        _ssd_kernel,
        grid=(2, n_c),
        in_specs=[pl.BlockSpec((q, SSD_XBC), lambda d, c: (chunk(d, c), 0)),
                  pl.BlockSpec((None, q, SSD_HEADS), lambda d, c: (d, chunk(d, c), 0)),
                  pl.BlockSpec((None, SSD_HEADS, q), lambda d, c: (d, 0, chunk(d, c))),
                  small((1, SSD_HEADS)), small((SSD_HEADS, 1)), small((1, SSD_HEADS)), small((SSD_HEADS, 1)),
                  pl.BlockSpec((None, SSD_PAIRS, SSD_STATE, LANES), lambda d, c: (d, 0, 0, 0))],
        out_specs=[pl.BlockSpec((None, q, SSD_D_INNER), lambda d, c: (d, chunk(d, c), 0)),
                   pl.BlockSpec((None, SSD_PAIRS, SSD_STATE, LANES), lambda d, c: (d, 0, 0, 0))],
        out_shape=[jax.ShapeDtypeStruct((2, n_rows, SSD_D_INNER), F32),
                   jax.ShapeDtypeStruct((2, SSD_PAIRS, SSD_STATE, LANES), F32)],
        scratch_shapes=[pltpu.VMEM((SSD_PAIRS, SSD_STATE, LANES), F32)],
        compiler_params=_params("arbitrary", "arbitrary"),
        name="ssd_scan",
    )(xbc, dt, dtt, dt_bias[:, None, :], dt_bias[:, :, None], a_log[:, None, :], a_log[:, :, None], h0)


def _ssd_out_kernel(y_ref, xbc_ref, z_ref, dsk_ref, nw_ref, o_ref):
    y = y_ref[0] + y_ref[1] + dsk_ref[...] * xbc_ref[...]
    g = y * _silu(z_ref[...])
    ms = jnp.mean(g * g, axis=-1, keepdims=True)
    o_ref[...] = (g * lax.rsqrt(ms + EPS) * nw_ref[...]).astype(o_ref.dtype)


def _ssd_out(y2, xbc, zx, d_skip, norm_w):
    n_rows = xbc.shape[0]
    tl = _pick(n_rows, (512, 256, 128))
    w = SSD_D_INNER
    return pl.pallas_call(
        _ssd_out_kernel,
        grid=(n_rows // tl,),
        in_specs=[pl.BlockSpec((2, tl, w), lambda i: (0, i, 0)),
                  pl.BlockSpec((tl, w), lambda i: (i, 0)),
                  pl.BlockSpec((tl, w), lambda i: (i, 0)),
                  pl.BlockSpec((1, w), lambda i: (0, 0)),
                  pl.BlockSpec((1, w), lambda i: (0, 0))],
        out_specs=pl.BlockSpec((tl, w), lambda i: (i, 0)),
        out_shape=jax.ShapeDtypeStruct((n_rows, w), MXU_DTYPE),
        compiler_params=_params("parallel"),
        name="ssd_out",
    )(y2, xbc, zx, jnp.repeat(d_skip, SSD_HEAD_DIM)[None, :], norm_w[None, :])


def _head_norm(x, w, scale):
    blk = (lax.broadcasted_iota(jnp.int32, (LANES, LANES), 0) // NA_HEAD_DIM
           == lax.broadcasted_iota(jnp.int32, (LANES, LANES), 1) // NA_HEAD_DIM).astype(F32)
    ms = _dot(x * x, blk, HI) * (1.0 / NA_HEAD_DIM)
    return x * lax.rsqrt(ms + EPS) * (w * scale)


def _qkv_prep_kernel(q_ref, k_ref, v_ref, qw_ref, kw_ref, qo_ref, ko_ref, vo_ref):
    qo_ref[...] = _head_norm(q_ref[...], qw_ref[...], NA_HEAD_DIM ** -0.5).astype(qo_ref.dtype)
    ko_ref[...] = _head_norm(k_ref[...], kw_ref[...], 1.0).astype(ko_ref.dtype)
    vo_ref[...] = v_ref[...].astype(vo_ref.dtype)


def _qkv_prep(qkvd, q_norm_w, k_norm_w):
    n_rows = qkvd.shape[0]
    tl = _pick(n_rows, (512, 256, 128))
    nb = NA_D // LANES
    tile = lambda off: pl.BlockSpec((tl, LANES), lambda i, j: (i, j + off))
    wspec = pl.BlockSpec((1, LANES), lambda i, j: (0, 0))
    out = jax.ShapeDtypeStruct((n_rows, NA_D), MXU_DTYPE)
    return pl.pallas_call(
        _qkv_prep_kernel,
        grid=(n_rows // tl, nb),
        in_specs=[tile(0), tile(nb), tile(2 * nb), wspec, wspec],
        out_specs=[tile(0), tile(0), tile(0)],
        out_shape=[out, out, out],
        compiler_params=_params("parallel", "parallel"),
        name="qkv_prep",
    )(qkvd, qkvd, qkvd, jnp.tile(q_norm_w, 2)[None, :], jnp.tile(k_norm_w, 2)[None, :])


def _attend(q2, parts):
    first = lax.broadcasted_iota(jnp.int32, q2.shape, 1) < NA_HEAD_DIM
    outs = []
    for head_mask in (first, jnp.logical_not(first)):
        qa = jnp.where(head_mask, q2, jnp.zeros_like(q2))
        scores = []
        for k, _, bias in parts:
            s = _dot_nt(qa, k)
            scores.append(s if bias is None else s + bias)
        m = functools.reduce(jnp.maximum, [jnp.max(s, axis=-1, keepdims=True) for s in scores])
        probs = [jnp.exp(s - m) for s in scores]
        denom = functools.reduce(jnp.add, [jnp.sum(p, axis=-1, keepdims=True) for p in probs])
        acc = functools.reduce(jnp.add, [_dot(p.astype(MXU_DTYPE), v) for p, (_, v, _) in zip(probs, parts)])
        outs.append(acc / denom)
    return jnp.where(first, outs[0], outs[1])


def _natten_kernel(q_ref, k_ref, v_ref, kc_ref, vc_ref, tbl_ref, o_ref, *, rb, rows):
    i = pl.program_id(1)
    kc, vc = kc_ref[...], vc_ref[...]
    n_win = NA_WIN_ROWS * GRID_W

    def body(t, carry):
        r = i * rb + t
        start = jnp.clip(r - NA_WIN_ROWS // 2, 0, rows - NA_WIN_ROWS)
        dr0 = start - r + (NA_WIN_ROWS - 1)
        qs = pl.ds(pl.multiple_of(t * GRID_W, GRID_W), GRID_W)
        ks = pl.ds(pl.multiple_of(start * GRID_W, GRID_W), n_win)
        q2 = q_ref[qs, :]
        kw, vw = k_ref[ks, :], v_ref[ks, :]
        first = lax.broadcasted_iota(jnp.int32, q2.shape, 1) < NA_HEAD_DIM
        outs = []
        for a in range(2):
            qa = jnp.where(first if a == 0 else jnp.logical_not(first), q2, jnp.zeros_like(q2))
            s_loc = _dot_nt(qa, kw) + tbl_ref[a, dr0]
            s_ctx = _dot_nt(qa, kc)
            m = jnp.maximum(jnp.max(s_loc, axis=-1, keepdims=True), jnp.max(s_ctx, axis=-1, keepdims=True))
            p_loc, p_ctx = jnp.exp(s_loc - m), jnp.exp(s_ctx - m)
            denom = jnp.sum(p_loc, axis=-1, keepdims=True) + jnp.sum(p_ctx, axis=-1, keepdims=True)
            acc = _dot(p_loc.astype(MXU_DTYPE), vw) + _dot(p_ctx.astype(MXU_DTYPE), vc)
            outs.append(acc / denom)
        o_ref[qs, :] = jnp.where(first, outs[0], outs[1]).astype(o_ref.dtype)
        return carry

    lax.fori_loop(0, rb, body, 0)


def _bias_table(rpb):
    col = jnp.arange(GRID_W)
    c0 = jnp.clip(col - NA_WIN_COLS // 2, 0, GRID_W - NA_WIN_COLS)
    col_in = (col[None, :] >= c0[:, None]) & (col[None, :] < c0[:, None] + NA_WIN_COLS)
    dc = jnp.clip(col[None, :] - col[:, None], 1 - NA_WIN_COLS, NA_WIN_COLS - 1) + (NA_WIN_COLS - 1)
    t = jnp.where(col_in, rpb.astype(F32)[:, :, dc], NEG)
    dr = jnp.arange(NA_WIN_ROWS)[:, None] + jnp.arange(NA_WIN_ROWS)[None, :]
    t = t[:, dr]
    return t.transpose(0, 1, 3, 2, 4).reshape(NA_HEADS, NA_WIN_ROWS, GRID_W, NA_WIN_ROWS * GRID_W)


def _natten(qn, kn, vb, kc, vc, rpb):
    n_rows = qn.shape[0]
    rows = n_rows // GRID_W
    n_ctx = kc.shape[0]
    rb = _pick(rows, (4, 2, 1))
    n_win = NA_WIN_ROWS * GRID_W
    seq = lambda n: pl.BlockSpec((n, LANES), lambda p, i: (0, p))
    tile = pl.BlockSpec((rb * GRID_W, LANES), lambda p, i: (i, p))
    return pl.pallas_call(
        functools.partial(_natten_kernel, rb=rb, rows=rows),
        grid=(NA_HEADS // 2, rows // rb),
        in_specs=[tile, seq(n_rows), seq(n_rows), seq(n_ctx), seq(n_ctx),
                  pl.BlockSpec((2, NA_WIN_ROWS, GRID_W, n_win), lambda p, i: (p, 0, 0, 0))],
        out_specs=tile,
        out_shape=jax.ShapeDtypeStruct((n_rows, NA_D), MXU_DTYPE),
        compiler_params=_params("parallel", "arbitrary"),
        name="natten",
    )(qn, kn, vb, kc, vc, _bias_table(rpb))


def _ctx_attn_kernel(q_ref, k_ref, v_ref, o_ref):
    o_ref[...] = _attend(q_ref[...], [(k_ref[...], v_ref[...], None)]).astype(o_ref.dtype)


def _ctx_attn(qn, kn, vb):
    n_ctx = qn.shape[0]
    spec = pl.BlockSpec((n_ctx, LANES), lambda p: (0, p))
    return pl.pallas_call(
        _ctx_attn_kernel,
        grid=(NA_HEADS // 2,),
        in_specs=[spec, spec, spec],
        out_specs=spec,
        out_shape=jax.ShapeDtypeStruct((n_ctx, NA_D), MXU_DTYPE),
        compiler_params=_params("parallel"),
        name="ctx_attn",
    )(qn, kn, vb)


def _router_kernel(x_ref, nw_ref, sh_ref, sc_ref, wr_ref, m_ref, e_ref, g_ref):
    m = _norm_mod(x_ref[...], nw_ref[...], sh_ref[...], sc_ref[...])
    m_ref[...] = m.astype(m_ref.dtype)
    logits = _dot(m, wr_ref[...], HI)
    lane = lax.broadcasted_iota(jnp.int32, logits.shape, 1)
    big = jnp.int32(LANES)

    def top(vals):
        v = jnp.max(vals, axis=-1, keepdims=True)
        idx = jnp.min(jnp.where(vals == v, lane, big), axis=-1, keepdims=True)
        return v, idx

    gl = jnp.where(lane < N_GROUPS, logits, -jnp.inf)
    g_max, grp = top(gl)
    p_grp = 1.0 / jnp.sum(jnp.exp(gl - g_max), axis=-1, keepdims=True)
    e_lane = lane - N_GROUPS
    in_grp = (e_lane >= grp * EXPERTS_PER_GROUP) & (e_lane < (grp + 1) * EXPERTS_PER_GROUP)
    el = jnp.where(in_grp, logits, -jnp.inf)
    v1, i1 = top(el)
    v2, i2 = top(jnp.where(lane == i1, -jnp.inf, el))
    t = jnp.exp(v2 - v1)
    g1 = p_grp / (1.0 + t)
    g2 = p_grp * t / (1.0 + t)
    e_ref[...] = jnp.where(lane == 0, i1 - N_GROUPS, jnp.where(lane == 1, i2 - N_GROUPS, 0))
    g_ref[...] = jnp.where(lane == 0, g1, jnp.where(lane == 1, g2, 0.0))


def _router(x, nw, sh, sc, w_route):
    m_rows, d = x.shape
    tm = _pick(m_rows, (512, 256, 128))
    wide = pl.BlockSpec((tm, LANES), lambda i: (i, 0))
    m, e, g = pl.pallas_call(
        _router_kernel,
        grid=(m_rows // tm,),
        in_specs=[pl.BlockSpec((tm, d), lambda i: (i, 0)),
                  _vec_spec(nw, d, _zero), _vec_spec(sh, d, _zero), _vec_spec(sc, d, _zero),
                  pl.BlockSpec((d, LANES), lambda i: (0, 0))],
        out_specs=[pl.BlockSpec((tm, d), lambda i: (i, 0)), wide, wide],
        out_shape=[jax.ShapeDtypeStruct((m_rows, d), F32),
                   jax.ShapeDtypeStruct((m_rows, LANES), jnp.int32),
                   jax.ShapeDtypeStruct((m_rows, LANES), F32)],
        compiler_params=_params("parallel"),
        name="router",
    )(x, nw[0], sh[0], sc[0], w_route)
    return m, e[:, :TOP_K], g[:, :TOP_K]


def _moe_kernel(be_ref, tok_ref, nact_ref, m_hbm, gt_ref, wg_ref, wu_ref, wd_ref, o_ref,
                xbuf, sem, wgb_ref, wub_ref, wdb_ref):
    b = pl.program_id(0)
    n_act = nact_ref[0]

    def issue(blk, slot):
        def body(r, carry):
            tok = tok_ref[blk * MOE_BLOCK + r]
            pltpu.make_async_copy(m_hbm.at[pl.ds(tok, 1), :], xbuf.at[slot, pl.ds(r, 1), :], sem.at[slot]).start()
            return carry

        lax.fori_loop(0, MOE_BLOCK, body, 0)

    @pl.when(jnp.logical_and(b == 0, n_act > 0))
    def _():
        issue(0, 0)

    @pl.when(b + 1 < n_act)
    def _():
        issue(b + 1, (b + 1) % 2)

    @pl.when(b < n_act)
    def _():
        slot = b % 2
        pltpu.make_async_copy(m_hbm.at[pl.ds(0, MOE_BLOCK), :], xbuf.at[slot], sem.at[slot]).wait()
        changed = jnp.logical_or(b == 0, be_ref[b] != be_ref[jnp.maximum(b - 1, 0)])

        @pl.when(changed)
        def _():
            wgb_ref[...] = wg_ref[...].astype(wgb_ref.dtype)
            wub_ref[...] = wu_ref[...].astype(wub_ref.dtype)
            wdb_ref[...] = wd_ref[...].astype(wdb_ref.dtype)

        x = xbuf[slot].astype(MXU_DTYPE)
        h = _silu(_dot(x, wgb_ref[...])) * _dot(x, wub_ref[...])
        o_ref[...] = _dot(h.astype(MXU_DTYPE), wdb_ref[...]) * gt_ref[...]

    @pl.when(b >= n_act)
    def _():
        o_ref[...] = jnp.zeros_like(o_ref)


def _moe_ffn(m_tok, slot_tok, slot_gate, block_expert, n_act, w_gate, w_up, w_down, layer):
    n_rows = slot_tok.shape[0]
    d = m_tok.shape[1]
    n_blocks = n_rows // MOE_BLOCK
    f = w_gate.shape[-1]
    wspec = lambda r, c: pl.BlockSpec((None, None, r, c), lambda b, be, tok, nact: (layer, be[b], 0, 0))
    grid_spec = pltpu.PrefetchScalarGridSpec(
        num_scalar_prefetch=3,
        grid=(n_blocks,),
        in_specs=[pl.BlockSpec(memory_space=pl.ANY),
                  pl.BlockSpec((MOE_BLOCK, 1), lambda b, be, tok, nact: (b, 0)),
                  wspec(d, f), wspec(d, f), wspec(f, d)],
        out_specs=pl.BlockSpec((MOE_BLOCK, d), lambda b, be, tok, nact: (b, 0)),
        scratch_shapes=[pltpu.VMEM((2, MOE_BLOCK, d), F32), pltpu.SemaphoreType.DMA((2,)),
                        pltpu.VMEM((d, f), MXU_DTYPE), pltpu.VMEM((d, f), MXU_DTYPE), pltpu.VMEM((f, d), MXU_DTYPE)],
    )
    return pl.pallas_call(
        _moe_kernel,
        grid_spec=grid_spec,
        out_shape=jax.ShapeDtypeStruct((n_rows, d), F32),
        compiler_params=_params("arbitrary"),
        name="moe_ffn",
    )(block_expert, slot_tok, n_act, m_tok, slot_gate[:, None], w_gate, w_up, w_down)


def _combine_kernel(pos_ref, x_ref, g_ref, yb_hbm, o_ref, buf, sem, *, tm):
    i, n_i = pl.program_id(0), pl.num_programs(0)

    def issue(tile, slot):
        def body(j, carry):
            for k in range(TOP_K):
                p = pos_ref[(tile * tm + j) * TOP_K + k]
                pltpu.make_async_copy(yb_hbm.at[pl.ds(p, 1), :], buf.at[slot, k, pl.ds(j, 1), :],
                                      sem.at[slot]).start()
            return carry

        lax.fori_loop(0, tm, body, 0)

    @pl.when(i == 0)
    def _():
        issue(0, 0)

    @pl.when(i + 1 < n_i)
    def _():
        issue(i + 1, (i + 1) % 2)

    slot = i % 2
    for k in range(TOP_K):
        pltpu.make_async_copy(yb_hbm.at[pl.ds(0, tm), :], buf.at[slot, k], sem.at[slot]).wait()
    o_ref[...] = x_ref[...] + g_ref[...] * (buf[slot, 0] + buf[slot, 1])


def _combine(x, g, yb, pos):
    m, d = x.shape
    tm = MOE_BLOCK
    grid_spec = pltpu.PrefetchScalarGridSpec(
        num_scalar_prefetch=1,
        grid=(m // tm,),
        in_specs=[pl.BlockSpec((tm, d), lambda i, pos: (i, 0)),
                  _vec_spec(g, d, _zero),
                  pl.BlockSpec(memory_space=pl.ANY)],
        out_specs=pl.BlockSpec((tm, d), lambda i, pos: (i, 0)),
        scratch_shapes=[pltpu.VMEM((2, TOP_K, tm, d), F32), pltpu.SemaphoreType.DMA((2,))],
    )
    return pl.pallas_call(
        functools.partial(_combine_kernel, tm=tm),
        grid_spec=grid_spec,
        out_shape=jax.ShapeDtypeStruct((m, d), F32),
        compiler_params=_params("arbitrary"),
        name="moe_combine",
    )(pos, x, g[0], yb)


def _dispatch(expert, gates):
    n_tok = expert.shape[0]
    n = n_tok * TOP_K
    n_blocks = -(-n // MOE_BLOCK) + N_EXPERTS
    n_rows = n_blocks * MOE_BLOCK
    flat_e = expert.reshape(-1)
    onehot = (flat_e[:, None] == jnp.arange(N_EXPERTS, dtype=jnp.int32)[None, :]).astype(jnp.int32)
    csum = jnp.cumsum(onehot, axis=0)
    rank = jnp.take_along_axis(csum, flat_e[:, None], axis=1)[:, 0] - 1
    counts = csum[-1]
    padded = (counts + MOE_BLOCK - 1) // MOE_BLOCK * MOE_BLOCK
    pad_end = jnp.cumsum(padded)
    dest = (pad_end - padded)[flat_e] + rank
    pair_tok = jnp.arange(n, dtype=jnp.int32) // TOP_K
    slot_tok = jnp.zeros((n_rows,), jnp.int32).at[dest].set(pair_tok)
    slot_gate = jnp.zeros((n_rows,), F32).at[dest].set(gates.reshape(-1))
    block_expert = jnp.minimum(jnp.searchsorted(pad_end, jnp.arange(n_blocks) * MOE_BLOCK, side='right'),
                               N_EXPERTS - 1).astype(jnp.int32)
    n_act = (pad_end[-1:] // MOE_BLOCK).astype(jnp.int32)
    return dest.astype(jnp.int32), slot_tok, slot_gate, block_expert, n_act


def _even_mixer(lat, cx, nw, mod_lat, mod_ctx, e, w_in, conv_w, conv_b, dt_bias, a_log, d_skip, ssd_norm_w,
                q_norm_w, k_norm_w, rpb, w_out, ctx_out):
    w_e = w_in[e]
    w_qkvd = jnp.concatenate([w_e[:, S_DT:], w_e[:, S_ZX:S_DT],
                              jnp.zeros((w_e.shape[0], LANES - 2 * SSD_HEADS), w_e.dtype)], axis=1)[None]

    def project(x, mod, tag):
        sh, sc = mod[0], mod[1]
        zx = _nm_matmul(x, nw, sh, sc, w_in, e, S_ZX, "proj_zx_" + tag)
        qkvd = _nm_matmul(x, nw, sh, sc, w_qkvd, 0, QKVD, "proj_qkvd_" + tag)
        xbc = _xbc_conv(zx, conv_w[e], conv_b[e])
        qn, kn, vb = _qkv_prep(qkvd, q_norm_w[e], k_norm_w[e])
        return zx, xbc, qkvd[:, 3 * NA_D:3 * NA_D + 2 * SSD_HEADS], qn, kn, vb

    zx_c, xbc_c, dt_c, qn_c, kn_c, vb_c = project(cx, mod_ctx, "ctx")
    zx_l, xbc_l, dt_l, qn_l, kn_l, vb_l = project(lat, mod_lat, "lat")
    h0 = jnp.zeros((2, SSD_PAIRS, SSD_STATE, LANES), F32)
    y_c, h_ctx = _ssd(xbc_c, dt_c, dt_bias[e], a_log[e], h0)
    y_l, _ = _ssd(xbc_l, dt_l, dt_bias[e], a_log[e], h_ctx)
    yssd_l = _ssd_out(y_l, xbc_l, zx_l, d_skip[e], ssd_norm_w[e])
    yatt_l = _natten(qn_l, kn_l, vb_l, kn_c, vb_c, rpb[e])
    lat = _matmul_res([yssd_l, yatt_l], w_out, e, lat, mod_lat[2], "out_even_lat")
    if ctx_out:
        yssd_c = _ssd_out(y_c, xbc_c, zx_c, d_skip[e], ssd_norm_w[e])
        yatt_c = _ctx_attn(qn_c, kn_c, vb_c)
        cx = _matmul_res([yssd_c, yatt_c], w_out, e, cx, mod_ctx[2], "out_even_ctx")
    return lat, cx


def _odd_mixer(lat, cx, nw, mod_lat, mod_ctx, o, w_in, conv_w, w_out, ctx_out):
    def mix(x, mod, tag):
        h3 = _nm_matmul(x, nw, mod[0], mod[1], w_in, o, w_in.shape[2], "proj_odd_" + tag)
        return _matmul_res([_gate_conv(h3, conv_w[o])], w_out, o, x, mod[2], "out_odd_" + tag)

    lat = mix(lat, mod_lat, "lat")
    if ctx_out:
        cx = mix(cx, mod_ctx, "ctx")
    return lat, cx


def kernel(x, c, ctx, c_ctx, w_mod, b_mod, norm_mix_w, norm_ffn_w, ev_w_in, ev_conv_w, ev_conv_b, ev_dt_bias,
           ev_a_log, ev_d_skip, ev_ssd_norm_w, ev_q_norm_w, ev_k_norm_w, ev_rpb, ev_w_out, od_w_in, od_conv_w,
           od_w_out, moe_w_group, moe_w_router, moe_w_gate, moe_w_up, moe_w_down):
    bsz, n_lat, d = x.shape
    assert bsz == 1 and ctx.shape[0] == 1
    assert n_lat % GRID_W == 0 and n_lat // GRID_W >= NA_WIN_ROWS and n_lat % SSD_CHUNK == 0
    assert ctx.shape[1] % SSD_CHUNK == 0 and d % LANES == 0
    depth = w_mod.shape[0]
    lat, cx = x[0], ctx[0]
    mods = _modulation(jnp.stack([c[0], c_ctx], axis=1), w_mod, b_mod)
    mods = mods.reshape(depth * 2 * 6, d)

    for l in range(depth):
        even = l % 2 == 0
        ctx_out = any(j % 2 == 0 for j in range(l + 1, depth))
        mod_lat = [_vec(mods, (l * 2 + 0) * 6 + i) for i in range(6)]
        mod_ctx = [_vec(mods, (l * 2 + 1) * 6 + i) for i in range(6)]
        nw = _vec(norm_mix_w, l)
        if even:
            lat, cx = _even_mixer(lat, cx, nw, mod_lat, mod_ctx, l // 2, ev_w_in, ev_conv_w, ev_conv_b,
                                  ev_dt_bias, ev_a_log, ev_d_skip, ev_ssd_norm_w, ev_q_norm_w,
                                  ev_k_norm_w, ev_rpb, ev_w_out, ctx_out)
        else:
            lat, cx = _odd_mixer(lat, cx, nw, mod_lat, mod_ctx, l // 2, od_w_in, od_conv_w, od_w_out, ctx_out)

        nfw = _vec(norm_ffn_w, l)
        w_route = jnp.concatenate([moe_w_group[l], moe_w_router[l],
                                   jnp.zeros((d, LANES - N_GROUPS - N_EXPERTS), F32)], axis=1)
        m_tok, expert, gates = _router(lat, nfw, mod_lat[3], mod_lat[4], w_route)
        if ctx_out:
            m_c, e_c, g_c = _router(cx, nfw, mod_ctx[3], mod_ctx[4], w_route)
            m_tok = jnp.concatenate([m_tok, m_c], axis=0)
            expert = jnp.concatenate([expert, e_c], axis=0)
            gates = jnp.concatenate([gates, g_c], axis=0)
        dest, slot_tok, slot_gate, block_expert, n_act = _dispatch(expert, gates)
        yb = _moe_ffn(m_tok, slot_tok, slot_gate, block_expert, n_act, moe_w_gate, moe_w_up, moe_w_down, l)
        lat = _combine(lat, mod_lat[5], yb, dest[:n_lat * TOP_K])
        if ctx_out:
            cx = _combine(cx, mod_ctx[5], yb, dest[n_lat * TOP_K:])
    return lat[None]
```

```python
import functools

import jax
import jax.numpy as jnp
from jax import lax
from jax.experimental import pallas as pl
from jax.experimental.pallas import tpu as pltpu

F32 = jnp.float32
MXU_DTYPE = jnp.bfloat16
HI = lax.Precision.HIGHEST
EPS = 1e-6
NEG = -1e30

LANES = 128
SUBLANES = 8
VMEM_LIMIT = 48 * 1024 * 1024

SSD_HEADS = 16
SSD_HEAD_DIM = 64
SSD_D_INNER = SSD_HEADS * SSD_HEAD_DIM
SSD_STATE = 128
SSD_GROUPS = 2
SSD_CHUNK = 128
SSD_BC = SSD_GROUPS * SSD_STATE
SSD_XBC = SSD_D_INNER + 2 * SSD_BC
SSD_PAIRS = SSD_HEADS // 2
NA_HEADS = 16
NA_HEAD_DIM = 64
NA_D = NA_HEADS * NA_HEAD_DIM
NA_WIN_ROWS = 8
NA_WIN_COLS = 16
GRID_W = 64
N_GROUPS = 8
EXPERTS_PER_GROUP = 8
N_EXPERTS = N_GROUPS * EXPERTS_PER_GROUP
TOP_K = 2
D_FF_EXPERT = 384
MOE_BLOCK = 128
S_ZX = SSD_D_INNER + SSD_XBC
S_DT = S_ZX + 2 * SSD_HEADS
QKVD = 3 * NA_D + LANES


def _pick(n, prefs):
    for p in prefs:
        if n % p == 0:
            return p
    return n


def _params(*sem):
    return pltpu.CompilerParams(dimension_semantics=sem, vmem_limit_bytes=VMEM_LIMIT)


def _vec(arr, idx):
    return arr.reshape(arr.shape[0], 1, arr.shape[-1]), idx


def _vec_spec(vec, width, col):
    idx = vec[1]
    return pl.BlockSpec((None, 1, width), lambda *g: (idx, 0, col(*g)))


def _mat_spec(layer, block, idx):
    return pl.BlockSpec((None,) + block, lambda *g: (layer,) + idx(*g))


def _zero(*g):
    return 0


def _silu(x):
    return x * (1.0 / (1.0 + jnp.exp(-x)))


def _softplus(x):
    return jnp.maximum(x, 0.0) + jnp.log1p(jnp.exp(-jnp.abs(x)))


def _dot(a, b, precision=None):
    return jnp.dot(a, b, preferred_element_type=F32, precision=precision)


def _dot_nt(a, b):
    return lax.dot_general(a, b, (((1,), (1,)), ((), ())), preferred_element_type=F32)


def _norm_mod(x, nw, sh, sc):
    ms = jnp.mean(x * x, axis=-1, keepdims=True)
    n = x * lax.rsqrt(ms + EPS) * nw
    return n * (1.0 + sc) + sh


def _mod_kernel(ct_ref, w_ref, b_ref, o_ref):
    ct = ct_ref[...]
    s = _silu(ct)
    w = w_ref[...]
    r0 = jnp.sum(s[:, 0:1] * w, axis=0, keepdims=True)
    r1 = jnp.sum(s[:, 1:2] * w, axis=0, keepdims=True)
    o_ref[...] = jnp.concatenate([r0, r1], axis=0) + b_ref[...]


def _modulation(ct, w_mod, b_mod):
    depth, d, n = w_mod.shape
    tn = _pick(n, (512, 256, 128))
    return pl.pallas_call(
        _mod_kernel,
        grid=(depth, n // tn),
        in_specs=[pl.BlockSpec((d, 2), lambda l, j: (0, 0)),
                  pl.BlockSpec((None, d, tn), lambda l, j: (l, 0, j)),
                  pl.BlockSpec((None, 1, tn), lambda l, j: (l, 0, j))],
        out_specs=pl.BlockSpec((None, 2, tn), lambda l, j: (l, 0, j)),
        out_shape=jax.ShapeDtypeStruct((depth, 2, n), F32),
        compiler_params=_params("parallel", "parallel"),
        name="modulation",
    )(ct, w_mod, b_mod.reshape(depth, 1, n))


def _nm_matmul_kernel(x_ref, nw_ref, sh_ref, sc_ref, w_ref, o_ref, a_ref):
    @pl.when(pl.program_id(1) == 0)
    def _():
        a_ref[...] = _norm_mod(x_ref[...], nw_ref[...], sh_ref[...], sc_ref[...]).astype(a_ref.dtype)

    o_ref[...] = _dot(a_ref[...], w_ref[...].astype(a_ref.dtype)).astype(o_ref.dtype)


def _nm_matmul(x, nw, sh, sc, w, layer, n_cols, name):
    m, d = x.shape
    tm = _pick(m, (1024, 512, 256, 128))
    tn = _pick(n_cols, (512, 640, 256, 128))
    return pl.pallas_call(
        _nm_matmul_kernel,
        grid=(m // tm, n_cols // tn),
        in_specs=[pl.BlockSpec((tm, d), lambda i, j: (i, 0)),
                  _vec_spec(nw, d, _zero), _vec_spec(sh, d, _zero), _vec_spec(sc, d, _zero),
                  _mat_spec(layer, (d, tn), lambda i, j: (0, j))],
        out_specs=pl.BlockSpec((tm, tn), lambda i, j: (i, j)),
        out_shape=jax.ShapeDtypeStruct((m, n_cols), F32),
        scratch_shapes=[pltpu.VMEM((tm, d), MXU_DTYPE)],
        compiler_params=_params("parallel", "arbitrary"),
        name=name,
    )(x, nw[0], sh[0], sc[0], w)


def _matmul_res_kernel(*refs, n_a):
    a_refs, w_refs = refs[:n_a], refs[n_a:2 * n_a]
    res_ref, g_ref, o_ref = refs[2 * n_a:]
    acc = None
    for a_ref, w_ref in zip(a_refs, w_refs):
        t = _dot(a_ref[...].astype(MXU_DTYPE), w_ref[...].astype(MXU_DTYPE))
        acc = t if acc is None else acc + t
    o_ref[...] = res_ref[...] + g_ref[...] * acc


def _matmul_res(a_list, w, layer, res, g, name):
    n_a = len(a_list)
    m, n = res.shape
    kp = w.shape[1] // n_a
    tm = _pick(m, (1024, 512, 256, 128))
    tn = _pick(n, (512, 256, 128))
    in_specs = [pl.BlockSpec((tm, kp), lambda i, j: (i, 0)) for _ in range(n_a)]
    in_specs += [_mat_spec(layer, (kp, tn), functools.partial(lambda i, j, p: (p, j), p=p)) for p in range(n_a)]
    in_specs += [pl.BlockSpec((tm, tn), lambda i, j: (i, j)), _vec_spec(g, tn, lambda i, j: j)]
    return pl.pallas_call(
        functools.partial(_matmul_res_kernel, n_a=n_a),
        grid=(m // tm, n // tn),
        in_specs=in_specs,
        out_specs=pl.BlockSpec((tm, tn), lambda i, j: (i, j)),
        out_shape=jax.ShapeDtypeStruct((m, n), F32),
        compiler_params=_params("parallel", "parallel"),
        name=name,
    )(*a_list, *([w] * n_a), res, g[0])


def _shifted(x, prev_ref, next_ref):
    i, n_i = pl.program_id(0), pl.num_programs(0)
    tl = x.shape[0]
    row = lax.broadcasted_iota(jnp.int32, x.shape, 0)
    prev_row = jnp.where(i == 0, 0.0, prev_ref[SUBLANES - 1:SUBLANES, :])
    next_row = jnp.where(i == n_i - 1, 0.0, next_ref[0:1, :])
    up = jnp.where(row == 0, prev_row, pltpu.roll(x, 1, 0))
    down = jnp.where(row == tl - 1, next_row, pltpu.roll(x, tl - 1, 0))
    return up, down


def _halo_specs(tl, tc, n_rows, col_off):
    nb = tl // SUBLANES
    last = n_rows // SUBLANES - 1
    return [pl.BlockSpec((tl, tc), lambda i, j: (i, j + col_off)),
            pl.BlockSpec((SUBLANES, tc), lambda i, j: (jnp.maximum(i * nb - 1, 0), j + col_off)),
            pl.BlockSpec((SUBLANES, tc), lambda i, j: (jnp.minimum((i + 1) * nb, last), j + col_off))]


def _xbc_conv_kernel(x_ref, prev_ref, next_ref, w_ref, b_ref, o_ref):
    x = x_ref[...]
    up, down = _shifted(x, prev_ref, next_ref)
    w = w_ref[...]
    o_ref[...] = _silu(w[0:1] * up + w[1:2] * x + w[2:3] * down + b_ref[...])


def _xbc_conv(zx, conv_w, conv_b):
    n_rows = zx.shape[0]
    tl = _pick(n_rows, (512, 256, 128))
    tc = 512
    off = SSD_D_INNER // tc
    return pl.pallas_call(
        _xbc_conv_kernel,
        grid=(n_rows // tl, SSD_XBC // tc),
        in_specs=_halo_specs(tl, tc, n_rows, off) + [pl.BlockSpec((3, tc), lambda i, j: (0, j)),
                                                     pl.BlockSpec((1, tc), lambda i, j: (0, j))],
        out_specs=pl.BlockSpec((tl, tc), lambda i, j: (i, j)),
        out_shape=jax.ShapeDtypeStruct((n_rows, SSD_XBC), F32),
        compiler_params=_params("parallel", "parallel"),
        name="xbc_conv",
    )(zx, zx, zx, conv_w, conv_b.reshape(1, -1))


def _gate_conv_kernel(bg_ref, cg_ref, cgp_ref, cgn_ref, xv_ref, xvp_ref, xvn_ref, w_ref, o_ref):
    i, n_i = pl.program_id(0), pl.num_programs(0)
    u = cg_ref[...] * xv_ref[...]
    tl = u.shape[0]
    row = lax.broadcasted_iota(jnp.int32, u.shape, 0)
    prev_row = jnp.where(i == 0, 0.0, cgp_ref[SUBLANES - 1:SUBLANES, :] * xvp_ref[SUBLANES - 1:SUBLANES, :])
    next_row = jnp.where(i == n_i - 1, 0.0, cgn_ref[0:1, :] * xvn_ref[0:1, :])
    up = jnp.where(row == 0, prev_row, pltpu.roll(u, 1, 0))
    down = jnp.where(row == tl - 1, next_row, pltpu.roll(u, tl - 1, 0))
    w = w_ref[...]
    o_ref[...] = (bg_ref[...] * (w[0:1] * up + w[1:2] * u + w[2:3] * down)).astype(o_ref.dtype)


def _gate_conv(h3, conv_w):
    n_rows, c3 = h3.shape
    c = c3 // 3
    tl = _pick(n_rows, (512, 256, 128))
    tc = _pick(c, (512, 256, 128))
    nb = c // tc
    return pl.pallas_call(
        _gate_conv_kernel,
        grid=(n_rows // tl, nb),
        in_specs=[pl.BlockSpec((tl, tc), lambda i, j: (i, j))] + _halo_specs(tl, tc, n_rows, nb)
        + _halo_specs(tl, tc, n_rows, 2 * nb) + [pl.BlockSpec((3, tc), lambda i, j: (0, j))],
        out_specs=pl.BlockSpec((tl, tc), lambda i, j: (i, j)),
        out_shape=jax.ShapeDtypeStruct((n_rows, c), MXU_DTYPE),
        compiler_params=_params("parallel", "parallel"),
        name="gate_conv",
    )(h3, h3, h3, h3, h3, h3, h3, conv_w)


def _ssd_kernel(xbc_ref, dt_ref, dtt_ref, dtb_ref, dtbt_ref, alog_ref, alogt_ref, h0_ref,
                y_ref, hout_ref, h_ref):
    d, c, n_c = pl.program_id(0), pl.program_id(1), pl.num_programs(1)
    q = SSD_CHUNK

    @pl.when(c == 0)
    def _():
        h_ref[...] = h0_ref[...]

    dt = _softplus(dt_ref[...] + dtb_ref[...])
    dtt = _softplus(dtt_ref[...] + dtbt_ref[...])
    a = dt * -jnp.exp(alog_ref[...])
    at = dtt * -jnp.exp(alogt_ref[...])
    row = lax.broadcasted_iota(jnp.int32, (q, q), 0)
    col = lax.broadcasted_iota(jnp.int32, (q, q), 1)
    sign = jnp.where(d == 0, 1, -1)
    mask = (row - col) * sign >= 0
    tri = mask.astype(F32)
    tri_t = ((col - row) * sign >= 0).astype(F32)
    cs = _dot(tri, a, HI)
    cst = _dot(at, tri_t, HI)
    tot = jnp.broadcast_to(jnp.sum(a, axis=0, keepdims=True), (SUBLANES, SSD_HEADS))
    expand = (lax.broadcasted_iota(jnp.int32, (SSD_HEADS, SSD_D_INNER), 1) // SSD_HEAD_DIM
              == lax.broadcasted_iota(jnp.int32, (SSD_HEADS, SSD_D_INNER), 0)).astype(F32)
    dt_e = _dot(dt, expand, HI)
    cs_e = _dot(cs, expand, HI)
    tot_e = _dot(tot, expand, HI)[0:1]

    xs = xbc_ref[:, 0:SSD_D_INNER]
    xdt = xs * dt_e
    xdt_m = xdt.astype(MXU_DTYPE)
    xw_m = (xdt * jnp.exp(tot_e - cs_e)).astype(MXU_DTYPE)
    e_e = jnp.exp(cs_e)
    dec_e = jnp.exp(tot_e)
    first = lax.broadcasted_iota(jnp.int32, (q, LANES), 1) < SSD_HEAD_DIM

    ppg = SSD_PAIRS // SSD_GROUPS
    for g in range(SSD_GROUPS):
        bg = xbc_ref[:, SSD_D_INNER + g * SSD_STATE:SSD_D_INNER + (g + 1) * SSD_STATE]
        cg = xbc_ref[:, SSD_D_INNER + SSD_BC + g * SSD_STATE:SSD_D_INNER + SSD_BC + (g + 1) * SSD_STATE]
        bg_m, cg_m = bg.astype(MXU_DTYPE), cg.astype(MXU_DTYPE)
        bgt_m = bg.T.astype(MXU_DTYPE)
        scores = _dot_nt(cg_m, bg_m)
        for pp in range(ppg):
            p = g * ppg + pp
            sl = slice(p * LANES, (p + 1) * LANES)
            ys = []
            for hh in range(2):
                h = 2 * p + hh
                diff = cs[:, h:h + 1] - cst[h:h + 1, :]
                decay = jnp.exp(jnp.where(mask, diff, -jnp.inf))
                ys.append(_dot((scores * decay).astype(MXU_DTYPE), xdt_m[:, sl]))
            hp = h_ref[p]
            y_off = _dot(cg_m, hp.astype(MXU_DTYPE)) * e_e[:, sl]
            y_ref[:, sl] = jnp.where(first, ys[0], ys[1]) + y_off
            h_ref[p] = dec_e[:, sl] * hp + _dot(bgt_m, xw_m[:, sl])

    @pl.when(c == n_c - 1)
    def _():
        hout_ref[...] = h_ref[...]


def _ssd(xbc, dt_raw, dt_bias, a_log, h0):
    n_rows = xbc.shape[0]
    q = SSD_CHUNK
    n_c = n_rows // q
    dt = dt_raw.reshape(n_rows, 2, SSD_HEADS).transpose(1, 0, 2)
    dtt = dt.transpose(0, 2, 1)

    def chunk(d, c):
        return jnp.where(d == 0, c, n_c - 1 - c)

    small = lambda shape: pl.BlockSpec((None,) + shape, lambda d, c: (d, 0, 0))
    return pl.pallas_call(
        _ssd_kernel,
        grid=(2, n_c),
        in_specs=[pl.BlockSpec((q, SSD_XBC), lambda d, c: (chunk(d, c), 0)),
                  pl.BlockSpec((None, q, SSD_HEADS), lambda d, c: (d, chunk(d, c), 0)),
                  pl.BlockSpec((None, SSD_HEADS, q), lambda d, c: (d, 0, chunk(d, c))),
                  small((1, SSD_HEADS)), small((SSD_HEADS, 1)), small((1, SSD_HEADS)), small((SSD_HEADS, 1)),
                  pl.BlockSpec((None, SSD_PAIRS, SSD_STATE, LANES), lambda d, c: (d, 0, 0, 0))],
        out_specs=[pl.BlockSpec((None, q, SSD_D_INNER), lambda d, c: (d, chunk(d, c), 0)),
                   pl.BlockSpec((None, SSD_PAIRS, SSD_STATE, LANES), lambda d, c: (d, 0, 0, 0))],
        out_shape=[jax.ShapeDtypeStruct((2, n_rows, SSD_D_INNER), F32),
                   jax.ShapeDtypeStruct((2, SSD_PAIRS, SSD_STATE, LANES), F32)],
        scratch_shapes=[pltpu.VMEM((SSD_PAIRS, SSD_STATE, LANES), F32)],
        compiler_params=_params("arbitrary", "arbitrary"),
        name="ssd_scan",
    )(xbc, dt, dtt, dt_bias[:, None, :], dt_bias[:, :, None], a_log[:, None, :], a_log[:, :, None], h0)


def _ssd_out_kernel(y_ref, xbc_ref, z_ref, dsk_ref, nw_ref, o_ref):
    y = y_ref[0] + y_ref[1] + dsk_ref[...] * xbc_ref[...]
    g = y * _silu(z_ref[...])
    ms = jnp.mean(g * g, axis=-1, keepdims=True)
    o_ref[...] = (g * lax.rsqrt(ms + EPS) * nw_ref[...]).astype(o_ref.dtype)


def _ssd_out(y2, xbc, zx, d_skip, norm_w):
    n_rows = xbc.shape[0]
    tl = _pick(n_rows, (512, 256, 128))
    w = SSD_D_INNER
    return pl.pallas_call(
        _ssd_out_kernel,
        grid=(n_rows // tl,),
        in_specs=[pl.BlockSpec((2, tl, w), lambda i: (0, i, 0)),
                  pl.BlockSpec((tl, w), lambda i: (i, 0)),
                  pl.BlockSpec((tl, w), lambda i: (i, 0)),
                  pl.BlockSpec((1, w), lambda i: (0, 0)),
                  pl.BlockSpec((1, w), lambda i: (0, 0))],
        out_specs=pl.BlockSpec((tl, w), lambda i: (i, 0)),
        out_shape=jax.ShapeDtypeStruct((n_rows, w), MXU_DTYPE),
        compiler_params=_params("parallel"),
        name="ssd_out",
    )(y2, xbc, zx, jnp.repeat(d_skip, SSD_HEAD_DIM)[None, :], norm_w[None, :])


def _head_norm(x, w, scale):
    blk = (lax.broadcasted_iota(jnp.int32, (LANES, LANES), 0) // NA_HEAD_DIM
           == lax.broadcasted_iota(jnp.int32, (LANES, LANES), 1) // NA_HEAD_DIM).astype(F32)
    ms = _dot(x * x, blk, HI) * (1.0 / NA_HEAD_DIM)
    return x * lax.rsqrt(ms + EPS) * (w * scale)


def _qkv_prep_kernel(q_ref, k_ref, v_ref, qw_ref, kw_ref, qo_ref, ko_ref, vo_ref):
    qo_ref[...] = _head_norm(q_ref[...], qw_ref[...], NA_HEAD_DIM ** -0.5).astype(qo_ref.dtype)
    ko_ref[...] = _head_norm(k_ref[...], kw_ref[...], 1.0).astype(ko_ref.dtype)
    vo_ref[...] = v_ref[...].astype(vo_ref.dtype)


def _qkv_prep(qkvd, q_norm_w, k_norm_w):
    n_rows = qkvd.shape[0]
    tl = _pick(n_rows, (512, 256, 128))
    nb = NA_D // LANES
    tile = lambda off: pl.BlockSpec((tl, LANES), lambda i, j: (i, j + off))
    wspec = pl.BlockSpec((1, LANES), lambda i, j: (0, 0))
    out = jax.ShapeDtypeStruct((n_rows, NA_D), MXU_DTYPE)
    return pl.pallas_call(
        _qkv_prep_kernel,
        grid=(n_rows // tl, nb),
        in_specs=[tile(0), tile(nb), tile(2 * nb), wspec, wspec],
        out_specs=[tile(0), tile(0), tile(0)],
        out_shape=[out, out, out],
        compiler_params=_params("parallel", "parallel"),
        name="qkv_prep",
    )(qkvd, qkvd, qkvd, jnp.tile(q_norm_w, 2)[None, :], jnp.tile(k_norm_w, 2)[None, :])


def _attend(q2, parts):
    first = lax.broadcasted_iota(jnp.int32, q2.shape, 1) < NA_HEAD_DIM
    outs = []
    for head_mask in (first, jnp.logical_not(first)):
        qa = jnp.where(head_mask, q2, jnp.zeros_like(q2))
        scores = []
        for k, _, bias in parts:
            s = _dot_nt(qa, k)
            scores.append(s if bias is None else s + bias)
        m = functools.reduce(jnp.maximum, [jnp.max(s, axis=-1, keepdims=True) for s in scores])
        probs = [jnp.exp(s - m) for s in scores]
        denom = functools.reduce(jnp.add, [jnp.sum(p, axis=-1, keepdims=True) for p in probs])
        acc = functools.reduce(jnp.add, [_dot(p.astype(MXU_DTYPE), v) for p, (_, v, _) in zip(probs, parts)])
        outs.append(acc / denom)
    return jnp.where(first, outs[0], outs[1])


def _natten_kernel(q_ref, k_ref, v_ref, kc_ref, vc_ref, tbl_ref, o_ref, *, rb, rows):
    i = pl.program_id(1)
    kc, vc = kc_ref[...], vc_ref[...]
    n_win = NA_WIN_ROWS * GRID_W

    def body(t, carry):
        r = i * rb + t
        start = jnp.clip(r - NA_WIN_ROWS // 2, 0, rows - NA_WIN_ROWS)
        dr0 = start - r + (NA_WIN_ROWS - 1)
        qs = pl.ds(pl.multiple_of(t * GRID_W, GRID_W), GRID_W)
        ks = pl.ds(pl.multiple_of(start * GRID_W, GRID_W), n_win)
        q2 = q_ref[qs, :]
        kw, vw = k_ref[ks, :], v_ref[ks, :]
        first = lax.broadcasted_iota(jnp.int32, q2.shape, 1) < NA_HEAD_DIM
        outs = []
        for a in range(2):
            qa = jnp.where(first if a == 0 else jnp.logical_not(first), q2, jnp.zeros_like(q2))
            s_loc = _dot_nt(qa, kw) + tbl_ref[a, dr0]
            s_ctx = _dot_nt(qa, kc)
            m = jnp.maximum(jnp.max(s_loc, axis=-1, keepdims=True), jnp.max(s_ctx, axis=-1, keepdims=True))
            p_loc, p_ctx = jnp.exp(s_loc - m), jnp.exp(s_ctx - m)
            denom = jnp.sum(p_loc, axis=-1, keepdims=True) + jnp.sum(p_ctx, axis=-1, keepdims=True)
            acc = _dot(p_loc.astype(MXU_DTYPE), vw) + _dot(p_ctx.astype(MXU_DTYPE), vc)
            outs.append(acc / denom)
        o_ref[qs, :] = jnp.where(first, outs[0], outs[1]).astype(o_ref.dtype)
        return carry

    lax.fori_loop(0, rb, body, 0)


def _bias_table(rpb):
    col = jnp.arange(GRID_W)
    c0 = jnp.clip(col - NA_WIN_COLS // 2, 0, GRID_W - NA_WIN_COLS)
    col_in = (col[None, :] >= c0[:, None]) & (col[None, :] < c0[:, None] + NA_WIN_COLS)
    dc = jnp.clip(col[None, :] - col[:, None], 1 - NA_WIN_COLS, NA_WIN_COLS - 1) + (NA_WIN_COLS - 1)
    t = jnp.where(col_in, rpb.astype(F32)[:, :, dc], NEG)
    dr = jnp.arange(NA_WIN_ROWS)[:, None] + jnp.arange(NA_WIN_ROWS)[None, :]
    t = t[:, dr]
    return t.transpose(0, 1, 3, 2, 4).reshape(NA_HEADS, NA_WIN_ROWS, GRID_W, NA_WIN_ROWS * GRID_W)


def _natten(qn, kn, vb, kc, vc, rpb):
    n_rows = qn.shape[0]
    rows = n_rows // GRID_W
    n_ctx = kc.shape[0]
    rb = _pick(rows, (4, 2, 1))
    n_win = NA_WIN_ROWS * GRID_W
    seq = lambda n: pl.BlockSpec((n, LANES), lambda p, i: (0, p))
    tile = pl.BlockSpec((rb * GRID_W, LANES), lambda p, i: (i, p))
    return pl.pallas_call(
        functools.partial(_natten_kernel, rb=rb, rows=rows),
        grid=(NA_HEADS // 2, rows // rb),
        in_specs=[tile, seq(n_rows), seq(n_rows), seq(n_ctx), seq(n_ctx),
                  pl.BlockSpec((2, NA_WIN_ROWS, GRID_W, n_win), lambda p, i: (p, 0, 0, 0))],
        out_specs=tile,
        out_shape=jax.ShapeDtypeStruct((n_rows, NA_D), MXU_DTYPE),
        compiler_params=_params("parallel", "arbitrary"),
        name="natten",
    )(qn, kn, vb, kc, vc, _bias_table(rpb))


def _ctx_attn_kernel(q_ref, k_ref, v_ref, o_ref):
    o_ref[...] = _attend(q_ref[...], [(k_ref[...], v_ref[...], None)]).astype(o_ref.dtype)


def _ctx_attn(qn, kn, vb):
    n_ctx = qn.shape[0]
    spec = pl.BlockSpec((n_ctx, LANES), lambda p: (0, p))
    return pl.pallas_call(
        _ctx_attn_kernel,
        grid=(NA_HEADS // 2,),
        in_specs=[spec, spec, spec],
        out_specs=spec,
        out_shape=jax.ShapeDtypeStruct((n_ctx, NA_D), MXU_DTYPE),
        compiler_params=_params("parallel"),
        name="ctx_attn",
    )(qn, kn, vb)


def _router_kernel(x_ref, nw_ref, sh_ref, sc_ref, wr_ref, m_ref, e_ref, g_ref):
    m = _norm_mod(x_ref[...], nw_ref[...], sh_ref[...], sc_ref[...])
    m_ref[...] = m.astype(m_ref.dtype)
    logits = _dot(m, wr_ref[...], HI)
    lane = lax.broadcasted_iota(jnp.int32, logits.shape, 1)
    big = jnp.int32(LANES)

    def top(vals):
        v = jnp.max(vals, axis=-1, keepdims=True)
        idx = jnp.min(jnp.where(vals == v, lane, big), axis=-1, keepdims=True)
        return v, idx

    gl = jnp.where(lane < N_GROUPS, logits, -jnp.inf)
    g_max, grp = top(gl)
    p_grp = 1.0 / jnp.sum(jnp.exp(gl - g_max), axis=-1, keepdims=True)
    e_lane = lane - N_GROUPS
    in_grp = (e_lane >= grp * EXPERTS_PER_GROUP) & (e_lane < (grp + 1) * EXPERTS_PER_GROUP)
    el = jnp.where(in_grp, logits, -jnp.inf)
    v1, i1 = top(el)
    v2, i2 = top(jnp.where(lane == i1, -jnp.inf, el))
    t = jnp.exp(v2 - v1)
    g1 = p_grp / (1.0 + t)
    g2 = p_grp * t / (1.0 + t)
    e_ref[...] = jnp.where(lane == 0, i1 - N_GROUPS, jnp.where(lane == 1, i2 - N_GROUPS, 0))
    g_ref[...] = jnp.where(lane == 0, g1, jnp.where(lane == 1, g2, 0.0))


def _router(x, nw, sh, sc, w_route):
    m_rows, d = x.shape
    tm = _pick(m_rows, (512, 256, 128))
    wide = pl.BlockSpec((tm, LANES), lambda i: (i, 0))
    m, e, g = pl.pallas_call(
        _router_kernel,
        grid=(m_rows // tm,),
        in_specs=[pl.BlockSpec((tm, d), lambda i: (i, 0)),
                  _vec_spec(nw, d, _zero), _vec_spec(sh, d, _zero), _vec_spec(sc, d, _zero),
                  pl.BlockSpec((d, LANES), lambda i: (0, 0))],
        out_specs=[pl.BlockSpec((tm, d), lambda i: (i, 0)), wide, wide],
        out_shape=[jax.ShapeDtypeStruct((m_rows, d), F32),
                   jax.ShapeDtypeStruct((m_rows, LANES), jnp.int32),
                   jax.ShapeDtypeStruct((m_rows, LANES), F32)],
        compiler_params=_params("parallel"),
        name="router",
    )(x, nw[0], sh[0], sc[0], w_route)
    return m, e[:, :TOP_K], g


def _moe_kernel(first_ref, nblk_ref, tok_ref, nact_ref, m_hbm, wg_ref, wu_ref, wd_ref, yb_hbm,
                xbuf, ybuf, gsem, osem, wgb_ref, wub_ref, wdb_ref):
    e, n_e = pl.program_id(0), pl.num_programs(0)
    n_act = nact_ref[0]
    n_blocks = yb_hbm.shape[0] // MOE_BLOCK

    def gather_start(blk, slot):
        for r in range(MOE_BLOCK):
            tok = tok_ref[blk * MOE_BLOCK + r]
            pltpu.make_async_copy(m_hbm.at[pl.ds(tok, 1), :], xbuf.at[slot, pl.ds(r, 1), :], gsem.at[slot]).start()

    def gather_wait(slot):
        pltpu.make_async_copy(m_hbm.at[pl.ds(0, MOE_BLOCK), :], xbuf.at[slot], gsem.at[slot]).wait()

    def out_copy(blk, slot):
        rows = pl.ds(pl.multiple_of(blk * MOE_BLOCK, MOE_BLOCK), MOE_BLOCK)
        return pltpu.make_async_copy(ybuf.at[slot], yb_hbm.at[rows, :], osem.at[slot])

    @pl.when(e == 0)
    def _():
        gather_start(0, 0)

    @pl.when(nblk_ref[e] > 0)
    def _():
        wgb_ref[...] = wg_ref[...].astype(wgb_ref.dtype)
        wub_ref[...] = wu_ref[...].astype(wub_ref.dtype)
        wdb_ref[...] = wd_ref[...].astype(wdb_ref.dtype)

    def block(j, carry):
        b = first_ref[e] + j
        slot = b % 2
        gather_wait(slot)

        @pl.when(b >= 2)
        def _():
            out_copy(b - 2, slot).wait()

        gather_start(jnp.minimum(b + 1, n_act - 1), 1 - slot)
        x = xbuf[slot].astype(MXU_DTYPE)
        h = _silu(_dot(x, wgb_ref[...])) * _dot(x, wub_ref[...])
        ybuf[slot] = _dot(h.astype(MXU_DTYPE), wdb_ref[...])
        out_copy(b, slot).start()
        return carry

    lax.fori_loop(0, nblk_ref[e], block, 0)

    @pl.when(e == n_e - 1)
    def _():
        gather_wait(n_act % 2)
        for back in (1, 2):
            @pl.when(n_act >= back)
            def _():
                out_copy(n_act - back, (n_act - back) % 2).wait()

        ybuf[0] = jnp.zeros(ybuf.shape[1:], ybuf.dtype)

        def fill(b, carry):
            out_copy(b, 0).start()
            out_copy(b, 0).wait()
            return carry

        lax.fori_loop(n_act, n_blocks, fill, 0)


def _moe_ffn(m_tok, slot_tok, first_blk, n_blk, n_act, w_gate, w_up, w_down, layer):
    n_rows = slot_tok.shape[0]
    d = m_tok.shape[1]
    f = w_gate.shape[-1]
    wspec = lambda r, c: pl.BlockSpec((None, None, r, c), lambda e, *_: (layer, e, 0, 0))
    grid_spec = pltpu.PrefetchScalarGridSpec(
        num_scalar_prefetch=4,
        grid=(N_EXPERTS,),
        in_specs=[pl.BlockSpec(memory_space=pl.ANY), wspec(d, f), wspec(d, f), wspec(f, d)],
        out_specs=pl.BlockSpec(memory_space=pl.ANY),
        scratch_shapes=[pltpu.VMEM((2, MOE_BLOCK, d), F32), pltpu.VMEM((2, MOE_BLOCK, d), F32),
                        pltpu.SemaphoreType.DMA((2,)), pltpu.SemaphoreType.DMA((2,)),
                        pltpu.VMEM((d, f), MXU_DTYPE), pltpu.VMEM((d, f), MXU_DTYPE), pltpu.VMEM((f, d), MXU_DTYPE)],
    )
    return pl.pallas_call(
        _moe_kernel,
        grid_spec=grid_spec,
        out_shape=jax.ShapeDtypeStruct((n_rows, d), F32),
        compiler_params=_params("arbitrary"),
        name="moe_ffn",
    )(first_blk, n_blk, slot_tok, n_act, m_tok, w_gate, w_up, w_down)


def _combine_kernel(pos_ref, x_ref, g_ref, gt_ref, yb_hbm, o_ref, buf, sem, *, tm):
    i, n_i = pl.program_id(0), pl.num_programs(0)

    def gather_start(tile, slot):
        for j in range(tm):
            for k in range(TOP_K):
                p = pos_ref[(tile * tm + j) * TOP_K + k]
                pltpu.make_async_copy(yb_hbm.at[pl.ds(p, 1), :], buf.at[slot, k, pl.ds(j, 1), :],
                                      sem.at[slot]).start()

    @pl.when(i == 0)
    def _():
        gather_start(0, 0)

    slot = i % 2
    for k in range(TOP_K):
        pltpu.make_async_copy(yb_hbm.at[pl.ds(0, tm), :], buf.at[slot, k], sem.at[slot]).wait()
    gather_start(jnp.minimum(i + 1, n_i - 1), 1 - slot)
    gt = gt_ref[...]
    o_ref[...] = x_ref[...] + g_ref[...] * (gt[:, 0:1] * buf[slot, 0] + gt[:, 1:2] * buf[slot, 1])

    @pl.when(i == n_i - 1)
    def _():
        for k in range(TOP_K):
            pltpu.make_async_copy(yb_hbm.at[pl.ds(0, tm), :], buf.at[1 - slot, k], sem.at[1 - slot]).wait()


def _combine(x, g, gates, yb, pos):
    m, d = x.shape
    tm = MOE_BLOCK
    grid_spec = pltpu.PrefetchScalarGridSpec(
        num_scalar_prefetch=1,
        grid=(m // tm,),
        in_specs=[pl.BlockSpec((tm, d), lambda i, pos: (i, 0)),
                  _vec_spec(g, d, _zero),
                  pl.BlockSpec((tm, LANES), lambda i, pos: (i, 0)),
                  pl.BlockSpec(memory_space=pl.ANY)],
        out_specs=pl.BlockSpec((tm, d), lambda i, pos: (i, 0)),
        scratch_shapes=[pltpu.VMEM((2, TOP_K, tm, d), F32), pltpu.SemaphoreType.DMA((2,))],
    )
    return pl.pallas_call(
        functools.partial(_combine_kernel, tm=tm),
        grid_spec=grid_spec,
        out_shape=jax.ShapeDtypeStruct((m, d), F32),
        compiler_params=_params("arbitrary"),
        name="moe_combine",
    )(pos, x, g[0], gates, yb)


def _dispatch(expert):
    n_tok = expert.shape[0]
    n = n_tok * TOP_K
    n_blocks = -(-n // MOE_BLOCK) + N_EXPERTS
    n_rows = n_blocks * MOE_BLOCK
    flat_e = expert.reshape(-1)
    onehot = (flat_e[:, None] == jnp.arange(N_EXPERTS, dtype=jnp.int32)[None, :]).astype(jnp.int32)
    csum = jnp.cumsum(onehot, axis=0)
    rank = jnp.take_along_axis(csum, flat_e[:, None], axis=1)[:, 0] - 1
    counts = csum[-1]
    padded = (counts + MOE_BLOCK - 1) // MOE_BLOCK * MOE_BLOCK
    pad_end = jnp.cumsum(padded)
    dest = (pad_end - padded)[flat_e] + rank
    pair_tok = jnp.arange(n, dtype=jnp.int32) // TOP_K
    slot_tok = jnp.zeros((n_rows,), jnp.int32).at[dest].set(pair_tok)
    first_blk = ((pad_end - padded) // MOE_BLOCK).astype(jnp.int32)
    n_blk = (padded // MOE_BLOCK).astype(jnp.int32)
    n_act = (pad_end[-1:] // MOE_BLOCK).astype(jnp.int32)
    return dest.astype(jnp.int32), slot_tok, first_blk, n_blk, n_act


def _even_mixer(lat, cx, nw, mod_lat, mod_ctx, e, w_in, conv_w, conv_b, dt_bias, a_log, d_skip, ssd_norm_w,
                q_norm_w, k_norm_w, rpb, w_out, ctx_out):
    w_e = w_in[e]
    w_qkvd = jnp.concatenate([w_e[:, S_DT:], w_e[:, S_ZX:S_DT],
                              jnp.zeros((w_e.shape[0], LANES - 2 * SSD_HEADS), w_e.dtype)], axis=1)[None]

    def project(x, mod, tag):
        sh, sc = mod[0], mod[1]
        zx = _nm_matmul(x, nw, sh, sc, w_in, e, S_ZX, "proj_zx_" + tag)
        qkvd = _nm_matmul(x, nw, sh, sc, w_qkvd, 0, QKVD, "proj_qkvd_" + tag)
        xbc = _xbc_conv(zx, conv_w[e], conv_b[e])
        qn, kn, vb = _qkv_prep(qkvd, q_norm_w[e], k_norm_w[e])
        return zx, xbc, qkvd[:, 3 * NA_D:3 * NA_D + 2 * SSD_HEADS], qn, kn, vb

    zx_c, xbc_c, dt_c, qn_c, kn_c, vb_c = project(cx, mod_ctx, "ctx")
    zx_l, xbc_l, dt_l, qn_l, kn_l, vb_l = project(lat, mod_lat, "lat")
    h0 = jnp.zeros((2, SSD_PAIRS, SSD_STATE, LANES), F32)
    y_c, h_ctx = _ssd(xbc_c, dt_c, dt_bias[e], a_log[e], h0)
    y_l, _ = _ssd(xbc_l, dt_l, dt_bias[e], a_log[e], h_ctx)
    yssd_l = _ssd_out(y_l, xbc_l, zx_l, d_skip[e], ssd_norm_w[e])
    yatt_l = _natten(qn_l, kn_l, vb_l, kn_c, vb_c, rpb[e])
    lat = _matmul_res([yssd_l, yatt_l], w_out, e, lat, mod_lat[2], "out_even_lat")
    if ctx_out:
        yssd_c = _ssd_out(y_c, xbc_c, zx_c, d_skip[e], ssd_norm_w[e])
        yatt_c = _ctx_attn(qn_c, kn_c, vb_c)
        cx = _matmul_res([yssd_c, yatt_c], w_out, e, cx, mod_ctx[2], "out_even_ctx")
    return lat, cx


def _odd_mixer(lat, cx, nw, mod_lat, mod_ctx, o, w_in, conv_w, w_out, ctx_out):
    def mix(x, mod, tag):
        h3 = _nm_matmul(x, nw, mod[0], mod[1], w_in, o, w_in.shape[2], "proj_odd_" + tag)
        return _matmul_res([_gate_conv(h3, conv_w[o])], w_out, o, x, mod[2], "out_odd_" + tag)

    lat = mix(lat, mod_lat, "lat")
    if ctx_out:
        cx = mix(cx, mod_ctx, "ctx")
    return lat, cx


def kernel(x, c, ctx, c_ctx, w_mod, b_mod, norm_mix_w, norm_ffn_w, ev_w_in, ev_conv_w, ev_conv_b, ev_dt_bias,
           ev_a_log, ev_d_skip, ev_ssd_norm_w, ev_q_norm_w, ev_k_norm_w, ev_rpb, ev_w_out, od_w_in, od_conv_w,
           od_w_out, moe_w_group, moe_w_router, moe_w_gate, moe_w_up, moe_w_down):
    bsz, n_lat, d = x.shape
    assert bsz == 1 and ctx.shape[0] == 1
    assert n_lat % GRID_W == 0 and n_lat // GRID_W >= NA_WIN_ROWS and n_lat % SSD_CHUNK == 0
    assert ctx.shape[1] % SSD_CHUNK == 0 and d % LANES == 0
    depth = w_mod.shape[0]
    lat, cx = x[0], ctx[0]
    mods = _modulation(jnp.stack([c[0], c_ctx], axis=1), w_mod, b_mod)
    mods = mods.reshape(depth * 2 * 6, d)

    for l in range(depth):
        even = l % 2 == 0
        ctx_out = any(j % 2 == 0 for j in range(l + 1, depth))
        mod_lat = [_vec(mods, (l * 2 + 0) * 6 + i) for i in range(6)]
        mod_ctx = [_vec(mods, (l * 2 + 1) * 6 + i) for i in range(6)]
        nw = _vec(norm_mix_w, l)
        if even:
            lat, cx = _even_mixer(lat, cx, nw, mod_lat, mod_ctx, l // 2, ev_w_in, ev_conv_w, ev_conv_b,
                                  ev_dt_bias, ev_a_log, ev_d_skip, ev_ssd_norm_w, ev_q_norm_w,
                                  ev_k_norm_w, ev_rpb, ev_w_out, ctx_out)
        else:
            lat, cx = _odd_mixer(lat, cx, nw, mod_lat, mod_ctx, l // 2, od_w_in, od_conv_w, od_w_out, ctx_out)

        nfw = _vec(norm_ffn_w, l)
        w_route = jnp.concatenate([moe_w_group[l], moe_w_router[l],
                                   jnp.zeros((d, LANES - N_GROUPS - N_EXPERTS), F32)], axis=1)
        m_tok, expert, gates = _router(lat, nfw, mod_lat[3], mod_lat[4], w_route)
        if ctx_out:
            m_c, e_c, g_c = _router(cx, nfw, mod_ctx[3], mod_ctx[4], w_route)
            m_tok = jnp.concatenate([m_tok, m_c], axis=0)
            expert = jnp.concatenate([expert, e_c], axis=0)
            gates = jnp.concatenate([gates, g_c], axis=0)
        dest, slot_tok, first_blk, n_blk, n_act = _dispatch(expert)
        yb = _moe_ffn(m_tok, slot_tok, first_blk, n_blk, n_act, moe_w_gate, moe_w_up, moe_w_down, l)
        lat = _combine(lat, mod_lat[5], gates[:n_lat], yb, dest[:n_lat * TOP_K])
        if ctx_out:
            cx = _combine(cx, mod_ctx[5], gates[n_lat:], yb, dest[n_lat * TOP_K:])
    return lat[None]
```

```python
import functools

import jax
import jax.numpy as jnp
from jax import lax
from jax.experimental import pallas as pl
from jax.experimental.pallas import tpu as pltpu

F32 = jnp.float32
MXU_DTYPE = jnp.bfloat16
HI = lax.Precision.HIGHEST
EPS = 1e-6
NEG = -1e30

LANES = 128
SUBLANES = 8
VMEM_LIMIT = 48 * 1024 * 1024
ROW_DMA_PRIORITY = 1

SSD_HEADS = 16
SSD_HEAD_DIM = 64
SSD_D_INNER = SSD_HEADS * SSD_HEAD_DIM
SSD_STATE = 128
SSD_GROUPS = 2
SSD_CHUNK = 128
SSD_BC = SSD_GROUPS * SSD_STATE
SSD_XBC = SSD_D_INNER + 2 * SSD_BC
SSD_PAIRS = SSD_HEADS // 2
NA_HEADS = 16
NA_HEAD_DIM = 64
NA_D = NA_HEADS * NA_HEAD_DIM
NA_WIN_ROWS = 8
NA_WIN_COLS = 16
GRID_W = 64
N_GROUPS = 8
EXPERTS_PER_GROUP = 8
N_EXPERTS = N_GROUPS * EXPERTS_PER_GROUP
TOP_K = 2
D_FF_EXPERT = 384
MOE_BLOCK = 128
S_ZX = SSD_D_INNER + SSD_XBC
S_DT = S_ZX + 2 * SSD_HEADS
QKVD = 3 * NA_D + LANES


def _pick(n, prefs):
    for p in prefs:
        if n % p == 0:
            return p
    return n


def _params(*sem):
    return pltpu.CompilerParams(dimension_semantics=sem, vmem_limit_bytes=VMEM_LIMIT)


def _vec(arr, idx):
    return arr.reshape(arr.shape[0], 1, arr.shape[-1]), idx


def _vec_spec(vec, width, col):
    idx = vec[1]
    return pl.BlockSpec((None, 1, width), lambda *g: (idx, 0, col(*g)))


def _mat_spec(layer, block, idx):
    return pl.BlockSpec((None,) + block, lambda *g: (layer,) + idx(*g))


def _zero(*g):
    return 0


def _silu(x):
    return x * (1.0 / (1.0 + jnp.exp(-x)))


def _softplus(x):
    return jnp.maximum(x, 0.0) + jnp.log1p(jnp.exp(-jnp.abs(x)))


def _dot(a, b, precision=None):
    return jnp.dot(a, b, preferred_element_type=F32, precision=precision)


def _dot_nt(a, b):
    return lax.dot_general(a, b, (((1,), (1,)), ((), ())), preferred_element_type=F32)


def _norm_mod(x, nw, sh, sc):
    ms = jnp.mean(x * x, axis=-1, keepdims=True)
    n = x * lax.rsqrt(ms + EPS) * nw
    return n * (1.0 + sc) + sh


def _mod_kernel(ct_ref, w_ref, b_ref, o_ref):
    ct = ct_ref[...]
    s = _silu(ct)
    w = w_ref[...]
    r0 = jnp.sum(s[:, 0:1] * w, axis=0, keepdims=True)
    r1 = jnp.sum(s[:, 1:2] * w, axis=0, keepdims=True)
    o_ref[...] = jnp.concatenate([r0, r1], axis=0) + b_ref[...]


def _modulation(ct, w_mod, b_mod):
    depth, d, n = w_mod.shape
    tn = _pick(n, (512, 256, 128))
    return pl.pallas_call(
        _mod_kernel,
        grid=(depth, n // tn),
        in_specs=[pl.BlockSpec((d, 2), lambda l, j: (0, 0)),
                  pl.BlockSpec((None, d, tn), lambda l, j: (l, 0, j)),
                  pl.BlockSpec((None, 1, tn), lambda l, j: (l, 0, j))],
        out_specs=pl.BlockSpec((None, 2, tn), lambda l, j: (l, 0, j)),
        out_shape=jax.ShapeDtypeStruct((depth, 2, n), F32),
        compiler_params=_params("parallel", "parallel"),
        name="modulation",
    )(ct, w_mod, b_mod.reshape(depth, 1, n))


def _nm_matmul_kernel(x_ref, nw_ref, sh_ref, sc_ref, w_ref, o_ref, a_ref):
    @pl.when(pl.program_id(1) == 0)
    def _():
        a_ref[...] = _norm_mod(x_ref[...], nw_ref[...], sh_ref[...], sc_ref[...]).astype(a_ref.dtype)

    o_ref[...] = _dot(a_ref[...], w_ref[...].astype(a_ref.dtype)).astype(o_ref.dtype)


def _nm_matmul(x, nw, sh, sc, w, layer, n_cols, name):
    m, d = x.shape
    tm = _pick(m, (1024, 512, 256, 128))
    tn = _pick(n_cols, (512, 640, 256, 128))
    return pl.pallas_call(
        _nm_matmul_kernel,
        grid=(m // tm, n_cols // tn),
        in_specs=[pl.BlockSpec((tm, d), lambda i, j: (i, 0)),
                  _vec_spec(nw, d, _zero), _vec_spec(sh, d, _zero), _vec_spec(sc, d, _zero),
                  _mat_spec(layer, (d, tn), lambda i, j: (0, j))],
        out_specs=pl.BlockSpec((tm, tn), lambda i, j: (i, j)),
        out_shape=jax.ShapeDtypeStruct((m, n_cols), F32),
        scratch_shapes=[pltpu.VMEM((tm, d), MXU_DTYPE)],
        compiler_params=_params("parallel", "arbitrary"),
        name=name,
    )(x, nw[0], sh[0], sc[0], w)


def _matmul_res_kernel(*refs, n_a):
    a_refs, w_refs = refs[:n_a], refs[n_a:2 * n_a]
    res_ref, g_ref, o_ref = refs[2 * n_a:]
    acc = None
    for a_ref, w_ref in zip(a_refs, w_refs):
        t = _dot(a_ref[...].astype(MXU_DTYPE), w_ref[...].astype(MXU_DTYPE))
        acc = t if acc is None else acc + t
    o_ref[...] = res_ref[...] + g_ref[...] * acc


def _matmul_res(a_list, w, layer, res, g, name):
    n_a = len(a_list)
    m, n = res.shape
    kp = w.shape[1] // n_a
    tm = _pick(m, (1024, 512, 256, 128))
    tn = _pick(n, (512, 256, 128))
    in_specs = [pl.BlockSpec((tm, kp), lambda i, j: (i, 0)) for _ in range(n_a)]
    in_specs += [_mat_spec(layer, (kp, tn), functools.partial(lambda i, j, p: (p, j), p=p)) for p in range(n_a)]
    in_specs += [pl.BlockSpec((tm, tn), lambda i, j: (i, j)), _vec_spec(g, tn, lambda i, j: j)]
    return pl.pallas_call(
        functools.partial(_matmul_res_kernel, n_a=n_a),
        grid=(m // tm, n // tn),
        in_specs=in_specs,
        out_specs=pl.BlockSpec((tm, tn), lambda i, j: (i, j)),
        out_shape=jax.ShapeDtypeStruct((m, n), F32),
        compiler_params=_params("parallel", "parallel"),
        name=name,
    )(*a_list, *([w] * n_a), res, g[0])


def _shifted(x, prev_ref, next_ref):
    i, n_i = pl.program_id(0), pl.num_programs(0)
    tl = x.shape[0]
    row = lax.broadcasted_iota(jnp.int32, x.shape, 0)
    prev_row = jnp.where(i == 0, 0.0, prev_ref[SUBLANES - 1:SUBLANES, :])
    next_row = jnp.where(i == n_i - 1, 0.0, next_ref[0:1, :])
    up = jnp.where(row == 0, prev_row, pltpu.roll(x, 1, 0))
    down = jnp.where(row == tl - 1, next_row, pltpu.roll(x, tl - 1, 0))
    return up, down


def _halo_specs(tl, tc, n_rows, col_off):
    nb = tl // SUBLANES
    last = n_rows // SUBLANES - 1
    return [pl.BlockSpec((tl, tc), lambda i, j: (i, j + col_off)),
            pl.BlockSpec((SUBLANES, tc), lambda i, j: (jnp.maximum(i * nb - 1, 0), j + col_off)),
            pl.BlockSpec((SUBLANES, tc), lambda i, j: (jnp.minimum((i + 1) * nb, last), j + col_off))]


def _xbc_conv_kernel(x_ref, prev_ref, next_ref, w_ref, b_ref, o_ref):
    x = x_ref[...]
    up, down = _shifted(x, prev_ref, next_ref)
    w = w_ref[...]
    o_ref[...] = _silu(w[0:1] * up + w[1:2] * x + w[2:3] * down + b_ref[...])


def _xbc_conv(zx, conv_w, conv_b):
    n_rows = zx.shape[0]
    tl = _pick(n_rows, (512, 256, 128))
    tc = 512
    off = SSD_D_INNER // tc
    return pl.pallas_call(
        _xbc_conv_kernel,
        grid=(n_rows // tl, SSD_XBC // tc),
        in_specs=_halo_specs(tl, tc, n_rows, off) + [pl.BlockSpec((3, tc), lambda i, j: (0, j)),
                                                     pl.BlockSpec((1, tc), lambda i, j: (0, j))],
        out_specs=pl.BlockSpec((tl, tc), lambda i, j: (i, j)),
        out_shape=jax.ShapeDtypeStruct((n_rows, SSD_XBC), F32),
        compiler_params=_params("parallel", "parallel"),
        name="xbc_conv",
    )(zx, zx, zx, conv_w, conv_b.reshape(1, -1))


def _gate_conv_kernel(bg_ref, cg_ref, cgp_ref, cgn_ref, xv_ref, xvp_ref, xvn_ref, w_ref, o_ref):
    i, n_i = pl.program_id(0), pl.num_programs(0)
    u = cg_ref[...] * xv_ref[...]
    tl = u.shape[0]
    row = lax.broadcasted_iota(jnp.int32, u.shape, 0)
    prev_row = jnp.where(i == 0, 0.0, cgp_ref[SUBLANES - 1:SUBLANES, :] * xvp_ref[SUBLANES - 1:SUBLANES, :])
    next_row = jnp.where(i == n_i - 1, 0.0, cgn_ref[0:1, :] * xvn_ref[0:1, :])
    up = jnp.where(row == 0, prev_row, pltpu.roll(u, 1, 0))
    down = jnp.where(row == tl - 1, next_row, pltpu.roll(u, tl - 1, 0))
    w = w_ref[...]
    o_ref[...] = (bg_ref[...] * (w[0:1] * up + w[1:2] * u + w[2:3] * down)).astype(o_ref.dtype)


def _gate_conv(h3, conv_w):
    n_rows, c3 = h3.shape
    c = c3 // 3
    tl = _pick(n_rows, (512, 256, 128))
    tc = _pick(c, (512, 256, 128))
    nb = c // tc
    return pl.pallas_call(
        _gate_conv_kernel,
        grid=(n_rows // tl, nb),
        in_specs=[pl.BlockSpec((tl, tc), lambda i, j: (i, j))] + _halo_specs(tl, tc, n_rows, nb)
        + _halo_specs(tl, tc, n_rows, 2 * nb) + [pl.BlockSpec((3, tc), lambda i, j: (0, j))],
        out_specs=pl.BlockSpec((tl, tc), lambda i, j: (i, j)),
        out_shape=jax.ShapeDtypeStruct((n_rows, c), MXU_DTYPE),
        compiler_params=_params("parallel", "parallel"),
        name="gate_conv",
    )(h3, h3, h3, h3, h3, h3, h3, conv_w)


def _ssd_kernel(xbc_ref, dt_ref, dtt_ref, dtb_ref, dtbt_ref, alog_ref, alogt_ref, h0_ref,
                y_ref, hout_ref, h_ref):
    d, c, n_c = pl.program_id(0), pl.program_id(1), pl.num_programs(1)
    q = SSD_CHUNK

    @pl.when(c == 0)
    def _():
        h_ref[...] = h0_ref[...]

    dt = _softplus(dt_ref[...] + dtb_ref[...])
    dtt = _softplus(dtt_ref[...] + dtbt_ref[...])
    a = dt * -jnp.exp(alog_ref[...])
    at = dtt * -jnp.exp(alogt_ref[...])
    row = lax.broadcasted_iota(jnp.int32, (q, q), 0)
    col = lax.broadcasted_iota(jnp.int32, (q, q), 1)
    sign = jnp.where(d == 0, 1, -1)
    mask = (row - col) * sign >= 0
    tri = mask.astype(F32)
    tri_t = ((col - row) * sign >= 0).astype(F32)
    cs = _dot(tri, a, HI)
    cst = _dot(at, tri_t, HI)
    tot = jnp.broadcast_to(jnp.sum(a, axis=0, keepdims=True), (SUBLANES, SSD_HEADS))
    expand = (lax.broadcasted_iota(jnp.int32, (SSD_HEADS, SSD_D_INNER), 1) // SSD_HEAD_DIM
              == lax.broadcasted_iota(jnp.int32, (SSD_HEADS, SSD_D_INNER), 0)).astype(F32)
    dt_e = _dot(dt, expand, HI)
    cs_e = _dot(cs, expand, HI)
    tot_e = _dot(tot, expand, HI)[0:1]

    xs = xbc_ref[:, 0:SSD_D_INNER]
    xdt = xs * dt_e
    xdt_m = xdt.astype(MXU_DTYPE)
    xw_m = (xdt * jnp.exp(tot_e - cs_e)).astype(MXU_DTYPE)
    e_e = jnp.exp(cs_e)
    dec_e = jnp.exp(tot_e)
    first = lax.broadcasted_iota(jnp.int32, (q, LANES), 1) < SSD_HEAD_DIM

    ppg = SSD_PAIRS // SSD_GROUPS
    for g in range(SSD_GROUPS):
        bg = xbc_ref[:, SSD_D_INNER + g * SSD_STATE:SSD_D_INNER + (g + 1) * SSD_STATE]
        cg = xbc_ref[:, SSD_D_INNER + SSD_BC + g * SSD_STATE:SSD_D_INNER + SSD_BC + (g + 1) * SSD_STATE]
        bg_m, cg_m = bg.astype(MXU_DTYPE), cg.astype(MXU_DTYPE)
        bgt_m = bg.T.astype(MXU_DTYPE)
        scores = _dot_nt(cg_m, bg_m)
        for pp in range(ppg):
            p = g * ppg + pp
            sl = slice(p * LANES, (p + 1) * LANES)
            ys = []
            for hh in range(2):
                h = 2 * p + hh
                diff = cs[:, h:h + 1] - cst[h:h + 1, :]
                decay = jnp.exp(jnp.where(mask, diff, -jnp.inf))
                ys.append(_dot((scores * decay).astype(MXU_DTYPE), xdt_m[:, sl]))
            hp = h_ref[p]
            y_off = _dot(cg_m, hp.astype(MXU_DTYPE)) * e_e[:, sl]
            y_ref[:, sl] = jnp.where(first, ys[0], ys[1]) + y_off
            h_ref[p] = dec_e[:, sl] * hp + _dot(bgt_m, xw_m[:, sl])

    @pl.when(c == n_c - 1)
    def _():
        hout_ref[...] = h_ref[...]


def _ssd(xbc, dt_raw, dt_bias, a_log, h0):
    n_rows = xbc.shape[0]
    q = SSD_CHUNK
    n_c = n_rows // q
    dt = dt_raw.reshape(n_rows, 2, SSD_HEADS).transpose(1, 0, 2)
    dtt = dt.transpose(0, 2, 1)

    def chunk(d, c):
        return jnp.where(d == 0, c, n_c - 1 - c)

    small = lambda shape: pl.BlockSpec((None,) + shape, lambda d, c: (d, 0, 0))
    return pl.pallas_call(
        _ssd_kernel,
        grid=(2, n_c),
        in_specs=[pl.BlockSpec((q, SSD_XBC), lambda d, c: (chunk(d, c), 0)),
                  pl.BlockSpec((None, q, SSD_HEADS), lambda d, c: (d, chunk(d, c), 0)),
                  pl.BlockSpec((None, SSD_HEADS, q), lambda d, c: (d, 0, chunk(d, c))),
                  small((1, SSD_HEADS)), small((SSD_HEADS, 1)), small((1, SSD_HEADS)), small((SSD_HEADS, 1)),
                  pl.BlockSpec((None, SSD_PAIRS, SSD_STATE, LANES), lambda d, c: (d, 0, 0, 0))],
        out_specs=[pl.BlockSpec((None, q, SSD_D_INNER), lambda d, c: (d, chunk(d, c), 0)),
                   pl.BlockSpec((None, SSD_PAIRS, SSD_STATE, LANES), lambda d, c: (d, 0, 0, 0))],
        out_shape=[jax.ShapeDtypeStruct((2, n_rows, SSD_D_INNER), F32),
                   jax.ShapeDtypeStruct((2, SSD_PAIRS, SSD_STATE, LANES), F32)],
        scratch_shapes=[pltpu.VMEM((SSD_PAIRS, SSD_STATE, LANES), F32)],
        compiler_params=_params("arbitrary", "arbitrary"),
        name="ssd_scan",
    )(xbc, dt, dtt, dt_bias[:, None, :], dt_bias[:, :, None], a_log[:, None, :], a_log[:, :, None], h0)


def _ssd_out_kernel(y_ref, xbc_ref, z_ref, dsk_ref, nw_ref, o_ref):
    y = y_ref[0] + y_ref[1] + dsk_ref[...] * xbc_ref[...]
    g = y * _silu(z_ref[...])
    ms = jnp.mean(g * g, axis=-1, keepdims=True)
    o_ref[...] = (g * lax.rsqrt(ms + EPS) * nw_ref[...]).astype(o_ref.dtype)


def _ssd_out(y2, xbc, zx, d_skip, norm_w):
    n_rows = xbc.shape[0]
    tl = _pick(n_rows, (512, 256, 128))
    w = SSD_D_INNER
    return pl.pallas_call(
        _ssd_out_kernel,
        grid=(n_rows // tl,),
        in_specs=[pl.BlockSpec((2, tl, w), lambda i: (0, i, 0)),
                  pl.BlockSpec((tl, w), lambda i: (i, 0)),
                  pl.BlockSpec((tl, w), lambda i: (i, 0)),
                  pl.BlockSpec((1, w), lambda i: (0, 0)),
                  pl.BlockSpec((1, w), lambda i: (0, 0))],
        out_specs=pl.BlockSpec((tl, w), lambda i: (i, 0)),
        out_shape=jax.ShapeDtypeStruct((n_rows, w), MXU_DTYPE),
        compiler_params=_params("parallel"),
        name="ssd_out",
    )(y2, xbc, zx, jnp.repeat(d_skip, SSD_HEAD_DIM)[None, :], norm_w[None, :])


def _head_norm(x, w, scale):
    blk = (lax.broadcasted_iota(jnp.int32, (LANES, LANES), 0) // NA_HEAD_DIM
           == lax.broadcasted_iota(jnp.int32, (LANES, LANES), 1) // NA_HEAD_DIM).astype(F32)
    ms = _dot(x * x, blk, HI) * (1.0 / NA_HEAD_DIM)
    return x * lax.rsqrt(ms + EPS) * (w * scale)


def _qkv_prep_kernel(q_ref, k_ref, v_ref, qw_ref, kw_ref, qo_ref, ko_ref, vo_ref):
    qo_ref[...] = _head_norm(q_ref[...], qw_ref[...], NA_HEAD_DIM ** -0.5).astype(qo_ref.dtype)
    ko_ref[...] = _head_norm(k_ref[...], kw_ref[...], 1.0).astype(ko_ref.dtype)
    vo_ref[...] = v_ref[...].astype(vo_ref.dtype)


def _qkv_prep(qkvd, q_norm_w, k_norm_w):
    n_rows = qkvd.shape[0]
    tl = _pick(n_rows, (512, 256, 128))
    nb = NA_D // LANES
    tile = lambda off: pl.BlockSpec((tl, LANES), lambda i, j: (i, j + off))
    wspec = pl.BlockSpec((1, LANES), lambda i, j: (0, 0))
    out = jax.ShapeDtypeStruct((n_rows, NA_D), MXU_DTYPE)
    return pl.pallas_call(
        _qkv_prep_kernel,
        grid=(n_rows // tl, nb),
        in_specs=[tile(0), tile(nb), tile(2 * nb), wspec, wspec],
        out_specs=[tile(0), tile(0), tile(0)],
        out_shape=[out, out, out],
        compiler_params=_params("parallel", "parallel"),
        name="qkv_prep",
    )(qkvd, qkvd, qkvd, jnp.tile(q_norm_w, 2)[None, :], jnp.tile(k_norm_w, 2)[None, :])


def _attend(q2, parts):
    first = lax.broadcasted_iota(jnp.int32, q2.shape, 1) < NA_HEAD_DIM
    outs = []
    for head_mask in (first, jnp.logical_not(first)):
        qa = jnp.where(head_mask, q2, jnp.zeros_like(q2))
        scores = []
        for k, _, bias in parts:
            s = _dot_nt(qa, k)
            scores.append(s if bias is None else s + bias)
        m = functools.reduce(jnp.maximum, [jnp.max(s, axis=-1, keepdims=True) for s in scores])
        probs = [jnp.exp(s - m) for s in scores]
        denom = functools.reduce(jnp.add, [jnp.sum(p, axis=-1, keepdims=True) for p in probs])
        acc = functools.reduce(jnp.add, [_dot(p.astype(MXU_DTYPE), v) for p, (_, v, _) in zip(probs, parts)])
        outs.append(acc / denom)
    return jnp.where(first, outs[0], outs[1])


def _natten_kernel(q_ref, k_ref, v_ref, kc_ref, vc_ref, tbl_ref, o_ref, *, rb, rows):
    i = pl.program_id(1)
    kc, vc = kc_ref[...], vc_ref[...]
    n_win = NA_WIN_ROWS * GRID_W
    first = lax.broadcasted_iota(jnp.int32, (GRID_W, LANES), 1) < NA_HEAD_DIM
    for t in range(rb):
        r = i * rb + t
        start = jnp.clip(r - NA_WIN_ROWS // 2, 0, rows - NA_WIN_ROWS)
        dr0 = start - r + (NA_WIN_ROWS - 1)
        ks = pl.ds(pl.multiple_of(start * GRID_W, GRID_W), n_win)
        q2 = q_ref[t * GRID_W:(t + 1) * GRID_W, :]
        kw, vw = k_ref[ks, :], v_ref[ks, :]
        zero = jnp.zeros_like(q2)
        qs = jnp.concatenate([jnp.where(first, q2, zero), jnp.where(first, zero, q2)], axis=0)
        s_loc = _dot_nt(qs, kw) + tbl_ref[dr0]
        s_ctx = _dot_nt(qs, kc)
        m = jnp.maximum(jnp.max(s_loc, axis=-1, keepdims=True), jnp.max(s_ctx, axis=-1, keepdims=True))
        p_loc, p_ctx = jnp.exp(s_loc - m), jnp.exp(s_ctx - m)
        denom = jnp.sum(p_loc, axis=-1, keepdims=True) + jnp.sum(p_ctx, axis=-1, keepdims=True)
        o = (_dot(p_loc.astype(MXU_DTYPE), vw) + _dot(p_ctx.astype(MXU_DTYPE), vc)) / denom
        o_ref[t * GRID_W:(t + 1) * GRID_W, :] = jnp.where(first, o[:GRID_W], o[GRID_W:]).astype(o_ref.dtype)


def _bias_table(rpb):
    col = jnp.arange(GRID_W)
    c0 = jnp.clip(col - NA_WIN_COLS // 2, 0, GRID_W - NA_WIN_COLS)
    col_in = (col[None, :] >= c0[:, None]) & (col[None, :] < c0[:, None] + NA_WIN_COLS)
    dc = jnp.clip(col[None, :] - col[:, None], 1 - NA_WIN_COLS, NA_WIN_COLS - 1) + (NA_WIN_COLS - 1)
    t = jnp.where(col_in, rpb.astype(F32)[:, :, dc], NEG)
    dr = jnp.arange(NA_WIN_ROWS)[:, None] + jnp.arange(NA_WIN_ROWS)[None, :]
    t = t[:, dr]
    t = t.reshape(NA_HEADS // 2, 2, NA_WIN_ROWS, NA_WIN_ROWS, GRID_W, GRID_W)
    return t.transpose(0, 2, 1, 4, 3, 5).reshape(NA_HEADS // 2, NA_WIN_ROWS, 2 * GRID_W, NA_WIN_ROWS * GRID_W)


def _natten(qn, kn, vb, kc, vc, rpb):
    n_rows = qn.shape[0]
    rows = n_rows // GRID_W
    n_ctx = kc.shape[0]
    rb = _pick(rows, (4, 2, 1))
    n_win = NA_WIN_ROWS * GRID_W
    seq = lambda n: pl.BlockSpec((n, LANES), lambda p, i: (0, p))
    tile = pl.BlockSpec((rb * GRID_W, LANES), lambda p, i: (i, p))
    return pl.pallas_call(
        functools.partial(_natten_kernel, rb=rb, rows=rows),
        grid=(NA_HEADS // 2, rows // rb),
        in_specs=[tile, seq(n_rows), seq(n_rows), seq(n_ctx), seq(n_ctx),
                  pl.BlockSpec((None, NA_WIN_ROWS, 2 * GRID_W, n_win), lambda p, i: (p, 0, 0, 0))],
        out_specs=tile,
        out_shape=jax.ShapeDtypeStruct((n_rows, NA_D), MXU_DTYPE),
        compiler_params=_params("parallel", "arbitrary"),
        name="natten",
    )(qn, kn, vb, kc, vc, _bias_table(rpb))


def _ctx_attn_kernel(q_ref, k_ref, v_ref, o_ref):
    o_ref[...] = _attend(q_ref[...], [(k_ref[...], v_ref[...], None)]).astype(o_ref.dtype)


def _ctx_attn(qn, kn, vb):
    n_ctx = qn.shape[0]
    spec = pl.BlockSpec((n_ctx, LANES), lambda p: (0, p))
    return pl.pallas_call(
        _ctx_attn_kernel,
        grid=(NA_HEADS // 2,),
        in_specs=[spec, spec, spec],
        out_specs=spec,
        out_shape=jax.ShapeDtypeStruct((n_ctx, NA_D), MXU_DTYPE),
        compiler_params=_params("parallel"),
        name="ctx_attn",
    )(qn, kn, vb)


def _router_kernel(x_ref, nw_ref, sh_ref, sc_ref, wr_ref, m_ref, e_ref, g_ref):
    m = _norm_mod(x_ref[...], nw_ref[...], sh_ref[...], sc_ref[...])
    m_ref[...] = m.astype(m_ref.dtype)
    logits = _dot(m, wr_ref[...], HI)
    lane = lax.broadcasted_iota(jnp.int32, logits.shape, 1)
    big = jnp.int32(LANES)

    def top(vals):
        v = jnp.max(vals, axis=-1, keepdims=True)
        idx = jnp.min(jnp.where(vals == v, lane, big), axis=-1, keepdims=True)
        return v, idx

    gl = jnp.where(lane < N_GROUPS, logits, -jnp.inf)
    g_max, grp = top(gl)
    p_grp = 1.0 / jnp.sum(jnp.exp(gl - g_max), axis=-1, keepdims=True)
    e_lane = lane - N_GROUPS
    in_grp = (e_lane >= grp * EXPERTS_PER_GROUP) & (e_lane < (grp + 1) * EXPERTS_PER_GROUP)
    el = jnp.where(in_grp, logits, -jnp.inf)
    v1, i1 = top(el)
    v2, i2 = top(jnp.where(lane == i1, -jnp.inf, el))
    t = jnp.exp(v2 - v1)
    g1 = p_grp / (1.0 + t)
    g2 = p_grp * t / (1.0 + t)
    e_ref[...] = jnp.where(lane == 0, i1 - N_GROUPS, jnp.where(lane == 1, i2 - N_GROUPS, 0))
    g_ref[...] = jnp.where(lane == 0, g1, jnp.where(lane == 1, g2, 0.0))


def _router(x, nw, sh, sc, w_route):
    m_rows, d = x.shape
    tm = _pick(m_rows, (512, 256, 128))
    wide = pl.BlockSpec((tm, LANES), lambda i: (i, 0))
    m, e, g = pl.pallas_call(
        _router_kernel,
        grid=(m_rows // tm,),
        in_specs=[pl.BlockSpec((tm, d), lambda i: (i, 0)),
                  _vec_spec(nw, d, _zero), _vec_spec(sh, d, _zero), _vec_spec(sc, d, _zero),
                  pl.BlockSpec((d, LANES), lambda i: (0, 0))],
        out_specs=[pl.BlockSpec((tm, d), lambda i: (i, 0)), wide, wide],
        out_shape=[jax.ShapeDtypeStruct((m_rows, d), F32),
                   jax.ShapeDtypeStruct((m_rows, LANES), jnp.int32),
                   jax.ShapeDtypeStruct((m_rows, LANES), F32)],
        compiler_params=_params("parallel"),
        name="router",
    )(x, nw[0], sh[0], sc[0], w_route)
    return m, e[:, :TOP_K], g


def _moe_kernel(first_ref, nblk_ref, tok_ref, nact_ref, m_hbm, wg_ref, wu_ref, wd_ref, yb_hbm,
                xbuf, ybuf, gsem, osem, wgb_ref, wub_ref, wdb_ref):
    e, n_e = pl.program_id(0), pl.num_programs(0)
    n_act = nact_ref[0]
    n_blocks = yb_hbm.shape[0] // MOE_BLOCK

    def gather_start(blk, slot):
        for r in range(MOE_BLOCK):
            tok = tok_ref[blk * MOE_BLOCK + r]
            pltpu.make_async_copy(m_hbm.at[pl.ds(tok, 1), :], xbuf.at[slot, pl.ds(r, 1), :],
                                  gsem.at[slot]).start(priority=ROW_DMA_PRIORITY)

    def gather_wait(slot):
        pltpu.make_async_copy(m_hbm.at[pl.ds(0, MOE_BLOCK), :], xbuf.at[slot], gsem.at[slot]).wait()

    def out_copy(blk, slot):
        rows = pl.ds(pl.multiple_of(blk * MOE_BLOCK, MOE_BLOCK), MOE_BLOCK)
        return pltpu.make_async_copy(ybuf.at[slot], yb_hbm.at[rows, :], osem.at[slot])

    @pl.when(e == 0)
    def _():
        gather_start(0, 0)

    @pl.when(nblk_ref[e] > 0)
    def _():
        wgb_ref[...] = wg_ref[...].astype(wgb_ref.dtype)
        wub_ref[...] = wu_ref[...].astype(wub_ref.dtype)
        wdb_ref[...] = wd_ref[...].astype(wdb_ref.dtype)

    def block(j, carry):
        b = first_ref[e] + j
        slot = b % 2
        gather_wait(slot)

        @pl.when(b >= 2)
        def _():
            out_copy(b - 2, slot).wait()

        gather_start(jnp.minimum(b + 1, n_act - 1), 1 - slot)
        x = xbuf[slot].astype(MXU_DTYPE)
        h = _silu(_dot(x, wgb_ref[...])) * _dot(x, wub_ref[...])
        ybuf[slot] = _dot(h.astype(MXU_DTYPE), wdb_ref[...])
        out_copy(b, slot).start()
        return carry

    lax.fori_loop(0, nblk_ref[e], block, 0)

    @pl.when(e == n_e - 1)
    def _():
        gather_wait(n_act % 2)
        for back in (1, 2):
            @pl.when(n_act >= back)
            def _():
                out_copy(n_act - back, (n_act - back) % 2).wait()

        ybuf[0] = jnp.zeros(ybuf.shape[1:], ybuf.dtype)

        def fill(b, carry):
            out_copy(b, 0).start()
            out_copy(b, 0).wait()
            return carry

        lax.fori_loop(n_act, n_blocks, fill, 0)


def _moe_ffn(m_tok, slot_tok, first_blk, n_blk, n_act, w_gate, w_up, w_down, layer):
    n_rows = slot_tok.shape[0]
    d = m_tok.shape[1]
    f = w_gate.shape[-1]
    wspec = lambda r, c: pl.BlockSpec((None, None, r, c), lambda e, *_: (layer, e, 0, 0))
    grid_spec = pltpu.PrefetchScalarGridSpec(
        num_scalar_prefetch=4,
        grid=(N_EXPERTS,),
        in_specs=[pl.BlockSpec(memory_space=pl.ANY), wspec(d, f), wspec(d, f), wspec(f, d)],
        out_specs=pl.BlockSpec(memory_space=pl.ANY),
        scratch_shapes=[pltpu.VMEM((2, MOE_BLOCK, d), F32), pltpu.VMEM((2, MOE_BLOCK, d), F32),
                        pltpu.SemaphoreType.DMA((2,)), pltpu.SemaphoreType.DMA((2,)),
                        pltpu.VMEM((d, f), MXU_DTYPE), pltpu.VMEM((d, f), MXU_DTYPE), pltpu.VMEM((f, d), MXU_DTYPE)],
    )
    return pl.pallas_call(
        _moe_kernel,
        grid_spec=grid_spec,
        out_shape=jax.ShapeDtypeStruct((n_rows, d), F32),
        compiler_params=_params("arbitrary"),
        name="moe_ffn",
    )(first_blk, n_blk, slot_tok, n_act, m_tok, w_gate, w_up, w_down)


def _combine_kernel(pos_ref, x_ref, g_ref, gt_ref, yb_hbm, o_ref, buf, sem, *, tm):
    i, n_i = pl.program_id(0), pl.num_programs(0)

    def gather_start(tile, slot):
        for j in range(tm):
            for k in range(TOP_K):
                p = pos_ref[(tile * tm + j) * TOP_K + k]
                pltpu.make_async_copy(yb_hbm.at[pl.ds(p, 1), :], buf.at[slot, k, pl.ds(j, 1), :],
                                      sem.at[slot]).start(priority=ROW_DMA_PRIORITY)

    @pl.when(i == 0)
    def _():
        gather_start(0, 0)

    slot = i % 2
    for k in range(TOP_K):
        pltpu.make_async_copy(yb_hbm.at[pl.ds(0, tm), :], buf.at[slot, k], sem.at[slot]).wait()
    gather_start(jnp.minimum(i + 1, n_i - 1), 1 - slot)
    gt = gt_ref[...]
    o_ref[...] = x_ref[...] + g_ref[...] * (gt[:, 0:1] * buf[slot, 0] + gt[:, 1:2] * buf[slot, 1])

    @pl.when(i == n_i - 1)
    def _():
        for k in range(TOP_K):
            pltpu.make_async_copy(yb_hbm.at[pl.ds(0, tm), :], buf.at[1 - slot, k], sem.at[1 - slot]).wait()


def _combine(x, g, gates, yb, pos):
    m, d = x.shape
    tm = MOE_BLOCK
    grid_spec = pltpu.PrefetchScalarGridSpec(
        num_scalar_prefetch=1,
        grid=(m // tm,),
        in_specs=[pl.BlockSpec((tm, d), lambda i, pos: (i, 0)),
                  _vec_spec(g, d, _zero),
                  pl.BlockSpec((tm, LANES), lambda i, pos: (i, 0)),
                  pl.BlockSpec(memory_space=pl.ANY)],
        out_specs=pl.BlockSpec((tm, d), lambda i, pos: (i, 0)),
        scratch_shapes=[pltpu.VMEM((2, TOP_K, tm, d), F32), pltpu.SemaphoreType.DMA((2,))],
    )
    return pl.pallas_call(
        functools.partial(_combine_kernel, tm=tm),
        grid_spec=grid_spec,
        out_shape=jax.ShapeDtypeStruct((m, d), F32),
        compiler_params=_params("arbitrary"),
        name="moe_combine",
    )(pos, x, g[0], gates, yb)


def _dispatch(expert):
    n_tok = expert.shape[0]
    n = n_tok * TOP_K
    n_blocks = -(-n // MOE_BLOCK) + N_EXPERTS
    n_rows = n_blocks * MOE_BLOCK
    flat_e = expert.reshape(-1)
    onehot = (flat_e[:, None] == jnp.arange(N_EXPERTS, dtype=jnp.int32)[None, :]).astype(jnp.int32)
    csum = jnp.cumsum(onehot, axis=0)
    rank = jnp.take_along_axis(csum, flat_e[:, None], axis=1)[:, 0] - 1
    counts = csum[-1]
    padded = (counts + MOE_BLOCK - 1) // MOE_BLOCK * MOE_BLOCK
    pad_end = jnp.cumsum(padded)
    dest = (pad_end - padded)[flat_e] + rank
    pair_tok = jnp.arange(n, dtype=jnp.int32) // TOP_K
    slot_tok = jnp.zeros((n_rows,), jnp.int32).at[dest].set(pair_tok)
    first_blk = ((pad_end - padded) // MOE_BLOCK).astype(jnp.int32)
    n_blk = (padded // MOE_BLOCK).astype(jnp.int32)
    n_act = (pad_end[-1:] // MOE_BLOCK).astype(jnp.int32)
    return dest.astype(jnp.int32), slot_tok, first_blk, n_blk, n_act


def _even_mixer(lat, cx, nw, mod_lat, mod_ctx, e, w_in, conv_w, conv_b, dt_bias, a_log, d_skip, ssd_norm_w,
                q_norm_w, k_norm_w, rpb, w_out, ctx_out):
    w_e = w_in[e]
    w_qkvd = jnp.concatenate([w_e[:, S_DT:], w_e[:, S_ZX:S_DT],
                              jnp.zeros((w_e.shape[0], LANES - 2 * SSD_HEADS), w_e.dtype)], axis=1)[None]

    def project(x, mod, tag):
        sh, sc = mod[0], mod[1]
        zx = _nm_matmul(x, nw, sh, sc, w_in, e, S_ZX, "proj_zx_" + tag)
        qkvd = _nm_matmul(x, nw, sh, sc, w_qkvd, 0, QKVD, "proj_qkvd_" + tag)
        xbc = _xbc_conv(zx, conv_w[e], conv_b[e])
        qn, kn, vb = _qkv_prep(qkvd, q_norm_w[e], k_norm_w[e])
        return zx, xbc, qkvd[:, 3 * NA_D:3 * NA_D + 2 * SSD_HEADS], qn, kn, vb

    zx_c, xbc_c, dt_c, qn_c, kn_c, vb_c = project(cx, mod_ctx, "ctx")
    zx_l, xbc_l, dt_l, qn_l, kn_l, vb_l = project(lat, mod_lat, "lat")
    h0 = jnp.zeros((2, SSD_PAIRS, SSD_STATE, LANES), F32)
    y_c, h_ctx = _ssd(xbc_c, dt_c, dt_bias[e], a_log[e], h0)
    y_l, _ = _ssd(xbc_l, dt_l, dt_bias[e], a_log[e], h_ctx)
    yssd_l = _ssd_out(y_l, xbc_l, zx_l, d_skip[e], ssd_norm_w[e])
    yatt_l = _natten(qn_l, kn_l, vb_l, kn_c, vb_c, rpb[e])
    lat = _matmul_res([yssd_l, yatt_l], w_out, e, lat, mod_lat[2], "out_even_lat")
    if ctx_out:
        yssd_c = _ssd_out(y_c, xbc_c, zx_c, d_skip[e], ssd_norm_w[e])
        yatt_c = _ctx_attn(qn_c, kn_c, vb_c)
        cx = _matmul_res([yssd_c, yatt_c], w_out, e, cx, mod_ctx[2], "out_even_ctx")
    return lat, cx


def _odd_mixer(lat, cx, nw, mod_lat, mod_ctx, o, w_in, conv_w, w_out, ctx_out):
    def mix(x, mod, tag):
        h3 = _nm_matmul(x, nw, mod[0], mod[1], w_in, o, w_in.shape[2], "proj_odd_" + tag)
        return _matmul_res([_gate_conv(h3, conv_w[o])], w_out, o, x, mod[2], "out_odd_" + tag)

    lat = mix(lat, mod_lat, "lat")
    if ctx_out:
        cx = mix(cx, mod_ctx, "ctx")
    return lat, cx


def kernel(x, c, ctx, c_ctx, w_mod, b_mod, norm_mix_w, norm_ffn_w, ev_w_in, ev_conv_w, ev_conv_b, ev_dt_bias,
           ev_a_log, ev_d_skip, ev_ssd_norm_w, ev_q_norm_w, ev_k_norm_w, ev_rpb, ev_w_out, od_w_in, od_conv_w,
           od_w_out, moe_w_group, moe_w_router, moe_w_gate, moe_w_up, moe_w_down):
    bsz, n_lat, d = x.shape
    assert bsz == 1 and ctx.shape[0] == 1
    assert n_lat % GRID_W == 0 and n_lat // GRID_W >= NA_WIN_ROWS and n_lat % SSD_CHUNK == 0
    assert ctx.shape[1] % SSD_CHUNK == 0 and d % LANES == 0
    depth = w_mod.shape[0]
    lat, cx = x[0], ctx[0]
    mods = _modulation(jnp.stack([c[0], c_ctx], axis=1), w_mod, b_mod)
    mods = mods.reshape(depth * 2 * 6, d)

    for l in range(depth):
        even = l % 2 == 0
        ctx_out = any(j % 2 == 0 for j in range(l + 1, depth))
        mod_lat = [_vec(mods, (l * 2 + 0) * 6 + i) for i in range(6)]
        mod_ctx = [_vec(mods, (l * 2 + 1) * 6 + i) for i in range(6)]
        nw = _vec(norm_mix_w, l)
        if even:
            lat, cx = _even_mixer(lat, cx, nw, mod_lat, mod_ctx, l // 2, ev_w_in, ev_conv_w, ev_conv_b,
                                  ev_dt_bias, ev_a_log, ev_d_skip, ev_ssd_norm_w, ev_q_norm_w,
                                  ev_k_norm_w, ev_rpb, ev_w_out, ctx_out)
        else:
            lat, cx = _odd_mixer(lat, cx, nw, mod_lat, mod_ctx, l // 2, od_w_in, od_conv_w, od_w_out, ctx_out)

        nfw = _vec(norm_ffn_w, l)
        w_route = jnp.concatenate([moe_w_group[l], moe_w_router[l],
                                   jnp.zeros((d, LANES - N_GROUPS - N_EXPERTS), F32)], axis=1)
        m_tok, expert, gates = _router(lat, nfw, mod_lat[3], mod_lat[4], w_route)
        if ctx_out:
            m_c, e_c, g_c = _router(cx, nfw, mod_ctx[3], mod_ctx[4], w_route)
            m_tok = jnp.concatenate([m_tok, m_c], axis=0)
            expert = jnp.concatenate([expert, e_c], axis=0)
            gates = jnp.concatenate([gates, g_c], axis=0)
        dest, slot_tok, first_blk, n_blk, n_act = _dispatch(expert)
        yb = _moe_ffn(m_tok, slot_tok, first_blk, n_blk, n_act, moe_w_gate, moe_w_up, moe_w_down, l)
        lat = _combine(lat, mod_lat[5], gates[:n_lat], yb, dest[:n_lat * TOP_K])
        if ctx_out:
            cx = _combine(cx, mod_ctx[5], gates[n_lat:], yb, dest[n_lat * TOP_K:])
    return lat[None]
```

```python
import functools

import jax
import jax.numpy as jnp
from jax import lax
from jax.experimental import pallas as pl
from jax.experimental.pallas import tpu as pltpu

F32 = jnp.float32
MXU_DTYPE = jnp.bfloat16
HI = lax.Precision.HIGHEST
EPS = 1e-6
NEG = -1e30

LANES = 128
SUBLANES = 8
VMEM_LIMIT = 48 * 1024 * 1024
ROW_DMA_PRIORITY = 1

SSD_HEADS = 16
SSD_HEAD_DIM = 64
SSD_D_INNER = SSD_HEADS * SSD_HEAD_DIM
SSD_STATE = 128
SSD_GROUPS = 2
SSD_CHUNK = 128
SSD_BC = SSD_GROUPS * SSD_STATE
SSD_XBC = SSD_D_INNER + 2 * SSD_BC
SSD_PAIRS = SSD_HEADS // 2
NA_HEADS = 16
NA_HEAD_DIM = 64
NA_D = NA_HEADS * NA_HEAD_DIM
NA_WIN_ROWS = 8
NA_WIN_COLS = 16
GRID_W = 64
N_GROUPS = 8
EXPERTS_PER_GROUP = 8
N_EXPERTS = N_GROUPS * EXPERTS_PER_GROUP
TOP_K = 2
D_FF_EXPERT = 384
MOE_BLOCK = 128
S_ZX = SSD_D_INNER + SSD_XBC
S_DT = S_ZX + 2 * SSD_HEADS
QKVD = 3 * NA_D + LANES


def _pick(n, prefs):
    for p in prefs:
        if n % p == 0:
            return p
    return n


def _params(*sem):
    return pltpu.CompilerParams(dimension_semantics=sem, vmem_limit_bytes=VMEM_LIMIT)


def _vec(arr, idx):
    return arr.reshape(arr.shape[0], 1, arr.shape[-1]), idx


def _vec_spec(vec, width, col):
    idx = vec[1]
    return pl.BlockSpec((None, 1, width), lambda *g: (idx, 0, col(*g)))


def _mat_spec(layer, block, idx):
    return pl.BlockSpec((None,) + block, lambda *g: (layer,) + idx(*g))


def _zero(*g):
    return 0


def _silu(x):
    return x * (1.0 / (1.0 + jnp.exp(-x)))


def _softplus(x):
    return jnp.maximum(x, 0.0) + jnp.log1p(jnp.exp(-jnp.abs(x)))


def _dot(a, b, precision=None):
    return jnp.dot(a, b, preferred_element_type=F32, precision=precision)


def _dot_nt(a, b):
    return lax.dot_general(a, b, (((1,), (1,)), ((), ())), preferred_element_type=F32)


def _norm_mod(x, nw, sh, sc):
    ms = jnp.mean(x * x, axis=-1, keepdims=True)
    n = x * lax.rsqrt(ms + EPS) * nw
    return n * (1.0 + sc) + sh


def _mod_kernel(ct_ref, w_ref, b_ref, o_ref):
    ct = ct_ref[...]
    s = _silu(ct)
    w = w_ref[...]
    r0 = jnp.sum(s[:, 0:1] * w, axis=0, keepdims=True)
    r1 = jnp.sum(s[:, 1:2] * w, axis=0, keepdims=True)
    o_ref[...] = jnp.concatenate([r0, r1], axis=0) + b_ref[...]


def _modulation(ct, w_mod, b_mod):
    depth, d, n = w_mod.shape
    tn = _pick(n, (512, 256, 128))
    return pl.pallas_call(
        _mod_kernel,
        grid=(depth, n // tn),
        in_specs=[pl.BlockSpec((d, 2), lambda l, j: (0, 0)),
                  pl.BlockSpec((None, d, tn), lambda l, j: (l, 0, j)),
                  pl.BlockSpec((None, 1, tn), lambda l, j: (l, 0, j))],
        out_specs=pl.BlockSpec((None, 2, tn), lambda l, j: (l, 0, j)),
        out_shape=jax.ShapeDtypeStruct((depth, 2, n), F32),
        compiler_params=_params("parallel", "parallel"),
        name="modulation",
    )(ct, w_mod, b_mod.reshape(depth, 1, n))


def _nm_matmul_kernel(x_ref, nw_ref, sh_ref, sc_ref, w_ref, o_ref, a_ref):
    @pl.when(pl.program_id(1) == 0)
    def _():
        a_ref[...] = _norm_mod(x_ref[...], nw_ref[...], sh_ref[...], sc_ref[...]).astype(a_ref.dtype)

    o_ref[...] = _dot(a_ref[...], w_ref[...].astype(a_ref.dtype)).astype(o_ref.dtype)


def _nm_matmul(x, nw, sh, sc, w, layer, n_cols, name):
    m, d = x.shape
    tm = _pick(m, (1024, 512, 256, 128))
    tn = _pick(n_cols, (512, 640, 256, 128))
    return pl.pallas_call(
        _nm_matmul_kernel,
        grid=(m // tm, n_cols // tn),
        in_specs=[pl.BlockSpec((tm, d), lambda i, j: (i, 0)),
                  _vec_spec(nw, d, _zero), _vec_spec(sh, d, _zero), _vec_spec(sc, d, _zero),
                  _mat_spec(layer, (d, tn), lambda i, j: (0, j))],
        out_specs=pl.BlockSpec((tm, tn), lambda i, j: (i, j)),
        out_shape=jax.ShapeDtypeStruct((m, n_cols), F32),
        scratch_shapes=[pltpu.VMEM((tm, d), MXU_DTYPE)],
        compiler_params=_params("parallel", "arbitrary"),
        name=name,
    )(x, nw[0], sh[0], sc[0], w)


def _matmul_res_kernel(*refs, n_a):
    a_refs, w_refs = refs[:n_a], refs[n_a:2 * n_a]
    res_ref, g_ref, o_ref = refs[2 * n_a:]
    acc = None
    for a_ref, w_ref in zip(a_refs, w_refs):
        t = _dot(a_ref[...].astype(MXU_DTYPE), w_ref[...].astype(MXU_DTYPE))
        acc = t if acc is None else acc + t
    o_ref[...] = res_ref[...] + g_ref[...] * acc


def _matmul_res(a_list, w, layer, res, g, name):
    n_a = len(a_list)
    m, n = res.shape
    kp = w.shape[1] // n_a
    tm = _pick(m, (1024, 512, 256, 128))
    tn = _pick(n, (512, 256, 128))
    in_specs = [pl.BlockSpec((tm, kp), lambda i, j: (i, 0)) for _ in range(n_a)]
    in_specs += [_mat_spec(layer, (kp, tn), functools.partial(lambda i, j, p: (p, j), p=p)) for p in range(n_a)]
    in_specs += [pl.BlockSpec((tm, tn), lambda i, j: (i, j)), _vec_spec(g, tn, lambda i, j: j)]
    return pl.pallas_call(
        functools.partial(_matmul_res_kernel, n_a=n_a),
        grid=(m // tm, n // tn),
        in_specs=in_specs,
        out_specs=pl.BlockSpec((tm, tn), lambda i, j: (i, j)),
        out_shape=jax.ShapeDtypeStruct((m, n), F32),
        compiler_params=_params("parallel", "parallel"),
        name=name,
    )(*a_list, *([w] * n_a), res, g[0])


def _shifted(x, prev_ref, next_ref):
    i, n_i = pl.program_id(0), pl.num_programs(0)
    tl = x.shape[0]
    row = lax.broadcasted_iota(jnp.int32, x.shape, 0)
    prev_row = jnp.where(i == 0, 0.0, prev_ref[SUBLANES - 1:SUBLANES, :])
    next_row = jnp.where(i == n_i - 1, 0.0, next_ref[0:1, :])
    up = jnp.where(row == 0, prev_row, pltpu.roll(x, 1, 0))
    down = jnp.where(row == tl - 1, next_row, pltpu.roll(x, tl - 1, 0))
    return up, down


def _halo_specs(tl, tc, n_rows, col_off):
    nb = tl // SUBLANES
    last = n_rows // SUBLANES - 1
    return [pl.BlockSpec((tl, tc), lambda i, j: (i, j + col_off)),
            pl.BlockSpec((SUBLANES, tc), lambda i, j: (jnp.maximum(i * nb - 1, 0), j + col_off)),
            pl.BlockSpec((SUBLANES, tc), lambda i, j: (jnp.minimum((i + 1) * nb, last), j + col_off))]


def _xbc_conv_kernel(x_ref, prev_ref, next_ref, w_ref, b_ref, o_ref):
    x = x_ref[...]
    up, down = _shifted(x, prev_ref, next_ref)
    w = w_ref[...]
    o_ref[...] = _silu(w[0:1] * up + w[1:2] * x + w[2:3] * down + b_ref[...])


def _xbc_conv(zx, conv_w, conv_b):
    n_rows = zx.shape[0]
    tl = _pick(n_rows, (512, 256, 128))
    tc = 512
    off = SSD_D_INNER // tc
    return pl.pallas_call(
        _xbc_conv_kernel,
        grid=(n_rows // tl, SSD_XBC // tc),
        in_specs=_halo_specs(tl, tc, n_rows, off) + [pl.BlockSpec((3, tc), lambda i, j: (0, j)),
                                                     pl.BlockSpec((1, tc), lambda i, j: (0, j))],
        out_specs=pl.BlockSpec((tl, tc), lambda i, j: (i, j)),
        out_shape=jax.ShapeDtypeStruct((n_rows, SSD_XBC), F32),
        compiler_params=_params("parallel", "parallel"),
        name="xbc_conv",
    )(zx, zx, zx, conv_w, conv_b.reshape(1, -1))


def _gate_conv_kernel(bg_ref, cg_ref, cgp_ref, cgn_ref, xv_ref, xvp_ref, xvn_ref, w_ref, o_ref):
    i, n_i = pl.program_id(0), pl.num_programs(0)
    u = cg_ref[...] * xv_ref[...]
    tl = u.shape[0]
    row = lax.broadcasted_iota(jnp.int32, u.shape, 0)
    prev_row = jnp.where(i == 0, 0.0, cgp_ref[SUBLANES - 1:SUBLANES, :] * xvp_ref[SUBLANES - 1:SUBLANES, :])
    next_row = jnp.where(i == n_i - 1, 0.0, cgn_ref[0:1, :] * xvn_ref[0:1, :])
    up = jnp.where(row == 0, prev_row, pltpu.roll(u, 1, 0))
    down = jnp.where(row == tl - 1, next_row, pltpu.roll(u, tl - 1, 0))
    w = w_ref[...]
    o_ref[...] = (bg_ref[...] * (w[0:1] * up + w[1:2] * u + w[2:3] * down)).astype(o_ref.dtype)


def _gate_conv(h3, conv_w):
    n_rows, c3 = h3.shape
    c = c3 // 3
    tl = _pick(n_rows, (512, 256, 128))
    tc = _pick(c, (512, 256, 128))
    nb = c // tc
    return pl.pallas_call(
        _gate_conv_kernel,
        grid=(n_rows // tl, nb),
        in_specs=[pl.BlockSpec((tl, tc), lambda i, j: (i, j))] + _halo_specs(tl, tc, n_rows, nb)
        + _halo_specs(tl, tc, n_rows, 2 * nb) + [pl.BlockSpec((3, tc), lambda i, j: (0, j))],
        out_specs=pl.BlockSpec((tl, tc), lambda i, j: (i, j)),
        out_shape=jax.ShapeDtypeStruct((n_rows, c), MXU_DTYPE),
        compiler_params=_params("parallel", "parallel"),
        name="gate_conv",
    )(h3, h3, h3, h3, h3, h3, h3, conv_w)


def _ssd_kernel(xbc_ref, dt_ref, dtt_ref, dtb_ref, dtbt_ref, alog_ref, alogt_ref, h0_ref,
                y_ref, hout_ref, h_ref):
    d, c, n_c = pl.program_id(0), pl.program_id(1), pl.num_programs(1)
    q = SSD_CHUNK

    @pl.when(c == 0)
    def _():
        h_ref[...] = h0_ref[...]

    dt = _softplus(dt_ref[...] + dtb_ref[...])
    dtt = _softplus(dtt_ref[...] + dtbt_ref[...])
    a = dt * -jnp.exp(alog_ref[...])
    at = dtt * -jnp.exp(alogt_ref[...])
    row = lax.broadcasted_iota(jnp.int32, (q, q), 0)
    col = lax.broadcasted_iota(jnp.int32, (q, q), 1)
    sign = jnp.where(d == 0, 1, -1)
    mask = (row - col) * sign >= 0
    tri = mask.astype(F32)
    tri_t = ((col - row) * sign >= 0).astype(F32)
    cs = _dot(tri, a, HI)
    cst = _dot(at, tri_t, HI)
    tot = jnp.broadcast_to(jnp.sum(a, axis=0, keepdims=True), (SUBLANES, SSD_HEADS))
    expand = (lax.broadcasted_iota(jnp.int32, (SSD_HEADS, SSD_D_INNER), 1) // SSD_HEAD_DIM
              == lax.broadcasted_iota(jnp.int32, (SSD_HEADS, SSD_D_INNER), 0)).astype(F32)
    dt_e = _dot(dt, expand, HI)
    cs_e = _dot(cs, expand, HI)
    tot_e = _dot(tot, expand, HI)[0:1]

    xs = xbc_ref[:, 0:SSD_D_INNER]
    xdt = xs * dt_e
    xdt_m = xdt.astype(MXU_DTYPE)
    xw_m = (xdt * jnp.exp(tot_e - cs_e)).astype(MXU_DTYPE)
    e_e = jnp.exp(cs_e)
    dec_e = jnp.exp(tot_e)
    first = lax.broadcasted_iota(jnp.int32, (q, LANES), 1) < SSD_HEAD_DIM

    ppg = SSD_PAIRS // SSD_GROUPS
    for g in range(SSD_GROUPS):
        bg = xbc_ref[:, SSD_D_INNER + g * SSD_STATE:SSD_D_INNER + (g + 1) * SSD_STATE]
        cg = xbc_ref[:, SSD_D_INNER + SSD_BC + g * SSD_STATE:SSD_D_INNER + SSD_BC + (g + 1) * SSD_STATE]
        bg_m, cg_m = bg.astype(MXU_DTYPE), cg.astype(MXU_DTYPE)
        bgt_m = bg.T.astype(MXU_DTYPE)
        scores = _dot_nt(cg_m, bg_m)
        for pp in range(ppg):
            p = g * ppg + pp
            sl = slice(p * LANES, (p + 1) * LANES)
            ys = []
            for hh in range(2):
                h = 2 * p + hh
                diff = cs[:, h:h + 1] - cst[h:h + 1, :]
                decay = jnp.exp(jnp.where(mask, diff, -jnp.inf))
                ys.append(_dot((scores * decay).astype(MXU_DTYPE), xdt_m[:, sl]))
            hp = h_ref[p]
            y_off = _dot(cg_m, hp.astype(MXU_DTYPE)) * e_e[:, sl]
            y_ref[:, sl] = jnp.where(first, ys[0], ys[1]) + y_off
            h_ref[p] = dec_e[:, sl] * hp + _dot(bgt_m, xw_m[:, sl])

    @pl.when(c == n_c - 1)
    def _():
        hout_ref[...] = h_ref[...]


def _ssd(xbc, dt_raw, dt_bias, a_log, h0):
    n_rows = xbc.shape[0]
    q = SSD_CHUNK
    n_c = n_rows // q
    dt = dt_raw.reshape(n_rows, 2, SSD_HEADS).transpose(1, 0, 2)
    dtt = dt.transpose(0, 2, 1)

    def chunk(d, c):
        return jnp.where(d == 0, c, n_c - 1 - c)

    small = lambda shape: pl.BlockSpec((None,) + shape, lambda d, c: (d, 0, 0))
    return pl.pallas_call(
        _ssd_kernel,
        grid=(2, n_c),
        in_specs=[pl.BlockSpec((q, SSD_XBC), lambda d, c: (chunk(d, c), 0)),
                  pl.BlockSpec((None, q, SSD_HEADS), lambda d, c: (d, chunk(d, c), 0)),
                  pl.BlockSpec((None, SSD_HEADS, q), lambda d, c: (d, 0, chunk(d, c))),
                  small((1, SSD_HEADS)), small((SSD_HEADS, 1)), small((1, SSD_HEADS)), small((SSD_HEADS, 1)),
                  pl.BlockSpec((None, SSD_PAIRS, SSD_STATE, LANES), lambda d, c: (d, 0, 0, 0))],
        out_specs=[pl.BlockSpec((None, q, SSD_D_INNER), lambda d, c: (d, chunk(d, c), 0)),
                   pl.BlockSpec((None, SSD_PAIRS, SSD_STATE, LANES), lambda d, c: (d, 0, 0, 0))],
        out_shape=[jax.ShapeDtypeStruct((2, n_rows, SSD_D_INNER), F32),
                   jax.ShapeDtypeStruct((2, SSD_PAIRS, SSD_STATE, LANES), F32)],
        scratch_shapes=[pltpu.VMEM((SSD_PAIRS, SSD_STATE, LANES), F32)],
        compiler_params=_params("arbitrary", "arbitrary"),
        name="ssd_scan",
    )(xbc, dt, dtt, dt_bias[:, None, :], dt_bias[:, :, None], a_log[:, None, :], a_log[:, :, None], h0)


def _ssd_out_kernel(y_ref, xbc_ref, z_ref, dsk_ref, nw_ref, o_ref):
    y = y_ref[0] + y_ref[1] + dsk_ref[...] * xbc_ref[...]
    g = y * _silu(z_ref[...])
    ms = jnp.mean(g * g, axis=-1, keepdims=True)
    o_ref[...] = (g * lax.rsqrt(ms + EPS) * nw_ref[...]).astype(o_ref.dtype)


def _ssd_out(y2, xbc, zx, d_skip, norm_w):
    n_rows = xbc.shape[0]
    tl = _pick(n_rows, (512, 256, 128))
    w = SSD_D_INNER
    return pl.pallas_call(
        _ssd_out_kernel,
        grid=(n_rows // tl,),
        in_specs=[pl.BlockSpec((2, tl, w), lambda i: (0, i, 0)),
                  pl.BlockSpec((tl, w), lambda i: (i, 0)),
                  pl.BlockSpec((tl, w), lambda i: (i, 0)),
                  pl.BlockSpec((1, w), lambda i: (0, 0)),
                  pl.BlockSpec((1, w), lambda i: (0, 0))],
        out_specs=pl.BlockSpec((tl, w), lambda i: (i, 0)),
        out_shape=jax.ShapeDtypeStruct((n_rows, w), MXU_DTYPE),
        compiler_params=_params("parallel"),
        name="ssd_out",
    )(y2, xbc, zx, jnp.repeat(d_skip, SSD_HEAD_DIM)[None, :], norm_w[None, :])


def _head_norm(x, w, scale):
    blk = (lax.broadcasted_iota(jnp.int32, (LANES, LANES), 0) // NA_HEAD_DIM
           == lax.broadcasted_iota(jnp.int32, (LANES, LANES), 1) // NA_HEAD_DIM).astype(F32)
    ms = _dot(x * x, blk, HI) * (1.0 / NA_HEAD_DIM)
    return x * lax.rsqrt(ms + EPS) * (w * scale)


def _qkv_prep_kernel(q_ref, k_ref, v_ref, qw_ref, kw_ref, qo_ref, ko_ref, vo_ref):
    qo_ref[...] = _head_norm(q_ref[...], qw_ref[...], NA_HEAD_DIM ** -0.5).astype(qo_ref.dtype)
    ko_ref[...] = _head_norm(k_ref[...], kw_ref[...], 1.0).astype(ko_ref.dtype)
    vo_ref[...] = v_ref[...].astype(vo_ref.dtype)


def _qkv_prep(qkvd, q_norm_w, k_norm_w):
    n_rows = qkvd.shape[0]
    tl = _pick(n_rows, (512, 256, 128))
    nb = NA_D // LANES
    tile = lambda off: pl.BlockSpec((tl, LANES), lambda i, j: (i, j + off))
    wspec = pl.BlockSpec((1, LANES), lambda i, j: (0, 0))
    out = jax.ShapeDtypeStruct((n_rows, NA_D), MXU_DTYPE)
    return pl.pallas_call(
        _qkv_prep_kernel,
        grid=(n_rows // tl, nb),
        in_specs=[tile(0), tile(nb), tile(2 * nb), wspec, wspec],
        out_specs=[tile(0), tile(0), tile(0)],
        out_shape=[out, out, out],
        compiler_params=_params("parallel", "parallel"),
        name="qkv_prep",
    )(qkvd, qkvd, qkvd, jnp.tile(q_norm_w, 2)[None, :], jnp.tile(k_norm_w, 2)[None, :])


def _attend(q2, parts):
    first = lax.broadcasted_iota(jnp.int32, q2.shape, 1) < NA_HEAD_DIM
    outs = []
    for head_mask in (first, jnp.logical_not(first)):
        qa = jnp.where(head_mask, q2, jnp.zeros_like(q2))
        scores = []
        for k, _, bias in parts:
            s = _dot_nt(qa, k)
            scores.append(s if bias is None else s + bias)
        m = functools.reduce(jnp.maximum, [jnp.max(s, axis=-1, keepdims=True) for s in scores])
        probs = [jnp.exp(s - m) for s in scores]
        denom = functools.reduce(jnp.add, [jnp.sum(p, axis=-1, keepdims=True) for p in probs])
        acc = functools.reduce(jnp.add, [_dot(p.astype(MXU_DTYPE), v) for p, (_, v, _) in zip(probs, parts)])
        outs.append(acc / denom)
    return jnp.where(first, outs[0], outs[1])


def _natten_kernel(q_ref, k_ref, v_ref, kc_ref, vc_ref, tbl_ref, o_ref, *, rb, rows):
    i = pl.program_id(1)
    kc, vc = kc_ref[...], vc_ref[...]
    n_win = NA_WIN_ROWS * GRID_W
    first = lax.broadcasted_iota(jnp.int32, (GRID_W, LANES), 1) < NA_HEAD_DIM
    for t in range(rb):
        r = i * rb + t
        start = jnp.clip(r - NA_WIN_ROWS // 2, 0, rows - NA_WIN_ROWS)
        dr0 = start - r + (NA_WIN_ROWS - 1)
        ks = pl.ds(pl.multiple_of(start * GRID_W, GRID_W), n_win)
        q2 = q_ref[t * GRID_W:(t + 1) * GRID_W, :]
        kw, vw = k_ref[ks, :], v_ref[ks, :]
        zero = jnp.zeros_like(q2)
        qs = jnp.concatenate([jnp.where(first, q2, zero), jnp.where(first, zero, q2)], axis=0)
        s_loc = _dot_nt(qs, kw) + tbl_ref[dr0]
        s_ctx = _dot_nt(qs, kc)
        m = jnp.maximum(jnp.max(s_loc, axis=-1, keepdims=True), jnp.max(s_ctx, axis=-1, keepdims=True))
        p_loc, p_ctx = jnp.exp(s_loc - m), jnp.exp(s_ctx - m)
        denom = jnp.sum(p_loc, axis=-1, keepdims=True) + jnp.sum(p_ctx, axis=-1, keepdims=True)
        o = (_dot(p_loc.astype(MXU_DTYPE), vw) + _dot(p_ctx.astype(MXU_DTYPE), vc)) / denom
        o_ref[t * GRID_W:(t + 1) * GRID_W, :] = jnp.where(first, o[:GRID_W], o[GRID_W:]).astype(o_ref.dtype)


def _bias_table(rpb):
    col = jnp.arange(GRID_W)
    c0 = jnp.clip(col - NA_WIN_COLS // 2, 0, GRID_W - NA_WIN_COLS)
    col_in = (col[None, :] >= c0[:, None]) & (col[None, :] < c0[:, None] + NA_WIN_COLS)
    dc = jnp.clip(col[None, :] - col[:, None], 1 - NA_WIN_COLS, NA_WIN_COLS - 1) + (NA_WIN_COLS - 1)
    t = jnp.where(col_in, rpb.astype(F32)[:, :, dc], NEG)
    dr = jnp.arange(NA_WIN_ROWS)[:, None] + jnp.arange(NA_WIN_ROWS)[None, :]
    t = t[:, dr]
    t = t.reshape(NA_HEADS // 2, 2, NA_WIN_ROWS, NA_WIN_ROWS, GRID_W, GRID_W)
    return t.transpose(0, 2, 1, 4, 3, 5).reshape(NA_HEADS // 2, NA_WIN_ROWS, 2 * GRID_W, NA_WIN_ROWS * GRID_W)


def _natten(qn, kn, vb, kc, vc, rpb):
    n_rows = qn.shape[0]
    rows = n_rows // GRID_W
    n_ctx = kc.shape[0]
    rb = _pick(rows, (4, 2, 1))
    n_win = NA_WIN_ROWS * GRID_W
    seq = lambda n: pl.BlockSpec((n, LANES), lambda p, i: (0, p))
    tile = pl.BlockSpec((rb * GRID_W, LANES), lambda p, i: (i, p))
    return pl.pallas_call(
        functools.partial(_natten_kernel, rb=rb, rows=rows),
        grid=(NA_HEADS // 2, rows // rb),
        in_specs=[tile, seq(n_rows), seq(n_rows), seq(n_ctx), seq(n_ctx),
                  pl.BlockSpec((None, NA_WIN_ROWS, 2 * GRID_W, n_win), lambda p, i: (p, 0, 0, 0))],
        out_specs=tile,
        out_shape=jax.ShapeDtypeStruct((n_rows, NA_D), MXU_DTYPE),
        compiler_params=_params("parallel", "arbitrary"),
        name="natten",
    )(qn, kn, vb, kc, vc, _bias_table(rpb))


def _ctx_attn_kernel(q_ref, k_ref, v_ref, o_ref):
    o_ref[...] = _attend(q_ref[...], [(k_ref[...], v_ref[...], None)]).astype(o_ref.dtype)


def _ctx_attn(qn, kn, vb):
    n_ctx = qn.shape[0]
    spec = pl.BlockSpec((n_ctx, LANES), lambda p: (0, p))
    return pl.pallas_call(
        _ctx_attn_kernel,
        grid=(NA_HEADS // 2,),
        in_specs=[spec, spec, spec],
        out_specs=spec,
        out_shape=jax.ShapeDtypeStruct((n_ctx, NA_D), MXU_DTYPE),
        compiler_params=_params("parallel"),
        name="ctx_attn",
    )(qn, kn, vb)


def _load_slabs(ref, n, s):
    return jnp.concatenate([ref[pl.ds(c, n, stride=s), :] for c in range(s)], axis=1)


def _store_slabs(ref, val):
    n = val.shape[0]
    s = val.shape[1] // LANES
    for c in range(s):
        ref[pl.ds(c, n, stride=s), :] = val[:, c * LANES:(c + 1) * LANES].astype(ref.dtype)

def _router_kernel(x_ref, nw_ref, sh_ref, sc_ref, wr_ref, m_ref, e_ref, g_ref):
    m = _norm_mod(x_ref[...], nw_ref[...], sh_ref[...], sc_ref[...])
    _store_slabs(m_ref, m)
    logits = _dot(m, wr_ref[...], HI)
    lane = lax.broadcasted_iota(jnp.int32, logits.shape, 1)
    big = jnp.int32(LANES)

    def top(vals):
        v = jnp.max(vals, axis=-1, keepdims=True)
        idx = jnp.min(jnp.where(vals == v, lane, big), axis=-1, keepdims=True)
        return v, idx

    gl = jnp.where(lane < N_GROUPS, logits, -jnp.inf)
    g_max, grp = top(gl)
    p_grp = 1.0 / jnp.sum(jnp.exp(gl - g_max), axis=-1, keepdims=True)
    e_lane = lane - N_GROUPS
    in_grp = (e_lane >= grp * EXPERTS_PER_GROUP) & (e_lane < (grp + 1) * EXPERTS_PER_GROUP)
    el = jnp.where(in_grp, logits, -jnp.inf)
    v1, i1 = top(el)
    v2, i2 = top(jnp.where(lane == i1, -jnp.inf, el))
    t = jnp.exp(v2 - v1)
    g1 = p_grp / (1.0 + t)
    g2 = p_grp * t / (1.0 + t)
    e_ref[...] = jnp.where(lane == 0, i1 - N_GROUPS, jnp.where(lane == 1, i2 - N_GROUPS, 0))
    g_ref[...] = jnp.where(lane == 0, g1, jnp.where(lane == 1, g2, 0.0))


def _router(x, nw, sh, sc, w_route):
    m_rows, d = x.shape
    tm = _pick(m_rows, (512, 256, 128))
    wide = pl.BlockSpec((tm, LANES), lambda i: (i, 0))
    m, e, g = pl.pallas_call(
        _router_kernel,
        grid=(m_rows // tm,),
        in_specs=[pl.BlockSpec((tm, d), lambda i: (i, 0)),
                  _vec_spec(nw, d, _zero), _vec_spec(sh, d, _zero), _vec_spec(sc, d, _zero),
                  pl.BlockSpec((d, LANES), lambda i: (0, 0))],
        out_specs=[pl.BlockSpec((tm * (d // LANES), LANES), lambda i: (i, 0)), wide, wide],
        out_shape=[jax.ShapeDtypeStruct((m_rows * (d // LANES), LANES), F32),
                   jax.ShapeDtypeStruct((m_rows, LANES), jnp.int32),
                   jax.ShapeDtypeStruct((m_rows, LANES), F32)],
        compiler_params=_params("parallel"),
        name="router",
    )(x, nw[0], sh[0], sc[0], w_route)
    return m, e[:, :TOP_K], g


def _moe_kernel(first_ref, nblk_ref, tok_ref, nact_ref, m_hbm, wg_ref, wu_ref, wd_ref, yb_hbm,
                xbuf, ybuf, gsem, osem, wgb_ref, wub_ref, wdb_ref):
    e, n_e = pl.program_id(0), pl.num_programs(0)
    n_act = nact_ref[0]
    s = wg_ref.shape[0] // LANES
    blk_rows = MOE_BLOCK * s
    n_blocks = yb_hbm.shape[0] // blk_rows

    def gather_start(blk, slot):
        for r in range(MOE_BLOCK):
            tok = tok_ref[blk * MOE_BLOCK + r]
            pltpu.make_async_copy(m_hbm.at[pl.ds(pl.multiple_of(tok * s, s), s), :],
                                  xbuf.at[slot, pl.ds(r * s, s), :],
                                  gsem.at[slot]).start(priority=ROW_DMA_PRIORITY)

    def gather_wait(slot):
        pltpu.make_async_copy(m_hbm.at[pl.ds(0, blk_rows), :], xbuf.at[slot], gsem.at[slot]).wait()

    def out_copy(blk, slot):
        rows = pl.ds(pl.multiple_of(blk * blk_rows, blk_rows), blk_rows)
        return pltpu.make_async_copy(ybuf.at[slot], yb_hbm.at[rows, :], osem.at[slot])

    @pl.when(e == 0)
    def _():
        gather_start(0, 0)

    @pl.when(nblk_ref[e] > 0)
    def _():
        wgb_ref[...] = wg_ref[...].astype(wgb_ref.dtype)
        wub_ref[...] = wu_ref[...].astype(wub_ref.dtype)
        wdb_ref[...] = wd_ref[...].astype(wdb_ref.dtype)

    def block(j, carry):
        b = first_ref[e] + j
        slot = b % 2
        gather_wait(slot)

        @pl.when(b >= 2)
        def _():
            out_copy(b - 2, slot).wait()

        gather_start(jnp.minimum(b + 1, n_act - 1), 1 - slot)
        x = _load_slabs(xbuf.at[slot], MOE_BLOCK, s).astype(MXU_DTYPE)
        h = _silu(_dot(x, wgb_ref[...])) * _dot(x, wub_ref[...])
        _store_slabs(ybuf.at[slot], _dot(h.astype(MXU_DTYPE), wdb_ref[...]))
        out_copy(b, slot).start()
        return carry

    lax.fori_loop(0, nblk_ref[e], block, 0)

    @pl.when(e == n_e - 1)
    def _():
        gather_wait(n_act % 2)
        for back in (1, 2):
            @pl.when(n_act >= back)
            def _():
                out_copy(n_act - back, (n_act - back) % 2).wait()

        ybuf[0] = jnp.zeros(ybuf.shape[1:], ybuf.dtype)

        def fill(b, carry):
            out_copy(b, 0).start()
            out_copy(b, 0).wait()
            return carry

        lax.fori_loop(n_act, n_blocks, fill, 0)


def _moe_ffn(m_tok, slot_tok, first_blk, n_blk, n_act, w_gate, w_up, w_down, layer):
    n_rows = slot_tok.shape[0]
    d, f = w_gate.shape[-2:]
    blk_rows = MOE_BLOCK * (d // LANES)
    wspec = lambda r, c: pl.BlockSpec((None, None, r, c), lambda e, *_: (layer, e, 0, 0))
    grid_spec = pltpu.PrefetchScalarGridSpec(
        num_scalar_prefetch=4,
        grid=(N_EXPERTS,),
        in_specs=[pl.BlockSpec(memory_space=pl.ANY), wspec(d, f), wspec(d, f), wspec(f, d)],
        out_specs=pl.BlockSpec(memory_space=pl.ANY),
        scratch_shapes=[pltpu.VMEM((2, blk_rows, LANES), F32), pltpu.VMEM((2, blk_rows, LANES), F32),
                        pltpu.SemaphoreType.DMA((2,)), pltpu.SemaphoreType.DMA((2,)),
                        pltpu.VMEM((d, f), MXU_DTYPE), pltpu.VMEM((d, f), MXU_DTYPE), pltpu.VMEM((f, d), MXU_DTYPE)],
    )
    return pl.pallas_call(
        _moe_kernel,
        grid_spec=grid_spec,
        out_shape=jax.ShapeDtypeStruct((n_rows * (d // LANES), LANES), F32),
        compiler_params=_params("arbitrary"),
        name="moe_ffn",
    )(first_blk, n_blk, slot_tok, n_act, m_tok, w_gate, w_up, w_down)


def _combine_kernel(pos_ref, x_ref, g_ref, gt_ref, yb_hbm, o_ref, buf, sem, *, tm):
    i, n_i = pl.program_id(0), pl.num_programs(0)
    s = x_ref.shape[1] // LANES

    def gather_start(tile, slot):
        for j in range(tm):
            for k in range(TOP_K):
                p = pos_ref[(tile * tm + j) * TOP_K + k]
                pltpu.make_async_copy(yb_hbm.at[pl.ds(pl.multiple_of(p * s, s), s), :],
                                      buf.at[slot, k, pl.ds(j * s, s), :],
                                      sem.at[slot]).start(priority=ROW_DMA_PRIORITY)

    def gather_wait(slot):
        for k in range(TOP_K):
            pltpu.make_async_copy(yb_hbm.at[pl.ds(0, tm * s), :], buf.at[slot, k], sem.at[slot]).wait()

    @pl.when(i == 0)
    def _():
        gather_start(0, 0)

    slot = i % 2
    gather_wait(slot)
    gather_start(jnp.minimum(i + 1, n_i - 1), 1 - slot)
    gt = gt_ref[...]
    y0 = _load_slabs(buf.at[slot, 0], tm, s)
    y1 = _load_slabs(buf.at[slot, 1], tm, s)
    o_ref[...] = x_ref[...] + g_ref[...] * (gt[:, 0:1] * y0 + gt[:, 1:2] * y1)

    @pl.when(i == n_i - 1)
    def _():
        gather_wait(1 - slot)


def _combine(x, g, gates, yb, pos):
    m, d = x.shape
    tm = MOE_BLOCK
    grid_spec = pltpu.PrefetchScalarGridSpec(
        num_scalar_prefetch=1,
        grid=(m // tm,),
        in_specs=[pl.BlockSpec((tm, d), lambda i, pos: (i, 0)),
                  _vec_spec(g, d, _zero),
                  pl.BlockSpec((tm, LANES), lambda i, pos: (i, 0)),
                  pl.BlockSpec(memory_space=pl.ANY)],
        out_specs=pl.BlockSpec((tm, d), lambda i, pos: (i, 0)),
        scratch_shapes=[pltpu.VMEM((2, TOP_K, tm * (d // LANES), LANES), F32), pltpu.SemaphoreType.DMA((2,))],
    )
    return pl.pallas_call(
        functools.partial(_combine_kernel, tm=tm),
        grid_spec=grid_spec,
        out_shape=jax.ShapeDtypeStruct((m, d), F32),
        compiler_params=_params("arbitrary"),
        name="moe_combine",
    )(pos, x, g[0], gates, yb)


def _dispatch(expert):
    n_tok = expert.shape[0]
    n = n_tok * TOP_K
    n_blocks = -(-n // MOE_BLOCK) + N_EXPERTS
    n_rows = n_blocks * MOE_BLOCK
    flat_e = expert.reshape(-1)
    onehot = (flat_e[:, None] == jnp.arange(N_EXPERTS, dtype=jnp.int32)[None, :]).astype(jnp.int32)
    csum = jnp.cumsum(onehot, axis=0)
    rank = jnp.take_along_axis(csum, flat_e[:, None], axis=1)[:, 0] - 1
    counts = csum[-1]
    padded = (counts + MOE_BLOCK - 1) // MOE_BLOCK * MOE_BLOCK
    pad_end = jnp.cumsum(padded)
    dest = (pad_end - padded)[flat_e] + rank
    pair_tok = jnp.arange(n, dtype=jnp.int32) // TOP_K
    slot_tok = jnp.zeros((n_rows,), jnp.int32).at[dest].set(pair_tok)
    first_blk = ((pad_end - padded) // MOE_BLOCK).astype(jnp.int32)
    n_blk = (padded // MOE_BLOCK).astype(jnp.int32)
    n_act = (pad_end[-1:] // MOE_BLOCK).astype(jnp.int32)
    return dest.astype(jnp.int32), slot_tok, first_blk, n_blk, n_act


def _even_mixer(lat, cx, nw, mod_lat, mod_ctx, e, w_in, conv_w, conv_b, dt_bias, a_log, d_skip, ssd_norm_w,
                q_norm_w, k_norm_w, rpb, w_out, ctx_out):
    w_e = w_in[e]
    w_qkvd = jnp.concatenate([w_e[:, S_DT:], w_e[:, S_ZX:S_DT],
                              jnp.zeros((w_e.shape[0], LANES - 2 * SSD_HEADS), w_e.dtype)], axis=1)[None]

    def project(x, mod, tag):
        sh, sc = mod[0], mod[1]
        zx = _nm_matmul(x, nw, sh, sc, w_in, e, S_ZX, "proj_zx_" + tag)
        qkvd = _nm_matmul(x, nw, sh, sc, w_qkvd, 0, QKVD, "proj_qkvd_" + tag)
        xbc = _xbc_conv(zx, conv_w[e], conv_b[e])
        qn, kn, vb = _qkv_prep(qkvd, q_norm_w[e], k_norm_w[e])
        return zx, xbc, qkvd[:, 3 * NA_D:3 * NA_D + 2 * SSD_HEADS], qn, kn, vb

    zx_c, xbc_c, dt_c, qn_c, kn_c, vb_c = project(cx, mod_ctx, "ctx")
    zx_l, xbc_l, dt_l, qn_l, kn_l, vb_l = project(lat, mod_lat, "lat")
    h0 = jnp.zeros((2, SSD_PAIRS, SSD_STATE, LANES), F32)
    y_c, h_ctx = _ssd(xbc_c, dt_c, dt_bias[e], a_log[e], h0)
    y_l, _ = _ssd(xbc_l, dt_l, dt_bias[e], a_log[e], h_ctx)
    yssd_l = _ssd_out(y_l, xbc_l, zx_l, d_skip[e], ssd_norm_w[e])
    yatt_l = _natten(qn_l, kn_l, vb_l, kn_c, vb_c, rpb[e])
    lat = _matmul_res([yssd_l, yatt_l], w_out, e, lat, mod_lat[2], "out_even_lat")
    if ctx_out:
        yssd_c = _ssd_out(y_c, xbc_c, zx_c, d_skip[e], ssd_norm_w[e])
        yatt_c = _ctx_attn(qn_c, kn_c, vb_c)
        cx = _matmul_res([yssd_c, yatt_c], w_out, e, cx, mod_ctx[2], "out_even_ctx")
    return lat, cx


def _odd_mixer(lat, cx, nw, mod_lat, mod_ctx, o, w_in, conv_w, w_out, ctx_out):
    def mix(x, mod, tag):
        h3 = _nm_matmul(x, nw, mod[0], mod[1], w_in, o, w_in.shape[2], "proj_odd_" + tag)
        return _matmul_res([_gate_conv(h3, conv_w[o])], w_out, o, x, mod[2], "out_odd_" + tag)

    lat = mix(lat, mod_lat, "lat")
    if ctx_out:
        cx = mix(cx, mod_ctx, "ctx")
    return lat, cx


def kernel(x, c, ctx, c_ctx, w_mod, b_mod, norm_mix_w, norm_ffn_w, ev_w_in, ev_conv_w, ev_conv_b, ev_dt_bias,
           ev_a_log, ev_d_skip, ev_ssd_norm_w, ev_q_norm_w, ev_k_norm_w, ev_rpb, ev_w_out, od_w_in, od_conv_w,
           od_w_out, moe_w_group, moe_w_router, moe_w_gate, moe_w_up, moe_w_down):
    bsz, n_lat, d = x.shape
    assert bsz == 1 and ctx.shape[0] == 1
    assert n_lat % GRID_W == 0 and n_lat // GRID_W >= NA_WIN_ROWS and n_lat % SSD_CHUNK == 0
    assert ctx.shape[1] % SSD_CHUNK == 0 and d % LANES == 0
    depth = w_mod.shape[0]
    lat, cx = x[0], ctx[0]
    mods = _modulation(jnp.stack([c[0], c_ctx], axis=1), w_mod, b_mod)
    mods = mods.reshape(depth * 2 * 6, d)

    for l in range(depth):
        even = l % 2 == 0
        ctx_out = any(j % 2 == 0 for j in range(l + 1, depth))
        mod_lat = [_vec(mods, (l * 2 + 0) * 6 + i) for i in range(6)]
        mod_ctx = [_vec(mods, (l * 2 + 1) * 6 + i) for i in range(6)]
        nw = _vec(norm_mix_w, l)
        if even:
            lat, cx = _even_mixer(lat, cx, nw, mod_lat, mod_ctx, l // 2, ev_w_in, ev_conv_w, ev_conv_b,
                                  ev_dt_bias, ev_a_log, ev_d_skip, ev_ssd_norm_w, ev_q_norm_w,
                                  ev_k_norm_w, ev_rpb, ev_w_out, ctx_out)
        else:
            lat, cx = _odd_mixer(lat, cx, nw, mod_lat, mod_ctx, l // 2, od_w_in, od_conv_w, od_w_out, ctx_out)

        nfw = _vec(norm_ffn_w, l)
        w_route = jnp.concatenate([moe_w_group[l], moe_w_router[l],
                                   jnp.zeros((d, LANES - N_GROUPS - N_EXPERTS), F32)], axis=1)
        m_tok, expert, gates = _router(lat, nfw, mod_lat[3], mod_lat[4], w_route)
        if ctx_out:
            m_c, e_c, g_c = _router(cx, nfw, mod_ctx[3], mod_ctx[4], w_route)
            m_tok = jnp.concatenate([m_tok, m_c], axis=0)
            expert = jnp.concatenate([expert, e_c], axis=0)
            gates = jnp.concatenate([gates, g_c], axis=0)
        dest, slot_tok, first_blk, n_blk, n_act = _dispatch(expert)
        yb = _moe_ffn(m_tok, slot_tok, first_blk, n_blk, n_act, moe_w_gate, moe_w_up, moe_w_down, l)
        lat = _combine(lat, mod_lat[5], gates[:n_lat], yb, dest[:n_lat * TOP_K])
        if ctx_out:
            cx = _combine(cx, mod_ctx[5], gates[n_lat:], yb, dest[n_lat * TOP_K:])
    return lat[None]
```

```python
import functools

import jax
import jax.numpy as jnp
from jax import lax
from jax.experimental import pallas as pl
from jax.experimental.pallas import tpu as pltpu

F32 = jnp.float32
MXU_DTYPE = jnp.bfloat16
HI = lax.Precision.HIGHEST
EPS = 1e-6
NEG = -1e30

LANES = 128
SUBLANES = 8
VMEM_LIMIT = 48 * 1024 * 1024
ROW_DMA_PRIORITY = 1

SSD_HEADS = 16
SSD_HEAD_DIM = 64
SSD_D_INNER = SSD_HEADS * SSD_HEAD_DIM
SSD_STATE = 128
SSD_GROUPS = 2
SSD_CHUNK = 128
SSD_BC = SSD_GROUPS * SSD_STATE
SSD_XBC = SSD_D_INNER + 2 * SSD_BC
SSD_PAIRS = SSD_HEADS // 2
NA_HEADS = 16
NA_HEAD_DIM = 64
NA_D = NA_HEADS * NA_HEAD_DIM
NA_WIN_ROWS = 8
NA_WIN_COLS = 16
GRID_W = 64
N_GROUPS = 8
EXPERTS_PER_GROUP = 8
N_EXPERTS = N_GROUPS * EXPERTS_PER_GROUP
TOP_K = 2
D_FF_EXPERT = 384
MOE_BLOCK = 128
MOE_AHEAD = 2
MOE_SLOTS = MOE_AHEAD + 1
S_ZX = SSD_D_INNER + SSD_XBC
S_DT = S_ZX + 2 * SSD_HEADS
QKVD = 3 * NA_D + LANES


def _pick(n, prefs):
    for p in prefs:
        if n % p == 0:
            return p
    return n


def _params(*sem):
    return pltpu.CompilerParams(dimension_semantics=sem, vmem_limit_bytes=VMEM_LIMIT)


def _vec(arr, idx):
    return arr.reshape(arr.shape[0], 1, arr.shape[-1]), idx


def _vec_spec(vec, width, col):
    idx = vec[1]
    return pl.BlockSpec((None, 1, width), lambda *g: (idx, 0, col(*g)))


def _mat_spec(layer, block, idx):
    return pl.BlockSpec((None,) + block, lambda *g: (layer,) + idx(*g))


def _zero(*g):
    return 0


def _silu(x):
    return x * (1.0 / (1.0 + jnp.exp(-x)))


def _softplus(x):
    return jnp.maximum(x, 0.0) + jnp.log1p(jnp.exp(-jnp.abs(x)))


def _dot(a, b, precision=None):
    return jnp.dot(a, b, preferred_element_type=F32, precision=precision)


def _dot_nt(a, b):
    return lax.dot_general(a, b, (((1,), (1,)), ((), ())), preferred_element_type=F32)


def _norm_mod(x, nw, sh, sc):
    ms = jnp.mean(x * x, axis=-1, keepdims=True)
    n = x * lax.rsqrt(ms + EPS) * nw
    return n * (1.0 + sc) + sh


def _mod_kernel(ct_ref, w_ref, b_ref, o_ref):
    ct = ct_ref[...]
    s = _silu(ct)
    w = w_ref[...]
    r0 = jnp.sum(s[:, 0:1] * w, axis=0, keepdims=True)
    r1 = jnp.sum(s[:, 1:2] * w, axis=0, keepdims=True)
    o_ref[...] = jnp.concatenate([r0, r1], axis=0) + b_ref[...]


def _modulation(ct, w_mod, b_mod):
    depth, d, n = w_mod.shape
    tn = _pick(n, (512, 256, 128))
    return pl.pallas_call(
        _mod_kernel,
        grid=(depth, n // tn),
        in_specs=[pl.BlockSpec((d, 2), lambda l, j: (0, 0)),
                  pl.BlockSpec((None, d, tn), lambda l, j: (l, 0, j)),
                  pl.BlockSpec((None, 1, tn), lambda l, j: (l, 0, j))],
        out_specs=pl.BlockSpec((None, 2, tn), lambda l, j: (l, 0, j)),
        out_shape=jax.ShapeDtypeStruct((depth, 2, n), F32),
        compiler_params=_params("parallel", "parallel"),
        name="modulation",
    )(ct, w_mod, b_mod.reshape(depth, 1, n))


def _nm_matmul_kernel(x_ref, nw_ref, sh_ref, sc_ref, w_ref, o_ref, a_ref):
    @pl.when(pl.program_id(1) == 0)
    def _():
        a_ref[...] = _norm_mod(x_ref[...], nw_ref[...], sh_ref[...], sc_ref[...]).astype(a_ref.dtype)

    o_ref[...] = _dot(a_ref[...], w_ref[...].astype(a_ref.dtype)).astype(o_ref.dtype)


def _nm_matmul(x, nw, sh, sc, w, layer, n_cols, name):
    m, d = x.shape
    tm = _pick(m, (1024, 512, 256, 128))
    tn = _pick(n_cols, (512, 640, 256, 128))
    return pl.pallas_call(
        _nm_matmul_kernel,
        grid=(m // tm, n_cols // tn),
        in_specs=[pl.BlockSpec((tm, d), lambda i, j: (i, 0)),
                  _vec_spec(nw, d, _zero), _vec_spec(sh, d, _zero), _vec_spec(sc, d, _zero),
                  _mat_spec(layer, (d, tn), lambda i, j: (0, j))],
        out_specs=pl.BlockSpec((tm, tn), lambda i, j: (i, j)),
        out_shape=jax.ShapeDtypeStruct((m, n_cols), F32),
        scratch_shapes=[pltpu.VMEM((tm, d), MXU_DTYPE)],
        compiler_params=_params("parallel", "arbitrary"),
        name=name,
    )(x, nw[0], sh[0], sc[0], w)


def _matmul_res_kernel(*refs, n_a):
    a_refs, w_refs = refs[:n_a], refs[n_a:2 * n_a]
    res_ref, g_ref, o_ref = refs[2 * n_a:]
    acc = None
    for a_ref, w_ref in zip(a_refs, w_refs):
        t = _dot(a_ref[...].astype(MXU_DTYPE), w_ref[...].astype(MXU_DTYPE))
        acc = t if acc is None else acc + t
    o_ref[...] = res_ref[...] + g_ref[...] * acc


def _matmul_res(a_list, w, layer, res, g, name):
    n_a = len(a_list)
    m, n = res.shape
    kp = w.shape[1] // n_a
    tm = _pick(m, (1024, 512, 256, 128))
    tn = _pick(n, (512, 256, 128))
    in_specs = [pl.BlockSpec((tm, kp), lambda i, j: (i, 0)) for _ in range(n_a)]
    in_specs += [_mat_spec(layer, (kp, tn), functools.partial(lambda i, j, p: (p, j), p=p)) for p in range(n_a)]
    in_specs += [pl.BlockSpec((tm, tn), lambda i, j: (i, j)), _vec_spec(g, tn, lambda i, j: j)]
    return pl.pallas_call(
        functools.partial(_matmul_res_kernel, n_a=n_a),
        grid=(m // tm, n // tn),
        in_specs=in_specs,
        out_specs=pl.BlockSpec((tm, tn), lambda i, j: (i, j)),
        out_shape=jax.ShapeDtypeStruct((m, n), F32),
        compiler_params=_params("parallel", "parallel"),
        name=name,
    )(*a_list, *([w] * n_a), res, g[0])


def _shifted(x, prev_ref, next_ref):
    i, n_i = pl.program_id(0), pl.num_programs(0)
    tl = x.shape[0]
    row = lax.broadcasted_iota(jnp.int32, x.shape, 0)
    prev_row = jnp.where(i == 0, 0.0, prev_ref[SUBLANES - 1:SUBLANES, :])
    next_row = jnp.where(i == n_i - 1, 0.0, next_ref[0:1, :])
    up = jnp.where(row == 0, prev_row, pltpu.roll(x, 1, 0))
    down = jnp.where(row == tl - 1, next_row, pltpu.roll(x, tl - 1, 0))
    return up, down


def _halo_specs(tl, tc, n_rows, col_off):
    nb = tl // SUBLANES
    last = n_rows // SUBLANES - 1
    return [pl.BlockSpec((tl, tc), lambda i, j: (i, j + col_off)),
            pl.BlockSpec((SUBLANES, tc), lambda i, j: (jnp.maximum(i * nb - 1, 0), j + col_off)),
            pl.BlockSpec((SUBLANES, tc), lambda i, j: (jnp.minimum((i + 1) * nb, last), j + col_off))]


def _xbc_conv_kernel(x_ref, prev_ref, next_ref, w_ref, b_ref, o_ref):
    x = x_ref[...]
    up, down = _shifted(x, prev_ref, next_ref)
    w = w_ref[...]
    o_ref[...] = _silu(w[0:1] * up + w[1:2] * x + w[2:3] * down + b_ref[...])


def _xbc_conv(zx, conv_w, conv_b):
    n_rows = zx.shape[0]
    tl = _pick(n_rows, (512, 256, 128))
    tc = 512
    off = SSD_D_INNER // tc
    return pl.pallas_call(
        _xbc_conv_kernel,
        grid=(n_rows // tl, SSD_XBC // tc),
        in_specs=_halo_specs(tl, tc, n_rows, off) + [pl.BlockSpec((3, tc), lambda i, j: (0, j)),
                                                     pl.BlockSpec((1, tc), lambda i, j: (0, j))],
        out_specs=pl.BlockSpec((tl, tc), lambda i, j: (i, j)),
        out_shape=jax.ShapeDtypeStruct((n_rows, SSD_XBC), F32),
        compiler_params=_params("parallel", "parallel"),
        name="xbc_conv",
    )(zx, zx, zx, conv_w, conv_b.reshape(1, -1))


def _gate_conv_kernel(bg_ref, cg_ref, cgp_ref, cgn_ref, xv_ref, xvp_ref, xvn_ref, w_ref, o_ref):
    i, n_i = pl.program_id(0), pl.num_programs(0)
    u = cg_ref[...] * xv_ref[...]
    tl = u.shape[0]
    row = lax.broadcasted_iota(jnp.int32, u.shape, 0)
    prev_row = jnp.where(i == 0, 0.0, cgp_ref[SUBLANES - 1:SUBLANES, :] * xvp_ref[SUBLANES - 1:SUBLANES, :])
    next_row = jnp.where(i == n_i - 1, 0.0, cgn_ref[0:1, :] * xvn_ref[0:1, :])
    up = jnp.where(row == 0, prev_row, pltpu.roll(u, 1, 0))
    down = jnp.where(row == tl - 1, next_row, pltpu.roll(u, tl - 1, 0))
    w = w_ref[...]
    o_ref[...] = (bg_ref[...] * (w[0:1] * up + w[1:2] * u + w[2:3] * down)).astype(o_ref.dtype)


def _gate_conv(h3, conv_w):
    n_rows, c3 = h3.shape
    c = c3 // 3
    tl = _pick(n_rows, (512, 256, 128))
    tc = _pick(c, (512, 256, 128))
    nb = c // tc
    return pl.pallas_call(
        _gate_conv_kernel,
        grid=(n_rows // tl, nb),
        in_specs=[pl.BlockSpec((tl, tc), lambda i, j: (i, j))] + _halo_specs(tl, tc, n_rows, nb)
        + _halo_specs(tl, tc, n_rows, 2 * nb) + [pl.BlockSpec((3, tc), lambda i, j: (0, j))],
        out_specs=pl.BlockSpec((tl, tc), lambda i, j: (i, j)),
        out_shape=jax.ShapeDtypeStruct((n_rows, c), MXU_DTYPE),
        compiler_params=_params("parallel", "parallel"),
        name="gate_conv",
    )(h3, h3, h3, h3, h3, h3, h3, conv_w)


def _ssd_kernel(xbc_ref, dt_ref, dtt_ref, dtb_ref, dtbt_ref, alog_ref, alogt_ref, h0_ref,
                y_ref, hout_ref, h_ref):
    d, c, n_c = pl.program_id(0), pl.program_id(1), pl.num_programs(1)
    q = SSD_CHUNK

    @pl.when(c == 0)
    def _():
        h_ref[...] = h0_ref[...]

    dt = _softplus(dt_ref[...] + dtb_ref[...])
    dtt = _softplus(dtt_ref[...] + dtbt_ref[...])
    a = dt * -jnp.exp(alog_ref[...])
    at = dtt * -jnp.exp(alogt_ref[...])
    row = lax.broadcasted_iota(jnp.int32, (q, q), 0)
    col = lax.broadcasted_iota(jnp.int32, (q, q), 1)
    sign = jnp.where(d == 0, 1, -1)
    mask = (row - col) * sign >= 0
    tri = mask.astype(F32)
    tri_t = ((col - row) * sign >= 0).astype(F32)
    cs = _dot(tri, a, HI)
    cst = _dot(at, tri_t, HI)
    tot = jnp.broadcast_to(jnp.sum(a, axis=0, keepdims=True), (SUBLANES, SSD_HEADS))
    expand = (lax.broadcasted_iota(jnp.int32, (SSD_HEADS, SSD_D_INNER), 1) // SSD_HEAD_DIM
              == lax.broadcasted_iota(jnp.int32, (SSD_HEADS, SSD_D_INNER), 0)).astype(F32)
    dt_e = _dot(dt, expand, HI)
    cs_e = _dot(cs, expand, HI)
    tot_e = _dot(tot, expand, HI)[0:1]

    xs = xbc_ref[:, 0:SSD_D_INNER]
    xdt = xs * dt_e
    xdt_m = xdt.astype(MXU_DTYPE)
    xw_m = (xdt * jnp.exp(tot_e - cs_e)).astype(MXU_DTYPE)
    e_e = jnp.exp(cs_e)
    dec_e = jnp.exp(tot_e)
    first = lax.broadcasted_iota(jnp.int32, (q, LANES), 1) < SSD_HEAD_DIM

    ppg = SSD_PAIRS // SSD_GROUPS
    for g in range(SSD_GROUPS):
        bg = xbc_ref[:, SSD_D_INNER + g * SSD_STATE:SSD_D_INNER + (g + 1) * SSD_STATE]
        cg = xbc_ref[:, SSD_D_INNER + SSD_BC + g * SSD_STATE:SSD_D_INNER + SSD_BC + (g + 1) * SSD_STATE]
        bg_m, cg_m = bg.astype(MXU_DTYPE), cg.astype(MXU_DTYPE)
        bgt_m = bg.T.astype(MXU_DTYPE)
        scores = _dot_nt(cg_m, bg_m)
        for pp in range(ppg):
            p = g * ppg + pp
            sl = slice(p * LANES, (p + 1) * LANES)
            ys = []
            for hh in range(2):
                h = 2 * p + hh
                diff = cs[:, h:h + 1] - cst[h:h + 1, :]
                decay = jnp.exp(jnp.where(mask, diff, -jnp.inf))
                ys.append(_dot((scores * decay).astype(MXU_DTYPE), xdt_m[:, sl]))
            hp = h_ref[p]
            y_off = _dot(cg_m, hp.astype(MXU_DTYPE)) * e_e[:, sl]
            y_ref[:, sl] = jnp.where(first, ys[0], ys[1]) + y_off
            h_ref[p] = dec_e[:, sl] * hp + _dot(bgt_m, xw_m[:, sl])

    @pl.when(c == n_c - 1)
    def _():
        hout_ref[...] = h_ref[...]


def _ssd(xbc, dt_raw, dt_bias, a_log, h0):
    n_rows = xbc.shape[0]
    q = SSD_CHUNK
    n_c = n_rows // q
    dt = dt_raw.reshape(n_rows, 2, SSD_HEADS).transpose(1, 0, 2)
    dtt = dt.transpose(0, 2, 1)

    def chunk(d, c):
        return jnp.where(d == 0, c, n_c - 1 - c)

    small = lambda shape: pl.BlockSpec((None,) + shape, lambda d, c: (d, 0, 0))
    return pl.pallas_call(
        _ssd_kernel,
        grid=(2, n_c),
        in_specs=[pl.BlockSpec((q, SSD_XBC), lambda d, c: (chunk(d, c), 0)),
                  pl.BlockSpec((None, q, SSD_HEADS), lambda d, c: (d, chunk(d, c), 0)),
                  pl.BlockSpec((None, SSD_HEADS, q), lambda d, c: (d, 0, chunk(d, c))),
                  small((1, SSD_HEADS)), small((SSD_HEADS, 1)), small((1, SSD_HEADS)), small((SSD_HEADS, 1)),
                  pl.BlockSpec((None, SSD_PAIRS, SSD_STATE, LANES), lambda d, c: (d, 0, 0, 0))],
        out_specs=[pl.BlockSpec((None, q, SSD_D_INNER), lambda d, c: (d, chunk(d, c), 0)),
                   pl.BlockSpec((None, SSD_PAIRS, SSD_STATE, LANES), lambda d, c: (d, 0, 0, 0))],
        out_shape=[jax.ShapeDtypeStruct((2, n_rows, SSD_D_INNER), F32),
                   jax.ShapeDtypeStruct((2, SSD_PAIRS, SSD_STATE, LANES), F32)],
        scratch_shapes=[pltpu.VMEM((SSD_PAIRS, SSD_STATE, LANES), F32)],
        compiler_params=_params("arbitrary", "arbitrary"),
        name="ssd_scan",
    )(xbc, dt, dtt, dt_bias[:, None, :], dt_bias[:, :, None], a_log[:, None, :], a_log[:, :, None], h0)


def _ssd_out_kernel(y_ref, xbc_ref, z_ref, dsk_ref, nw_ref, o_ref):
    y = y_ref[0] + y_ref[1] + dsk_ref[...] * xbc_ref[...]
    g = y * _silu(z_ref[...])
    ms = jnp.mean(g * g, axis=-1, keepdims=True)
    o_ref[...] = (g * lax.rsqrt(ms + EPS) * nw_ref[...]).astype(o_ref.dtype)


def _ssd_out(y2, xbc, zx, d_skip, norm_w):
    n_rows = xbc.shape[0]
    tl = _pick(n_rows, (512, 256, 128))
    w = SSD_D_INNER
    return pl.pallas_call(
        _ssd_out_kernel,
        grid=(n_rows // tl,),
        in_specs=[pl.BlockSpec((2, tl, w), lambda i: (0, i, 0)),
                  pl.BlockSpec((tl, w), lambda i: (i, 0)),
                  pl.BlockSpec((tl, w), lambda i: (i, 0)),
                  pl.BlockSpec((1, w), lambda i: (0, 0)),
                  pl.BlockSpec((1, w), lambda i: (0, 0))],
        out_specs=pl.BlockSpec((tl, w), lambda i: (i, 0)),
        out_shape=jax.ShapeDtypeStruct((n_rows, w), MXU_DTYPE),
        compiler_params=_params("parallel"),
        name="ssd_out",
    )(y2, xbc, zx, jnp.repeat(d_skip, SSD_HEAD_DIM)[None, :], norm_w[None, :])


def _head_norm(x, w, scale):
    blk = (lax.broadcasted_iota(jnp.int32, (LANES, LANES), 0) // NA_HEAD_DIM
           == lax.broadcasted_iota(jnp.int32, (LANES, LANES), 1) // NA_HEAD_DIM).astype(F32)
    ms = _dot(x * x, blk, HI) * (1.0 / NA_HEAD_DIM)
    return x * lax.rsqrt(ms + EPS) * (w * scale)


def _qkv_prep_kernel(q_ref, k_ref, v_ref, qw_ref, kw_ref, qo_ref, ko_ref, vo_ref):
    qo_ref[...] = _head_norm(q_ref[...], qw_ref[...], NA_HEAD_DIM ** -0.5).astype(qo_ref.dtype)
    ko_ref[...] = _head_norm(k_ref[...], kw_ref[...], 1.0).astype(ko_ref.dtype)
    vo_ref[...] = v_ref[...].astype(vo_ref.dtype)


def _qkv_prep(qkvd, q_norm_w, k_norm_w):
    n_rows = qkvd.shape[0]
    tl = _pick(n_rows, (512, 256, 128))
    nb = NA_D // LANES
    tile = lambda off: pl.BlockSpec((tl, LANES), lambda i, j: (i, j + off))
    wspec = pl.BlockSpec((1, LANES), lambda i, j: (0, 0))
    out = jax.ShapeDtypeStruct((n_rows, NA_D), MXU_DTYPE)
    return pl.pallas_call(
        _qkv_prep_kernel,
        grid=(n_rows // tl, nb),
        in_specs=[tile(0), tile(nb), tile(2 * nb), wspec, wspec],
        out_specs=[tile(0), tile(0), tile(0)],
        out_shape=[out, out, out],
        compiler_params=_params("parallel", "parallel"),
        name="qkv_prep",
    )(qkvd, qkvd, qkvd, jnp.tile(q_norm_w, 2)[None, :], jnp.tile(k_norm_w, 2)[None, :])


def _attend(q2, parts):
    first = lax.broadcasted_iota(jnp.int32, q2.shape, 1) < NA_HEAD_DIM
    outs = []
    for head_mask in (first, jnp.logical_not(first)):
        qa = jnp.where(head_mask, q2, jnp.zeros_like(q2))
        scores = []
        for k, _, bias in parts:
            s = _dot_nt(qa, k)
            scores.append(s if bias is None else s + bias)
        m = functools.reduce(jnp.maximum, [jnp.max(s, axis=-1, keepdims=True) for s in scores])
        probs = [jnp.exp(s - m) for s in scores]
        denom = functools.reduce(jnp.add, [jnp.sum(p, axis=-1, keepdims=True) for p in probs])
        acc = functools.reduce(jnp.add, [_dot(p.astype(MXU_DTYPE), v) for p, (_, v, _) in zip(probs, parts)])
        outs.append(acc / denom)
    return jnp.where(first, outs[0], outs[1])


def _natten_kernel(q_ref, k_ref, v_ref, kc_ref, vc_ref, tbl_ref, o_ref, *, rb, rows):
    i = pl.program_id(1)
    kc, vc = kc_ref[...], vc_ref[...]
    n_win = NA_WIN_ROWS * GRID_W
    first = lax.broadcasted_iota(jnp.int32, (GRID_W, LANES), 1) < NA_HEAD_DIM
    for t in range(rb):
        r = i * rb + t
        start = jnp.clip(r - NA_WIN_ROWS // 2, 0, rows - NA_WIN_ROWS)
        dr0 = start - r + (NA_WIN_ROWS - 1)
        ks = pl.ds(pl.multiple_of(start * GRID_W, GRID_W), n_win)
        q2 = q_ref[t * GRID_W:(t + 1) * GRID_W, :]
        kw, vw = k_ref[ks, :], v_ref[ks, :]
        zero = jnp.zeros_like(q2)
        qs = jnp.concatenate([jnp.where(first, q2, zero), jnp.where(first, zero, q2)], axis=0)
        s_loc = _dot_nt(qs, kw) + tbl_ref[dr0]
        s_ctx = _dot_nt(qs, kc)
        m = jnp.maximum(jnp.max(s_loc, axis=-1, keepdims=True), jnp.max(s_ctx, axis=-1, keepdims=True))
        p_loc, p_ctx = jnp.exp(s_loc - m), jnp.exp(s_ctx - m)
        denom = jnp.sum(p_loc, axis=-1, keepdims=True) + jnp.sum(p_ctx, axis=-1, keepdims=True)
        o = (_dot(p_loc.astype(MXU_DTYPE), vw) + _dot(p_ctx.astype(MXU_DTYPE), vc)) / denom
        o_ref[t * GRID_W:(t + 1) * GRID_W, :] = jnp.where(first, o[:GRID_W], o[GRID_W:]).astype(o_ref.dtype)


def _bias_table(rpb):
    col = jnp.arange(GRID_W)
    c0 = jnp.clip(col - NA_WIN_COLS // 2, 0, GRID_W - NA_WIN_COLS)
    col_in = (col[None, :] >= c0[:, None]) & (col[None, :] < c0[:, None] + NA_WIN_COLS)
    dc = jnp.clip(col[None, :] - col[:, None], 1 - NA_WIN_COLS, NA_WIN_COLS - 1) + (NA_WIN_COLS - 1)
    t = jnp.where(col_in, rpb.astype(F32)[:, :, dc], NEG)
    dr = jnp.arange(NA_WIN_ROWS)[:, None] + jnp.arange(NA_WIN_ROWS)[None, :]
    t = t[:, dr]
    t = t.reshape(NA_HEADS // 2, 2, NA_WIN_ROWS, NA_WIN_ROWS, GRID_W, GRID_W)
    return t.transpose(0, 2, 1, 4, 3, 5).reshape(NA_HEADS // 2, NA_WIN_ROWS, 2 * GRID_W, NA_WIN_ROWS * GRID_W)


def _natten(qn, kn, vb, kc, vc, rpb):
    n_rows = qn.shape[0]
    rows = n_rows // GRID_W
    n_ctx = kc.shape[0]
    rb = _pick(rows, (4, 2, 1))
    n_win = NA_WIN_ROWS * GRID_W
    seq = lambda n: pl.BlockSpec((n, LANES), lambda p, i: (0, p))
    tile = pl.BlockSpec((rb * GRID_W, LANES), lambda p, i: (i, p))
    return pl.pallas_call(
        functools.partial(_natten_kernel, rb=rb, rows=rows),
        grid=(NA_HEADS // 2, rows // rb),
        in_specs=[tile, seq(n_rows), seq(n_rows), seq(n_ctx), seq(n_ctx),
                  pl.BlockSpec((None, NA_WIN_ROWS, 2 * GRID_W, n_win), lambda p, i: (p, 0, 0, 0))],
        out_specs=tile,
        out_shape=jax.ShapeDtypeStruct((n_rows, NA_D), MXU_DTYPE),
        compiler_params=_params("parallel", "arbitrary"),
        name="natten",
    )(qn, kn, vb, kc, vc, _bias_table(rpb))


def _ctx_attn_kernel(q_ref, k_ref, v_ref, o_ref):
    o_ref[...] = _attend(q_ref[...], [(k_ref[...], v_ref[...], None)]).astype(o_ref.dtype)


def _ctx_attn(qn, kn, vb):
    n_ctx = qn.shape[0]
    spec = pl.BlockSpec((n_ctx, LANES), lambda p: (0, p))
    return pl.pallas_call(
        _ctx_attn_kernel,
        grid=(NA_HEADS // 2,),
        in_specs=[spec, spec, spec],
        out_specs=spec,
        out_shape=jax.ShapeDtypeStruct((n_ctx, NA_D), MXU_DTYPE),
        compiler_params=_params("parallel"),
        name="ctx_attn",
    )(qn, kn, vb)


def _load_slabs(ref, n, s):
    return jnp.concatenate([ref[pl.ds(c, n, stride=s), :] for c in range(s)], axis=1)


def _store_slabs(ref, val):
    n = val.shape[0]
    s = val.shape[1] // LANES
    for c in range(s):
        ref[pl.ds(c, n, stride=s), :] = val[:, c * LANES:(c + 1) * LANES].astype(ref.dtype)

def _router_kernel(x_ref, nw_ref, sh_ref, sc_ref, wr_ref, m_ref, e_ref, g_ref):
    m = _norm_mod(x_ref[...], nw_ref[...], sh_ref[...], sc_ref[...])
    _store_slabs(m_ref, m)
    logits = _dot(m, wr_ref[...], HI)
    lane = lax.broadcasted_iota(jnp.int32, logits.shape, 1)
    big = jnp.int32(LANES)

    def top(vals):
        v = jnp.max(vals, axis=-1, keepdims=True)
        idx = jnp.min(jnp.where(vals == v, lane, big), axis=-1, keepdims=True)
        return v, idx

    gl = jnp.where(lane < N_GROUPS, logits, -jnp.inf)
    g_max, grp = top(gl)
    p_grp = 1.0 / jnp.sum(jnp.exp(gl - g_max), axis=-1, keepdims=True)
    e_lane = lane - N_GROUPS
    in_grp = (e_lane >= grp * EXPERTS_PER_GROUP) & (e_lane < (grp + 1) * EXPERTS_PER_GROUP)
    el = jnp.where(in_grp, logits, -jnp.inf)
    v1, i1 = top(el)
    v2, i2 = top(jnp.where(lane == i1, -jnp.inf, el))
    t = jnp.exp(v2 - v1)
    g1 = p_grp / (1.0 + t)
    g2 = p_grp * t / (1.0 + t)
    e_ref[...] = jnp.where(lane == 0, i1 - N_GROUPS, jnp.where(lane == 1, i2 - N_GROUPS, 0))
    g_ref[...] = jnp.where(lane == 0, g1, jnp.where(lane == 1, g2, 0.0))


def _router(x, nw, sh, sc, w_route):
    m_rows, d = x.shape
    tm = _pick(m_rows, (512, 256, 128))
    wide = pl.BlockSpec((tm, LANES), lambda i: (i, 0))
    m, e, g = pl.pallas_call(
        _router_kernel,
        grid=(m_rows // tm,),
        in_specs=[pl.BlockSpec((tm, d), lambda i: (i, 0)),
                  _vec_spec(nw, d, _zero), _vec_spec(sh, d, _zero), _vec_spec(sc, d, _zero),
                  pl.BlockSpec((d, LANES), lambda i: (0, 0))],
        out_specs=[pl.BlockSpec((tm * (d // LANES), LANES), lambda i: (i, 0)), wide, wide],
        out_shape=[jax.ShapeDtypeStruct((m_rows * (d // LANES), LANES), F32),
                   jax.ShapeDtypeStruct((m_rows, LANES), jnp.int32),
                   jax.ShapeDtypeStruct((m_rows, LANES), F32)],
        compiler_params=_params("parallel"),
        name="router",
    )(x, nw[0], sh[0], sc[0], w_route)
    return m, e[:, :TOP_K], g


def _moe_kernel(first_ref, nblk_ref, tok_ref, nact_ref, m_hbm, wg_ref, wu_ref, wd_ref, yb_hbm,
                xbuf, ybuf, gsem, osem, wgb_ref, wub_ref, wdb_ref):
    e, n_e = pl.program_id(0), pl.num_programs(0)
    n_act = nact_ref[0]
    s = wg_ref.shape[0] // LANES
    blk_rows = MOE_BLOCK * s
    n_blocks = yb_hbm.shape[0] // blk_rows

    def gather_start(blk, slot):
        for r in range(MOE_BLOCK):
            tok = tok_ref[blk * MOE_BLOCK + r]
            pltpu.make_async_copy(m_hbm.at[pl.ds(pl.multiple_of(tok * s, s), s), :],
                                  xbuf.at[slot, pl.ds(r * s, s), :],
                                  gsem.at[slot]).start(priority=ROW_DMA_PRIORITY)

    def gather_wait(slot):
        pltpu.make_async_copy(m_hbm.at[pl.ds(0, blk_rows), :], xbuf.at[slot], gsem.at[slot]).wait()

    def out_copy(blk, slot):
        rows = pl.ds(pl.multiple_of(blk * blk_rows, blk_rows), blk_rows)
        return pltpu.make_async_copy(ybuf.at[slot], yb_hbm.at[rows, :], osem.at[slot])

    @pl.when(e == 0)
    def _():
        for ahead in range(MOE_AHEAD):
            gather_start(jnp.minimum(ahead, n_act - 1), ahead)

    @pl.when(nblk_ref[e] > 0)
    def _():
        wgb_ref[...] = wg_ref[...].astype(wgb_ref.dtype)
        wub_ref[...] = wu_ref[...].astype(wub_ref.dtype)
        wdb_ref[...] = wd_ref[...].astype(wdb_ref.dtype)

    def block(j, carry):
        b = first_ref[e] + j
        slot = b % MOE_SLOTS
        gather_wait(slot)

        @pl.when(b >= MOE_SLOTS)
        def _():
            out_copy(b - MOE_SLOTS, slot).wait()

        gather_start(jnp.minimum(b + MOE_AHEAD, n_act - 1), (b + MOE_AHEAD) % MOE_SLOTS)
        x = _load_slabs(xbuf.at[slot], MOE_BLOCK, s).astype(MXU_DTYPE)
        h = _silu(_dot(x, wgb_ref[...])) * _dot(x, wub_ref[...])
        _store_slabs(ybuf.at[slot], _dot(h.astype(MXU_DTYPE), wdb_ref[...]))
        out_copy(b, slot).start()
        return carry

    lax.fori_loop(0, nblk_ref[e], block, 0)

    @pl.when(e == n_e - 1)
    def _():
        for ahead in range(MOE_AHEAD):
            gather_wait((n_act + ahead) % MOE_SLOTS)
        for back in range(1, MOE_SLOTS + 1):
            @pl.when(n_act >= back)
            def _():
                out_copy(n_act - back, (n_act - back) % MOE_SLOTS).wait()

        ybuf[0] = jnp.zeros(ybuf.shape[1:], ybuf.dtype)

        def fill(b, carry):
            out_copy(b, 0).start()
            out_copy(b, 0).wait()
            return carry

        lax.fori_loop(n_act, n_blocks, fill, 0)


def _moe_ffn(m_tok, slot_tok, first_blk, n_blk, n_act, w_gate, w_up, w_down, layer):
    n_rows = slot_tok.shape[0]
    d, f = w_gate.shape[-2:]
    blk_rows = MOE_BLOCK * (d // LANES)
    wspec = lambda r, c: pl.BlockSpec((None, None, r, c), lambda e, *_: (layer, e, 0, 0))
    grid_spec = pltpu.PrefetchScalarGridSpec(
        num_scalar_prefetch=4,
        grid=(N_EXPERTS,),
        in_specs=[pl.BlockSpec(memory_space=pl.ANY), wspec(d, f), wspec(d, f), wspec(f, d)],
        out_specs=pl.BlockSpec(memory_space=pl.ANY),
        scratch_shapes=[pltpu.VMEM((MOE_SLOTS, blk_rows, LANES), F32), pltpu.VMEM((MOE_SLOTS, blk_rows, LANES), F32),
                        pltpu.SemaphoreType.DMA((MOE_SLOTS,)), pltpu.SemaphoreType.DMA((MOE_SLOTS,)),
                        pltpu.VMEM((d, f), MXU_DTYPE), pltpu.VMEM((d, f), MXU_DTYPE), pltpu.VMEM((f, d), MXU_DTYPE)],
    )
    return pl.pallas_call(
        _moe_kernel,
        grid_spec=grid_spec,
        out_shape=jax.ShapeDtypeStruct((n_rows * (d // LANES), LANES), F32),
        compiler_params=_params("arbitrary"),
        name="moe_ffn",
    )(first_blk, n_blk, slot_tok, n_act, m_tok, w_gate, w_up, w_down)


def _combine_kernel(pos_ref, x_ref, g_ref, gt_ref, yb_hbm, o_ref, buf, sem, *, tm):
    i, n_i = pl.program_id(0), pl.num_programs(0)
    s = x_ref.shape[1] // LANES

    def gather_start(tile, slot):
        for j in range(tm):
            for k in range(TOP_K):
                p = pos_ref[(tile * tm + j) * TOP_K + k]
                pltpu.make_async_copy(yb_hbm.at[pl.ds(pl.multiple_of(p * s, s), s), :],
                                      buf.at[slot, k, pl.ds(j * s, s), :],
                                      sem.at[slot]).start(priority=ROW_DMA_PRIORITY)

    def gather_wait(slot):
        for k in range(TOP_K):
            pltpu.make_async_copy(yb_hbm.at[pl.ds(0, tm * s), :], buf.at[slot, k], sem.at[slot]).wait()

    @pl.when(i == 0)
    def _():
        for ahead in range(MOE_AHEAD):
            gather_start(jnp.minimum(ahead, n_i - 1), ahead)

    slot = i % MOE_SLOTS
    gather_wait(slot)
    gather_start(jnp.minimum(i + MOE_AHEAD, n_i - 1), (i + MOE_AHEAD) % MOE_SLOTS)
    gt = gt_ref[...]
    y0 = _load_slabs(buf.at[slot, 0], tm, s)
    y1 = _load_slabs(buf.at[slot, 1], tm, s)
    o_ref[...] = x_ref[...] + g_ref[...] * (gt[:, 0:1] * y0 + gt[:, 1:2] * y1)

    @pl.when(i == n_i - 1)
    def _():
        for ahead in range(1, MOE_AHEAD + 1):
            gather_wait((i + ahead) % MOE_SLOTS)


def _combine(x, g, gates, yb, pos):
    m, d = x.shape
    tm = MOE_BLOCK
    grid_spec = pltpu.PrefetchScalarGridSpec(
        num_scalar_prefetch=1,
        grid=(m // tm,),
        in_specs=[pl.BlockSpec((tm, d), lambda i, pos: (i, 0)),
                  _vec_spec(g, d, _zero),
                  pl.BlockSpec((tm, LANES), lambda i, pos: (i, 0)),
                  pl.BlockSpec(memory_space=pl.ANY)],
        out_specs=pl.BlockSpec((tm, d), lambda i, pos: (i, 0)),
        scratch_shapes=[pltpu.VMEM((MOE_SLOTS, TOP_K, tm * (d // LANES), LANES), F32),
                        pltpu.SemaphoreType.DMA((MOE_SLOTS,))],
    )
    return pl.pallas_call(
        functools.partial(_combine_kernel, tm=tm),
        grid_spec=grid_spec,
        out_shape=jax.ShapeDtypeStruct((m, d), F32),
        compiler_params=_params("arbitrary"),
        name="moe_combine",
    )(pos, x, g[0], gates, yb)


def _dispatch(expert):
    n_tok = expert.shape[0]
    n = n_tok * TOP_K
    n_blocks = -(-n // MOE_BLOCK) + N_EXPERTS
    n_rows = n_blocks * MOE_BLOCK
    flat_e = expert.reshape(-1)
    onehot = (flat_e[:, None] == jnp.arange(N_EXPERTS, dtype=jnp.int32)[None, :]).astype(jnp.int32)
    csum = jnp.cumsum(onehot, axis=0)
    rank = jnp.take_along_axis(csum, flat_e[:, None], axis=1)[:, 0] - 1
    counts = csum[-1]
    padded = (counts + MOE_BLOCK - 1) // MOE_BLOCK * MOE_BLOCK
    pad_end = jnp.cumsum(padded)
    dest = (pad_end - padded)[flat_e] + rank
    pair_tok = jnp.arange(n, dtype=jnp.int32) // TOP_K
    slot_tok = jnp.zeros((n_rows,), jnp.int32).at[dest].set(pair_tok)
    first_blk = ((pad_end - padded) // MOE_BLOCK).astype(jnp.int32)
    n_blk = (padded // MOE_BLOCK).astype(jnp.int32)
    n_act = (pad_end[-1:] // MOE_BLOCK).astype(jnp.int32)
    return dest.astype(jnp.int32), slot_tok, first_blk, n_blk, n_act


def _even_mixer(lat, cx, nw, mod_lat, mod_ctx, e, w_in, conv_w, conv_b, dt_bias, a_log, d_skip, ssd_norm_w,
                q_norm_w, k_norm_w, rpb, w_out, ctx_out):
    w_e = w_in[e]
    w_qkvd = jnp.concatenate([w_e[:, S_DT:], w_e[:, S_ZX:S_DT],
                              jnp.zeros((w_e.shape[0], LANES - 2 * SSD_HEADS), w_e.dtype)], axis=1)[None]

    def project(x, mod, tag):
        sh, sc = mod[0], mod[1]
        zx = _nm_matmul(x, nw, sh, sc, w_in, e, S_ZX, "proj_zx_" + tag)
        qkvd = _nm_matmul(x, nw, sh, sc, w_qkvd, 0, QKVD, "proj_qkvd_" + tag)
        xbc = _xbc_conv(zx, conv_w[e], conv_b[e])
        qn, kn, vb = _qkv_prep(qkvd, q_norm_w[e], k_norm_w[e])
        return zx, xbc, qkvd[:, 3 * NA_D:3 * NA_D + 2 * SSD_HEADS], qn, kn, vb

    zx_c, xbc_c, dt_c, qn_c, kn_c, vb_c = project(cx, mod_ctx, "ctx")
    zx_l, xbc_l, dt_l, qn_l, kn_l, vb_l = project(lat, mod_lat, "lat")
    h0 = jnp.zeros((2, SSD_PAIRS, SSD_STATE, LANES), F32)
    y_c, h_ctx = _ssd(xbc_c, dt_c, dt_bias[e], a_log[e], h0)
    y_l, _ = _ssd(xbc_l, dt_l, dt_bias[e], a_log[e], h_ctx)
    yssd_l = _ssd_out(y_l, xbc_l, zx_l, d_skip[e], ssd_norm_w[e])
    yatt_l = _natten(qn_l, kn_l, vb_l, kn_c, vb_c, rpb[e])
    lat = _matmul_res([yssd_l, yatt_l], w_out, e, lat, mod_lat[2], "out_even_lat")
    if ctx_out:
        yssd_c = _ssd_out(y_c, xbc_c, zx_c, d_skip[e], ssd_norm_w[e])
        yatt_c = _ctx_attn(qn_c, kn_c, vb_c)
        cx = _matmul_res([yssd_c, yatt_c], w_out, e, cx, mod_ctx[2], "out_even_ctx")
    return lat, cx


def _odd_mixer(lat, cx, nw, mod_lat, mod_ctx, o, w_in, conv_w, w_out, ctx_out):
    def mix(x, mod, tag):
        h3 = _nm_matmul(x, nw, mod[0], mod[1], w_in, o, w_in.shape[2], "proj_odd_" + tag)
        return _matmul_res([_gate_conv(h3, conv_w[o])], w_out, o, x, mod[2], "out_odd_" + tag)

    lat = mix(lat, mod_lat, "lat")
    if ctx_out:
        cx = mix(cx, mod_ctx, "ctx")
    return lat, cx


def kernel(x, c, ctx, c_ctx, w_mod, b_mod, norm_mix_w, norm_ffn_w, ev_w_in, ev_conv_w, ev_conv_b, ev_dt_bias,
           ev_a_log, ev_d_skip, ev_ssd_norm_w, ev_q_norm_w, ev_k_norm_w, ev_rpb, ev_w_out, od_w_in, od_conv_w,
           od_w_out, moe_w_group, moe_w_router, moe_w_gate, moe_w_up, moe_w_down):
    bsz, n_lat, d = x.shape
    assert bsz == 1 and ctx.shape[0] == 1
    assert n_lat % GRID_W == 0 and n_lat // GRID_W >= NA_WIN_ROWS and n_lat % SSD_CHUNK == 0
    assert ctx.shape[1] % SSD_CHUNK == 0 and d % LANES == 0
    depth = w_mod.shape[0]
    lat, cx = x[0], ctx[0]
    mods = _modulation(jnp.stack([c[0], c_ctx], axis=1), w_mod, b_mod)
    mods = mods.reshape(depth * 2 * 6, d)

    for l in range(depth):
        even = l % 2 == 0
        ctx_out = any(j % 2 == 0 for j in range(l + 1, depth))
        mod_lat = [_vec(mods, (l * 2 + 0) * 6 + i) for i in range(6)]
        mod_ctx = [_vec(mods, (l * 2 + 1) * 6 + i) for i in range(6)]
        nw = _vec(norm_mix_w, l)
        if even:
            lat, cx = _even_mixer(lat, cx, nw, mod_lat, mod_ctx, l // 2, ev_w_in, ev_conv_w, ev_conv_b,
                                  ev_dt_bias, ev_a_log, ev_d_skip, ev_ssd_norm_w, ev_q_norm_w,
                                  ev_k_norm_w, ev_rpb, ev_w_out, ctx_out)
        else:
            lat, cx = _odd_mixer(lat, cx, nw, mod_lat, mod_ctx, l // 2, od_w_in, od_conv_w, od_w_out, ctx_out)

        nfw = _vec(norm_ffn_w, l)
        w_route = jnp.concatenate([moe_w_group[l], moe_w_router[l],
                                   jnp.zeros((d, LANES - N_GROUPS - N_EXPERTS), F32)], axis=1)
        m_tok, expert, gates = _router(lat, nfw, mod_lat[3], mod_lat[4], w_route)
        if ctx_out:
            m_c, e_c, g_c = _router(cx, nfw, mod_ctx[3], mod_ctx[4], w_route)
            m_tok = jnp.concatenate([m_tok, m_c], axis=0)
            expert = jnp.concatenate([expert, e_c], axis=0)
            gates = jnp.concatenate([gates, g_c], axis=0)
        dest, slot_tok, first_blk, n_blk, n_act = _dispatch(expert)
        yb = _moe_ffn(m_tok, slot_tok, first_blk, n_blk, n_act, moe_w_gate, moe_w_up, moe_w_down, l)
        lat = _combine(lat, mod_lat[5], gates[:n_lat], yb, dest[:n_lat * TOP_K])
        if ctx_out:
            cx = _combine(cx, mod_ctx[5], gates[n_lat:], yb, dest[n_lat * TOP_K:])
    return lat[None]
```

```python
import functools

import jax
import jax.numpy as jnp
from jax import lax
from jax.experimental import pallas as pl
from jax.experimental.pallas import tpu as pltpu

F32 = jnp.float32
MXU_DTYPE = jnp.bfloat16
HI = lax.Precision.HIGHEST
EPS = 1e-6
NEG = -1e30

LANES = 128
SUBLANES = 8
VMEM_LIMIT = 48 * 1024 * 1024
ROW_DMA_PRIORITY = 1

SSD_HEADS = 16
SSD_HEAD_DIM = 64
SSD_D_INNER = SSD_HEADS * SSD_HEAD_DIM
SSD_STATE = 128
SSD_GROUPS = 2
SSD_CHUNK = 128
SSD_BC = SSD_GROUPS * SSD_STATE
SSD_XBC = SSD_D_INNER + 2 * SSD_BC
SSD_PAIRS = SSD_HEADS // 2
NA_HEADS = 16
NA_HEAD_DIM = 64
NA_D = NA_HEADS * NA_HEAD_DIM
NA_WIN_ROWS = 8
NA_WIN_COLS = 16
GRID_W = 64
N_GROUPS = 8
EXPERTS_PER_GROUP = 8
N_EXPERTS = N_GROUPS * EXPERTS_PER_GROUP
TOP_K = 2
D_FF_EXPERT = 384
MOE_BLOCK = 128
MOE_AHEAD = 2
MOE_SLOTS = MOE_AHEAD + 1
S_ZX = SSD_D_INNER + SSD_XBC
S_DT = S_ZX + 2 * SSD_HEADS
QKVD = 3 * NA_D + LANES


def _pick(n, prefs):
    for p in prefs:
        if n % p == 0:
            return p
    return n


def _params(*sem):
    return pltpu.CompilerParams(dimension_semantics=sem, vmem_limit_bytes=VMEM_LIMIT)


def _vec(arr, idx):
    return arr.reshape(arr.shape[0], 1, arr.shape[-1]), idx


def _vec_spec(vec, width, col):
    idx = vec[1]
    return pl.BlockSpec((None, 1, width), lambda *g: (idx, 0, col(*g)))


def _mat_spec(layer, block, idx):
    return pl.BlockSpec((None,) + block, lambda *g: (layer,) + idx(*g))


def _zero(*g):
    return 0


def _silu(x):
    return x * (1.0 / (1.0 + jnp.exp(-x)))


def _softplus(x):
    return jnp.maximum(x, 0.0) + jnp.log1p(jnp.exp(-jnp.abs(x)))


def _dot(a, b, precision=None):
    return jnp.dot(a, b, preferred_element_type=F32, precision=precision)


def _dot_nt(a, b):
    return lax.dot_general(a, b, (((1,), (1,)), ((), ())), preferred_element_type=F32)


def _norm_mod(x, nw, sh, sc):
    ms = jnp.mean(x * x, axis=-1, keepdims=True)
    n = x * lax.rsqrt(ms + EPS) * nw
    return n * (1.0 + sc) + sh


def _mod_kernel(ct_ref, w_ref, b_ref, o_ref):
    ct = ct_ref[...]
    s = _silu(ct)
    w = w_ref[...]
    r0 = jnp.sum(s[:, 0:1] * w, axis=0, keepdims=True)
    r1 = jnp.sum(s[:, 1:2] * w, axis=0, keepdims=True)
    o_ref[...] = jnp.concatenate([r0, r1], axis=0) + b_ref[...]


def _modulation(ct, w_mod, b_mod):
    depth, d, n = w_mod.shape
    tn = _pick(n, (512, 256, 128))
    return pl.pallas_call(
        _mod_kernel,
        grid=(depth, n // tn),
        in_specs=[pl.BlockSpec((d, 2), lambda l, j: (0, 0)),
                  pl.BlockSpec((None, d, tn), lambda l, j: (l, 0, j)),
                  pl.BlockSpec((None, 1, tn), lambda l, j: (l, 0, j))],
        out_specs=pl.BlockSpec((None, 2, tn), lambda l, j: (l, 0, j)),
        out_shape=jax.ShapeDtypeStruct((depth, 2, n), F32),
        compiler_params=_params("parallel", "parallel"),
        name="modulation",
    )(ct, w_mod, b_mod.reshape(depth, 1, n))


def _nm_matmul_kernel(x_ref, nw_ref, sh_ref, sc_ref, w_ref, o_ref, a_ref):
    @pl.when(pl.program_id(1) == 0)
    def _():
        a_ref[...] = _norm_mod(x_ref[...], nw_ref[...], sh_ref[...], sc_ref[...]).astype(a_ref.dtype)

    o_ref[...] = _dot(a_ref[...], w_ref[...].astype(a_ref.dtype)).astype(o_ref.dtype)


def _nm_matmul(x, nw, sh, sc, w, layer, n_cols, name):
    m, d = x.shape
    tm = _pick(m, (1024, 512, 256, 128))
    tn = _pick(n_cols, (512, 640, 256, 128))
    return pl.pallas_call(
        _nm_matmul_kernel,
        grid=(m // tm, n_cols // tn),
        in_specs=[pl.BlockSpec((tm, d), lambda i, j: (i, 0)),
                  _vec_spec(nw, d, _zero), _vec_spec(sh, d, _zero), _vec_spec(sc, d, _zero),
                  _mat_spec(layer, (d, tn), lambda i, j: (0, j))],
        out_specs=pl.BlockSpec((tm, tn), lambda i, j: (i, j)),
        out_shape=jax.ShapeDtypeStruct((m, n_cols), F32),
        scratch_shapes=[pltpu.VMEM((tm, d), MXU_DTYPE)],
        compiler_params=_params("parallel", "arbitrary"),
        name=name,
    )(x, nw[0], sh[0], sc[0], w)


def _matmul_res_kernel(*refs, n_a):
    a_refs, w_refs = refs[:n_a], refs[n_a:2 * n_a]
    res_ref, g_ref, o_ref = refs[2 * n_a:]
    acc = None
    for a_ref, w_ref in zip(a_refs, w_refs):
        t = _dot(a_ref[...].astype(MXU_DTYPE), w_ref[...].astype(MXU_DTYPE))
        acc = t if acc is None else acc + t
    o_ref[...] = res_ref[...] + g_ref[...] * acc


def _matmul_res(a_list, w, layer, res, g, name):
    n_a = len(a_list)
    m, n = res.shape
    kp = w.shape[1] // n_a
    tm = _pick(m, (1024, 512, 256, 128))
    tn = _pick(n, (512, 256, 128))
    in_specs = [pl.BlockSpec((tm, kp), lambda i, j: (i, 0)) for _ in range(n_a)]
    in_specs += [_mat_spec(layer, (kp, tn), functools.partial(lambda i, j, p: (p, j), p=p)) for p in range(n_a)]
    in_specs += [pl.BlockSpec((tm, tn), lambda i, j: (i, j)), _vec_spec(g, tn, lambda i, j: j)]
    return pl.pallas_call(
        functools.partial(_matmul_res_kernel, n_a=n_a),
        grid=(m // tm, n // tn),
        in_specs=in_specs,
        out_specs=pl.BlockSpec((tm, tn), lambda i, j: (i, j)),
        out_shape=jax.ShapeDtypeStruct((m, n), F32),
        compiler_params=_params("parallel", "parallel"),
        name=name,
    )(*a_list, *([w] * n_a), res, g[0])


def _shifted(x, prev_ref, next_ref):
    i, n_i = pl.program_id(0), pl.num_programs(0)
    tl = x.shape[0]
    row = lax.broadcasted_iota(jnp.int32, x.shape, 0)
    prev_row = jnp.where(i == 0, 0.0, prev_ref[SUBLANES - 1:SUBLANES, :])
    next_row = jnp.where(i == n_i - 1, 0.0, next_ref[0:1, :])
    up = jnp.where(row == 0, prev_row, pltpu.roll(x, 1, 0))
    down = jnp.where(row == tl - 1, next_row, pltpu.roll(x, tl - 1, 0))
    return up, down


def _halo_specs(tl, tc, n_rows, col_off):
    nb = tl // SUBLANES
    last = n_rows // SUBLANES - 1
    return [pl.BlockSpec((tl, tc), lambda i, j: (i, j + col_off)),
            pl.BlockSpec((SUBLANES, tc), lambda i, j: (jnp.maximum(i * nb - 1, 0), j + col_off)),
            pl.BlockSpec((SUBLANES, tc), lambda i, j: (jnp.minimum((i + 1) * nb, last), j + col_off))]


def _xbc_conv_kernel(x_ref, prev_ref, next_ref, w_ref, b_ref, o_ref):
    x = x_ref[...]
    up, down = _shifted(x, prev_ref, next_ref)
    w = w_ref[...]
    o_ref[...] = _silu(w[0:1] * up + w[1:2] * x + w[2:3] * down + b_ref[...])


def _xbc_conv(zx, conv_w, conv_b):
    n_rows = zx.shape[0]
    tl = _pick(n_rows, (512, 256, 128))
    tc = 512
    off = SSD_D_INNER // tc
    return pl.pallas_call(
        _xbc_conv_kernel,
        grid=(n_rows // tl, SSD_XBC // tc),
        in_specs=_halo_specs(tl, tc, n_rows, off) + [pl.BlockSpec((3, tc), lambda i, j: (0, j)),
                                                     pl.BlockSpec((1, tc), lambda i, j: (0, j))],
        out_specs=pl.BlockSpec((tl, tc), lambda i, j: (i, j)),
        out_shape=jax.ShapeDtypeStruct((n_rows, SSD_XBC), F32),
        compiler_params=_params("parallel", "parallel"),
        name="xbc_conv",
    )(zx, zx, zx, conv_w, conv_b.reshape(1, -1))


def _gated_conv_proj_kernel(x_ref, xp_ref, xn_ref, nw_ref, sh_ref, sc_ref, wb_ref, wc_ref, wx_ref, cw_ref, o_ref,
                            a_ref, ah_ref):
    i, n_i = pl.program_id(0), pl.num_programs(0)

    @pl.when(pl.program_id(1) == 0)
    def _():
        nw, sh, sc = nw_ref[...], sh_ref[...], sc_ref[...]
        a_ref[...] = _norm_mod(x_ref[...], nw, sh, sc).astype(a_ref.dtype)
        ah_ref[0:SUBLANES, :] = _norm_mod(xp_ref[...], nw, sh, sc)
        ah_ref[SUBLANES:, :] = _norm_mod(xn_ref[...], nw, sh, sc)

    a, ah = a_ref[...], ah_ref[...].astype(a_ref.dtype)
    wc, wx = wc_ref[...].astype(a.dtype), wx_ref[...].astype(a.dtype)
    u = _dot(a, wc) * _dot(a, wx)
    uh = _dot(ah, wc) * _dot(ah, wx)
    tm = u.shape[0]
    row = lax.broadcasted_iota(jnp.int32, u.shape, 0)
    prev_row = jnp.where(i == 0, 0.0, uh[SUBLANES - 1:SUBLANES, :])
    next_row = jnp.where(i == n_i - 1, 0.0, uh[SUBLANES:SUBLANES + 1, :])
    up = jnp.where(row == 0, prev_row, pltpu.roll(u, 1, 0))
    down = jnp.where(row == tm - 1, next_row, pltpu.roll(u, tm - 1, 0))
    cw = cw_ref[...]
    conv = cw[0:1] * up + cw[1:2] * u + cw[2:3] * down
    o_ref[...] = (_dot(a, wb_ref[...].astype(a.dtype)) * conv).astype(o_ref.dtype)


def _gated_conv_proj(x, nw, sh, sc, w, layer, conv_w, name):
    m, d = x.shape
    c = w.shape[2] // 3
    tm = _pick(m, (1024, 512, 256, 128))
    tc = _pick(c, (256, 128))
    nb = c // tc
    blocks = tm // SUBLANES
    last = m // SUBLANES - 1
    wspec = lambda off: _mat_spec(layer, (d, tc), lambda i, j: (0, j + off))
    return pl.pallas_call(
        _gated_conv_proj_kernel,
        grid=(m // tm, nb),
        in_specs=[pl.BlockSpec((tm, d), lambda i, j: (i, 0)),
                  pl.BlockSpec((SUBLANES, d), lambda i, j: (jnp.maximum(i * blocks - 1, 0), 0)),
                  pl.BlockSpec((SUBLANES, d), lambda i, j: (jnp.minimum((i + 1) * blocks, last), 0)),
                  _vec_spec(nw, d, _zero), _vec_spec(sh, d, _zero), _vec_spec(sc, d, _zero),
                  wspec(0), wspec(nb), wspec(2 * nb),
                  pl.BlockSpec((3, tc), lambda i, j: (0, j))],
        out_specs=pl.BlockSpec((tm, tc), lambda i, j: (i, j)),
        out_shape=jax.ShapeDtypeStruct((m, c), MXU_DTYPE),
        scratch_shapes=[pltpu.VMEM((tm, d), MXU_DTYPE), pltpu.VMEM((2 * SUBLANES, d), F32)],
        compiler_params=_params("parallel", "arbitrary"),
        name=name,
    )(x, x, x, nw[0], sh[0], sc[0], w, w, w, conv_w)


def _ssd_kernel(xbc_ref, dt_ref, dtt_ref, dtb_ref, dtbt_ref, alog_ref, alogt_ref, h0_ref,
                y_ref, hout_ref, h_ref):
    d, c, n_c = pl.program_id(0), pl.program_id(1), pl.num_programs(1)
    q = SSD_CHUNK

    @pl.when(c == 0)
    def _():
        h_ref[...] = h0_ref[...]

    dt = _softplus(dt_ref[...] + dtb_ref[...])
    dtt = _softplus(dtt_ref[...] + dtbt_ref[...])
    a = dt * -jnp.exp(alog_ref[...])
    at = dtt * -jnp.exp(alogt_ref[...])
    row = lax.broadcasted_iota(jnp.int32, (q, q), 0)
    col = lax.broadcasted_iota(jnp.int32, (q, q), 1)
    sign = jnp.where(d == 0, 1, -1)
    mask = (row - col) * sign >= 0
    tri = mask.astype(F32)
    tri_t = ((col - row) * sign >= 0).astype(F32)
    cs = _dot(tri, a, HI)
    cst = _dot(at, tri_t, HI)
    tot = jnp.broadcast_to(jnp.sum(a, axis=0, keepdims=True), (SUBLANES, SSD_HEADS))
    expand = (lax.broadcasted_iota(jnp.int32, (SSD_HEADS, SSD_D_INNER), 1) // SSD_HEAD_DIM
              == lax.broadcasted_iota(jnp.int32, (SSD_HEADS, SSD_D_INNER), 0)).astype(F32)
    dt_e = _dot(dt, expand, HI)
    cs_e = _dot(cs, expand, HI)
    tot_e = _dot(tot, expand, HI)[0:1]

    xs = xbc_ref[:, 0:SSD_D_INNER]
    xdt = xs * dt_e
    xdt_m = xdt.astype(MXU_DTYPE)
    xw_m = (xdt * jnp.exp(tot_e - cs_e)).astype(MXU_DTYPE)
    e_e = jnp.exp(cs_e)
    dec_e = jnp.exp(tot_e)
    first = lax.broadcasted_iota(jnp.int32, (q, LANES), 1) < SSD_HEAD_DIM

    ppg = SSD_PAIRS // SSD_GROUPS
    for g in range(SSD_GROUPS):
        bg = xbc_ref[:, SSD_D_INNER + g * SSD_STATE:SSD_D_INNER + (g + 1) * SSD_STATE]
        cg = xbc_ref[:, SSD_D_INNER + SSD_BC + g * SSD_STATE:SSD_D_INNER + SSD_BC + (g + 1) * SSD_STATE]
        bg_m, cg_m = bg.astype(MXU_DTYPE), cg.astype(MXU_DTYPE)
        bgt_m = bg.T.astype(MXU_DTYPE)
        scores = _dot_nt(cg_m, bg_m)
        for pp in range(ppg):
            p = g * ppg + pp
            sl = slice(p * LANES, (p + 1) * LANES)
            ys = []
            for hh in range(2):
                h = 2 * p + hh
                diff = cs[:, h:h + 1] - cst[h:h + 1, :]
                decay = jnp.exp(jnp.where(mask, diff, -jnp.inf))
                ys.append(_dot((scores * decay).astype(MXU_DTYPE), xdt_m[:, sl]))
            hp = h_ref[p]
            y_off = _dot(cg_m, hp.astype(MXU_DTYPE)) * e_e[:, sl]
            y_ref[:, sl] = jnp.where(first, ys[0], ys[1]) + y_off
            h_ref[p] = dec_e[:, sl] * hp + _dot(bgt_m, xw_m[:, sl])

    @pl.when(c == n_c - 1)
    def _():
        hout_ref[...] = h_ref[...]


def _ssd(xbc, dt_raw, dt_bias, a_log, h0):
    n_rows = xbc.shape[0]
    q = SSD_CHUNK
    n_c = n_rows // q
    dt = dt_raw.reshape(n_rows, 2, SSD_HEADS).transpose(1, 0, 2)
    dtt = dt.transpose(0, 2, 1)

    def chunk(d, c):
        return jnp.where(d == 0, c, n_c - 1 - c)

    small = lambda shape: pl.BlockSpec((None,) + shape, lambda d, c: (d, 0, 0))
    return pl.pallas_call(
        _ssd_kernel,
        grid=(2, n_c),
        in_specs=[pl.BlockSpec((q, SSD_XBC), lambda d, c: (chunk(d, c), 0)),
                  pl.BlockSpec((None, q, SSD_HEADS), lambda d, c: (d, chunk(d, c), 0)),
                  pl.BlockSpec((None, SSD_HEADS, q), lambda d, c: (d, 0, chunk(d, c))),
                  small((1, SSD_HEADS)), small((SSD_HEADS, 1)), small((1, SSD_HEADS)), small((SSD_HEADS, 1)),
                  pl.BlockSpec((None, SSD_PAIRS, SSD_STATE, LANES), lambda d, c: (d, 0, 0, 0))],
        out_specs=[pl.BlockSpec((None, q, SSD_D_INNER), lambda d, c: (d, chunk(d, c), 0)),
                   pl.BlockSpec((None, SSD_PAIRS, SSD_STATE, LANES), lambda d, c: (d, 0, 0, 0))],
        out_shape=[jax.ShapeDtypeStruct((2, n_rows, SSD_D_INNER), F32),
                   jax.ShapeDtypeStruct((2, SSD_PAIRS, SSD_STATE, LANES), F32)],
        scratch_shapes=[pltpu.VMEM((SSD_PAIRS, SSD_STATE, LANES), F32)],
        compiler_params=_params("arbitrary", "arbitrary"),
        name="ssd_scan",
    )(xbc, dt, dtt, dt_bias[:, None, :], dt_bias[:, :, None], a_log[:, None, :], a_log[:, :, None], h0)


def _ssd_out_kernel(y_ref, xbc_ref, z_ref, dsk_ref, nw_ref, o_ref):
    y = y_ref[0] + y_ref[1] + dsk_ref[...] * xbc_ref[...]
    g = y * _silu(z_ref[...])
    ms = jnp.mean(g * g, axis=-1, keepdims=True)
    o_ref[...] = (g * lax.rsqrt(ms + EPS) * nw_ref[...]).astype(o_ref.dtype)


def _ssd_out(y2, xbc, zx, d_skip, norm_w):
    n_rows = xbc.shape[0]
    tl = _pick(n_rows, (512, 256, 128))
    w = SSD_D_INNER
    return pl.pallas_call(
        _ssd_out_kernel,
        grid=(n_rows // tl,),
        in_specs=[pl.BlockSpec((2, tl, w), lambda i: (0, i, 0)),
                  pl.BlockSpec((tl, w), lambda i: (i, 0)),
                  pl.BlockSpec((tl, w), lambda i: (i, 0)),
                  pl.BlockSpec((1, w), lambda i: (0, 0)),
                  pl.BlockSpec((1, w), lambda i: (0, 0))],
        out_specs=pl.BlockSpec((tl, w), lambda i: (i, 0)),
        out_shape=jax.ShapeDtypeStruct((n_rows, w), MXU_DTYPE),
        compiler_params=_params("parallel"),
        name="ssd_out",
    )(y2, xbc, zx, jnp.repeat(d_skip, SSD_HEAD_DIM)[None, :], norm_w[None, :])


def _head_mean_sq(x):
    blk = (lax.broadcasted_iota(jnp.int32, (LANES, LANES), 0) // NA_HEAD_DIM
           == lax.broadcasted_iota(jnp.int32, (LANES, LANES), 1) // NA_HEAD_DIM).astype(MXU_DTYPE)
    x2 = x * x
    hi = x2.astype(MXU_DTYPE)
    lo = (x2 - hi.astype(F32)).astype(MXU_DTYPE)
    return (_dot(hi, blk) + _dot(lo, blk)) * (1.0 / NA_HEAD_DIM)


def _qkv_proj_kernel(x_ref, nw_ref, sh_ref, sc_ref, w_ref, qw_ref, kw_ref, o_ref, dt_ref, a_ref, *, chunks):
    j = pl.program_id(1)

    @pl.when(j == 0)
    def _():
        a_ref[...] = _norm_mod(x_ref[...], nw_ref[...], sh_ref[...], sc_ref[...]).astype(a_ref.dtype)

    acc = _dot(a_ref[...], w_ref[...].astype(a_ref.dtype))
    n_head_chunks = NA_D // LANES
    for c in range(chunks):
        chunk = j * chunks + c
        x = acc[:, c * LANES:(c + 1) * LANES]
        w = jnp.where(chunk < n_head_chunks, qw_ref[...] * NA_HEAD_DIM ** -0.5, kw_ref[...])
        normed = x * lax.rsqrt(_head_mean_sq(x) + EPS) * w
        o_ref[:, c * LANES:(c + 1) * LANES] = jnp.where(chunk < 2 * n_head_chunks, normed, x).astype(o_ref.dtype)

    @pl.when(j == pl.num_programs(1) - 1)
    def _():
        dt_ref[...] = acc[:, (chunks - 1) * LANES:]


def _qkv_proj(x, nw, sh, sc, w_qkvd, q_norm_w, k_norm_w, name):
    m, d = x.shape
    tm = _pick(m, (1024, 512, 256, 128))
    tn = 5 * LANES
    wspec = pl.BlockSpec((1, LANES), lambda i, j: (0, 0))
    return pl.pallas_call(
        functools.partial(_qkv_proj_kernel, chunks=tn // LANES),
        grid=(m // tm, QKVD // tn),
        in_specs=[pl.BlockSpec((tm, d), lambda i, j: (i, 0)),
                  _vec_spec(nw, d, _zero), _vec_spec(sh, d, _zero), _vec_spec(sc, d, _zero),
                  pl.BlockSpec((d, tn), lambda i, j: (0, j)), wspec, wspec],
        out_specs=[pl.BlockSpec((tm, tn), lambda i, j: (i, j)), pl.BlockSpec((tm, LANES), lambda i, j: (i, 0))],
        out_shape=[jax.ShapeDtypeStruct((m, QKVD), MXU_DTYPE), jax.ShapeDtypeStruct((m, LANES), F32)],
        scratch_shapes=[pltpu.VMEM((tm, d), MXU_DTYPE)],
        compiler_params=_params("parallel", "arbitrary"),
        name=name,
    )(x, nw[0], sh[0], sc[0], w_qkvd, jnp.tile(q_norm_w, 2)[None, :], jnp.tile(k_norm_w, 2)[None, :])


def _attend(q2, parts):
    first = lax.broadcasted_iota(jnp.int32, q2.shape, 1) < NA_HEAD_DIM
    outs = []
    for head_mask in (first, jnp.logical_not(first)):
        qa = jnp.where(head_mask, q2, jnp.zeros_like(q2))
        scores = []
        for k, _, bias in parts:
            s = _dot_nt(qa, k)
            scores.append(s if bias is None else s + bias)
        m = functools.reduce(jnp.maximum, [jnp.max(s, axis=-1, keepdims=True) for s in scores])
        probs = [jnp.exp(s - m) for s in scores]
        denom = functools.reduce(jnp.add, [jnp.sum(p, axis=-1, keepdims=True) for p in probs])
        acc = functools.reduce(jnp.add, [_dot(p.astype(MXU_DTYPE), v) for p, (_, v, _) in zip(probs, parts)])
        outs.append(acc / denom)
    return jnp.where(first, outs[0], outs[1])


def _natten_kernel(q_ref, k_ref, v_ref, kc_ref, vc_ref, tbl_ref, o_ref, *, rb, rows):
    i = pl.program_id(1)
    kc, vc = kc_ref[...], vc_ref[...]
    n_win = NA_WIN_ROWS * GRID_W
    first = lax.broadcasted_iota(jnp.int32, (GRID_W, LANES), 1) < NA_HEAD_DIM
    for t in range(rb):
        r = i * rb + t
        start = jnp.clip(r - NA_WIN_ROWS // 2, 0, rows - NA_WIN_ROWS)
        dr0 = start - r + (NA_WIN_ROWS - 1)
        ks = pl.ds(pl.multiple_of(start * GRID_W, GRID_W), n_win)
        q2 = q_ref[t * GRID_W:(t + 1) * GRID_W, :]
        kw, vw = k_ref[ks, :], v_ref[ks, :]
        zero = jnp.zeros_like(q2)
        qs = jnp.concatenate([jnp.where(first, q2, zero), jnp.where(first, zero, q2)], axis=0)
        s_loc = _dot_nt(qs, kw) + tbl_ref[dr0]
        s_ctx = _dot_nt(qs, kc)
        m = jnp.maximum(jnp.max(s_loc, axis=-1, keepdims=True), jnp.max(s_ctx, axis=-1, keepdims=True))
        p_loc, p_ctx = jnp.exp(s_loc - m), jnp.exp(s_ctx - m)
        denom = jnp.sum(p_loc, axis=-1, keepdims=True) + jnp.sum(p_ctx, axis=-1, keepdims=True)
        o = (_dot(p_loc.astype(MXU_DTYPE), vw) + _dot(p_ctx.astype(MXU_DTYPE), vc)) / denom
        o_ref[t * GRID_W:(t + 1) * GRID_W, :] = jnp.where(first, o[:GRID_W], o[GRID_W:]).astype(o_ref.dtype)


def _bias_table(rpb):
    col = jnp.arange(GRID_W)
    c0 = jnp.clip(col - NA_WIN_COLS // 2, 0, GRID_W - NA_WIN_COLS)
    col_in = (col[None, :] >= c0[:, None]) & (col[None, :] < c0[:, None] + NA_WIN_COLS)
    dc = jnp.clip(col[None, :] - col[:, None], 1 - NA_WIN_COLS, NA_WIN_COLS - 1) + (NA_WIN_COLS - 1)
    t = jnp.where(col_in, rpb.astype(F32)[:, :, dc], NEG)
    dr = jnp.arange(NA_WIN_ROWS)[:, None] + jnp.arange(NA_WIN_ROWS)[None, :]
    t = t[:, dr]
    t = t.reshape(NA_HEADS // 2, 2, NA_WIN_ROWS, NA_WIN_ROWS, GRID_W, GRID_W)
    return t.transpose(0, 2, 1, 4, 3, 5).reshape(NA_HEADS // 2, NA_WIN_ROWS, 2 * GRID_W, NA_WIN_ROWS * GRID_W)


def _natten(qkv, qkv_ctx, rpb):
    n_rows = qkv.shape[0]
    rows = n_rows // GRID_W
    n_ctx = qkv_ctx.shape[0]
    rb = _pick(rows, (8, 4, 2, 1))
    n_win = NA_WIN_ROWS * GRID_W
    nb = NA_D // LANES
    seq = lambda n, off: pl.BlockSpec((n, LANES), lambda p, i: (0, p + off))
    tile = pl.BlockSpec((rb * GRID_W, LANES), lambda p, i: (i, p))
    return pl.pallas_call(
        functools.partial(_natten_kernel, rb=rb, rows=rows),
        grid=(NA_HEADS // 2, rows // rb),
        in_specs=[tile, seq(n_rows, nb), seq(n_rows, 2 * nb), seq(n_ctx, nb), seq(n_ctx, 2 * nb),
                  pl.BlockSpec((None, NA_WIN_ROWS, 2 * GRID_W, n_win), lambda p, i: (p, 0, 0, 0))],
        out_specs=tile,
        out_shape=jax.ShapeDtypeStruct((n_rows, NA_D), MXU_DTYPE),
        compiler_params=_params("parallel", "arbitrary"),
        name="natten",
    )(qkv, qkv, qkv, qkv_ctx, qkv_ctx, _bias_table(rpb))


def _ctx_attn_kernel(q_ref, k_ref, v_ref, o_ref):
    o_ref[...] = _attend(q_ref[...], [(k_ref[...], v_ref[...], None)]).astype(o_ref.dtype)


def _ctx_attn(qkv_ctx):
    n_ctx = qkv_ctx.shape[0]
    nb = NA_D // LANES
    spec = lambda off: pl.BlockSpec((n_ctx, LANES), lambda p: (0, p + off))
    return pl.pallas_call(
        _ctx_attn_kernel,
        grid=(NA_HEADS // 2,),
        in_specs=[spec(0), spec(nb), spec(2 * nb)],
        out_specs=spec(0),
        out_shape=jax.ShapeDtypeStruct((n_ctx, NA_D), MXU_DTYPE),
        compiler_params=_params("parallel"),
        name="ctx_attn",
    )(qkv_ctx, qkv_ctx, qkv_ctx)


def _load_slabs(ref, n, s):
    return jnp.concatenate([ref[pl.ds(c, n, stride=s), :] for c in range(s)], axis=1)


def _store_slabs(ref, val):
    n = val.shape[0]
    s = val.shape[1] // LANES
    for c in range(s):
        ref[pl.ds(c, n, stride=s), :] = val[:, c * LANES:(c + 1) * LANES].astype(ref.dtype)

def _router_kernel(x_ref, nw_ref, sh_ref, sc_ref, wr_ref, m_ref, e_ref, g_ref):
    m = _norm_mod(x_ref[...], nw_ref[...], sh_ref[...], sc_ref[...])
    _store_slabs(m_ref, m)
    logits = _dot(m, wr_ref[...], HI)
    lane = lax.broadcasted_iota(jnp.int32, logits.shape, 1)
    big = jnp.int32(LANES)

    def top(vals):
        v = jnp.max(vals, axis=-1, keepdims=True)
        idx = jnp.min(jnp.where(vals == v, lane, big), axis=-1, keepdims=True)
        return v, idx

    gl = jnp.where(lane < N_GROUPS, logits, -jnp.inf)
    g_max, grp = top(gl)
    p_grp = 1.0 / jnp.sum(jnp.exp(gl - g_max), axis=-1, keepdims=True)
    e_lane = lane - N_GROUPS
    in_grp = (e_lane >= grp * EXPERTS_PER_GROUP) & (e_lane < (grp + 1) * EXPERTS_PER_GROUP)
    el = jnp.where(in_grp, logits, -jnp.inf)
    v1, i1 = top(el)
    v2, i2 = top(jnp.where(lane == i1, -jnp.inf, el))
    t = jnp.exp(v2 - v1)
    g1 = p_grp / (1.0 + t)
    g2 = p_grp * t / (1.0 + t)
    e_ref[...] = jnp.where(lane == 0, i1 - N_GROUPS, jnp.where(lane == 1, i2 - N_GROUPS, 0))
    g_ref[...] = jnp.where(lane == 0, g1, jnp.where(lane == 1, g2, 0.0))


def _router(x, nw, sh, sc, w_route):
    m_rows, d = x.shape
    tm = _pick(m_rows, (512, 256, 128))
    wide = pl.BlockSpec((tm, LANES), lambda i: (i, 0))
    m, e, g = pl.pallas_call(
        _router_kernel,
        grid=(m_rows // tm,),
        in_specs=[pl.BlockSpec((tm, d), lambda i: (i, 0)),
                  _vec_spec(nw, d, _zero), _vec_spec(sh, d, _zero), _vec_spec(sc, d, _zero),
                  pl.BlockSpec((d, LANES), lambda i: (0, 0))],
        out_specs=[pl.BlockSpec((tm * (d // LANES), LANES), lambda i: (i, 0)), wide, wide],
        out_shape=[jax.ShapeDtypeStruct((m_rows * (d // LANES), LANES), F32),
                   jax.ShapeDtypeStruct((m_rows, LANES), jnp.int32),
                   jax.ShapeDtypeStruct((m_rows, LANES), F32)],
        compiler_params=_params("parallel"),
        name="router",
    )(x, nw[0], sh[0], sc[0], w_route)
    return m, e[:, :TOP_K], g


def _moe_kernel(first_ref, nblk_ref, tok_ref, nact_ref, m_hbm, wg_ref, wu_ref, wd_ref, yb_hbm,
                xbuf, ybuf, gsem, osem, wgb_ref, wub_ref, wdb_ref):
    e, n_e = pl.program_id(0), pl.num_programs(0)
    n_act = nact_ref[0]
    s = wg_ref.shape[0] // LANES
    blk_rows = MOE_BLOCK * s
    n_blocks = yb_hbm.shape[0] // blk_rows

    def gather_start(blk, slot):
        for r in range(MOE_BLOCK):
            tok = tok_ref[blk * MOE_BLOCK + r]
            pltpu.make_async_copy(m_hbm.at[pl.ds(pl.multiple_of(tok * s, s), s), :],
                                  xbuf.at[slot, pl.ds(r * s, s), :],
                                  gsem.at[slot]).start(priority=ROW_DMA_PRIORITY)

    def gather_wait(slot):
        pltpu.make_async_copy(m_hbm.at[pl.ds(0, blk_rows), :], xbuf.at[slot], gsem.at[slot]).wait()

    def out_copy(blk, slot):
        rows = pl.ds(pl.multiple_of(blk * blk_rows, blk_rows), blk_rows)
        return pltpu.make_async_copy(ybuf.at[slot], yb_hbm.at[rows, :], osem.at[slot])

    @pl.when(e == 0)
    def _():
        for ahead in range(MOE_AHEAD):
            gather_start(jnp.minimum(ahead, n_act - 1), ahead)

    @pl.when(nblk_ref[e] > 0)
    def _():
        wgb_ref[...] = wg_ref[...].astype(wgb_ref.dtype)
        wub_ref[...] = wu_ref[...].astype(wub_ref.dtype)
        wdb_ref[...] = wd_ref[...].astype(wdb_ref.dtype)

    def block(j, carry):
        b = first_ref[e] + j
        slot = b % MOE_SLOTS
        gather_wait(slot)

        @pl.when(b >= MOE_SLOTS)
        def _():
            out_copy(b - MOE_SLOTS, slot).wait()

        gather_start(jnp.minimum(b + MOE_AHEAD, n_act - 1), (b + MOE_AHEAD) % MOE_SLOTS)
        x = _load_slabs(xbuf.at[slot], MOE_BLOCK, s).astype(MXU_DTYPE)
        h = _silu(_dot(x, wgb_ref[...])) * _dot(x, wub_ref[...])
        _store_slabs(ybuf.at[slot], _dot(h.astype(MXU_DTYPE), wdb_ref[...]))
        out_copy(b, slot).start()
        return carry

    lax.fori_loop(0, nblk_ref[e], block, 0)

    @pl.when(e == n_e - 1)
    def _():
        for ahead in range(MOE_AHEAD):
            gather_wait((n_act + ahead) % MOE_SLOTS)
        for back in range(1, MOE_SLOTS + 1):
            @pl.when(n_act >= back)
            def _():
                out_copy(n_act - back, (n_act - back) % MOE_SLOTS).wait()

        ybuf[0] = jnp.zeros(ybuf.shape[1:], ybuf.dtype)

        def fill(b, carry):
            out_copy(b, 0).start()
            out_copy(b, 0).wait()
            return carry

        lax.fori_loop(n_act, n_blocks, fill, 0)


def _moe_ffn(m_tok, slot_tok, first_blk, n_blk, n_act, w_gate, w_up, w_down, layer):
    n_rows = slot_tok.shape[0]
    d, f = w_gate.shape[-2:]
    blk_rows = MOE_BLOCK * (d // LANES)
    wspec = lambda r, c: pl.BlockSpec((None, None, r, c), lambda e, *_: (layer, e, 0, 0))
    grid_spec = pltpu.PrefetchScalarGridSpec(
        num_scalar_prefetch=4,
        grid=(N_EXPERTS,),
        in_specs=[pl.BlockSpec(memory_space=pl.ANY), wspec(d, f), wspec(d, f), wspec(f, d)],
        out_specs=pl.BlockSpec(memory_space=pl.ANY),
        scratch_shapes=[pltpu.VMEM((MOE_SLOTS, blk_rows, LANES), F32), pltpu.VMEM((MOE_SLOTS, blk_rows, LANES), F32),
                        pltpu.SemaphoreType.DMA((MOE_SLOTS,)), pltpu.SemaphoreType.DMA((MOE_SLOTS,)),
                        pltpu.VMEM((d, f), MXU_DTYPE), pltpu.VMEM((d, f), MXU_DTYPE), pltpu.VMEM((f, d), MXU_DTYPE)],
    )
    return pl.pallas_call(
        _moe_kernel,
        grid_spec=grid_spec,
        out_shape=jax.ShapeDtypeStruct((n_rows * (d // LANES), LANES), F32),
        compiler_params=_params("arbitrary"),
        name="moe_ffn",
    )(first_blk, n_blk, slot_tok, n_act, m_tok, w_gate, w_up, w_down)


def _combine_kernel(pos_ref, x_ref, g_ref, gt_ref, yb_hbm, o_ref, buf, sem, *, tm):
    i, n_i = pl.program_id(0), pl.num_programs(0)
    s = x_ref.shape[1] // LANES

    def gather_start(tile, slot):
        for j in range(tm):
            for k in range(TOP_K):
                p = pos_ref[(tile * tm + j) * TOP_K + k]
                pltpu.make_async_copy(yb_hbm.at[pl.ds(pl.multiple_of(p * s, s), s), :],
                                      buf.at[slot, k, pl.ds(j * s, s), :],
                                      sem.at[slot]).start(priority=ROW_DMA_PRIORITY)

    def gather_wait(slot):
        for k in range(TOP_K):
            pltpu.make_async_copy(yb_hbm.at[pl.ds(0, tm * s), :], buf.at[slot, k], sem.at[slot]).wait()

    @pl.when(i == 0)
    def _():
        for ahead in range(MOE_AHEAD):
            gather_start(jnp.minimum(ahead, n_i - 1), ahead)

    slot = i % MOE_SLOTS
    gather_wait(slot)
    gather_start(jnp.minimum(i + MOE_AHEAD, n_i - 1), (i + MOE_AHEAD) % MOE_SLOTS)
    gt = gt_ref[...]
    y0 = _load_slabs(buf.at[slot, 0], tm, s)
    y1 = _load_slabs(buf.at[slot, 1], tm, s)
    o_ref[...] = x_ref[...] + g_ref[...] * (gt[:, 0:1] * y0 + gt[:, 1:2] * y1)

    @pl.when(i == n_i - 1)
    def _():
        for ahead in range(1, MOE_AHEAD + 1):
            gather_wait((i + ahead) % MOE_SLOTS)


def _combine(x, g, gates, yb, pos):
    m, d = x.shape
    tm = MOE_BLOCK
    grid_spec = pltpu.PrefetchScalarGridSpec(
        num_scalar_prefetch=1,
        grid=(m // tm,),
        in_specs=[pl.BlockSpec((tm, d), lambda i, pos: (i, 0)),
                  _vec_spec(g, d, _zero),
                  pl.BlockSpec((tm, LANES), lambda i, pos: (i, 0)),
                  pl.BlockSpec(memory_space=pl.ANY)],
        out_specs=pl.BlockSpec((tm, d), lambda i, pos: (i, 0)),
        scratch_shapes=[pltpu.VMEM((MOE_SLOTS, TOP_K, tm * (d // LANES), LANES), F32),
                        pltpu.SemaphoreType.DMA((MOE_SLOTS,))],
    )
    return pl.pallas_call(
        functools.partial(_combine_kernel, tm=tm),
        grid_spec=grid_spec,
        out_shape=jax.ShapeDtypeStruct((m, d), F32),
        compiler_params=_params("arbitrary"),
        name="moe_combine",
    )(pos, x, g[0], gates, yb)


def _dispatch_kernel(e_ref, dest_ref, first_ref, nblk_ref, cnt_ref, base_ref):
    phase, i = pl.program_id(0), pl.program_id(1)
    tile = e_ref.shape[1]
    onehot = lax.broadcasted_iota(jnp.int32, (N_EXPERTS, tile), 0) == e_ref[...]

    @pl.when(jnp.logical_and(phase == 0, i == 0))
    def _():
        cnt_ref[...] = jnp.zeros_like(cnt_ref)

    @pl.when(phase == 0)
    def _():
        cnt_ref[...] += jnp.sum(onehot.astype(F32), axis=1, keepdims=True)

    @pl.when(jnp.logical_and(phase == 1, i == 0))
    def _():
        n_blk = jnp.right_shift(cnt_ref[...].astype(jnp.int32) + (MOE_BLOCK - 1),
                                MOE_BLOCK.bit_length() - 1)
        tri = (lax.broadcasted_iota(jnp.int32, (N_EXPERTS, N_EXPERTS), 0)
               >= lax.broadcasted_iota(jnp.int32, (N_EXPERTS, N_EXPERTS), 1)).astype(F32)
        blk_end = _dot(tri, n_blk.astype(F32), HI)
        first = blk_end.astype(jnp.int32) - n_blk
        nblk_ref[...] = n_blk
        first_ref[...] = first
        base_ref[...] = (first * MOE_BLOCK).astype(F32)

    @pl.when(phase == 1)
    def _():
        tri = (lax.broadcasted_iota(jnp.int32, (tile, tile), 0)
               <= lax.broadcasted_iota(jnp.int32, (tile, tile), 1)).astype(MXU_DTYPE)
        cum = _dot(onehot.astype(MXU_DTYPE), tri)
        slot = jnp.where(onehot, base_ref[:, 0:1] + cum - 1.0, 0.0)
        dest_ref[...] = jnp.sum(slot, axis=0, keepdims=True).astype(jnp.int32)
        base_ref[...] += cum[:, tile - 1:tile]


def _dispatch(expert):
    n_tok = expert.shape[0]
    n = n_tok * TOP_K
    n_blocks = -(-n // MOE_BLOCK) + N_EXPERTS
    n_rows = n_blocks * MOE_BLOCK
    tile = _pick(n, (512, 256, 128))
    per_expert = pl.BlockSpec((N_EXPERTS, LANES), lambda ph, i: (0, 0))
    dest, first, n_blk = pl.pallas_call(
        _dispatch_kernel,
        grid=(2, n // tile),
        in_specs=[pl.BlockSpec((1, tile), lambda ph, i: (0, i))],
        out_specs=[pl.BlockSpec((1, tile), lambda ph, i: (0, i * ph)), per_expert, per_expert],
        out_shape=[jax.ShapeDtypeStruct((1, n), jnp.int32),
                   jax.ShapeDtypeStruct((N_EXPERTS, LANES), jnp.int32),
                   jax.ShapeDtypeStruct((N_EXPERTS, LANES), jnp.int32)],
        scratch_shapes=[pltpu.VMEM((N_EXPERTS, LANES), F32), pltpu.VMEM((N_EXPERTS, LANES), F32)],
        compiler_params=_params("arbitrary", "arbitrary"),
        name="moe_dispatch",
    )(expert.reshape(1, n))
    dest, first_blk, n_blk = dest[0], first[:, 0], n_blk[:, 0]
    pair_tok = jnp.arange(n, dtype=jnp.int32) // TOP_K
    slot_tok = jnp.zeros((n_rows,), jnp.int32).at[dest].set(pair_tok)
    n_act = first_blk[-1:] + n_blk[-1:]
    return dest, slot_tok, first_blk, n_blk, n_act


def _even_mixer(lat, cx, nw, mod_lat, mod_ctx, e, w_in, conv_w, conv_b, dt_bias, a_log, d_skip, ssd_norm_w,
                q_norm_w, k_norm_w, rpb, w_out, ctx_out):
    w_e = w_in[e]
    w_qkvd = jnp.concatenate([w_e[:, S_DT:], w_e[:, S_ZX:S_DT],
                              jnp.zeros((w_e.shape[0], LANES - 2 * SSD_HEADS), w_e.dtype)], axis=1)

    def project(x, mod, tag):
        sh, sc = mod[0], mod[1]
        zx = _nm_matmul(x, nw, sh, sc, w_in, e, S_ZX, "proj_zx_" + tag)
        qkv, dt = _qkv_proj(x, nw, sh, sc, w_qkvd, q_norm_w[e], k_norm_w[e], "proj_qkvd_" + tag)
        xbc = _xbc_conv(zx, conv_w[e], conv_b[e])
        return zx, xbc, dt[:, :2 * SSD_HEADS], qkv

    zx_c, xbc_c, dt_c, qkv_c = project(cx, mod_ctx, "ctx")
    zx_l, xbc_l, dt_l, qkv_l = project(lat, mod_lat, "lat")
    h0 = jnp.zeros((2, SSD_PAIRS, SSD_STATE, LANES), F32)
    y_c, h_ctx = _ssd(xbc_c, dt_c, dt_bias[e], a_log[e], h0)
    y_l, _ = _ssd(xbc_l, dt_l, dt_bias[e], a_log[e], h_ctx)
    yssd_l = _ssd_out(y_l, xbc_l, zx_l, d_skip[e], ssd_norm_w[e])
    yatt_l = _natten(qkv_l, qkv_c, rpb[e])
    lat = _matmul_res([yssd_l, yatt_l], w_out, e, lat, mod_lat[2], "out_even_lat")
    if ctx_out:
        yssd_c = _ssd_out(y_c, xbc_c, zx_c, d_skip[e], ssd_norm_w[e])
        yatt_c = _ctx_attn(qkv_c)
        cx = _matmul_res([yssd_c, yatt_c], w_out, e, cx, mod_ctx[2], "out_even_ctx")
    return lat, cx


def _odd_mixer(lat, cx, nw, mod_lat, mod_ctx, o, w_in, conv_w, w_out, ctx_out):
    def mix(x, mod, tag):
        u = _gated_conv_proj(x, nw, mod[0], mod[1], w_in, o, conv_w[o], "proj_odd_" + tag)
        return _matmul_res([u], w_out, o, x, mod[2], "out_odd_" + tag)

    lat = mix(lat, mod_lat, "lat")
    if ctx_out:
        cx = mix(cx, mod_ctx, "ctx")
    return lat, cx


def kernel(x, c, ctx, c_ctx, w_mod, b_mod, norm_mix_w, norm_ffn_w, ev_w_in, ev_conv_w, ev_conv_b, ev_dt_bias,
           ev_a_log, ev_d_skip, ev_ssd_norm_w, ev_q_norm_w, ev_k_norm_w, ev_rpb, ev_w_out, od_w_in, od_conv_w,
           od_w_out, moe_w_group, moe_w_router, moe_w_gate, moe_w_up, moe_w_down):
    bsz, n_lat, d = x.shape
    assert bsz == 1 and ctx.shape[0] == 1
    assert n_lat % GRID_W == 0 and n_lat // GRID_W >= NA_WIN_ROWS and n_lat % SSD_CHUNK == 0
    assert ctx.shape[1] % SSD_CHUNK == 0 and d % LANES == 0
    depth = w_mod.shape[0]
    lat, cx = x[0], ctx[0]
    mods = _modulation(jnp.stack([c[0], c_ctx], axis=1), w_mod, b_mod)
    mods = mods.reshape(depth * 2 * 6, d)

    for l in range(depth):
        even = l % 2 == 0
        ctx_out = any(j % 2 == 0 for j in range(l + 1, depth))
        mod_lat = [_vec(mods, (l * 2 + 0) * 6 + i) for i in range(6)]
        mod_ctx = [_vec(mods, (l * 2 + 1) * 6 + i) for i in range(6)]
        nw = _vec(norm_mix_w, l)
        if even:
            lat, cx = _even_mixer(lat, cx, nw, mod_lat, mod_ctx, l // 2, ev_w_in, ev_conv_w, ev_conv_b,
                                  ev_dt_bias, ev_a_log, ev_d_skip, ev_ssd_norm_w, ev_q_norm_w,
                                  ev_k_norm_w, ev_rpb, ev_w_out, ctx_out)
        else:
            lat, cx = _odd_mixer(lat, cx, nw, mod_lat, mod_ctx, l // 2, od_w_in, od_conv_w, od_w_out, ctx_out)

        nfw = _vec(norm_ffn_w, l)
        w_route = jnp.concatenate([moe_w_group[l], moe_w_router[l],
                                   jnp.zeros((d, LANES - N_GROUPS - N_EXPERTS), F32)], axis=1)
        m_tok, expert, gates = _router(lat, nfw, mod_lat[3], mod_lat[4], w_route)
        if ctx_out:
            m_c, e_c, g_c = _router(cx, nfw, mod_ctx[3], mod_ctx[4], w_route)
            m_tok = jnp.concatenate([m_tok, m_c], axis=0)
            expert = jnp.concatenate([expert, e_c], axis=0)
            gates = jnp.concatenate([gates, g_c], axis=0)
        dest, slot_tok, first_blk, n_blk, n_act = _dispatch(expert)
        yb = _moe_ffn(m_tok, slot_tok, first_blk, n_blk, n_act, moe_w_gate, moe_w_up, moe_w_down, l)
        lat = _combine(lat, mod_lat[5], gates[:n_lat], yb, dest[:n_lat * TOP_K])
        if ctx_out:
            cx = _combine(cx, mod_ctx[5], gates[n_lat:], yb, dest[n_lat * TOP_K:])
    return lat[None]
```

```python
import functools

import jax
import jax.numpy as jnp
from jax import lax
from jax.experimental import pallas as pl
from jax.experimental.pallas import tpu as pltpu

F32 = jnp.float32
MXU_DTYPE = jnp.bfloat16
HI = lax.Precision.HIGHEST
EPS = 1e-6
NEG = -1e30

LANES = 128
SUBLANES = 8
VMEM_LIMIT = 48 * 1024 * 1024
ROW_DMA_THREADS = 2

SSD_HEADS = 16
SSD_HEAD_DIM = 64
SSD_D_INNER = SSD_HEADS * SSD_HEAD_DIM
SSD_STATE = 128
SSD_GROUPS = 2
SSD_CHUNK = 128
SSD_BC = SSD_GROUPS * SSD_STATE
SSD_XBC = SSD_D_INNER + 2 * SSD_BC
SSD_PAIRS = SSD_HEADS // 2
NA_HEADS = 16
NA_HEAD_DIM = 64
NA_D = NA_HEADS * NA_HEAD_DIM
NA_WIN_ROWS = 8
NA_WIN_COLS = 16
GRID_W = 64
N_GROUPS = 8
EXPERTS_PER_GROUP = 8
N_EXPERTS = N_GROUPS * EXPERTS_PER_GROUP
TOP_K = 2
D_FF_EXPERT = 384
MOE_BLOCK = 128
MOE_AHEAD = 2
MOE_SLOTS = MOE_AHEAD + 1
MOE_WEIGHT_AHEAD = 2
MOE_WEIGHT_SLOTS = MOE_WEIGHT_AHEAD + 1
S_ZX = SSD_D_INNER + SSD_XBC
S_DT = S_ZX + 2 * SSD_HEADS
QKVD = 3 * NA_D + LANES


def _pick(n, prefs):
    for p in prefs:
        if n % p == 0:
            return p
    return n


def _params(*sem):
    return pltpu.CompilerParams(dimension_semantics=sem, vmem_limit_bytes=VMEM_LIMIT)


def _vec(arr, idx):
    return arr.reshape(arr.shape[0], 1, arr.shape[-1]), idx


def _vec_spec(vec, width, col):
    idx = vec[1]
    return pl.BlockSpec((None, 1, width), lambda *g: (idx, 0, col(*g)))


def _mat_spec(layer, block, idx):
    return pl.BlockSpec((None,) + block, lambda *g: (layer,) + idx(*g))


def _zero(*g):
    return 0


def _silu(x):
    return x * (1.0 / (1.0 + jnp.exp(-x)))


def _softplus(x):
    return jnp.maximum(x, 0.0) + jnp.log1p(jnp.exp(-jnp.abs(x)))


def _dot(a, b, precision=None):
    return jnp.dot(a, b, preferred_element_type=F32, precision=precision)


def _dot_select(a, b, split_lhs):
    x = a if split_lhs else b
    hi = x.astype(MXU_DTYPE)
    rest = x - hi.astype(F32)
    mid = rest.astype(MXU_DTYPE)
    lo = (rest - mid.astype(F32)).astype(MXU_DTYPE)
    if split_lhs:
        return _dot(hi, b) + _dot(mid, b) + _dot(lo, b)
    return _dot(a, hi) + _dot(a, mid) + _dot(a, lo)


def _dot_nt(a, b):
    return lax.dot_general(a, b, (((1,), (1,)), ((), ())), preferred_element_type=F32)


def _norm_mod(x, nw, sh, sc):
    ms = jnp.mean(x * x, axis=-1, keepdims=True)
    n = x * lax.rsqrt(ms + EPS) * nw
    return n * (1.0 + sc) + sh


def _mod_kernel(ct_ref, w_ref, b_ref, o_ref):
    ct = ct_ref[...]
    s = _silu(ct)
    w = w_ref[...]
    r0 = jnp.sum(s[:, 0:1] * w, axis=0, keepdims=True)
    r1 = jnp.sum(s[:, 1:2] * w, axis=0, keepdims=True)
    o_ref[...] = jnp.concatenate([r0, r1], axis=0) + b_ref[...]


def _modulation(ct, w_mod, b_mod):
    depth, d, n = w_mod.shape
    tn = _pick(n, (512, 256, 128))
    return pl.pallas_call(
        _mod_kernel,
        grid=(depth, n // tn),
        in_specs=[pl.BlockSpec((d, 2), lambda l, j: (0, 0)),
                  pl.BlockSpec((None, d, tn), lambda l, j: (l, 0, j)),
                  pl.BlockSpec((None, 1, tn), lambda l, j: (l, 0, j))],
        out_specs=pl.BlockSpec((None, 2, tn), lambda l, j: (l, 0, j)),
        out_shape=jax.ShapeDtypeStruct((depth, 2, n), F32),
        compiler_params=_params("parallel", "parallel"),
        name="modulation",
    )(ct, w_mod, b_mod.reshape(depth, 1, n))


def _nm_matmul_kernel(x_ref, nw_ref, sh_ref, sc_ref, w_ref, o_ref, a_ref):
    @pl.when(pl.program_id(1) == 0)
    def _():
        a_ref[...] = _norm_mod(x_ref[...], nw_ref[...], sh_ref[...], sc_ref[...]).astype(a_ref.dtype)

    o_ref[...] = _dot(a_ref[...], w_ref[...].astype(a_ref.dtype)).astype(o_ref.dtype)


def _nm_matmul(x, nw, sh, sc, w, layer, n_cols, name):
    m, d = x.shape
    tm = _pick(m, (1024, 512, 256, 128))
    tn = _pick(n_cols, (512, 640, 256, 128))
    return pl.pallas_call(
        _nm_matmul_kernel,
        grid=(m // tm, n_cols // tn),
        in_specs=[pl.BlockSpec((tm, d), lambda i, j: (i, 0)),
                  _vec_spec(nw, d, _zero), _vec_spec(sh, d, _zero), _vec_spec(sc, d, _zero),
                  _mat_spec(layer, (d, tn), lambda i, j: (0, j))],
        out_specs=pl.BlockSpec((tm, tn), lambda i, j: (i, j)),
        out_shape=jax.ShapeDtypeStruct((m, n_cols), F32),
        scratch_shapes=[pltpu.VMEM((tm, d), MXU_DTYPE)],
        compiler_params=_params("parallel", "arbitrary"),
        name=name,
    )(x, nw[0], sh[0], sc[0], w)


def _matmul_res_kernel(*refs, n_a):
    a_refs, w_refs = refs[:n_a], refs[n_a:2 * n_a]
    res_ref, g_ref, o_ref = refs[2 * n_a:]
    acc = None
    for a_ref, w_ref in zip(a_refs, w_refs):
        t = _dot(a_ref[...].astype(MXU_DTYPE), w_ref[...].astype(MXU_DTYPE))
        acc = t if acc is None else acc + t
    o_ref[...] = res_ref[...] + g_ref[...] * acc


def _matmul_res(a_list, w, layer, res, g, name):
    n_a = len(a_list)
    m, n = res.shape
    kp = w.shape[1] // n_a
    tm = _pick(m, (2048, 1024, 512, 256, 128))
    tn = _pick(n, (512, 256, 128))
    in_specs = [pl.BlockSpec((tm, kp), lambda i, j: (i, 0)) for _ in range(n_a)]
    in_specs += [_mat_spec(layer, (kp, tn), functools.partial(lambda i, j, p: (p, j), p=p)) for p in range(n_a)]
    in_specs += [pl.BlockSpec((tm, tn), lambda i, j: (i, j)), _vec_spec(g, tn, lambda i, j: j)]
    return pl.pallas_call(
        functools.partial(_matmul_res_kernel, n_a=n_a),
        grid=(m // tm, n // tn),
        in_specs=in_specs,
        out_specs=pl.BlockSpec((tm, tn), lambda i, j: (i, j)),
        out_shape=jax.ShapeDtypeStruct((m, n), F32),
        compiler_params=_params("parallel", "parallel"),
        name=name,
    )(*a_list, *([w] * n_a), res, g[0])


def _shifted(x, prev_ref, next_ref):
    i, n_i = pl.program_id(0), pl.num_programs(0)
    tl = x.shape[0]
    row = lax.broadcasted_iota(jnp.int32, x.shape, 0)
    prev_row = jnp.where(i == 0, 0.0, prev_ref[SUBLANES - 1:SUBLANES, :])
    next_row = jnp.where(i == n_i - 1, 0.0, next_ref[0:1, :])
    up = jnp.where(row == 0, prev_row, pltpu.roll(x, 1, 0))
    down = jnp.where(row == tl - 1, next_row, pltpu.roll(x, tl - 1, 0))
    return up, down


def _halo_specs(tl, tc, n_rows, col_off):
    nb = tl // SUBLANES
    last = n_rows // SUBLANES - 1
    return [pl.BlockSpec((tl, tc), lambda i, j: (i, j + col_off)),
            pl.BlockSpec((SUBLANES, tc), lambda i, j: (jnp.maximum(i * nb - 1, 0), j + col_off)),
            pl.BlockSpec((SUBLANES, tc), lambda i, j: (jnp.minimum((i + 1) * nb, last), j + col_off))]


def _xbc_conv_kernel(x_ref, prev_ref, next_ref, w_ref, b_ref, o_ref):
    x = x_ref[...]
    up, down = _shifted(x, prev_ref, next_ref)
    w = w_ref[...]
    o_ref[...] = _silu(w[0:1] * up + w[1:2] * x + w[2:3] * down + b_ref[...])


def _xbc_conv(zx, conv_w, conv_b):
    n_rows = zx.shape[0]
    tl = _pick(n_rows, (512, 256, 128))
    tc = 512
    off = SSD_D_INNER // tc
    return pl.pallas_call(
        _xbc_conv_kernel,
        grid=(n_rows // tl, SSD_XBC // tc),
        in_specs=_halo_specs(tl, tc, n_rows, off) + [pl.BlockSpec((3, tc), lambda i, j: (0, j)),
                                                     pl.BlockSpec((1, tc), lambda i, j: (0, j))],
        out_specs=pl.BlockSpec((tl, tc), lambda i, j: (i, j)),
        out_shape=jax.ShapeDtypeStruct((n_rows, SSD_XBC), F32),
        compiler_params=_params("parallel", "parallel"),
        name="xbc_conv",
    )(zx, zx, zx, conv_w, conv_b.reshape(1, -1))


def _gated_conv_proj_kernel(x_ref, xp_ref, xn_ref, nw_ref, sh_ref, sc_ref, wb_ref, wc_ref, wx_ref, cw_ref, o_ref,
                            a_ref, ah_ref):
    i, n_i = pl.program_id(0), pl.num_programs(0)

    @pl.when(pl.program_id(1) == 0)
    def _():
        nw, sh, sc = nw_ref[...], sh_ref[...], sc_ref[...]
        a_ref[...] = _norm_mod(x_ref[...], nw, sh, sc).astype(a_ref.dtype)
        ah_ref[0:SUBLANES, :] = _norm_mod(xp_ref[...], nw, sh, sc)
        ah_ref[SUBLANES:, :] = _norm_mod(xn_ref[...], nw, sh, sc)

    a, ah = a_ref[...], ah_ref[...].astype(a_ref.dtype)
    wc, wx = wc_ref[...].astype(a.dtype), wx_ref[...].astype(a.dtype)
    u = _dot(a, wc) * _dot(a, wx)
    uh = _dot(ah, wc) * _dot(ah, wx)
    tm = u.shape[0]
    row = lax.broadcasted_iota(jnp.int32, u.shape, 0)
    prev_row = jnp.where(i == 0, 0.0, uh[SUBLANES - 1:SUBLANES, :])
    next_row = jnp.where(i == n_i - 1, 0.0, uh[SUBLANES:SUBLANES + 1, :])
    up = jnp.where(row == 0, prev_row, pltpu.roll(u, 1, 0))
    down = jnp.where(row == tm - 1, next_row, pltpu.roll(u, tm - 1, 0))
    cw = cw_ref[...]
    conv = cw[0:1] * up + cw[1:2] * u + cw[2:3] * down
    o_ref[...] = (_dot(a, wb_ref[...].astype(a.dtype)) * conv).astype(o_ref.dtype)


def _gated_conv_proj(x, nw, sh, sc, w, layer, conv_w, name):
    m, d = x.shape
    c = w.shape[2] // 3
    tm = _pick(m, (1024, 512, 256, 128))
    tc = _pick(c, (256, 128))
    nb = c // tc
    blocks = tm // SUBLANES
    last = m // SUBLANES - 1
    wspec = lambda off: _mat_spec(layer, (d, tc), lambda i, j: (0, j + off))
    return pl.pallas_call(
        _gated_conv_proj_kernel,
        grid=(m // tm, nb),
        in_specs=[pl.BlockSpec((tm, d), lambda i, j: (i, 0)),
                  pl.BlockSpec((SUBLANES, d), lambda i, j: (jnp.maximum(i * blocks - 1, 0), 0)),
                  pl.BlockSpec((SUBLANES, d), lambda i, j: (jnp.minimum((i + 1) * blocks, last), 0)),
                  _vec_spec(nw, d, _zero), _vec_spec(sh, d, _zero), _vec_spec(sc, d, _zero),
                  wspec(0), wspec(nb), wspec(2 * nb),
                  pl.BlockSpec((3, tc), lambda i, j: (0, j))],
        out_specs=pl.BlockSpec((tm, tc), lambda i, j: (i, j)),
        out_shape=jax.ShapeDtypeStruct((m, c), MXU_DTYPE),
        scratch_shapes=[pltpu.VMEM((tm, d), MXU_DTYPE), pltpu.VMEM((2 * SUBLANES, d), F32)],
        compiler_params=_params("parallel", "arbitrary"),
        name=name,
    )(x, x, x, nw[0], sh[0], sc[0], w, w, w, conv_w)


def _ssd_kernel(xbc_ref, dt_ref, dtt_ref, dtb_ref, dtbt_ref, alog_ref, alogt_ref, h0_ref,
                y_ref, hout_ref, h_ref):
    d, c, n_c = pl.program_id(0), pl.program_id(1), pl.num_programs(1)
    q = SSD_CHUNK

    @pl.when(c == 0)
    def _():
        h_ref[...] = h0_ref[...]

    dt = _softplus(dt_ref[...] + dtb_ref[...])
    dtt = _softplus(dtt_ref[...] + dtbt_ref[...])
    a = dt * -jnp.exp(alog_ref[...])
    at = dtt * -jnp.exp(alogt_ref[...])
    row = lax.broadcasted_iota(jnp.int32, (q, q), 0)
    col = lax.broadcasted_iota(jnp.int32, (q, q), 1)
    sign = jnp.where(d == 0, 1, -1)
    mask = (row - col) * sign >= 0
    tri = mask.astype(MXU_DTYPE)
    tri_t = ((col - row) * sign >= 0).astype(MXU_DTYPE)
    cs = _dot_select(tri, a, split_lhs=False)
    cst = _dot_select(at, tri_t, split_lhs=True)
    tot = jnp.broadcast_to(jnp.sum(a, axis=0, keepdims=True), (SUBLANES, SSD_HEADS))
    expand = (lax.broadcasted_iota(jnp.int32, (SSD_HEADS, SSD_D_INNER), 1) // SSD_HEAD_DIM
              == lax.broadcasted_iota(jnp.int32, (SSD_HEADS, SSD_D_INNER), 0)).astype(MXU_DTYPE)
    dt_e = _dot_select(dt, expand, split_lhs=True)
    cs_e = _dot_select(cs, expand, split_lhs=True)
    tot_e = _dot_select(tot, expand, split_lhs=True)[0:1]

    xs = xbc_ref[:, 0:SSD_D_INNER]
    xdt = xs * dt_e
    xdt_m = xdt.astype(MXU_DTYPE)
    xw_m = (xdt * jnp.exp(tot_e - cs_e)).astype(MXU_DTYPE)
    e_e = jnp.exp(cs_e)
    dec_e = jnp.exp(tot_e)
    first = lax.broadcasted_iota(jnp.int32, (q, LANES), 1) < SSD_HEAD_DIM

    ppg = SSD_PAIRS // SSD_GROUPS
    for g in range(SSD_GROUPS):
        bg = xbc_ref[:, SSD_D_INNER + g * SSD_STATE:SSD_D_INNER + (g + 1) * SSD_STATE]
        cg = xbc_ref[:, SSD_D_INNER + SSD_BC + g * SSD_STATE:SSD_D_INNER + SSD_BC + (g + 1) * SSD_STATE]
        bg_m, cg_m = bg.astype(MXU_DTYPE), cg.astype(MXU_DTYPE)
        bgt_m = bg.T.astype(MXU_DTYPE)
        scores = _dot_nt(cg_m, bg_m)
        for pp in range(ppg):
            p = g * ppg + pp
            sl = slice(p * LANES, (p + 1) * LANES)
            ys = []
            for hh in range(2):
                h = 2 * p + hh
                diff = cs[:, h:h + 1] - cst[h:h + 1, :]
                decay = jnp.exp(jnp.where(mask, diff, -jnp.inf))
                ys.append(_dot((scores * decay).astype(MXU_DTYPE), xdt_m[:, sl]))
            hp = h_ref[p]
            y_off = _dot(cg_m, hp.astype(MXU_DTYPE)) * e_e[:, sl]
            y_ref[:, sl] = jnp.where(first, ys[0], ys[1]) + y_off
            h_ref[p] = dec_e[:, sl] * hp + _dot(bgt_m, xw_m[:, sl])

    @pl.when(c == n_c - 1)
    def _():
        hout_ref[...] = h_ref[...]


def _ssd(xbc, dt_raw, dt_bias, a_log, h0):
    n_rows = xbc.shape[0]
    q = SSD_CHUNK
    n_c = n_rows // q
    dt = dt_raw.reshape(n_rows, 2, SSD_HEADS).transpose(1, 0, 2)
    dtt = dt.transpose(0, 2, 1)

    def chunk(d, c):
        return jnp.where(d == 0, c, n_c - 1 - c)

    small = lambda shape: pl.BlockSpec((None,) + shape, lambda d, c: (d, 0, 0))
    return pl.pallas_call(
        _ssd_kernel,
        grid=(2, n_c),
        in_specs=[pl.BlockSpec((q, SSD_XBC), lambda d, c: (chunk(d, c), 0)),
                  pl.BlockSpec((None, q, SSD_HEADS), lambda d, c: (d, chunk(d, c), 0)),
                  pl.BlockSpec((None, SSD_HEADS, q), lambda d, c: (d, 0, chunk(d, c))),
                  small((1, SSD_HEADS)), small((SSD_HEADS, 1)), small((1, SSD_HEADS)), small((SSD_HEADS, 1)),
                  pl.BlockSpec((None, SSD_PAIRS, SSD_STATE, LANES), lambda d, c: (d, 0, 0, 0))],
        out_specs=[pl.BlockSpec((None, q, SSD_D_INNER), lambda d, c: (d, chunk(d, c), 0)),
                   pl.BlockSpec((None, SSD_PAIRS, SSD_STATE, LANES), lambda d, c: (d, 0, 0, 0))],
        out_shape=[jax.ShapeDtypeStruct((2, n_rows, SSD_D_INNER), F32),
                   jax.ShapeDtypeStruct((2, SSD_PAIRS, SSD_STATE, LANES), F32)],
        scratch_shapes=[pltpu.VMEM((SSD_PAIRS, SSD_STATE, LANES), F32)],
        compiler_params=_params("arbitrary", "arbitrary"),
        name="ssd_scan",
    )(xbc, dt, dtt, dt_bias[:, None, :], dt_bias[:, :, None], a_log[:, None, :], a_log[:, :, None], h0)


def _ssd_out_kernel(y_ref, xbc_ref, z_ref, dsk_ref, nw_ref, o_ref):
    y = y_ref[0] + y_ref[1] + dsk_ref[...] * xbc_ref[...]
    g = y * _silu(z_ref[...])
    ms = jnp.mean(g * g, axis=-1, keepdims=True)
    o_ref[...] = (g * lax.rsqrt(ms + EPS) * nw_ref[...]).astype(o_ref.dtype)


def _ssd_out(y2, xbc, zx, d_skip, norm_w):
    n_rows = xbc.shape[0]
    tl = _pick(n_rows, (512, 256, 128))
    w = SSD_D_INNER
    return pl.pallas_call(
        _ssd_out_kernel,
        grid=(n_rows // tl,),
        in_specs=[pl.BlockSpec((2, tl, w), lambda i: (0, i, 0)),
                  pl.BlockSpec((tl, w), lambda i: (i, 0)),
                  pl.BlockSpec((tl, w), lambda i: (i, 0)),
                  pl.BlockSpec((1, w), lambda i: (0, 0)),
                  pl.BlockSpec((1, w), lambda i: (0, 0))],
        out_specs=pl.BlockSpec((tl, w), lambda i: (i, 0)),
        out_shape=jax.ShapeDtypeStruct((n_rows, w), MXU_DTYPE),
        compiler_params=_params("parallel"),
        name="ssd_out",
    )(y2, xbc, zx, jnp.repeat(d_skip, SSD_HEAD_DIM)[None, :], norm_w[None, :])


def _head_mean_sq(x):
    blk = (lax.broadcasted_iota(jnp.int32, (LANES, LANES), 0) // NA_HEAD_DIM
           == lax.broadcasted_iota(jnp.int32, (LANES, LANES), 1) // NA_HEAD_DIM).astype(MXU_DTYPE)
    x2 = x * x
    hi = x2.astype(MXU_DTYPE)
    lo = (x2 - hi.astype(F32)).astype(MXU_DTYPE)
    return (_dot(hi, blk) + _dot(lo, blk)) * (1.0 / NA_HEAD_DIM)


def _qkv_proj_kernel(x_ref, nw_ref, sh_ref, sc_ref, w_ref, qw_ref, kw_ref, o_ref, dt_ref, a_ref, *, chunks):
    j = pl.program_id(1)

    @pl.when(j == 0)
    def _():
        a_ref[...] = _norm_mod(x_ref[...], nw_ref[...], sh_ref[...], sc_ref[...]).astype(a_ref.dtype)

    acc = _dot(a_ref[...], w_ref[...].astype(a_ref.dtype))
    n_head_chunks = NA_D // LANES
    for c in range(chunks):
        chunk = j * chunks + c
        x = acc[:, c * LANES:(c + 1) * LANES]
        w = jnp.where(chunk < n_head_chunks, qw_ref[...] * NA_HEAD_DIM ** -0.5, kw_ref[...])
        normed = x * lax.rsqrt(_head_mean_sq(x) + EPS) * w
        o_ref[:, c * LANES:(c + 1) * LANES] = jnp.where(chunk < 2 * n_head_chunks, normed, x).astype(o_ref.dtype)

    @pl.when(j == pl.num_programs(1) - 1)
    def _():
        dt_ref[...] = acc[:, (chunks - 1) * LANES:]


def _qkv_proj(x, nw, sh, sc, w_qkvd, q_norm_w, k_norm_w, name):
    m, d = x.shape
    tm = _pick(m, (1024, 512, 256, 128))
    tn = 5 * LANES
    wspec = pl.BlockSpec((1, LANES), lambda i, j: (0, 0))
    return pl.pallas_call(
        functools.partial(_qkv_proj_kernel, chunks=tn // LANES),
        grid=(m // tm, QKVD // tn),
        in_specs=[pl.BlockSpec((tm, d), lambda i, j: (i, 0)),
                  _vec_spec(nw, d, _zero), _vec_spec(sh, d, _zero), _vec_spec(sc, d, _zero),
                  pl.BlockSpec((d, tn), lambda i, j: (0, j)), wspec, wspec],
        out_specs=[pl.BlockSpec((tm, tn), lambda i, j: (i, j)), pl.BlockSpec((tm, LANES), lambda i, j: (i, 0))],
        out_shape=[jax.ShapeDtypeStruct((m, QKVD), MXU_DTYPE), jax.ShapeDtypeStruct((m, LANES), F32)],
        scratch_shapes=[pltpu.VMEM((tm, d), MXU_DTYPE)],
        compiler_params=_params("parallel", "arbitrary"),
        name=name,
    )(x, nw[0], sh[0], sc[0], w_qkvd, jnp.tile(q_norm_w, 2)[None, :], jnp.tile(k_norm_w, 2)[None, :])


def _attend(q2, parts):
    first = lax.broadcasted_iota(jnp.int32, q2.shape, 1) < NA_HEAD_DIM
    outs = []
    for head_mask in (first, jnp.logical_not(first)):
        qa = jnp.where(head_mask, q2, jnp.zeros_like(q2))
        scores = []
        for k, _, bias in parts:
            s = _dot_nt(qa, k)
            scores.append(s if bias is None else s + bias)
        m = functools.reduce(jnp.maximum, [jnp.max(s, axis=-1, keepdims=True) for s in scores])
        probs = [jnp.exp(s - m) for s in scores]
        denom = functools.reduce(jnp.add, [jnp.sum(p, axis=-1, keepdims=True) for p in probs])
        acc = functools.reduce(jnp.add, [_dot(p.astype(MXU_DTYPE), v) for p, (_, v, _) in zip(probs, parts)])
        outs.append(acc / denom)
    return jnp.where(first, outs[0], outs[1])


def _natten_kernel(q_ref, k_ref, v_ref, kc_ref, vc_ref, tbl_ref, o_ref, *, rb, rows):
    i = pl.program_id(1)
    kc, vc = kc_ref[...], vc_ref[...]
    n_win = NA_WIN_ROWS * GRID_W
    first = lax.broadcasted_iota(jnp.int32, (GRID_W, LANES), 1) < NA_HEAD_DIM
    for t in range(rb):
        r = i * rb + t
        start = jnp.clip(r - NA_WIN_ROWS // 2, 0, rows - NA_WIN_ROWS)
        dr0 = start - r + (NA_WIN_ROWS - 1)
        ks = pl.ds(pl.multiple_of(start * GRID_W, GRID_W), n_win)
        q2 = q_ref[t * GRID_W:(t + 1) * GRID_W, :]
        kw, vw = k_ref[ks, :], v_ref[ks, :]
        zero = jnp.zeros_like(q2)
        qs = jnp.concatenate([jnp.where(first, q2, zero), jnp.where(first, zero, q2)], axis=0)
        s_loc = _dot_nt(qs, kw) + tbl_ref[dr0]
        s_ctx = _dot_nt(qs, kc)
        m = jnp.maximum(jnp.max(s_loc, axis=-1, keepdims=True), jnp.max(s_ctx, axis=-1, keepdims=True))
        p_loc, p_ctx = jnp.exp(s_loc - m), jnp.exp(s_ctx - m)
        denom = jnp.sum(p_loc, axis=-1, keepdims=True) + jnp.sum(p_ctx, axis=-1, keepdims=True)
        o = (_dot(p_loc.astype(MXU_DTYPE), vw) + _dot(p_ctx.astype(MXU_DTYPE), vc)) / denom
        o_ref[t * GRID_W:(t + 1) * GRID_W, :] = jnp.where(first, o[:GRID_W], o[GRID_W:]).astype(o_ref.dtype)


def _bias_table(rpb):
    col = jnp.arange(GRID_W)
    c0 = jnp.clip(col - NA_WIN_COLS // 2, 0, GRID_W - NA_WIN_COLS)
    col_in = (col[None, :] >= c0[:, None]) & (col[None, :] < c0[:, None] + NA_WIN_COLS)
    dc = jnp.clip(col[None, :] - col[:, None], 1 - NA_WIN_COLS, NA_WIN_COLS - 1) + (NA_WIN_COLS - 1)
    t = jnp.where(col_in, rpb.astype(F32)[:, :, dc], NEG)
    dr = jnp.arange(NA_WIN_ROWS)[:, None] + jnp.arange(NA_WIN_ROWS)[None, :]
    t = t[:, dr]
    t = t.reshape(NA_HEADS // 2, 2, NA_WIN_ROWS, NA_WIN_ROWS, GRID_W, GRID_W)
    return t.transpose(0, 2, 1, 4, 3, 5).reshape(NA_HEADS // 2, NA_WIN_ROWS, 2 * GRID_W, NA_WIN_ROWS * GRID_W)


def _natten(qkv, qkv_ctx, rpb):
    n_rows = qkv.shape[0]
    rows = n_rows // GRID_W
    n_ctx = qkv_ctx.shape[0]
    rb = _pick(rows, (8, 4, 2, 1))
    n_win = NA_WIN_ROWS * GRID_W
    nb = NA_D // LANES
    seq = lambda n, off: pl.BlockSpec((n, LANES), lambda p, i: (0, p + off))
    tile = pl.BlockSpec((rb * GRID_W, LANES), lambda p, i: (i, p))
    return pl.pallas_call(
        functools.partial(_natten_kernel, rb=rb, rows=rows),
        grid=(NA_HEADS // 2, rows // rb),
        in_specs=[tile, seq(n_rows, nb), seq(n_rows, 2 * nb), seq(n_ctx, nb), seq(n_ctx, 2 * nb),
                  pl.BlockSpec((None, NA_WIN_ROWS, 2 * GRID_W, n_win), lambda p, i: (p, 0, 0, 0))],
        out_specs=tile,
        out_shape=jax.ShapeDtypeStruct((n_rows, NA_D), MXU_DTYPE),
        compiler_params=_params("parallel", "arbitrary"),
        name="natten",
    )(qkv, qkv, qkv, qkv_ctx, qkv_ctx, _bias_table(rpb))


def _ctx_attn_kernel(q_ref, k_ref, v_ref, o_ref):
    o_ref[...] = _attend(q_ref[...], [(k_ref[...], v_ref[...], None)]).astype(o_ref.dtype)


def _ctx_attn(qkv_ctx):
    n_ctx = qkv_ctx.shape[0]
    nb = NA_D // LANES
    spec = lambda off: pl.BlockSpec((n_ctx, LANES), lambda p: (0, p + off))
    return pl.pallas_call(
        _ctx_attn_kernel,
        grid=(NA_HEADS // 2,),
        in_specs=[spec(0), spec(nb), spec(2 * nb)],
        out_specs=spec(0),
        out_shape=jax.ShapeDtypeStruct((n_ctx, NA_D), MXU_DTYPE),
        compiler_params=_params("parallel"),
        name="ctx_attn",
    )(qkv_ctx, qkv_ctx, qkv_ctx)


def _load_slabs(ref, n, s):
    return jnp.concatenate([ref[pl.ds(c, n, stride=s), :] for c in range(s)], axis=1)


def _store_slabs(ref, val):
    n = val.shape[0]
    s = val.shape[1] // LANES
    for c in range(s):
        ref[pl.ds(c, n, stride=s), :] = val[:, c * LANES:(c + 1) * LANES].astype(ref.dtype)

def _router_kernel(x_ref, nw_ref, sh_ref, sc_ref, wr_ref, m_ref, e_ref, g_ref):
    m = _norm_mod(x_ref[...], nw_ref[...], sh_ref[...], sc_ref[...])
    _store_slabs(m_ref, m)
    logits = _dot(m, wr_ref[...], HI)
    lane = lax.broadcasted_iota(jnp.int32, logits.shape, 1)
    big = jnp.int32(LANES)

    def top(vals):
        v = jnp.max(vals, axis=-1, keepdims=True)
        idx = jnp.min(jnp.where(vals == v, lane, big), axis=-1, keepdims=True)
        return v, idx

    gl = jnp.where(lane < N_GROUPS, logits, -jnp.inf)
    g_max, grp = top(gl)
    p_grp = 1.0 / jnp.sum(jnp.exp(gl - g_max), axis=-1, keepdims=True)
    e_lane = lane - N_GROUPS
    in_grp = (e_lane >= grp * EXPERTS_PER_GROUP) & (e_lane < (grp + 1) * EXPERTS_PER_GROUP)
    el = jnp.where(in_grp, logits, -jnp.inf)
    v1, i1 = top(el)
    v2, i2 = top(jnp.where(lane == i1, -jnp.inf, el))
    t = jnp.exp(v2 - v1)
    g1 = p_grp / (1.0 + t)
    g2 = p_grp * t / (1.0 + t)
    e_ref[...] = jnp.where(lane == 0, i1 - N_GROUPS, jnp.where(lane == 1, i2 - N_GROUPS, 0))
    g_ref[...] = jnp.where(lane == 0, g1, jnp.where(lane == 1, g2, 0.0))


def _router(x, nw, sh, sc, w_route):
    m_rows, d = x.shape
    tm = _pick(m_rows, (512, 256, 128))
    wide = pl.BlockSpec((tm, LANES), lambda i: (i, 0))
    m, e, g = pl.pallas_call(
        _router_kernel,
        grid=(m_rows // tm,),
        in_specs=[pl.BlockSpec((tm, d), lambda i: (i, 0)),
                  _vec_spec(nw, d, _zero), _vec_spec(sh, d, _zero), _vec_spec(sc, d, _zero),
                  pl.BlockSpec((d, LANES), lambda i: (0, 0))],
        out_specs=[pl.BlockSpec((tm * (d // LANES), LANES), lambda i: (i, 0)), wide, wide],
        out_shape=[jax.ShapeDtypeStruct((m_rows * (d // LANES), LANES), F32),
                   jax.ShapeDtypeStruct((m_rows, LANES), jnp.int32),
                   jax.ShapeDtypeStruct((m_rows, LANES), F32)],
        compiler_params=_params("parallel"),
        name="router",
    )(x, nw[0], sh[0], sc[0], w_route)
    return m, e[:, :TOP_K], g


def _moe_kernel(first_ref, nblk_ref, tok_ref, nact_ref, m_hbm, wg_hbm, wu_hbm, wd_hbm, yb_hbm,
                xbuf, ybuf, gsem, osem, wgf_ref, wuf_ref, wdf_ref, wsem, wgb_ref, wub_ref, wdb_ref, *, layer):
    e, n_e = pl.program_id(0), pl.num_programs(0)
    n_act = nact_ref[0]
    s = wgf_ref.shape[1] // LANES
    blk_rows = MOE_BLOCK * s
    n_blocks = yb_hbm.shape[0] // blk_rows

    def weight_copies(expert, ws):
        return [pltpu.make_async_copy(src.at[layer, expert], dst.at[ws], wsem.at[k, ws])
                for k, (src, dst) in enumerate(((wg_hbm, wgf_ref), (wu_hbm, wuf_ref), (wd_hbm, wdf_ref)))]

    @pl.when(e == 0)
    def _():
        for ahead in range(MOE_WEIGHT_AHEAD):
            for copy in weight_copies(ahead, ahead):
                copy.start()

    ws = e % MOE_WEIGHT_SLOTS
    for copy in weight_copies(e, ws):
        copy.wait()

    @pl.when(e + MOE_WEIGHT_AHEAD < n_e)
    def _():
        for copy in weight_copies(e + MOE_WEIGHT_AHEAD, (e + MOE_WEIGHT_AHEAD) % MOE_WEIGHT_SLOTS):
            copy.start()

    def gather_start(blk, slot):
        for r in range(MOE_BLOCK):
            tok = tok_ref[blk * MOE_BLOCK + r]
            pltpu.make_async_copy(m_hbm.at[pl.ds(pl.multiple_of(tok * s, s), s), :],
                                  xbuf.at[slot, pl.ds(r * s, s), :],
                                  gsem.at[slot]).start(priority=r % ROW_DMA_THREADS)

    def gather_wait(slot):
        pltpu.make_async_copy(m_hbm.at[pl.ds(0, blk_rows), :], xbuf.at[slot], gsem.at[slot]).wait()

    def out_copy(blk, slot):
        rows = pl.ds(pl.multiple_of(blk * blk_rows, blk_rows), blk_rows)
        return pltpu.make_async_copy(ybuf.at[slot], yb_hbm.at[rows, :], osem.at[slot])

    @pl.when(e == 0)
    def _():
        for ahead in range(MOE_AHEAD):
            gather_start(jnp.minimum(ahead, n_act - 1), ahead)

    @pl.when(nblk_ref[e] > 0)
    def _():
        wgb_ref[...] = wgf_ref[ws].astype(wgb_ref.dtype)
        wub_ref[...] = wuf_ref[ws].astype(wub_ref.dtype)
        wdb_ref[...] = wdf_ref[ws].astype(wdb_ref.dtype)

    def block(j, carry):
        b = first_ref[e] + j
        slot = b % MOE_SLOTS
        gather_wait(slot)

        @pl.when(b >= MOE_SLOTS)
        def _():
            out_copy(b - MOE_SLOTS, slot).wait()

        gather_start(jnp.minimum(b + MOE_AHEAD, n_act - 1), (b + MOE_AHEAD) % MOE_SLOTS)
        x = _load_slabs(xbuf.at[slot], MOE_BLOCK, s).astype(MXU_DTYPE)
        h = _silu(_dot(x, wgb_ref[...])) * _dot(x, wub_ref[...])
        _store_slabs(ybuf.at[slot], _dot(h.astype(MXU_DTYPE), wdb_ref[...]))
        out_copy(b, slot).start()
        return carry

    lax.fori_loop(0, nblk_ref[e], block, 0)

    @pl.when(e == n_e - 1)
    def _():
        for ahead in range(MOE_AHEAD):
            gather_wait((n_act + ahead) % MOE_SLOTS)
        for back in range(1, MOE_SLOTS + 1):
            @pl.when(n_act >= back)
            def _():
                out_copy(n_act - back, (n_act - back) % MOE_SLOTS).wait()

        ybuf[0] = jnp.zeros(ybuf.shape[1:], ybuf.dtype)

        def fill(b, carry):
            out_copy(b, 0).start()
            out_copy(b, 0).wait()
            return carry

        lax.fori_loop(n_act, n_blocks, fill, 0)


def _moe_ffn(m_tok, slot_tok, first_blk, n_blk, n_act, w_gate, w_up, w_down, layer):
    n_rows = slot_tok.shape[0]
    d, f = w_gate.shape[-2:]
    blk_rows = MOE_BLOCK * (d // LANES)
    hbm = pl.BlockSpec(memory_space=pl.ANY)
    grid_spec = pltpu.PrefetchScalarGridSpec(
        num_scalar_prefetch=4,
        grid=(N_EXPERTS,),
        in_specs=[hbm, hbm, hbm, hbm],
        out_specs=hbm,
        scratch_shapes=[pltpu.VMEM((MOE_SLOTS, blk_rows, LANES), F32), pltpu.VMEM((MOE_SLOTS, blk_rows, LANES), F32),
                        pltpu.SemaphoreType.DMA((MOE_SLOTS,)), pltpu.SemaphoreType.DMA((MOE_SLOTS,)),
                        pltpu.VMEM((MOE_WEIGHT_SLOTS, d, f), F32), pltpu.VMEM((MOE_WEIGHT_SLOTS, d, f), F32),
                        pltpu.VMEM((MOE_WEIGHT_SLOTS, f, d), F32), pltpu.SemaphoreType.DMA((3, MOE_WEIGHT_SLOTS)),
                        pltpu.VMEM((d, f), MXU_DTYPE), pltpu.VMEM((d, f), MXU_DTYPE), pltpu.VMEM((f, d), MXU_DTYPE)],
    )
    return pl.pallas_call(
        functools.partial(_moe_kernel, layer=layer),
        grid_spec=grid_spec,
        out_shape=jax.ShapeDtypeStruct((n_rows * (d // LANES), LANES), F32),
        compiler_params=_params("arbitrary"),
        name="moe_ffn",
    )(first_blk, n_blk, slot_tok, n_act, m_tok, w_gate, w_up, w_down)


def _combine_kernel(pos_ref, x_ref, g_ref, gt_ref, yb_hbm, o_ref, buf, sem, *, tm):
    i, n_i = pl.program_id(0), pl.num_programs(0)
    s = x_ref.shape[1] // LANES

    def gather_start(tile, slot):
        for j in range(tm):
            for k in range(TOP_K):
                p = pos_ref[(tile * tm + j) * TOP_K + k]
                pltpu.make_async_copy(yb_hbm.at[pl.ds(pl.multiple_of(p * s, s), s), :],
                                      buf.at[slot, k, pl.ds(j * s, s), :],
                                      sem.at[slot]).start(priority=(j * TOP_K + k) % ROW_DMA_THREADS)

    def gather_wait(slot):
        for k in range(TOP_K):
            pltpu.make_async_copy(yb_hbm.at[pl.ds(0, tm * s), :], buf.at[slot, k], sem.at[slot]).wait()

    @pl.when(i == 0)
    def _():
        for ahead in range(MOE_AHEAD):
            gather_start(jnp.minimum(ahead, n_i - 1), ahead)

    slot = i % MOE_SLOTS
    gather_wait(slot)
    gather_start(jnp.minimum(i + MOE_AHEAD, n_i - 1), (i + MOE_AHEAD) % MOE_SLOTS)
    gt = gt_ref[...]
    y0 = _load_slabs(buf.at[slot, 0], tm, s)
    y1 = _load_slabs(buf.at[slot, 1], tm, s)
    o_ref[...] = x_ref[...] + g_ref[...] * (gt[:, 0:1] * y0 + gt[:, 1:2] * y1)

    @pl.when(i == n_i - 1)
    def _():
        for ahead in range(1, MOE_AHEAD + 1):
            gather_wait((i + ahead) % MOE_SLOTS)


def _combine(x, g, gates, yb, pos):
    m, d = x.shape
    tm = MOE_BLOCK
    grid_spec = pltpu.PrefetchScalarGridSpec(
        num_scalar_prefetch=1,
        grid=(m // tm,),
        in_specs=[pl.BlockSpec((tm, d), lambda i, pos: (i, 0)),
                  _vec_spec(g, d, _zero),
                  pl.BlockSpec((tm, LANES), lambda i, pos: (i, 0)),
                  pl.BlockSpec(memory_space=pl.ANY)],
        out_specs=pl.BlockSpec((tm, d), lambda i, pos: (i, 0)),
        scratch_shapes=[pltpu.VMEM((MOE_SLOTS, TOP_K, tm * (d // LANES), LANES), F32),
                        pltpu.SemaphoreType.DMA((MOE_SLOTS,))],
    )
    return pl.pallas_call(
        functools.partial(_combine_kernel, tm=tm),
        grid_spec=grid_spec,
        out_shape=jax.ShapeDtypeStruct((m, d), F32),
        compiler_params=_params("arbitrary"),
        name="moe_combine",
    )(pos, x, g[0], gates, yb)


def _dispatch_kernel(e_ref, dest_ref, first_ref, nblk_ref, cnt_ref, base_ref):
    phase, i = pl.program_id(0), pl.program_id(1)
    tile = e_ref.shape[1]
    onehot = lax.broadcasted_iota(jnp.int32, (N_EXPERTS, tile), 0) == e_ref[...]

    @pl.when(jnp.logical_and(phase == 0, i == 0))
    def _():
        cnt_ref[...] = jnp.zeros_like(cnt_ref)

    @pl.when(phase == 0)
    def _():
        cnt_ref[...] += jnp.sum(onehot.astype(F32), axis=1, keepdims=True)

    @pl.when(jnp.logical_and(phase == 1, i == 0))
    def _():
        n_blk = jnp.right_shift(cnt_ref[...].astype(jnp.int32) + (MOE_BLOCK - 1),
                                MOE_BLOCK.bit_length() - 1)
        tri = (lax.broadcasted_iota(jnp.int32, (N_EXPERTS, N_EXPERTS), 0)
               >= lax.broadcasted_iota(jnp.int32, (N_EXPERTS, N_EXPERTS), 1)).astype(F32)
        blk_end = _dot(tri, n_blk.astype(F32), HI)
        first = blk_end.astype(jnp.int32) - n_blk
        nblk_ref[...] = n_blk
        first_ref[...] = first
        base_ref[...] = (first * MOE_BLOCK).astype(F32)

    @pl.when(phase == 1)
    def _():
        tri = (lax.broadcasted_iota(jnp.int32, (tile, tile), 0)
               <= lax.broadcasted_iota(jnp.int32, (tile, tile), 1)).astype(MXU_DTYPE)
        cum = _dot(onehot.astype(MXU_DTYPE), tri)
        slot = jnp.where(onehot, base_ref[:, 0:1] + cum - 1.0, 0.0)
        dest_ref[...] = jnp.sum(slot, axis=0, keepdims=True).astype(jnp.int32)
        base_ref[...] += cum[:, tile - 1:tile]


def _dispatch(expert):
    n_tok = expert.shape[0]
    n = n_tok * TOP_K
    n_blocks = -(-n // MOE_BLOCK) + N_EXPERTS
    n_rows = n_blocks * MOE_BLOCK
    tile = _pick(n, (512, 256, 128))
    per_expert = pl.BlockSpec((N_EXPERTS, LANES), lambda ph, i: (0, 0))
    dest, first, n_blk = pl.pallas_call(
        _dispatch_kernel,
        grid=(2, n // tile),
        in_specs=[pl.BlockSpec((1, tile), lambda ph, i: (0, i))],
        out_specs=[pl.BlockSpec((1, tile), lambda ph, i: (0, i * ph)), per_expert, per_expert],
        out_shape=[jax.ShapeDtypeStruct((1, n), jnp.int32),
                   jax.ShapeDtypeStruct((N_EXPERTS, LANES), jnp.int32),
                   jax.ShapeDtypeStruct((N_EXPERTS, LANES), jnp.int32)],
        scratch_shapes=[pltpu.VMEM((N_EXPERTS, LANES), F32), pltpu.VMEM((N_EXPERTS, LANES), F32)],
        compiler_params=_params("arbitrary", "arbitrary"),
        name="moe_dispatch",
    )(expert.reshape(1, n))
    dest, first_blk, n_blk = dest[0], first[:, 0], n_blk[:, 0]
    pair_tok = jnp.arange(n, dtype=jnp.int32) // TOP_K
    slot_tok = jnp.zeros((n_rows,), jnp.int32).at[dest].set(pair_tok)
    n_act = first_blk[-1:] + n_blk[-1:]
    return dest, slot_tok, first_blk, n_blk, n_act


def _even_mixer(lat, cx, nw, mod_lat, mod_ctx, e, w_in, conv_w, conv_b, dt_bias, a_log, d_skip, ssd_norm_w,
                q_norm_w, k_norm_w, rpb, w_out, ctx_out):
    w_e = w_in[e]
    w_qkvd = jnp.concatenate([w_e[:, S_DT:], w_e[:, S_ZX:S_DT],
                              jnp.zeros((w_e.shape[0], LANES - 2 * SSD_HEADS), w_e.dtype)], axis=1)

    def project(x, mod, tag):
        sh, sc = mod[0], mod[1]
        zx = _nm_matmul(x, nw, sh, sc, w_in, e, S_ZX, "proj_zx_" + tag)
        qkv, dt = _qkv_proj(x, nw, sh, sc, w_qkvd, q_norm_w[e], k_norm_w[e], "proj_qkvd_" + tag)
        xbc = _xbc_conv(zx, conv_w[e], conv_b[e])
        return zx, xbc, dt[:, :2 * SSD_HEADS], qkv

    zx_c, xbc_c, dt_c, qkv_c = project(cx, mod_ctx, "ctx")
    zx_l, xbc_l, dt_l, qkv_l = project(lat, mod_lat, "lat")
    h0 = jnp.zeros((2, SSD_PAIRS, SSD_STATE, LANES), F32)
    y_c, h_ctx = _ssd(xbc_c, dt_c, dt_bias[e], a_log[e], h0)
    y_l, _ = _ssd(xbc_l, dt_l, dt_bias[e], a_log[e], h_ctx)
    yssd_l = _ssd_out(y_l, xbc_l, zx_l, d_skip[e], ssd_norm_w[e])
    yatt_l = _natten(qkv_l, qkv_c, rpb[e])
    lat = _matmul_res([yssd_l, yatt_l], w_out, e, lat, mod_lat[2], "out_even_lat")
    if ctx_out:
        yssd_c = _ssd_out(y_c, xbc_c, zx_c, d_skip[e], ssd_norm_w[e])
        yatt_c = _ctx_attn(qkv_c)
        cx = _matmul_res([yssd_c, yatt_c], w_out, e, cx, mod_ctx[2], "out_even_ctx")
    return lat, cx


def _odd_mixer(lat, cx, nw, mod_lat, mod_ctx, o, w_in, conv_w, w_out, ctx_out):
    def mix(x, mod, tag):
        u = _gated_conv_proj(x, nw, mod[0], mod[1], w_in, o, conv_w[o], "proj_odd_" + tag)
        return _matmul_res([u], w_out, o, x, mod[2], "out_odd_" + tag)

    lat = mix(lat, mod_lat, "lat")
    if ctx_out:
        cx = mix(cx, mod_ctx, "ctx")
    return lat, cx


def kernel(x, c, ctx, c_ctx, w_mod, b_mod, norm_mix_w, norm_ffn_w, ev_w_in, ev_conv_w, ev_conv_b, ev_dt_bias,
           ev_a_log, ev_d_skip, ev_ssd_norm_w, ev_q_norm_w, ev_k_norm_w, ev_rpb, ev_w_out, od_w_in, od_conv_w,
           od_w_out, moe_w_group, moe_w_router, moe_w_gate, moe_w_up, moe_w_down):
    bsz, n_lat, d = x.shape
    assert bsz == 1 and ctx.shape[0] == 1
    assert n_lat % GRID_W == 0 and n_lat // GRID_W >= NA_WIN_ROWS and n_lat % SSD_CHUNK == 0
    assert ctx.shape[1] % SSD_CHUNK == 0 and d % LANES == 0
    depth = w_mod.shape[0]
    lat, cx = x[0], ctx[0]
    mods = _modulation(jnp.stack([c[0], c_ctx], axis=1), w_mod, b_mod)
    mods = mods.reshape(depth * 2 * 6, d)

    for l in range(depth):
        even = l % 2 == 0
        ctx_out = any(j % 2 == 0 for j in range(l + 1, depth))
        mod_lat = [_vec(mods, (l * 2 + 0) * 6 + i) for i in range(6)]
        mod_ctx = [_vec(mods, (l * 2 + 1) * 6 + i) for i in range(6)]
        nw = _vec(norm_mix_w, l)
        if even:
            lat, cx = _even_mixer(lat, cx, nw, mod_lat, mod_ctx, l // 2, ev_w_in, ev_conv_w, ev_conv_b,
                                  ev_dt_bias, ev_a_log, ev_d_skip, ev_ssd_norm_w, ev_q_norm_w,
                                  ev_k_norm_w, ev_rpb, ev_w_out, ctx_out)
        else:
            lat, cx = _odd_mixer(lat, cx, nw, mod_lat, mod_ctx, l // 2, od_w_in, od_conv_w, od_w_out, ctx_out)

        nfw = _vec(norm_ffn_w, l)
        w_route = jnp.concatenate([moe_w_group[l], moe_w_router[l],
                                   jnp.zeros((d, LANES - N_GROUPS - N_EXPERTS), F32)], axis=1)
        m_tok, expert, gates = _router(lat, nfw, mod_lat[3], mod_lat[4], w_route)
        if ctx_out:
            m_c, e_c, g_c = _router(cx, nfw, mod_ctx[3], mod_ctx[4], w_route)
            m_tok = jnp.concatenate([m_tok, m_c], axis=0)
            expert = jnp.concatenate([expert, e_c], axis=0)
            gates = jnp.concatenate([gates, g_c], axis=0)
        dest, slot_tok, first_blk, n_blk, n_act = _dispatch(expert)
        yb = _moe_ffn(m_tok, slot_tok, first_blk, n_blk, n_act, moe_w_gate, moe_w_up, moe_w_down, l)
        lat = _combine(lat, mod_lat[5], gates[:n_lat], yb, dest[:n_lat * TOP_K])
        if ctx_out:
            cx = _combine(cx, mod_ctx[5], gates[n_lat:], yb, dest[n_lat * TOP_K:])
    return lat[None]
```

```python
import functools

import jax
import jax.numpy as jnp
from jax import lax
from jax.experimental import pallas as pl
from jax.experimental.pallas import tpu as pltpu

F32 = jnp.float32
MXU_DTYPE = jnp.bfloat16
HI = lax.Precision.HIGHEST
EPS = 1e-6
NEG = -1e30

LANES = 128
SUBLANES = 8
VMEM_LIMIT = 48 * 1024 * 1024
ROW_DMA_THREADS = 2

SSD_HEADS = 16
SSD_HEAD_DIM = 64
SSD_D_INNER = SSD_HEADS * SSD_HEAD_DIM
SSD_STATE = 128
SSD_GROUPS = 2
SSD_CHUNK = 128
SSD_BC = SSD_GROUPS * SSD_STATE
SSD_XBC = SSD_D_INNER + 2 * SSD_BC
SSD_PAIRS = SSD_HEADS // 2
NA_HEADS = 16
NA_HEAD_DIM = 64
NA_D = NA_HEADS * NA_HEAD_DIM
NA_WIN_ROWS = 8
NA_WIN_COLS = 16
GRID_W = 64
N_GROUPS = 8
EXPERTS_PER_GROUP = 8
N_EXPERTS = N_GROUPS * EXPERTS_PER_GROUP
TOP_K = 2
D_FF_EXPERT = 384
MOE_BLOCK = 128
MOE_AHEAD = 2
MOE_SLOTS = MOE_AHEAD + 1
MOE_WEIGHT_AHEAD = 2
MOE_WEIGHT_SLOTS = MOE_WEIGHT_AHEAD + 1
S_ZX = SSD_D_INNER + SSD_XBC
S_DT = S_ZX + 2 * SSD_HEADS
QKVD = 3 * NA_D + LANES


def _pick(n, prefs):
    for p in prefs:
        if n % p == 0:
            return p
    return n


def _params(*sem):
    return pltpu.CompilerParams(dimension_semantics=sem, vmem_limit_bytes=VMEM_LIMIT)


def _vec(arr, idx):
    return arr.reshape(arr.shape[0], 1, arr.shape[-1]), idx


def _vec_spec(vec, width, col):
    idx = vec[1]
    return pl.BlockSpec((None, 1, width), lambda *g: (idx, 0, col(*g)))


def _mat_spec(layer, block, idx):
    return pl.BlockSpec((None,) + block, lambda *g: (layer,) + idx(*g))


def _zero(*g):
    return 0


def _silu(x):
    return x * (1.0 / (1.0 + jnp.exp(-x)))


def _softplus(x):
    return jnp.maximum(x, 0.0) + jnp.log1p(jnp.exp(-jnp.abs(x)))


def _dot(a, b, precision=None):
    return jnp.dot(a, b, preferred_element_type=F32, precision=precision)


def _dot_select(a, b, split_lhs):
    x = a if split_lhs else b
    hi = x.astype(MXU_DTYPE)
    rest = x - hi.astype(F32)
    mid = rest.astype(MXU_DTYPE)
    lo = (rest - mid.astype(F32)).astype(MXU_DTYPE)
    if split_lhs:
        return _dot(hi, b) + _dot(mid, b) + _dot(lo, b)
    return _dot(a, hi) + _dot(a, mid) + _dot(a, lo)


def _dot_nt(a, b):
    return lax.dot_general(a, b, (((1,), (1,)), ((), ())), preferred_element_type=F32)


def _norm_mod(x, nw, sh, sc):
    ms = jnp.mean(x * x, axis=-1, keepdims=True)
    n = x * lax.rsqrt(ms + EPS) * nw
    return n * (1.0 + sc) + sh


def _mod_kernel(ct_ref, w_ref, b_ref, o_ref):
    ct = ct_ref[...]
    s = _silu(ct)
    w = w_ref[...]
    r0 = jnp.sum(s[:, 0:1] * w, axis=0, keepdims=True)
    r1 = jnp.sum(s[:, 1:2] * w, axis=0, keepdims=True)
    o_ref[...] = jnp.concatenate([r0, r1], axis=0) + b_ref[...]


def _modulation(ct, w_mod, b_mod):
    depth, d, n = w_mod.shape
    tn = _pick(n, (512, 256, 128))
    return pl.pallas_call(
        _mod_kernel,
        grid=(depth, n // tn),
        in_specs=[pl.BlockSpec((d, 2), lambda l, j: (0, 0)),
                  pl.BlockSpec((None, d, tn), lambda l, j: (l, 0, j)),
                  pl.BlockSpec((None, 1, tn), lambda l, j: (l, 0, j))],
        out_specs=pl.BlockSpec((None, 2, tn), lambda l, j: (l, 0, j)),
        out_shape=jax.ShapeDtypeStruct((depth, 2, n), F32),
        compiler_params=_params("parallel", "parallel"),
        name="modulation",
    )(ct, w_mod, b_mod.reshape(depth, 1, n))


def _nm_matmul_kernel(x_ref, nw_ref, sh_ref, sc_ref, w_ref, o_ref, a_ref):
    @pl.when(pl.program_id(1) == 0)
    def _():
        a_ref[...] = _norm_mod(x_ref[...], nw_ref[...], sh_ref[...], sc_ref[...]).astype(a_ref.dtype)

    o_ref[...] = _dot(a_ref[...], w_ref[...].astype(a_ref.dtype)).astype(o_ref.dtype)


def _nm_matmul(x, nw, sh, sc, w, layer, n_cols, name):
    m, d = x.shape
    tm = _pick(m, (1024, 512, 256, 128))
    tn = _pick(n_cols, (512, 640, 256, 128))
    return pl.pallas_call(
        _nm_matmul_kernel,
        grid=(m // tm, n_cols // tn),
        in_specs=[pl.BlockSpec((tm, d), lambda i, j: (i, 0)),
                  _vec_spec(nw, d, _zero), _vec_spec(sh, d, _zero), _vec_spec(sc, d, _zero),
                  _mat_spec(layer, (d, tn), lambda i, j: (0, j))],
        out_specs=pl.BlockSpec((tm, tn), lambda i, j: (i, j)),
        out_shape=jax.ShapeDtypeStruct((m, n_cols), F32),
        scratch_shapes=[pltpu.VMEM((tm, d), MXU_DTYPE)],
        compiler_params=_params("parallel", "arbitrary"),
        name=name,
    )(x, nw[0], sh[0], sc[0], w)


def _matmul_res_kernel(*refs, n_a):
    a_refs, w_refs = refs[:n_a], refs[n_a:2 * n_a]
    res_ref, g_ref, o_ref = refs[2 * n_a:]
    acc = None
    for a_ref, w_ref in zip(a_refs, w_refs):
        t = _dot(a_ref[...].astype(MXU_DTYPE), w_ref[...].astype(MXU_DTYPE))
        acc = t if acc is None else acc + t
    o_ref[...] = res_ref[...] + g_ref[...] * acc


def _matmul_res(a_list, w, layer, res, g, name):
    n_a = len(a_list)
    m, n = res.shape
    kp = w.shape[1] // n_a
    tm = _pick(m, (2048, 1024, 512, 256, 128))
    tn = _pick(n, (512, 256, 128))
    in_specs = [pl.BlockSpec((tm, kp), lambda i, j: (i, 0)) for _ in range(n_a)]
    in_specs += [_mat_spec(layer, (kp, tn), functools.partial(lambda i, j, p: (p, j), p=p)) for p in range(n_a)]
    in_specs += [pl.BlockSpec((tm, tn), lambda i, j: (i, j)), _vec_spec(g, tn, lambda i, j: j)]
    return pl.pallas_call(
        functools.partial(_matmul_res_kernel, n_a=n_a),
        grid=(m // tm, n // tn),
        in_specs=in_specs,
        out_specs=pl.BlockSpec((tm, tn), lambda i, j: (i, j)),
        out_shape=jax.ShapeDtypeStruct((m, n), F32),
        compiler_params=_params("parallel", "parallel"),
        name=name,
    )(*a_list, *([w] * n_a), res, g[0])


def _shifted(x, prev_ref, next_ref):
    i, n_i = pl.program_id(0), pl.num_programs(0)
    tl = x.shape[0]
    row = lax.broadcasted_iota(jnp.int32, x.shape, 0)
    prev_row = jnp.where(i == 0, 0.0, prev_ref[SUBLANES - 1:SUBLANES, :])
    next_row = jnp.where(i == n_i - 1, 0.0, next_ref[0:1, :])
    up = jnp.where(row == 0, prev_row, pltpu.roll(x, 1, 0))
    down = jnp.where(row == tl - 1, next_row, pltpu.roll(x, tl - 1, 0))
    return up, down


def _halo_specs(tl, tc, n_rows, col_off):
    nb = tl // SUBLANES
    last = n_rows // SUBLANES - 1
    return [pl.BlockSpec((tl, tc), lambda i, j: (i, j + col_off)),
            pl.BlockSpec((SUBLANES, tc), lambda i, j: (jnp.maximum(i * nb - 1, 0), j + col_off)),
            pl.BlockSpec((SUBLANES, tc), lambda i, j: (jnp.minimum((i + 1) * nb, last), j + col_off))]


def _xbc_conv_kernel(x_ref, prev_ref, next_ref, w_ref, b_ref, o_ref):
    x = x_ref[...]
    up, down = _shifted(x, prev_ref, next_ref)
    w = w_ref[...]
    o_ref[...] = _silu(w[0:1] * up + w[1:2] * x + w[2:3] * down + b_ref[...])


def _xbc_conv(zx, conv_w, conv_b):
    n_rows = zx.shape[0]
    tl = _pick(n_rows, (512, 256, 128))
    tc = 512
    off = SSD_D_INNER // tc
    return pl.pallas_call(
        _xbc_conv_kernel,
        grid=(n_rows // tl, SSD_XBC // tc),
        in_specs=_halo_specs(tl, tc, n_rows, off) + [pl.BlockSpec((3, tc), lambda i, j: (0, j)),
                                                     pl.BlockSpec((1, tc), lambda i, j: (0, j))],
        out_specs=pl.BlockSpec((tl, tc), lambda i, j: (i, j)),
        out_shape=jax.ShapeDtypeStruct((n_rows, SSD_XBC), F32),
        compiler_params=_params("parallel", "parallel"),
        name="xbc_conv",
    )(zx, zx, zx, conv_w, conv_b.reshape(1, -1))


def _gated_conv_proj_kernel(x_ref, xp_ref, xn_ref, nw_ref, sh_ref, sc_ref, wb_ref, wc_ref, wx_ref, cw_ref, o_ref,
                            a_ref, ah_ref):
    i, n_i = pl.program_id(0), pl.num_programs(0)

    @pl.when(pl.program_id(1) == 0)
    def _():
        nw, sh, sc = nw_ref[...], sh_ref[...], sc_ref[...]
        a_ref[...] = _norm_mod(x_ref[...], nw, sh, sc).astype(a_ref.dtype)
        ah_ref[0:SUBLANES, :] = _norm_mod(xp_ref[...], nw, sh, sc)
        ah_ref[SUBLANES:, :] = _norm_mod(xn_ref[...], nw, sh, sc)

    a, ah = a_ref[...], ah_ref[...].astype(a_ref.dtype)
    wc, wx = wc_ref[...].astype(a.dtype), wx_ref[...].astype(a.dtype)
    u = _dot(a, wc) * _dot(a, wx)
    uh = _dot(ah, wc) * _dot(ah, wx)
    tm = u.shape[0]
    row = lax.broadcasted_iota(jnp.int32, u.shape, 0)
    prev_row = jnp.where(i == 0, 0.0, uh[SUBLANES - 1:SUBLANES, :])
    next_row = jnp.where(i == n_i - 1, 0.0, uh[SUBLANES:SUBLANES + 1, :])
    up = jnp.where(row == 0, prev_row, pltpu.roll(u, 1, 0))
    down = jnp.where(row == tm - 1, next_row, pltpu.roll(u, tm - 1, 0))
    cw = cw_ref[...]
    conv = cw[0:1] * up + cw[1:2] * u + cw[2:3] * down
    o_ref[...] = (_dot(a, wb_ref[...].astype(a.dtype)) * conv).astype(o_ref.dtype)


def _gated_conv_proj(x, nw, sh, sc, w, layer, conv_w, name):
    m, d = x.shape
    c = w.shape[2] // 3
    tm = _pick(m, (1024, 512, 256, 128))
    tc = _pick(c, (256, 128))
    nb = c // tc
    blocks = tm // SUBLANES
    last = m // SUBLANES - 1
    wspec = lambda off: _mat_spec(layer, (d, tc), lambda i, j: (0, j + off))
    return pl.pallas_call(
        _gated_conv_proj_kernel,
        grid=(m // tm, nb),
        in_specs=[pl.BlockSpec((tm, d), lambda i, j: (i, 0)),
                  pl.BlockSpec((SUBLANES, d), lambda i, j: (jnp.maximum(i * blocks - 1, 0), 0)),
                  pl.BlockSpec((SUBLANES, d), lambda i, j: (jnp.minimum((i + 1) * blocks, last), 0)),
                  _vec_spec(nw, d, _zero), _vec_spec(sh, d, _zero), _vec_spec(sc, d, _zero),
                  wspec(0), wspec(nb), wspec(2 * nb),
                  pl.BlockSpec((3, tc), lambda i, j: (0, j))],
        out_specs=pl.BlockSpec((tm, tc), lambda i, j: (i, j)),
        out_shape=jax.ShapeDtypeStruct((m, c), MXU_DTYPE),
        scratch_shapes=[pltpu.VMEM((tm, d), MXU_DTYPE), pltpu.VMEM((2 * SUBLANES, d), F32)],
        compiler_params=_params("parallel", "arbitrary"),
        name=name,
    )(x, x, x, nw[0], sh[0], sc[0], w, w, w, conv_w)


def _ssd_kernel(xbc_ref, dt_ref, dtt_ref, dtb_ref, dtbt_ref, alog_ref, alogt_ref, h0_ref,
                y_ref, hout_ref, h_ref):
    d, c, n_c = pl.program_id(0), pl.program_id(1), pl.num_programs(1)
    q = SSD_CHUNK

    @pl.when(c == 0)
    def _():
        h_ref[...] = h0_ref[...]

    dt = _softplus(dt_ref[...] + dtb_ref[...])
    dtt = _softplus(dtt_ref[...] + dtbt_ref[...])
    a = dt * -jnp.exp(alog_ref[...])
    at = dtt * -jnp.exp(alogt_ref[...])
    row = lax.broadcasted_iota(jnp.int32, (q, q), 0)
    col = lax.broadcasted_iota(jnp.int32, (q, q), 1)
    sign = jnp.where(d == 0, 1, -1)
    mask = (row - col) * sign >= 0
    tri = mask.astype(MXU_DTYPE)
    tri_t = ((col - row) * sign >= 0).astype(MXU_DTYPE)
    cs = _dot_select(tri, a, split_lhs=False)
    cst = _dot_select(at, tri_t, split_lhs=True)
    tot = jnp.broadcast_to(jnp.sum(a, axis=0, keepdims=True), (SUBLANES, SSD_HEADS))
    expand = (lax.broadcasted_iota(jnp.int32, (SSD_HEADS, SSD_D_INNER), 1) // SSD_HEAD_DIM
              == lax.broadcasted_iota(jnp.int32, (SSD_HEADS, SSD_D_INNER), 0)).astype(MXU_DTYPE)
    dt_e = _dot_select(dt, expand, split_lhs=True)
    cs_e = _dot_select(cs, expand, split_lhs=True)
    tot_e = _dot_select(tot, expand, split_lhs=True)[0:1]

    xs = xbc_ref[:, 0:SSD_D_INNER]
    xdt = xs * dt_e
    xdt_m = xdt.astype(MXU_DTYPE)
    xw_m = (xdt * jnp.exp(tot_e - cs_e)).astype(MXU_DTYPE)
    e_e = jnp.exp(cs_e)
    dec_e = jnp.exp(tot_e)
    first = lax.broadcasted_iota(jnp.int32, (q, LANES), 1) < SSD_HEAD_DIM

    ppg = SSD_PAIRS // SSD_GROUPS
    for g in range(SSD_GROUPS):
        bg = xbc_ref[:, SSD_D_INNER + g * SSD_STATE:SSD_D_INNER + (g + 1) * SSD_STATE]
        cg = xbc_ref[:, SSD_D_INNER + SSD_BC + g * SSD_STATE:SSD_D_INNER + SSD_BC + (g + 1) * SSD_STATE]
        bg_m, cg_m = bg.astype(MXU_DTYPE), cg.astype(MXU_DTYPE)
        bgt_m = bg.T.astype(MXU_DTYPE)
        scores = _dot_nt(cg_m, bg_m)
        for pp in range(ppg):
            p = g * ppg + pp
            sl = slice(p * LANES, (p + 1) * LANES)
            ys = []
            for hh in range(2):
                h = 2 * p + hh
                diff = cs[:, h:h + 1] - cst[h:h + 1, :]
                decay = jnp.exp(jnp.where(mask, diff, -jnp.inf))
                ys.append(_dot((scores * decay).astype(MXU_DTYPE), xdt_m[:, sl]))
            hp = h_ref[p]
            y_off = _dot(cg_m, hp.astype(MXU_DTYPE)) * e_e[:, sl]
            y_ref[:, sl] = jnp.where(first, ys[0], ys[1]) + y_off
            h_ref[p] = dec_e[:, sl] * hp + _dot(bgt_m, xw_m[:, sl])

    @pl.when(c == n_c - 1)
    def _():
        hout_ref[...] = h_ref[...]


def _ssd(xbc, dt_raw, dt_bias, a_log, h0):
    n_rows = xbc.shape[0]
    q = SSD_CHUNK
    n_c = n_rows // q
    dt = dt_raw.reshape(n_rows, 2, SSD_HEADS).transpose(1, 0, 2)
    dtt = dt.transpose(0, 2, 1)

    def chunk(d, c):
        return jnp.where(d == 0, c, n_c - 1 - c)

    small = lambda shape: pl.BlockSpec((None,) + shape, lambda d, c: (d, 0, 0))
    return pl.pallas_call(
        _ssd_kernel,
        grid=(2, n_c),
        in_specs=[pl.BlockSpec((q, SSD_XBC), lambda d, c: (chunk(d, c), 0)),
                  pl.BlockSpec((None, q, SSD_HEADS), lambda d, c: (d, chunk(d, c), 0)),
                  pl.BlockSpec((None, SSD_HEADS, q), lambda d, c: (d, 0, chunk(d, c))),
                  small((1, SSD_HEADS)), small((SSD_HEADS, 1)), small((1, SSD_HEADS)), small((SSD_HEADS, 1)),
                  pl.BlockSpec((None, SSD_PAIRS, SSD_STATE, LANES), lambda d, c: (d, 0, 0, 0))],
        out_specs=[pl.BlockSpec((None, q, SSD_D_INNER), lambda d, c: (d, chunk(d, c), 0)),
                   pl.BlockSpec((None, SSD_PAIRS, SSD_STATE, LANES), lambda d, c: (d, 0, 0, 0))],
        out_shape=[jax.ShapeDtypeStruct((2, n_rows, SSD_D_INNER), F32),
                   jax.ShapeDtypeStruct((2, SSD_PAIRS, SSD_STATE, LANES), F32)],
        scratch_shapes=[pltpu.VMEM((SSD_PAIRS, SSD_STATE, LANES), F32)],
        compiler_params=_params("arbitrary", "arbitrary"),
        name="ssd_scan",
    )(xbc, dt, dtt, dt_bias[:, None, :], dt_bias[:, :, None], a_log[:, None, :], a_log[:, :, None], h0)


def _ssd_out_kernel(y_ref, xbc_ref, z_ref, dsk_ref, nw_ref, o_ref):
    y = y_ref[0] + y_ref[1] + dsk_ref[...] * xbc_ref[...]
    g = y * _silu(z_ref[...])
    ms = jnp.mean(g * g, axis=-1, keepdims=True)
    o_ref[...] = (g * lax.rsqrt(ms + EPS) * nw_ref[...]).astype(o_ref.dtype)


def _ssd_out(y2, xbc, zx, d_skip, norm_w):
    n_rows = xbc.shape[0]
    tl = _pick(n_rows, (512, 256, 128))
    w = SSD_D_INNER
    return pl.pallas_call(
        _ssd_out_kernel,
        grid=(n_rows // tl,),
        in_specs=[pl.BlockSpec((2, tl, w), lambda i: (0, i, 0)),
                  pl.BlockSpec((tl, w), lambda i: (i, 0)),
                  pl.BlockSpec((tl, w), lambda i: (i, 0)),
                  pl.BlockSpec((1, w), lambda i: (0, 0)),
                  pl.BlockSpec((1, w), lambda i: (0, 0))],
        out_specs=pl.BlockSpec((tl, w), lambda i: (i, 0)),
        out_shape=jax.ShapeDtypeStruct((n_rows, w), MXU_DTYPE),
        compiler_params=_params("parallel"),
        name="ssd_out",
    )(y2, xbc, zx, jnp.repeat(d_skip, SSD_HEAD_DIM)[None, :], norm_w[None, :])


def _head_mean_sq(x):
    blk = (lax.broadcasted_iota(jnp.int32, (LANES, LANES), 0) // NA_HEAD_DIM
           == lax.broadcasted_iota(jnp.int32, (LANES, LANES), 1) // NA_HEAD_DIM).astype(MXU_DTYPE)
    x2 = x * x
    hi = x2.astype(MXU_DTYPE)
    lo = (x2 - hi.astype(F32)).astype(MXU_DTYPE)
    return (_dot(hi, blk) + _dot(lo, blk)) * (1.0 / NA_HEAD_DIM)


def _qkv_proj_kernel(x_ref, nw_ref, sh_ref, sc_ref, w_ref, qw_ref, kw_ref, o_ref, dt_ref, a_ref, *, chunks):
    j = pl.program_id(1)

    @pl.when(j == 0)
    def _():
        a_ref[...] = _norm_mod(x_ref[...], nw_ref[...], sh_ref[...], sc_ref[...]).astype(a_ref.dtype)

    acc = _dot(a_ref[...], w_ref[...].astype(a_ref.dtype))
    n_head_chunks = NA_D // LANES
    for c in range(chunks):
        chunk = j * chunks + c
        x = acc[:, c * LANES:(c + 1) * LANES]
        w = jnp.where(chunk < n_head_chunks, qw_ref[...] * NA_HEAD_DIM ** -0.5, kw_ref[...])
        normed = x * lax.rsqrt(_head_mean_sq(x) + EPS) * w
        o_ref[:, c * LANES:(c + 1) * LANES] = jnp.where(chunk < 2 * n_head_chunks, normed, x).astype(o_ref.dtype)

    @pl.when(j == pl.num_programs(1) - 1)
    def _():
        dt_ref[...] = acc[:, (chunks - 1) * LANES:]


def _qkv_proj(x, nw, sh, sc, w_qkvd, q_norm_w, k_norm_w, name):
    m, d = x.shape
    tm = _pick(m, (1024, 512, 256, 128))
    tn = 5 * LANES
    wspec = pl.BlockSpec((1, LANES), lambda i, j: (0, 0))
    return pl.pallas_call(
        functools.partial(_qkv_proj_kernel, chunks=tn // LANES),
        grid=(m // tm, QKVD // tn),
        in_specs=[pl.BlockSpec((tm, d), lambda i, j: (i, 0)),
                  _vec_spec(nw, d, _zero), _vec_spec(sh, d, _zero), _vec_spec(sc, d, _zero),
                  pl.BlockSpec((d, tn), lambda i, j: (0, j)), wspec, wspec],
        out_specs=[pl.BlockSpec((tm, tn), lambda i, j: (i, j)), pl.BlockSpec((tm, LANES), lambda i, j: (i, 0))],
        out_shape=[jax.ShapeDtypeStruct((m, QKVD), MXU_DTYPE), jax.ShapeDtypeStruct((m, LANES), F32)],
        scratch_shapes=[pltpu.VMEM((tm, d), MXU_DTYPE)],
        compiler_params=_params("parallel", "arbitrary"),
        name=name,
    )(x, nw[0], sh[0], sc[0], w_qkvd, jnp.tile(q_norm_w, 2)[None, :], jnp.tile(k_norm_w, 2)[None, :])


def _attend(q2, parts):
    first = lax.broadcasted_iota(jnp.int32, q2.shape, 1) < NA_HEAD_DIM
    outs = []
    for head_mask in (first, jnp.logical_not(first)):
        qa = jnp.where(head_mask, q2, jnp.zeros_like(q2))
        scores = []
        for k, _, bias in parts:
            s = _dot_nt(qa, k)
            scores.append(s if bias is None else s + bias)
        m = functools.reduce(jnp.maximum, [jnp.max(s, axis=-1, keepdims=True) for s in scores])
        probs = [jnp.exp(s - m) for s in scores]
        denom = functools.reduce(jnp.add, [jnp.sum(p, axis=-1, keepdims=True) for p in probs])
        acc = functools.reduce(jnp.add, [_dot(p.astype(MXU_DTYPE), v) for p, (_, v, _) in zip(probs, parts)])
        outs.append(acc / denom)
    return jnp.where(first, outs[0], outs[1])


def _natten_kernel(q_ref, k_ref, v_ref, kc_ref, vc_ref, tbl_ref, o_ref, *, rb, rows):
    i = pl.program_id(1)
    kc, vc = kc_ref[...], vc_ref[...]
    n_win = NA_WIN_ROWS * GRID_W
    first = lax.broadcasted_iota(jnp.int32, (GRID_W, LANES), 1) < NA_HEAD_DIM
    for t in range(rb):
        r = i * rb + t
        start = jnp.clip(r - NA_WIN_ROWS // 2, 0, rows - NA_WIN_ROWS)
        dr0 = start - r + (NA_WIN_ROWS - 1)
        ks = pl.ds(pl.multiple_of(start * GRID_W, GRID_W), n_win)
        q2 = q_ref[t * GRID_W:(t + 1) * GRID_W, :]
        kw, vw = k_ref[ks, :], v_ref[ks, :]
        zero = jnp.zeros_like(q2)
        qs = jnp.concatenate([jnp.where(first, q2, zero), jnp.where(first, zero, q2)], axis=0)
        s_loc = _dot_nt(qs, kw) + tbl_ref[dr0]
        s_ctx = _dot_nt(qs, kc)
        m = jnp.maximum(jnp.max(s_loc, axis=-1, keepdims=True), jnp.max(s_ctx, axis=-1, keepdims=True))
        p_loc, p_ctx = jnp.exp(s_loc - m), jnp.exp(s_ctx - m)
        denom = jnp.sum(p_loc, axis=-1, keepdims=True) + jnp.sum(p_ctx, axis=-1, keepdims=True)
        o = (_dot(p_loc.astype(MXU_DTYPE), vw) + _dot(p_ctx.astype(MXU_DTYPE), vc)) / denom
        o_ref[t * GRID_W:(t + 1) * GRID_W, :] = jnp.where(first, o[:GRID_W], o[GRID_W:]).astype(o_ref.dtype)


def _bias_table(rpb):
    col = jnp.arange(GRID_W)
    c0 = jnp.clip(col - NA_WIN_COLS // 2, 0, GRID_W - NA_WIN_COLS)
    col_in = (col[None, :] >= c0[:, None]) & (col[None, :] < c0[:, None] + NA_WIN_COLS)
    dc = jnp.clip(col[None, :] - col[:, None], 1 - NA_WIN_COLS, NA_WIN_COLS - 1) + (NA_WIN_COLS - 1)
    pick = (dc[None] == jnp.arange(2 * NA_WIN_COLS - 1)[:, None, None]).astype(F32)
    t = jnp.einsum('hrc,cqk->hrqk', rpb.astype(F32), pick, precision=HI)
    t = jnp.where(col_in, t, NEG)
    t = jnp.stack([t[:, j:j + NA_WIN_ROWS] for j in range(NA_WIN_ROWS)], axis=2)
    t = t.reshape(NA_HEADS // 2, 2, NA_WIN_ROWS, NA_WIN_ROWS, GRID_W, GRID_W)
    return t.transpose(0, 2, 1, 4, 3, 5).reshape(NA_HEADS // 2, NA_WIN_ROWS, 2 * GRID_W, NA_WIN_ROWS * GRID_W)


def _natten(qkv, qkv_ctx, rpb):
    n_rows = qkv.shape[0]
    rows = n_rows // GRID_W
    n_ctx = qkv_ctx.shape[0]
    rb = _pick(rows, (8, 4, 2, 1))
    n_win = NA_WIN_ROWS * GRID_W
    nb = NA_D // LANES
    seq = lambda n, off: pl.BlockSpec((n, LANES), lambda p, i: (0, p + off))
    tile = pl.BlockSpec((rb * GRID_W, LANES), lambda p, i: (i, p))
    return pl.pallas_call(
        functools.partial(_natten_kernel, rb=rb, rows=rows),
        grid=(NA_HEADS // 2, rows // rb),
        in_specs=[tile, seq(n_rows, nb), seq(n_rows, 2 * nb), seq(n_ctx, nb), seq(n_ctx, 2 * nb),
                  pl.BlockSpec((None, NA_WIN_ROWS, 2 * GRID_W, n_win), lambda p, i: (p, 0, 0, 0))],
        out_specs=tile,
        out_shape=jax.ShapeDtypeStruct((n_rows, NA_D), MXU_DTYPE),
        compiler_params=_params("parallel", "arbitrary"),
        name="natten",
    )(qkv, qkv, qkv, qkv_ctx, qkv_ctx, _bias_table(rpb))


def _ctx_attn_kernel(q_ref, k_ref, v_ref, o_ref):
    o_ref[...] = _attend(q_ref[...], [(k_ref[...], v_ref[...], None)]).astype(o_ref.dtype)


def _ctx_attn(qkv_ctx):
    n_ctx = qkv_ctx.shape[0]
    nb = NA_D // LANES
    spec = lambda off: pl.BlockSpec((n_ctx, LANES), lambda p: (0, p + off))
    return pl.pallas_call(
        _ctx_attn_kernel,
        grid=(NA_HEADS // 2,),
        in_specs=[spec(0), spec(nb), spec(2 * nb)],
        out_specs=spec(0),
        out_shape=jax.ShapeDtypeStruct((n_ctx, NA_D), MXU_DTYPE),
        compiler_params=_params("parallel"),
        name="ctx_attn",
    )(qkv_ctx, qkv_ctx, qkv_ctx)


def _load_slabs(ref, n, s):
    return jnp.concatenate([ref[pl.ds(c, n, stride=s), :] for c in range(s)], axis=1)


def _store_slabs(ref, val):
    n = val.shape[0]
    s = val.shape[1] // LANES
    for c in range(s):
        ref[pl.ds(c, n, stride=s), :] = val[:, c * LANES:(c + 1) * LANES].astype(ref.dtype)

def _router_kernel(*refs, n_lat_tiles, with_ctx):
    if with_ctx:
        x_ref, cx_ref, nw_ref, sh_ref, sc_ref, shc_ref, scc_ref, wr_ref, m_ref, e_ref, g_ref = refs
        is_ctx = pl.program_id(0) >= n_lat_tiles
        x = jnp.where(is_ctx, cx_ref[...], x_ref[...])
        sh = jnp.where(is_ctx, shc_ref[...], sh_ref[...])
        sc = jnp.where(is_ctx, scc_ref[...], sc_ref[...])
    else:
        x_ref, nw_ref, sh_ref, sc_ref, wr_ref, m_ref, e_ref, g_ref = refs
        x, sh, sc = x_ref[...], sh_ref[...], sc_ref[...]
    m = _norm_mod(x, nw_ref[...], sh, sc)
    _store_slabs(m_ref, m)
    logits = _dot(m, wr_ref[...], HI)
    lane = lax.broadcasted_iota(jnp.int32, logits.shape, 1)
    big = jnp.int32(LANES)

    def top(vals):
        v = jnp.max(vals, axis=-1, keepdims=True)
        idx = jnp.min(jnp.where(vals == v, lane, big), axis=-1, keepdims=True)
        return v, idx

    gl = jnp.where(lane < N_GROUPS, logits, -jnp.inf)
    g_max, grp = top(gl)
    p_grp = 1.0 / jnp.sum(jnp.exp(gl - g_max), axis=-1, keepdims=True)
    e_lane = lane - N_GROUPS
    in_grp = (e_lane >= grp * EXPERTS_PER_GROUP) & (e_lane < (grp + 1) * EXPERTS_PER_GROUP)
    el = jnp.where(in_grp, logits, -jnp.inf)
    v1, i1 = top(el)
    v2, i2 = top(jnp.where(lane == i1, -jnp.inf, el))
    t = jnp.exp(v2 - v1)
    g1 = p_grp / (1.0 + t)
    g2 = p_grp * t / (1.0 + t)
    e_ref[...] = jnp.where(lane == 0, i1 - N_GROUPS, jnp.where(lane == 1, i2 - N_GROUPS, 0))
    g_ref[...] = jnp.where(lane == 0, g1, jnp.where(lane == 1, g2, 0.0))


def _router(lat, cx, nw, mod_lat, mod_ctx, w_route):
    n_lat, d = lat.shape
    with_ctx = cx is not None
    n_ctx = cx.shape[0] if with_ctx else 0
    tm = _pick(n_lat, (256, 128)) if with_ctx else _pick(n_lat, (512, 256, 128))
    assert n_ctx % tm == 0
    n_lat_tiles = n_lat // tm
    n_tok = n_lat + n_ctx
    lat_tile = pl.BlockSpec((tm, d), lambda i: (jnp.minimum(i, n_lat_tiles - 1), 0))
    vec = lambda v: _vec_spec(v, d, _zero)
    wide = pl.BlockSpec((tm, LANES), lambda i: (i, 0))
    w_spec = pl.BlockSpec((d, LANES), lambda i: (0, 0))
    if with_ctx:
        ctx_tile = pl.BlockSpec((tm, d), lambda i: (jnp.maximum(i - n_lat_tiles, 0), 0))
        in_specs = [lat_tile, ctx_tile, vec(nw), vec(mod_lat[3]), vec(mod_lat[4]), vec(mod_ctx[3]), vec(mod_ctx[4]),
                    w_spec]
        args = (lat, cx, nw[0], mod_lat[3][0], mod_lat[4][0], mod_ctx[3][0], mod_ctx[4][0], w_route)
    else:
        in_specs = [lat_tile, vec(nw), vec(mod_lat[3]), vec(mod_lat[4]), w_spec]
        args = (lat, nw[0], mod_lat[3][0], mod_lat[4][0], w_route)
    m, e, g = pl.pallas_call(
        functools.partial(_router_kernel, n_lat_tiles=n_lat_tiles, with_ctx=with_ctx),
        grid=(n_tok // tm,),
        in_specs=in_specs,
        out_specs=[pl.BlockSpec((tm * (d // LANES), LANES), lambda i: (i, 0)), wide, wide],
        out_shape=[jax.ShapeDtypeStruct((n_tok * (d // LANES), LANES), F32),
                   jax.ShapeDtypeStruct((n_tok, LANES), jnp.int32),
                   jax.ShapeDtypeStruct((n_tok, LANES), F32)],
        compiler_params=_params("parallel"),
        name="router",
    )(*args)
    return m, e[:, :TOP_K], g


def _moe_kernel(first_ref, nblk_ref, tok_ref, nact_ref, m_hbm, wg_hbm, wu_hbm, wd_hbm, yb_hbm,
                xbuf, ybuf, gsem, osem, wgf_ref, wuf_ref, wdf_ref, wsem, wgb_ref, wub_ref, wdb_ref, *, layer):
    e, n_e = pl.program_id(0), pl.num_programs(0)
    n_act = nact_ref[0]
    s = wgf_ref.shape[1] // LANES
    blk_rows = MOE_BLOCK * s
    n_blocks = yb_hbm.shape[0] // blk_rows

    def weight_copies(expert, ws):
        return [pltpu.make_async_copy(src.at[layer, expert], dst.at[ws], wsem.at[k, ws])
                for k, (src, dst) in enumerate(((wg_hbm, wgf_ref), (wu_hbm, wuf_ref), (wd_hbm, wdf_ref)))]

    @pl.when(e == 0)
    def _():
        for ahead in range(MOE_WEIGHT_AHEAD):
            for copy in weight_copies(ahead, ahead):
                copy.start()

    ws = e % MOE_WEIGHT_SLOTS
    for copy in weight_copies(e, ws):
        copy.wait()

    @pl.when(e + MOE_WEIGHT_AHEAD < n_e)
    def _():
        for copy in weight_copies(e + MOE_WEIGHT_AHEAD, (e + MOE_WEIGHT_AHEAD) % MOE_WEIGHT_SLOTS):
            copy.start()

    def gather_start(blk, slot):
        for r in range(MOE_BLOCK):
            tok = tok_ref[blk * MOE_BLOCK + r]
            pltpu.make_async_copy(m_hbm.at[pl.ds(pl.multiple_of(tok * s, s), s), :],
                                  xbuf.at[slot, pl.ds(r * s, s), :],
                                  gsem.at[slot]).start(priority=r % ROW_DMA_THREADS)

    def gather_wait(slot):
        pltpu.make_async_copy(m_hbm.at[pl.ds(0, blk_rows), :], xbuf.at[slot], gsem.at[slot]).wait()

    def out_copy(blk, slot):
        rows = pl.ds(pl.multiple_of(blk * blk_rows, blk_rows), blk_rows)
        return pltpu.make_async_copy(ybuf.at[slot], yb_hbm.at[rows, :], osem.at[slot])

    @pl.when(e == 0)
    def _():
        for ahead in range(MOE_AHEAD):
            gather_start(jnp.minimum(ahead, n_act - 1), ahead)

    @pl.when(nblk_ref[e] > 0)
    def _():
        wgb_ref[...] = wgf_ref[ws].astype(wgb_ref.dtype)
        wub_ref[...] = wuf_ref[ws].astype(wub_ref.dtype)
        wdb_ref[...] = wdf_ref[ws].astype(wdb_ref.dtype)

    def block(j, carry):
        b = first_ref[e] + j
        slot = b % MOE_SLOTS
        gather_wait(slot)

        @pl.when(b >= MOE_SLOTS)
        def _():
            out_copy(b - MOE_SLOTS, slot).wait()

        gather_start(jnp.minimum(b + MOE_AHEAD, n_act - 1), (b + MOE_AHEAD) % MOE_SLOTS)
        x = _load_slabs(xbuf.at[slot], MOE_BLOCK, s).astype(MXU_DTYPE)
        h = _silu(_dot(x, wgb_ref[...])) * _dot(x, wub_ref[...])
        _store_slabs(ybuf.at[slot], _dot(h.astype(MXU_DTYPE), wdb_ref[...]))
        out_copy(b, slot).start()
        return carry

    lax.fori_loop(0, nblk_ref[e], block, 0)

    @pl.when(e == n_e - 1)
    def _():
        for ahead in range(MOE_AHEAD):
            gather_wait((n_act + ahead) % MOE_SLOTS)
        for back in range(1, MOE_SLOTS + 1):
            @pl.when(n_act >= back)
            def _():
                out_copy(n_act - back, (n_act - back) % MOE_SLOTS).wait()

        ybuf[0] = jnp.zeros(ybuf.shape[1:], ybuf.dtype)

        def fill(b, carry):
            out_copy(b, 0).start()
            out_copy(b, 0).wait()
            return carry

        lax.fori_loop(n_act, n_blocks, fill, 0)


def _moe_ffn(m_tok, slot_tok, first_blk, n_blk, n_act, w_gate, w_up, w_down, layer):
    n_rows = slot_tok.shape[0]
    d, f = w_gate.shape[-2:]
    blk_rows = MOE_BLOCK * (d // LANES)
    hbm = pl.BlockSpec(memory_space=pl.ANY)
    grid_spec = pltpu.PrefetchScalarGridSpec(
        num_scalar_prefetch=4,
        grid=(N_EXPERTS,),
        in_specs=[hbm, hbm, hbm, hbm],
        out_specs=hbm,
        scratch_shapes=[pltpu.VMEM((MOE_SLOTS, blk_rows, LANES), F32), pltpu.VMEM((MOE_SLOTS, blk_rows, LANES), F32),
                        pltpu.SemaphoreType.DMA((MOE_SLOTS,)), pltpu.SemaphoreType.DMA((MOE_SLOTS,)),
                        pltpu.VMEM((MOE_WEIGHT_SLOTS, d, f), F32), pltpu.VMEM((MOE_WEIGHT_SLOTS, d, f), F32),
                        pltpu.VMEM((MOE_WEIGHT_SLOTS, f, d), F32), pltpu.SemaphoreType.DMA((3, MOE_WEIGHT_SLOTS)),
                        pltpu.VMEM((d, f), MXU_DTYPE), pltpu.VMEM((d, f), MXU_DTYPE), pltpu.VMEM((f, d), MXU_DTYPE)],
    )
    return pl.pallas_call(
        functools.partial(_moe_kernel, layer=layer),
        grid_spec=grid_spec,
        out_shape=jax.ShapeDtypeStruct((n_rows * (d // LANES), LANES), F32),
        compiler_params=_params("arbitrary"),
        name="moe_ffn",
    )(first_blk, n_blk, slot_tok, n_act, m_tok, w_gate, w_up, w_down)


def _combine_kernel(pos_ref, x_ref, g_ref, gt_ref, yb_hbm, o_ref, buf, sem, *, tm):
    i, n_i = pl.program_id(0), pl.num_programs(0)
    s = x_ref.shape[1] // LANES

    def gather_start(tile, slot):
        for j in range(tm):
            for k in range(TOP_K):
                p = pos_ref[(tile * tm + j) * TOP_K + k]
                pltpu.make_async_copy(yb_hbm.at[pl.ds(pl.multiple_of(p * s, s), s), :],
                                      buf.at[slot, k, pl.ds(j * s, s), :],
                                      sem.at[slot]).start(priority=(j * TOP_K + k) % ROW_DMA_THREADS)

    def gather_wait(slot):
        for k in range(TOP_K):
            pltpu.make_async_copy(yb_hbm.at[pl.ds(0, tm * s), :], buf.at[slot, k], sem.at[slot]).wait()

    @pl.when(i == 0)
    def _():
        for ahead in range(MOE_AHEAD):
            gather_start(jnp.minimum(ahead, n_i - 1), ahead)

    slot = i % MOE_SLOTS
    gather_wait(slot)
    gather_start(jnp.minimum(i + MOE_AHEAD, n_i - 1), (i + MOE_AHEAD) % MOE_SLOTS)
    gt = gt_ref[...]
    y0 = _load_slabs(buf.at[slot, 0], tm, s)
    y1 = _load_slabs(buf.at[slot, 1], tm, s)
    o_ref[...] = x_ref[...] + g_ref[...] * (gt[:, 0:1] * y0 + gt[:, 1:2] * y1)

    @pl.when(i == n_i - 1)
    def _():
        for ahead in range(1, MOE_AHEAD + 1):
            gather_wait((i + ahead) % MOE_SLOTS)


def _combine(x, g, gates, yb, pos):
    m, d = x.shape
    tm = MOE_BLOCK
    grid_spec = pltpu.PrefetchScalarGridSpec(
        num_scalar_prefetch=1,
        grid=(m // tm,),
        in_specs=[pl.BlockSpec((tm, d), lambda i, pos: (i, 0)),
                  _vec_spec(g, d, _zero),
                  pl.BlockSpec((tm, LANES), lambda i, pos: (i, 0)),
                  pl.BlockSpec(memory_space=pl.ANY)],
        out_specs=pl.BlockSpec((tm, d), lambda i, pos: (i, 0)),
        scratch_shapes=[pltpu.VMEM((MOE_SLOTS, TOP_K, tm * (d // LANES), LANES), F32),
                        pltpu.SemaphoreType.DMA((MOE_SLOTS,))],
    )
    return pl.pallas_call(
        functools.partial(_combine_kernel, tm=tm),
        grid_spec=grid_spec,
        out_shape=jax.ShapeDtypeStruct((m, d), F32),
        compiler_params=_params("arbitrary"),
        name="moe_combine",
    )(pos, x, g[0], gates, yb)


def _dispatch_kernel(e_ref, dest_ref, first_ref, nblk_ref, cnt_ref, base_ref):
    phase, i = pl.program_id(0), pl.program_id(1)
    tile = e_ref.shape[1]
    onehot = lax.broadcasted_iota(jnp.int32, (N_EXPERTS, tile), 0) == e_ref[...]

    @pl.when(jnp.logical_and(phase == 0, i == 0))
    def _():
        cnt_ref[...] = jnp.zeros_like(cnt_ref)

    @pl.when(phase == 0)
    def _():
        cnt_ref[...] += jnp.sum(onehot.astype(F32), axis=1, keepdims=True)

    @pl.when(jnp.logical_and(phase == 1, i == 0))
    def _():
        n_blk = jnp.right_shift(cnt_ref[...].astype(jnp.int32) + (MOE_BLOCK - 1),
                                MOE_BLOCK.bit_length() - 1)
        tri = (lax.broadcasted_iota(jnp.int32, (N_EXPERTS, N_EXPERTS), 0)
               >= lax.broadcasted_iota(jnp.int32, (N_EXPERTS, N_EXPERTS), 1)).astype(F32)
        blk_end = _dot(tri, n_blk.astype(F32), HI)
        first = blk_end.astype(jnp.int32) - n_blk
        nblk_ref[...] = n_blk
        first_ref[...] = first
        base_ref[...] = (first * MOE_BLOCK).astype(F32)

    @pl.when(phase == 1)
    def _():
        tri = (lax.broadcasted_iota(jnp.int32, (tile, tile), 0)
               <= lax.broadcasted_iota(jnp.int32, (tile, tile), 1)).astype(MXU_DTYPE)
        cum = _dot(onehot.astype(MXU_DTYPE), tri)
        slot = jnp.where(onehot, base_ref[:, 0:1] + cum - 1.0, 0.0)
        dest_ref[...] = jnp.sum(slot, axis=0, keepdims=True).astype(jnp.int32)
        base_ref[...] += cum[:, tile - 1:tile]


def _slot_token_kernel(dest_ref, tok_ref):
    def clear(s, carry):
        tok_ref[s] = 0
        return carry

    def put(p, carry):
        tok_ref[dest_ref[p]] = jnp.right_shift(p, TOP_K.bit_length() - 1)
        return carry

    lax.fori_loop(0, tok_ref.shape[0], clear, 0, unroll=8)
    lax.fori_loop(0, dest_ref.shape[0], put, 0, unroll=8)


def _dispatch(expert):
    n_tok = expert.shape[0]
    n = n_tok * TOP_K
    n_blocks = -(-n // MOE_BLOCK) + N_EXPERTS
    n_rows = n_blocks * MOE_BLOCK
    tile = _pick(n, (512, 256, 128))
    per_expert = pl.BlockSpec((N_EXPERTS, LANES), lambda ph, i: (0, 0))
    dest, first, n_blk = pl.pallas_call(
        _dispatch_kernel,
        grid=(2, n // tile),
        in_specs=[pl.BlockSpec((1, tile), lambda ph, i: (0, i))],
        out_specs=[pl.BlockSpec((1, tile), lambda ph, i: (0, i * ph)), per_expert, per_expert],
        out_shape=[jax.ShapeDtypeStruct((1, n), jnp.int32),
                   jax.ShapeDtypeStruct((N_EXPERTS, LANES), jnp.int32),
                   jax.ShapeDtypeStruct((N_EXPERTS, LANES), jnp.int32)],
        scratch_shapes=[pltpu.VMEM((N_EXPERTS, LANES), F32), pltpu.VMEM((N_EXPERTS, LANES), F32)],
        compiler_params=_params("arbitrary", "arbitrary"),
        name="moe_dispatch",
    )(expert.reshape(1, n))
    dest, first_blk, n_blk = dest[0], first[:, 0], n_blk[:, 0]
    slot_tok = pl.pallas_call(
        _slot_token_kernel,
        in_specs=[pl.BlockSpec(memory_space=pltpu.SMEM)],
        out_specs=pl.BlockSpec(memory_space=pltpu.SMEM),
        out_shape=jax.ShapeDtypeStruct((n_rows,), jnp.int32),
        name="moe_slot_token",
    )(dest)
    n_act = first_blk[-1:] + n_blk[-1:]
    return dest, slot_tok, first_blk, n_blk, n_act


def _even_mixer(lat, cx, nw, mod_lat, mod_ctx, e, w_in, conv_w, conv_b, dt_bias, a_log, d_skip, ssd_norm_w,
                q_norm_w, k_norm_w, rpb, w_out, ctx_out):
    w_e = w_in[e]
    w_qkvd = jnp.concatenate([w_e[:, S_DT:], w_e[:, S_ZX:S_DT],
                              jnp.zeros((w_e.shape[0], LANES - 2 * SSD_HEADS), w_e.dtype)], axis=1)

    def project(x, mod, tag):
        sh, sc = mod[0], mod[1]
        zx = _nm_matmul(x, nw, sh, sc, w_in, e, S_ZX, "proj_zx_" + tag)
        qkv, dt = _qkv_proj(x, nw, sh, sc, w_qkvd, q_norm_w[e], k_norm_w[e], "proj_qkvd_" + tag)
        xbc = _xbc_conv(zx, conv_w[e], conv_b[e])
        return zx, xbc, dt[:, :2 * SSD_HEADS], qkv

    zx_c, xbc_c, dt_c, qkv_c = project(cx, mod_ctx, "ctx")
    zx_l, xbc_l, dt_l, qkv_l = project(lat, mod_lat, "lat")
    h0 = jnp.zeros((2, SSD_PAIRS, SSD_STATE, LANES), F32)
    y_c, h_ctx = _ssd(xbc_c, dt_c, dt_bias[e], a_log[e], h0)
    y_l, _ = _ssd(xbc_l, dt_l, dt_bias[e], a_log[e], h_ctx)
    yssd_l = _ssd_out(y_l, xbc_l, zx_l, d_skip[e], ssd_norm_w[e])
    yatt_l = _natten(qkv_l, qkv_c, rpb[e])
    lat = _matmul_res([yssd_l, yatt_l], w_out, e, lat, mod_lat[2], "out_even_lat")
    if ctx_out:
        yssd_c = _ssd_out(y_c, xbc_c, zx_c, d_skip[e], ssd_norm_w[e])
        yatt_c = _ctx_attn(qkv_c)
        cx = _matmul_res([yssd_c, yatt_c], w_out, e, cx, mod_ctx[2], "out_even_ctx")
    return lat, cx


def _odd_mixer(lat, cx, nw, mod_lat, mod_ctx, o, w_in, conv_w, w_out, ctx_out):
    def mix(x, mod, tag):
        u = _gated_conv_proj(x, nw, mod[0], mod[1], w_in, o, conv_w[o], "proj_odd_" + tag)
        return _matmul_res([u], w_out, o, x, mod[2], "out_odd_" + tag)

    lat = mix(lat, mod_lat, "lat")
    if ctx_out:
        cx = mix(cx, mod_ctx, "ctx")
    return lat, cx


def kernel(x, c, ctx, c_ctx, w_mod, b_mod, norm_mix_w, norm_ffn_w, ev_w_in, ev_conv_w, ev_conv_b, ev_dt_bias,
           ev_a_log, ev_d_skip, ev_ssd_norm_w, ev_q_norm_w, ev_k_norm_w, ev_rpb, ev_w_out, od_w_in, od_conv_w,
           od_w_out, moe_w_group, moe_w_router, moe_w_gate, moe_w_up, moe_w_down):
    bsz, n_lat, d = x.shape
    assert bsz == 1 and ctx.shape[0] == 1
    assert n_lat % GRID_W == 0 and n_lat // GRID_W >= NA_WIN_ROWS and n_lat % SSD_CHUNK == 0
    assert ctx.shape[1] % SSD_CHUNK == 0 and d % LANES == 0
    depth = w_mod.shape[0]
    lat, cx = x[0], ctx[0]
    mods = _modulation(jnp.stack([c[0], c_ctx], axis=1), w_mod, b_mod)
    mods = mods.reshape(depth * 2 * 6, d)

    for l in range(depth):
        even = l % 2 == 0
        ctx_out = any(j % 2 == 0 for j in range(l + 1, depth))
        mod_lat = [_vec(mods, (l * 2 + 0) * 6 + i) for i in range(6)]
        mod_ctx = [_vec(mods, (l * 2 + 1) * 6 + i) for i in range(6)]
        nw = _vec(norm_mix_w, l)
        if even:
            lat, cx = _even_mixer(lat, cx, nw, mod_lat, mod_ctx, l // 2, ev_w_in, ev_conv_w, ev_conv_b,
                                  ev_dt_bias, ev_a_log, ev_d_skip, ev_ssd_norm_w, ev_q_norm_w,
                                  ev_k_norm_w, ev_rpb, ev_w_out, ctx_out)
        else:
            lat, cx = _odd_mixer(lat, cx, nw, mod_lat, mod_ctx, l // 2, od_w_in, od_conv_w, od_w_out, ctx_out)

        nfw = _vec(norm_ffn_w, l)
        w_route = jnp.concatenate([moe_w_group[l], moe_w_router[l],
                                   jnp.zeros((d, LANES - N_GROUPS - N_EXPERTS), F32)], axis=1)
        m_tok, expert, gates = _router(lat, cx if ctx_out else None, nfw, mod_lat, mod_ctx, w_route)
        dest, slot_tok, first_blk, n_blk, n_act = _dispatch(expert)
        yb = _moe_ffn(m_tok, slot_tok, first_blk, n_blk, n_act, moe_w_gate, moe_w_up, moe_w_down, l)
        lat = _combine(lat, mod_lat[5], gates[:n_lat], yb, dest[:n_lat * TOP_K])
        if ctx_out:
            cx = _combine(cx, mod_ctx[5], gates[n_lat:], yb, dest[n_lat * TOP_K:])
    return lat[None]
```

```python
import functools

import jax
import jax.numpy as jnp
from jax import lax
from jax.experimental import pallas as pl
from jax.experimental.pallas import tpu as pltpu

F32 = jnp.float32
MXU_DTYPE = jnp.bfloat16
HI = lax.Precision.HIGHEST
EPS = 1e-6
NEG = -1e30

LANES = 128
SUBLANES = 8
VMEM_LIMIT = 48 * 1024 * 1024
ROW_DMA_THREADS = 2

SSD_HEADS = 16
SSD_HEAD_DIM = 64
SSD_D_INNER = SSD_HEADS * SSD_HEAD_DIM
SSD_STATE = 128
SSD_GROUPS = 2
SSD_CHUNK = 128
SSD_BC = SSD_GROUPS * SSD_STATE
SSD_XBC = SSD_D_INNER + 2 * SSD_BC
SSD_PAIRS = SSD_HEADS // 2
NA_HEADS = 16
NA_HEAD_DIM = 64
NA_D = NA_HEADS * NA_HEAD_DIM
NA_WIN_ROWS = 8
NA_WIN_COLS = 16
GRID_W = 64
N_GROUPS = 8
EXPERTS_PER_GROUP = 8
N_EXPERTS = N_GROUPS * EXPERTS_PER_GROUP
TOP_K = 2
D_FF_EXPERT = 384
MOE_BLOCK = 128
MOE_AHEAD = 2
MOE_SLOTS = MOE_AHEAD + 1
MOE_WEIGHT_AHEAD = 2
MOE_WEIGHT_SLOTS = MOE_WEIGHT_AHEAD + 1
S_ZX = SSD_D_INNER + SSD_XBC
S_DT = S_ZX + 2 * SSD_HEADS
QKVD = 3 * NA_D + LANES


def _pick(n, prefs):
    for p in prefs:
        if n % p == 0:
            return p
    return n


def _params(*sem):
    return pltpu.CompilerParams(dimension_semantics=sem, vmem_limit_bytes=VMEM_LIMIT)


def _vec(arr, idx):
    return arr.reshape(arr.shape[0], 1, arr.shape[-1]), idx


def _vec_spec(vec, width, col):
    idx = vec[1]
    return pl.BlockSpec((None, 1, width), lambda *g: (idx, 0, col(*g)))


def _mat_spec(layer, block, idx):
    return pl.BlockSpec((None,) + block, lambda *g: (layer,) + idx(*g))


def _zero(*g):
    return 0


def _silu(x):
    return x * (1.0 / (1.0 + jnp.exp(-x)))


def _softplus(x):
    return jnp.maximum(x, 0.0) + jnp.log1p(jnp.exp(-jnp.abs(x)))


def _dot(a, b, precision=None):
    return jnp.dot(a, b, preferred_element_type=F32, precision=precision)


def _dot_select(a, b, split_lhs):
    x = a if split_lhs else b
    hi = x.astype(MXU_DTYPE)
    rest = x - hi.astype(F32)
    mid = rest.astype(MXU_DTYPE)
    lo = (rest - mid.astype(F32)).astype(MXU_DTYPE)
    if split_lhs:
        return _dot(hi, b) + _dot(mid, b) + _dot(lo, b)
    return _dot(a, hi) + _dot(a, mid) + _dot(a, lo)


def _dot_nt(a, b):
    return lax.dot_general(a, b, (((1,), (1,)), ((), ())), preferred_element_type=F32)


def _norm_mod(x, nw, sh, sc):
    ms = jnp.mean(x * x, axis=-1, keepdims=True)
    n = x * lax.rsqrt(ms + EPS) * nw
    return n * (1.0 + sc) + sh


def _mod_kernel(ct_ref, w_ref, b_ref, o_ref):
    ct = ct_ref[...]
    s = _silu(ct)
    w = w_ref[...]
    r0 = jnp.sum(s[:, 0:1] * w, axis=0, keepdims=True)
    r1 = jnp.sum(s[:, 1:2] * w, axis=0, keepdims=True)
    o_ref[...] = jnp.concatenate([r0, r1], axis=0) + b_ref[...]


def _modulation(ct, w_mod, b_mod):
    depth, d, n = w_mod.shape
    tn = _pick(n, (512, 256, 128))
    return pl.pallas_call(
        _mod_kernel,
        grid=(depth, n // tn),
        in_specs=[pl.BlockSpec((d, 2), lambda l, j: (0, 0)),
                  pl.BlockSpec((None, d, tn), lambda l, j: (l, 0, j)),
                  pl.BlockSpec((None, 1, tn), lambda l, j: (l, 0, j))],
        out_specs=pl.BlockSpec((None, 2, tn), lambda l, j: (l, 0, j)),
        out_shape=jax.ShapeDtypeStruct((depth, 2, n), F32),
        compiler_params=_params("parallel", "parallel"),
        name="modulation",
    )(ct, w_mod, b_mod.reshape(depth, 1, n))


def _repack_w_in_kernel(w_ref, o_ref):
    w = w_ref[...]
    rows = w.shape[0]
    o_ref[:, 0:S_ZX] = w[:, 0:S_ZX].astype(o_ref.dtype)
    o_ref[:, S_ZX:S_ZX + 3 * NA_D] = w[:, S_DT:S_DT + 3 * NA_D].astype(o_ref.dtype)
    tail = jnp.concatenate([w[:, S_ZX:S_DT], jnp.zeros((rows, LANES - 2 * SSD_HEADS), w.dtype)], axis=1)
    o_ref[:, S_ZX + 3 * NA_D:] = tail.astype(o_ref.dtype)


def _repack_w_in(w_in):
    layers, d, n_in = w_in.shape
    tr = _pick(d, (256, 128))
    n_out = S_ZX + QKVD
    return pl.pallas_call(
        _repack_w_in_kernel,
        grid=(layers, d // tr),
        in_specs=[pl.BlockSpec((None, tr, n_in), lambda l, i: (l, i, 0))],
        out_specs=pl.BlockSpec((None, tr, n_out), lambda l, i: (l, i, 0)),
        out_shape=jax.ShapeDtypeStruct((layers, d, n_out), MXU_DTYPE),
        compiler_params=_params("parallel", "parallel"),
        name="repack_w_in",
    )(w_in)


def _matmul_res_kernel(*refs, n_a):
    a_refs, w_refs = refs[:n_a], refs[n_a:2 * n_a]
    res_ref, g_ref, o_ref = refs[2 * n_a:]
    acc = None
    for a_ref, w_ref in zip(a_refs, w_refs):
        t = _dot(a_ref[...].astype(MXU_DTYPE), w_ref[...].astype(MXU_DTYPE))
        acc = t if acc is None else acc + t
    o_ref[...] = res_ref[...] + g_ref[...] * acc


def _matmul_res(a_list, w, layer, res, g, name):
    n_a = len(a_list)
    m, n = res.shape
    kp = w.shape[1] // n_a
    tm = _pick(m, (2048, 1024, 512, 256, 128))
    tn = _pick(n, (512, 256, 128))
    in_specs = [pl.BlockSpec((tm, kp), lambda i, j: (i, 0)) for _ in range(n_a)]
    in_specs += [_mat_spec(layer, (kp, tn), functools.partial(lambda i, j, p: (p, j), p=p)) for p in range(n_a)]
    in_specs += [pl.BlockSpec((tm, tn), lambda i, j: (i, j)), _vec_spec(g, tn, lambda i, j: j)]
    return pl.pallas_call(
        functools.partial(_matmul_res_kernel, n_a=n_a),
        grid=(m // tm, n // tn),
        in_specs=in_specs,
        out_specs=pl.BlockSpec((tm, tn), lambda i, j: (i, j)),
        out_shape=jax.ShapeDtypeStruct((m, n), F32),
        compiler_params=_params("parallel", "parallel"),
        name=name,
    )(*a_list, *([w] * n_a), res, g[0])


def _shifted(x, prev_ref, next_ref):
    i, n_i = pl.program_id(0), pl.num_programs(0)
    tl = x.shape[0]
    row = lax.broadcasted_iota(jnp.int32, x.shape, 0)
    prev_row = jnp.where(i == 0, 0.0, prev_ref[SUBLANES - 1:SUBLANES, :])
    next_row = jnp.where(i == n_i - 1, 0.0, next_ref[0:1, :])
    up = jnp.where(row == 0, prev_row, pltpu.roll(x, 1, 0))
    down = jnp.where(row == tl - 1, next_row, pltpu.roll(x, tl - 1, 0))
    return up, down


def _halo_specs(tl, tc, n_rows, col_off):
    nb = tl // SUBLANES
    last = n_rows // SUBLANES - 1
    return [pl.BlockSpec((tl, tc), lambda i, j: (i, j + col_off)),
            pl.BlockSpec((SUBLANES, tc), lambda i, j: (jnp.maximum(i * nb - 1, 0), j + col_off)),
            pl.BlockSpec((SUBLANES, tc), lambda i, j: (jnp.minimum((i + 1) * nb, last), j + col_off))]


def _xbc_conv_kernel(x_ref, prev_ref, next_ref, w_ref, b_ref, o_ref):
    x = x_ref[...]
    up, down = _shifted(x, prev_ref, next_ref)
    w = w_ref[...]
    o_ref[...] = _silu(w[0:1] * up + w[1:2] * x + w[2:3] * down + b_ref[...])


def _xbc_conv(zx, conv_w, conv_b):
    n_rows = zx.shape[0]
    tl = _pick(n_rows, (512, 256, 128))
    tc = 512
    off = SSD_D_INNER // tc
    return pl.pallas_call(
        _xbc_conv_kernel,
        grid=(n_rows // tl, SSD_XBC // tc),
        in_specs=_halo_specs(tl, tc, n_rows, off) + [pl.BlockSpec((3, tc), lambda i, j: (0, j)),
                                                     pl.BlockSpec((1, tc), lambda i, j: (0, j))],
        out_specs=pl.BlockSpec((tl, tc), lambda i, j: (i, j)),
        out_shape=jax.ShapeDtypeStruct((n_rows, SSD_XBC), F32),
        compiler_params=_params("parallel", "parallel"),
        name="xbc_conv",
    )(zx, zx, zx, conv_w, conv_b.reshape(1, -1))


def _gated_conv_proj_kernel(x_ref, xp_ref, xn_ref, nw_ref, sh_ref, sc_ref, wb_ref, wc_ref, wx_ref, cw_ref, o_ref,
                            a_ref, ah_ref):
    i, n_i = pl.program_id(0), pl.num_programs(0)

    @pl.when(pl.program_id(1) == 0)
    def _():
        nw, sh, sc = nw_ref[...], sh_ref[...], sc_ref[...]
        a_ref[...] = _norm_mod(x_ref[...], nw, sh, sc).astype(a_ref.dtype)
        ah_ref[0:SUBLANES, :] = _norm_mod(xp_ref[...], nw, sh, sc)
        ah_ref[SUBLANES:, :] = _norm_mod(xn_ref[...], nw, sh, sc)

    a, ah = a_ref[...], ah_ref[...].astype(a_ref.dtype)
    wc, wx = wc_ref[...].astype(a.dtype), wx_ref[...].astype(a.dtype)
    u = _dot(a, wc) * _dot(a, wx)
    uh = _dot(ah, wc) * _dot(ah, wx)
    tm = u.shape[0]
    row = lax.broadcasted_iota(jnp.int32, u.shape, 0)
    prev_row = jnp.where(i == 0, 0.0, uh[SUBLANES - 1:SUBLANES, :])
    next_row = jnp.where(i == n_i - 1, 0.0, uh[SUBLANES:SUBLANES + 1, :])
    up = jnp.where(row == 0, prev_row, pltpu.roll(u, 1, 0))
    down = jnp.where(row == tm - 1, next_row, pltpu.roll(u, tm - 1, 0))
    cw = cw_ref[...]
    conv = cw[0:1] * up + cw[1:2] * u + cw[2:3] * down
    o_ref[...] = (_dot(a, wb_ref[...].astype(a.dtype)) * conv).astype(o_ref.dtype)


def _gated_conv_proj(x, nw, sh, sc, w, layer, conv_w, name):
    m, d = x.shape
    c = w.shape[2] // 3
    tm = _pick(m, (1024, 512, 256, 128))
    tc = _pick(c, (256, 128))
    nb = c // tc
    blocks = tm // SUBLANES
    last = m // SUBLANES - 1
    wspec = lambda off: _mat_spec(layer, (d, tc), lambda i, j: (0, j + off))
    return pl.pallas_call(
        _gated_conv_proj_kernel,
        grid=(m // tm, nb),
        in_specs=[pl.BlockSpec((tm, d), lambda i, j: (i, 0)),
                  pl.BlockSpec((SUBLANES, d), lambda i, j: (jnp.maximum(i * blocks - 1, 0), 0)),
                  pl.BlockSpec((SUBLANES, d), lambda i, j: (jnp.minimum((i + 1) * blocks, last), 0)),
                  _vec_spec(nw, d, _zero), _vec_spec(sh, d, _zero), _vec_spec(sc, d, _zero),
                  wspec(0), wspec(nb), wspec(2 * nb),
                  pl.BlockSpec((3, tc), lambda i, j: (0, j))],
        out_specs=pl.BlockSpec((tm, tc), lambda i, j: (i, j)),
        out_shape=jax.ShapeDtypeStruct((m, c), MXU_DTYPE),
        scratch_shapes=[pltpu.VMEM((tm, d), MXU_DTYPE), pltpu.VMEM((2 * SUBLANES, d), F32)],
        compiler_params=_params("parallel", "arbitrary"),
        name=name,
    )(x, x, x, nw[0], sh[0], sc[0], w, w, w, conv_w)


def _ssd_kernel(xbc_ref, dt_ref, dtt_ref, dtb_ref, dtbt_ref, alog_ref, alogt_ref, h0_ref,
                y_ref, hout_ref, h_ref):
    d, c, n_c = pl.program_id(0), pl.program_id(1), pl.num_programs(1)
    q = SSD_CHUNK

    @pl.when(c == 0)
    def _():
        h_ref[...] = h0_ref[...]

    dt = _softplus(dt_ref[...] + dtb_ref[...])
    dtt = _softplus(dtt_ref[...] + dtbt_ref[...])
    a = dt * -jnp.exp(alog_ref[...])
    at = dtt * -jnp.exp(alogt_ref[...])
    row = lax.broadcasted_iota(jnp.int32, (q, q), 0)
    col = lax.broadcasted_iota(jnp.int32, (q, q), 1)
    sign = jnp.where(d == 0, 1, -1)
    mask = (row - col) * sign >= 0
    tri = mask.astype(MXU_DTYPE)
    tri_t = ((col - row) * sign >= 0).astype(MXU_DTYPE)
    cs = _dot_select(tri, a, split_lhs=False)
    cst = _dot_select(at, tri_t, split_lhs=True)
    tot = jnp.broadcast_to(jnp.sum(a, axis=0, keepdims=True), (SUBLANES, SSD_HEADS))
    expand = (lax.broadcasted_iota(jnp.int32, (SSD_HEADS, SSD_D_INNER), 1) // SSD_HEAD_DIM
              == lax.broadcasted_iota(jnp.int32, (SSD_HEADS, SSD_D_INNER), 0)).astype(MXU_DTYPE)
    dt_e = _dot_select(dt, expand, split_lhs=True)
    cs_e = _dot_select(cs, expand, split_lhs=True)
    tot_e = _dot_select(tot, expand, split_lhs=True)[0:1]

    xs = xbc_ref[:, 0:SSD_D_INNER]
    xdt = xs * dt_e
    xdt_m = xdt.astype(MXU_DTYPE)
    xw_m = (xdt * jnp.exp(tot_e - cs_e)).astype(MXU_DTYPE)
    e_e = jnp.exp(cs_e)
    dec_e = jnp.exp(tot_e)
    first = lax.broadcasted_iota(jnp.int32, (q, LANES), 1) < SSD_HEAD_DIM

    ppg = SSD_PAIRS // SSD_GROUPS
    for g in range(SSD_GROUPS):
        bg = xbc_ref[:, SSD_D_INNER + g * SSD_STATE:SSD_D_INNER + (g + 1) * SSD_STATE]
        cg = xbc_ref[:, SSD_D_INNER + SSD_BC + g * SSD_STATE:SSD_D_INNER + SSD_BC + (g + 1) * SSD_STATE]
        bg_m, cg_m = bg.astype(MXU_DTYPE), cg.astype(MXU_DTYPE)
        bgt_m = bg.T.astype(MXU_DTYPE)
        scores = _dot_nt(cg_m, bg_m)
        for pp in range(ppg):
            p = g * ppg + pp
            sl = slice(p * LANES, (p + 1) * LANES)
            ys = []
            for hh in range(2):
                h = 2 * p + hh
                diff = cs[:, h:h + 1] - cst[h:h + 1, :]
                decay = jnp.exp(jnp.where(mask, diff, -jnp.inf))
                ys.append(_dot((scores * decay).astype(MXU_DTYPE), xdt_m[:, sl]))
            hp = h_ref[p]
            y_off = _dot(cg_m, hp.astype(MXU_DTYPE)) * e_e[:, sl]
            y_ref[:, sl] = jnp.where(first, ys[0], ys[1]) + y_off
            h_ref[p] = dec_e[:, sl] * hp + _dot(bgt_m, xw_m[:, sl])

    @pl.when(c == n_c - 1)
    def _():
        hout_ref[...] = h_ref[...]


def _ssd(xbc, dt_raw, dt_bias, a_log, h0):
    n_rows = xbc.shape[0]
    q = SSD_CHUNK
    n_c = n_rows // q
    dt = dt_raw.reshape(n_rows, 2, SSD_HEADS).transpose(1, 0, 2)
    dtt = dt.transpose(0, 2, 1)

    def chunk(d, c):
        return jnp.where(d == 0, c, n_c - 1 - c)

    small = lambda shape: pl.BlockSpec((None,) + shape, lambda d, c: (d, 0, 0))
    return pl.pallas_call(
        _ssd_kernel,
        grid=(2, n_c),
        in_specs=[pl.BlockSpec((q, SSD_XBC), lambda d, c: (chunk(d, c), 0)),
                  pl.BlockSpec((None, q, SSD_HEADS), lambda d, c: (d, chunk(d, c), 0)),
                  pl.BlockSpec((None, SSD_HEADS, q), lambda d, c: (d, 0, chunk(d, c))),
                  small((1, SSD_HEADS)), small((SSD_HEADS, 1)), small((1, SSD_HEADS)), small((SSD_HEADS, 1)),
                  pl.BlockSpec((None, SSD_PAIRS, SSD_STATE, LANES), lambda d, c: (d, 0, 0, 0))],
        out_specs=[pl.BlockSpec((None, q, SSD_D_INNER), lambda d, c: (d, chunk(d, c), 0)),
                   pl.BlockSpec((None, SSD_PAIRS, SSD_STATE, LANES), lambda d, c: (d, 0, 0, 0))],
        out_shape=[jax.ShapeDtypeStruct((2, n_rows, SSD_D_INNER), F32),
                   jax.ShapeDtypeStruct((2, SSD_PAIRS, SSD_STATE, LANES), F32)],
        scratch_shapes=[pltpu.VMEM((SSD_PAIRS, SSD_STATE, LANES), F32)],
        compiler_params=_params("arbitrary", "arbitrary"),
        name="ssd_scan",
    )(xbc, dt, dtt, dt_bias[:, None, :], dt_bias[:, :, None], a_log[:, None, :], a_log[:, :, None], h0)


def _ssd_out_kernel(y_ref, xbc_ref, z_ref, dsk_ref, nw_ref, o_ref):
    y = y_ref[0] + y_ref[1] + dsk_ref[...] * xbc_ref[...]
    g = y * _silu(z_ref[...])
    ms = jnp.mean(g * g, axis=-1, keepdims=True)
    o_ref[...] = (g * lax.rsqrt(ms + EPS) * nw_ref[...]).astype(o_ref.dtype)


def _ssd_out(y2, xbc, zx, d_skip, norm_w):
    n_rows = xbc.shape[0]
    tl = _pick(n_rows, (512, 256, 128))
    w = SSD_D_INNER
    return pl.pallas_call(
        _ssd_out_kernel,
        grid=(n_rows // tl,),
        in_specs=[pl.BlockSpec((2, tl, w), lambda i: (0, i, 0)),
                  pl.BlockSpec((tl, w), lambda i: (i, 0)),
                  pl.BlockSpec((tl, w), lambda i: (i, 0)),
                  pl.BlockSpec((1, w), lambda i: (0, 0)),
                  pl.BlockSpec((1, w), lambda i: (0, 0))],
        out_specs=pl.BlockSpec((tl, w), lambda i: (i, 0)),
        out_shape=jax.ShapeDtypeStruct((n_rows, w), MXU_DTYPE),
        compiler_params=_params("parallel"),
        name="ssd_out",
    )(y2, xbc, zx, jnp.repeat(d_skip, SSD_HEAD_DIM)[None, :], norm_w[None, :])


def _head_mean_sq(x):
    blk = (lax.broadcasted_iota(jnp.int32, (LANES, LANES), 0) // NA_HEAD_DIM
           == lax.broadcasted_iota(jnp.int32, (LANES, LANES), 1) // NA_HEAD_DIM).astype(MXU_DTYPE)
    x2 = x * x
    hi = x2.astype(MXU_DTYPE)
    lo = (x2 - hi.astype(F32)).astype(MXU_DTYPE)
    return (_dot(hi, blk) + _dot(lo, blk)) * (1.0 / NA_HEAD_DIM)


def _even_proj_kernel(x_ref, nw_ref, sh_ref, sc_ref, w_ref, qw_ref, kw_ref, zx_ref, qkv_ref, dt_ref, a_ref,
                      *, zx_tiles, chunks):
    j = pl.program_id(1)

    @pl.when(j == 0)
    def _():
        a_ref[...] = _norm_mod(x_ref[...], nw_ref[...], sh_ref[...], sc_ref[...]).astype(a_ref.dtype)

    acc = _dot(a_ref[...], w_ref[...].astype(a_ref.dtype))

    @pl.when(j < zx_tiles)
    def _():
        zx_ref[...] = acc

    @pl.when(j >= zx_tiles)
    def _():
        n_head_chunks = NA_D // LANES
        for c in range(chunks):
            chunk = (j - zx_tiles) * chunks + c
            x = acc[:, c * LANES:(c + 1) * LANES]
            w = jnp.where(chunk < n_head_chunks, qw_ref[...] * NA_HEAD_DIM ** -0.5, kw_ref[...])
            normed = x * lax.rsqrt(_head_mean_sq(x) + EPS) * w
            qkv_ref[:, c * LANES:(c + 1) * LANES] = jnp.where(chunk < 2 * n_head_chunks, normed, x).astype(qkv_ref.dtype)

    @pl.when(j == pl.num_programs(1) - 1)
    def _():
        dt_ref[...] = acc[:, (chunks - 1) * LANES:]


def _even_proj(x, nw, sh, sc, w_packed, layer, q_norm_w, k_norm_w, name):
    m, d = x.shape
    tm = _pick(m, (1024, 512, 256, 128))
    tn = 5 * LANES
    zx_tiles = S_ZX // tn
    wspec = pl.BlockSpec((1, LANES), lambda i, j: (0, 0))
    return pl.pallas_call(
        functools.partial(_even_proj_kernel, zx_tiles=zx_tiles, chunks=tn // LANES),
        grid=(m // tm, (S_ZX + QKVD) // tn),
        in_specs=[pl.BlockSpec((tm, d), lambda i, j: (i, 0)),
                  _vec_spec(nw, d, _zero), _vec_spec(sh, d, _zero), _vec_spec(sc, d, _zero),
                  _mat_spec(layer, (d, tn), lambda i, j: (0, j)), wspec, wspec],
        out_specs=[pl.BlockSpec((tm, tn), lambda i, j: (i, jnp.minimum(j, zx_tiles - 1))),
                   pl.BlockSpec((tm, tn), lambda i, j: (i, jnp.maximum(j - zx_tiles, 0))),
                   pl.BlockSpec((tm, LANES), lambda i, j: (i, 0))],
        out_shape=[jax.ShapeDtypeStruct((m, S_ZX), F32), jax.ShapeDtypeStruct((m, QKVD), MXU_DTYPE),
                   jax.ShapeDtypeStruct((m, LANES), F32)],
        scratch_shapes=[pltpu.VMEM((tm, d), MXU_DTYPE)],
        compiler_params=_params("parallel", "arbitrary"),
        name=name,
    )(x, nw[0], sh[0], sc[0], w_packed, jnp.tile(q_norm_w, 2)[None, :], jnp.tile(k_norm_w, 2)[None, :])


def _attend(q2, parts):
    first = lax.broadcasted_iota(jnp.int32, q2.shape, 1) < NA_HEAD_DIM
    outs = []
    for head_mask in (first, jnp.logical_not(first)):
        qa = jnp.where(head_mask, q2, jnp.zeros_like(q2))
        scores = []
        for k, _, bias in parts:
            s = _dot_nt(qa, k)
            scores.append(s if bias is None else s + bias)
        m = functools.reduce(jnp.maximum, [jnp.max(s, axis=-1, keepdims=True) for s in scores])
        probs = [jnp.exp(s - m) for s in scores]
        denom = functools.reduce(jnp.add, [jnp.sum(p, axis=-1, keepdims=True) for p in probs])
        acc = functools.reduce(jnp.add, [_dot(p.astype(MXU_DTYPE), v) for p, (_, v, _) in zip(probs, parts)])
        outs.append(acc / denom)
    return jnp.where(first, outs[0], outs[1])


def _natten_kernel(q_ref, k_ref, v_ref, kc_ref, vc_ref, tbl_ref, o_ref, *, rb, rows):
    i = pl.program_id(1)
    kc, vc = kc_ref[...], vc_ref[...]
    n_win = NA_WIN_ROWS * GRID_W
    first = lax.broadcasted_iota(jnp.int32, (GRID_W, LANES), 1) < NA_HEAD_DIM
    for t in range(rb):
        r = i * rb + t
        start = jnp.clip(r - NA_WIN_ROWS // 2, 0, rows - NA_WIN_ROWS)
        dr0 = start - r + (NA_WIN_ROWS - 1)
        ks = pl.ds(pl.multiple_of(start * GRID_W, GRID_W), n_win)
        q2 = q_ref[t * GRID_W:(t + 1) * GRID_W, :]
        kw, vw = k_ref[ks, :], v_ref[ks, :]
        zero = jnp.zeros_like(q2)
        qs = jnp.concatenate([jnp.where(first, q2, zero), jnp.where(first, zero, q2)], axis=0)
        bias = jnp.concatenate([tbl_ref[dr0 + 2 * jj] for jj in range(NA_WIN_ROWS // 2)], axis=1)
        s_loc = _dot_nt(qs, kw) + bias
        s_ctx = _dot_nt(qs, kc)
        m = jnp.maximum(jnp.max(s_loc, axis=-1, keepdims=True), jnp.max(s_ctx, axis=-1, keepdims=True))
        p_loc, p_ctx = jnp.exp(s_loc - m), jnp.exp(s_ctx - m)
        denom = jnp.sum(p_loc, axis=-1, keepdims=True) + jnp.sum(p_ctx, axis=-1, keepdims=True)
        o = (_dot(p_loc.astype(MXU_DTYPE), vw) + _dot(p_ctx.astype(MXU_DTYPE), vc)) / denom
        o_ref[t * GRID_W:(t + 1) * GRID_W, :] = jnp.where(first, o[:GRID_W], o[GRID_W:]).astype(o_ref.dtype)


def _bias_table(rpb):
    col = jnp.arange(GRID_W)
    c0 = jnp.clip(col - NA_WIN_COLS // 2, 0, GRID_W - NA_WIN_COLS)
    col_in = (col[None, :] >= c0[:, None]) & (col[None, :] < c0[:, None] + NA_WIN_COLS)
    dc = jnp.clip(col[None, :] - col[:, None], 1 - NA_WIN_COLS, NA_WIN_COLS - 1) + (NA_WIN_COLS - 1)
    t = jnp.where(col_in, rpb.astype(F32)[:, :, dc], NEG)
    t = jnp.concatenate([t[:, :-1], t[:, 1:]], axis=-1)
    n_dr = 2 * NA_WIN_ROWS - 2
    t = t.reshape(NA_HEADS // 2, 2, n_dr, GRID_W, 2 * GRID_W).transpose(0, 2, 1, 3, 4)
    return t.reshape(NA_HEADS // 2, n_dr, 2 * GRID_W, 2 * GRID_W)


def _natten(qkv, qkv_ctx, rpb):
    n_rows = qkv.shape[0]
    rows = n_rows // GRID_W
    n_ctx = qkv_ctx.shape[0]
    rb = _pick(rows, (8, 4, 2, 1))
    n_win = NA_WIN_ROWS * GRID_W
    nb = NA_D // LANES
    seq = lambda n, off: pl.BlockSpec((n, LANES), lambda p, i: (0, p + off))
    tile = pl.BlockSpec((rb * GRID_W, LANES), lambda p, i: (i, p))
    return pl.pallas_call(
        functools.partial(_natten_kernel, rb=rb, rows=rows),
        grid=(NA_HEADS // 2, rows // rb),
        in_specs=[tile, seq(n_rows, nb), seq(n_rows, 2 * nb), seq(n_ctx, nb), seq(n_ctx, 2 * nb),
                  pl.BlockSpec((None, 2 * NA_WIN_ROWS - 2, 2 * GRID_W, 2 * GRID_W), lambda p, i: (p, 0, 0, 0))],
        out_specs=tile,
        out_shape=jax.ShapeDtypeStruct((n_rows, NA_D), MXU_DTYPE),
        compiler_params=_params("parallel", "arbitrary"),
        name="natten",
    )(qkv, qkv, qkv, qkv_ctx, qkv_ctx, _bias_table(rpb))


def _ctx_attn_kernel(q_ref, k_ref, v_ref, o_ref):
    o_ref[...] = _attend(q_ref[...], [(k_ref[...], v_ref[...], None)]).astype(o_ref.dtype)


def _ctx_attn(qkv_ctx):
    n_ctx = qkv_ctx.shape[0]
    nb = NA_D // LANES
    spec = lambda off: pl.BlockSpec((n_ctx, LANES), lambda p: (0, p + off))
    return pl.pallas_call(
        _ctx_attn_kernel,
        grid=(NA_HEADS // 2,),
        in_specs=[spec(0), spec(nb), spec(2 * nb)],
        out_specs=spec(0),
        out_shape=jax.ShapeDtypeStruct((n_ctx, NA_D), MXU_DTYPE),
        compiler_params=_params("parallel"),
        name="ctx_attn",
    )(qkv_ctx, qkv_ctx, qkv_ctx)


def _load_slabs(ref, n, s):
    return jnp.concatenate([ref[pl.ds(c, n, stride=s), :] for c in range(s)], axis=1)


def _store_slabs(ref, val):
    n = val.shape[0]
    s = val.shape[1] // LANES
    for c in range(s):
        ref[pl.ds(c, n, stride=s), :] = val[:, c * LANES:(c + 1) * LANES].astype(ref.dtype)

def _router_kernel(*refs, n_lat_tiles, with_ctx):
    if with_ctx:
        x_ref, cx_ref, nw_ref, sh_ref, sc_ref, shc_ref, scc_ref, wr_ref, m_ref, e_ref, g_ref = refs
        is_ctx = pl.program_id(0) >= n_lat_tiles
        x = jnp.where(is_ctx, cx_ref[...], x_ref[...])
        sh = jnp.where(is_ctx, shc_ref[...], sh_ref[...])
        sc = jnp.where(is_ctx, scc_ref[...], sc_ref[...])
    else:
        x_ref, nw_ref, sh_ref, sc_ref, wr_ref, m_ref, e_ref, g_ref = refs
        x, sh, sc = x_ref[...], sh_ref[...], sc_ref[...]
    m = _norm_mod(x, nw_ref[...], sh, sc)
    _store_slabs(m_ref, m)
    logits = _dot(m, wr_ref[...], HI)
    lane = lax.broadcasted_iota(jnp.int32, logits.shape, 1)
    big = jnp.int32(LANES)

    def top(vals):
        v = jnp.max(vals, axis=-1, keepdims=True)
        idx = jnp.min(jnp.where(vals == v, lane, big), axis=-1, keepdims=True)
        return v, idx

    gl = jnp.where(lane < N_GROUPS, logits, -jnp.inf)
    g_max, grp = top(gl)
    p_grp = 1.0 / jnp.sum(jnp.exp(gl - g_max), axis=-1, keepdims=True)
    e_lane = lane - N_GROUPS
    in_grp = (e_lane >= grp * EXPERTS_PER_GROUP) & (e_lane < (grp + 1) * EXPERTS_PER_GROUP)
    el = jnp.where(in_grp, logits, -jnp.inf)
    v1, i1 = top(el)
    v2, i2 = top(jnp.where(lane == i1, -jnp.inf, el))
    t = jnp.exp(v2 - v1)
    g1 = p_grp / (1.0 + t)
    g2 = p_grp * t / (1.0 + t)
    e_ref[...] = jnp.where(lane == 0, i1 - N_GROUPS, jnp.where(lane == 1, i2 - N_GROUPS, 0))
    g_ref[...] = jnp.where(lane == 0, g1, jnp.where(lane == 1, g2, 0.0))


def _router(lat, cx, nw, mod_lat, mod_ctx, w_route):
    n_lat, d = lat.shape
    with_ctx = cx is not None
    n_ctx = cx.shape[0] if with_ctx else 0
    tm = _pick(n_lat, (256, 128)) if with_ctx else _pick(n_lat, (512, 256, 128))
    assert n_ctx % tm == 0
    n_lat_tiles = n_lat // tm
    n_tok = n_lat + n_ctx
    lat_tile = pl.BlockSpec((tm, d), lambda i: (jnp.minimum(i, n_lat_tiles - 1), 0))
    vec = lambda v: _vec_spec(v, d, _zero)
    wide = pl.BlockSpec((tm, LANES), lambda i: (i, 0))
    w_spec = pl.BlockSpec((d, LANES), lambda i: (0, 0))
    if with_ctx:
        ctx_tile = pl.BlockSpec((tm, d), lambda i: (jnp.maximum(i - n_lat_tiles, 0), 0))
        in_specs = [lat_tile, ctx_tile, vec(nw), vec(mod_lat[3]), vec(mod_lat[4]), vec(mod_ctx[3]), vec(mod_ctx[4]),
                    w_spec]
        args = (lat, cx, nw[0], mod_lat[3][0], mod_lat[4][0], mod_ctx[3][0], mod_ctx[4][0], w_route)
    else:
        in_specs = [lat_tile, vec(nw), vec(mod_lat[3]), vec(mod_lat[4]), w_spec]
        args = (lat, nw[0], mod_lat[3][0], mod_lat[4][0], w_route)
    m, e, g = pl.pallas_call(
        functools.partial(_router_kernel, n_lat_tiles=n_lat_tiles, with_ctx=with_ctx),
        grid=(n_tok // tm,),
        in_specs=in_specs,
        out_specs=[pl.BlockSpec((tm * (d // LANES), LANES), lambda i: (i, 0)), wide, wide],
        out_shape=[jax.ShapeDtypeStruct((n_tok * (d // LANES), LANES), F32),
                   jax.ShapeDtypeStruct((n_tok, LANES), jnp.int32),
                   jax.ShapeDtypeStruct((n_tok, LANES), F32)],
        compiler_params=_params("parallel"),
        name="router",
    )(*args)
    return m, e[:, :TOP_K], g


def _moe_kernel(first_ref, nblk_ref, tok_ref, nact_ref, m_hbm, wg_hbm, wu_hbm, wd_hbm, yb_hbm,
                xbuf, ybuf, gsem, osem, wgf_ref, wuf_ref, wdf_ref, wsem, wgb_ref, wub_ref, wdb_ref, *, layer):
    e, n_e = pl.program_id(0), pl.num_programs(0)
    n_act = nact_ref[0]
    s = wgf_ref.shape[1] // LANES
    blk_rows = MOE_BLOCK * s
    n_blocks = yb_hbm.shape[0] // blk_rows

    def weight_copies(expert, ws):
        return [pltpu.make_async_copy(src.at[layer, expert], dst.at[ws], wsem.at[k, ws])
                for k, (src, dst) in enumerate(((wg_hbm, wgf_ref), (wu_hbm, wuf_ref), (wd_hbm, wdf_ref)))]

    @pl.when(e == 0)
    def _():
        for ahead in range(MOE_WEIGHT_AHEAD):
            for copy in weight_copies(ahead, ahead):
                copy.start()

    ws = e % MOE_WEIGHT_SLOTS
    for copy in weight_copies(e, ws):
        copy.wait()

    @pl.when(e + MOE_WEIGHT_AHEAD < n_e)
    def _():
        for copy in weight_copies(e + MOE_WEIGHT_AHEAD, (e + MOE_WEIGHT_AHEAD) % MOE_WEIGHT_SLOTS):
            copy.start()

    def gather_start(blk, slot):
        for r in range(MOE_BLOCK):
            tok = tok_ref[blk * MOE_BLOCK + r]
            pltpu.make_async_copy(m_hbm.at[pl.ds(pl.multiple_of(tok * s, s), s), :],
                                  xbuf.at[slot, pl.ds(r * s, s), :],
                                  gsem.at[slot]).start(priority=r % ROW_DMA_THREADS)

    def gather_wait(slot):
        pltpu.make_async_copy(m_hbm.at[pl.ds(0, blk_rows), :], xbuf.at[slot], gsem.at[slot]).wait()

    def out_copy(blk, slot):
        rows = pl.ds(pl.multiple_of(blk * blk_rows, blk_rows), blk_rows)
        return pltpu.make_async_copy(ybuf.at[slot], yb_hbm.at[rows, :], osem.at[slot])

    @pl.when(e == 0)
    def _():
        for ahead in range(MOE_AHEAD):
            gather_start(jnp.minimum(ahead, n_act - 1), ahead)

    @pl.when(nblk_ref[e] > 0)
    def _():
        wgb_ref[...] = wgf_ref[ws].astype(wgb_ref.dtype)
        wub_ref[...] = wuf_ref[ws].astype(wub_ref.dtype)
        wdb_ref[...] = wdf_ref[ws].astype(wdb_ref.dtype)

    def block(j, carry):
        b = first_ref[e] + j
        slot = b % MOE_SLOTS
        gather_wait(slot)

        @pl.when(b >= MOE_SLOTS)
        def _():
            out_copy(b - MOE_SLOTS, slot).wait()

        gather_start(jnp.minimum(b + MOE_AHEAD, n_act - 1), (b + MOE_AHEAD) % MOE_SLOTS)
        x = _load_slabs(xbuf.at[slot], MOE_BLOCK, s).astype(MXU_DTYPE)
        h = _silu(_dot(x, wgb_ref[...])) * _dot(x, wub_ref[...])
        _store_slabs(ybuf.at[slot], _dot(h.astype(MXU_DTYPE), wdb_ref[...]))
        out_copy(b, slot).start()
        return carry

    lax.fori_loop(0, nblk_ref[e], block, 0)

    @pl.when(e == n_e - 1)
    def _():
        for ahead in range(MOE_AHEAD):
            gather_wait((n_act + ahead) % MOE_SLOTS)
        for back in range(1, MOE_SLOTS + 1):
            @pl.when(n_act >= back)
            def _():
                out_copy(n_act - back, (n_act - back) % MOE_SLOTS).wait()

        ybuf[0] = jnp.zeros(ybuf.shape[1:], ybuf.dtype)

        def fill(b, carry):
            out_copy(b, 0).start()
            out_copy(b, 0).wait()
            return carry

        lax.fori_loop(n_act, n_blocks, fill, 0)


def _moe_ffn(m_tok, slot_tok, first_blk, n_blk, n_act, w_gate, w_up, w_down, layer):
    n_rows = slot_tok.shape[0]
    d, f = w_gate.shape[-2:]
    blk_rows = MOE_BLOCK * (d // LANES)
    hbm = pl.BlockSpec(memory_space=pl.ANY)
    grid_spec = pltpu.PrefetchScalarGridSpec(
        num_scalar_prefetch=4,
        grid=(N_EXPERTS,),
        in_specs=[hbm, hbm, hbm, hbm],
        out_specs=hbm,
        scratch_shapes=[pltpu.VMEM((MOE_SLOTS, blk_rows, LANES), F32), pltpu.VMEM((MOE_SLOTS, blk_rows, LANES), F32),
                        pltpu.SemaphoreType.DMA((MOE_SLOTS,)), pltpu.SemaphoreType.DMA((MOE_SLOTS,)),
                        pltpu.VMEM((MOE_WEIGHT_SLOTS, d, f), F32), pltpu.VMEM((MOE_WEIGHT_SLOTS, d, f), F32),
                        pltpu.VMEM((MOE_WEIGHT_SLOTS, f, d), F32), pltpu.SemaphoreType.DMA((3, MOE_WEIGHT_SLOTS)),
                        pltpu.VMEM((d, f), MXU_DTYPE), pltpu.VMEM((d, f), MXU_DTYPE), pltpu.VMEM((f, d), MXU_DTYPE)],
    )
    return pl.pallas_call(
        functools.partial(_moe_kernel, layer=layer),
        grid_spec=grid_spec,
        out_shape=jax.ShapeDtypeStruct((n_rows * (d // LANES), LANES), F32),
        compiler_params=_params("arbitrary"),
        name="moe_ffn",
    )(first_blk, n_blk, slot_tok, n_act, m_tok, w_gate, w_up, w_down)


def _combine_kernel(pos_ref, x_ref, g_ref, gt_ref, yb_hbm, o_ref, buf, sem, *, tm):
    i, n_i = pl.program_id(0), pl.num_programs(0)
    s = x_ref.shape[1] // LANES

    def gather_start(tile, slot):
        for j in range(tm):
            for k in range(TOP_K):
                p = pos_ref[(tile * tm + j) * TOP_K + k]
                pltpu.make_async_copy(yb_hbm.at[pl.ds(pl.multiple_of(p * s, s), s), :],
                                      buf.at[slot, k, pl.ds(j * s, s), :],
                                      sem.at[slot]).start(priority=(j * TOP_K + k) % ROW_DMA_THREADS)

    def gather_wait(slot):
        for k in range(TOP_K):
            pltpu.make_async_copy(yb_hbm.at[pl.ds(0, tm * s), :], buf.at[slot, k], sem.at[slot]).wait()

    @pl.when(i == 0)
    def _():
        for ahead in range(MOE_AHEAD):
            gather_start(jnp.minimum(ahead, n_i - 1), ahead)

    slot = i % MOE_SLOTS
    gather_wait(slot)
    gather_start(jnp.minimum(i + MOE_AHEAD, n_i - 1), (i + MOE_AHEAD) % MOE_SLOTS)
    gt = gt_ref[...]
    y0 = _load_slabs(buf.at[slot, 0], tm, s)
    y1 = _load_slabs(buf.at[slot, 1], tm, s)
    o_ref[...] = x_ref[...] + g_ref[...] * (gt[:, 0:1] * y0 + gt[:, 1:2] * y1)

    @pl.when(i == n_i - 1)
    def _():
        for ahead in range(1, MOE_AHEAD + 1):
            gather_wait((i + ahead) % MOE_SLOTS)


def _combine(x, g, gates, yb, pos):
    m, d = x.shape
    tm = MOE_BLOCK
    grid_spec = pltpu.PrefetchScalarGridSpec(
        num_scalar_prefetch=1,
        grid=(m // tm,),
        in_specs=[pl.BlockSpec((tm, d), lambda i, pos: (i, 0)),
                  _vec_spec(g, d, _zero),
                  pl.BlockSpec((tm, LANES), lambda i, pos: (i, 0)),
                  pl.BlockSpec(memory_space=pl.ANY)],
        out_specs=pl.BlockSpec((tm, d), lambda i, pos: (i, 0)),
        scratch_shapes=[pltpu.VMEM((MOE_SLOTS, TOP_K, tm * (d // LANES), LANES), F32),
                        pltpu.SemaphoreType.DMA((MOE_SLOTS,))],
    )
    return pl.pallas_call(
        functools.partial(_combine_kernel, tm=tm),
        grid_spec=grid_spec,
        out_shape=jax.ShapeDtypeStruct((m, d), F32),
        compiler_params=_params("arbitrary"),
        name="moe_combine",
    )(pos, x, g[0], gates, yb)


def _dispatch_kernel(e_ref, dest_ref, first_ref, nblk_ref, cnt_ref, base_ref):
    phase, i = pl.program_id(0), pl.program_id(1)
    tile = e_ref.shape[1]
    onehot = lax.broadcasted_iota(jnp.int32, (N_EXPERTS, tile), 0) == e_ref[...]

    @pl.when(jnp.logical_and(phase == 0, i == 0))
    def _():
        cnt_ref[...] = jnp.zeros_like(cnt_ref)

    @pl.when(phase == 0)
    def _():
        cnt_ref[...] += jnp.sum(onehot.astype(F32), axis=1, keepdims=True)

    @pl.when(jnp.logical_and(phase == 1, i == 0))
    def _():
        n_blk = jnp.right_shift(cnt_ref[...].astype(jnp.int32) + (MOE_BLOCK - 1),
                                MOE_BLOCK.bit_length() - 1)
        tri = (lax.broadcasted_iota(jnp.int32, (N_EXPERTS, N_EXPERTS), 0)
               >= lax.broadcasted_iota(jnp.int32, (N_EXPERTS, N_EXPERTS), 1)).astype(F32)
        blk_end = _dot(tri, n_blk.astype(F32), HI)
        first = blk_end.astype(jnp.int32) - n_blk
        nblk_ref[...] = n_blk
        first_ref[...] = first
        base_ref[...] = (first * MOE_BLOCK).astype(F32)

    @pl.when(phase == 1)
    def _():
        tri = (lax.broadcasted_iota(jnp.int32, (tile, tile), 0)
               <= lax.broadcasted_iota(jnp.int32, (tile, tile), 1)).astype(MXU_DTYPE)
        cum = _dot(onehot.astype(MXU_DTYPE), tri)
        slot = jnp.where(onehot, base_ref[:, 0:1] + cum - 1.0, 0.0)
        dest_ref[...] = jnp.sum(slot, axis=0, keepdims=True).astype(jnp.int32)
        base_ref[...] += cum[:, tile - 1:tile]


def _dispatch(expert):
    n_tok = expert.shape[0]
    n = n_tok * TOP_K
    n_blocks = -(-n // MOE_BLOCK) + N_EXPERTS
    n_rows = n_blocks * MOE_BLOCK
    tile = _pick(n, (512, 256, 128))
    per_expert = pl.BlockSpec((N_EXPERTS, LANES), lambda ph, i: (0, 0))
    dest, first, n_blk = pl.pallas_call(
        _dispatch_kernel,
        grid=(2, n // tile),
        in_specs=[pl.BlockSpec((1, tile), lambda ph, i: (0, i))],
        out_specs=[pl.BlockSpec((1, tile), lambda ph, i: (0, i * ph)), per_expert, per_expert],
        out_shape=[jax.ShapeDtypeStruct((1, n), jnp.int32),
                   jax.ShapeDtypeStruct((N_EXPERTS, LANES), jnp.int32),
                   jax.ShapeDtypeStruct((N_EXPERTS, LANES), jnp.int32)],
        scratch_shapes=[pltpu.VMEM((N_EXPERTS, LANES), F32), pltpu.VMEM((N_EXPERTS, LANES), F32)],
        compiler_params=_params("arbitrary", "arbitrary"),
        name="moe_dispatch",
    )(expert.reshape(1, n))
    dest, first_blk, n_blk = dest[0], first[:, 0], n_blk[:, 0]
    pair_tok = jnp.arange(n, dtype=jnp.int32) // TOP_K
    slot_tok = jnp.zeros((n_rows,), jnp.int32).at[dest].set(pair_tok)
    n_act = first_blk[-1:] + n_blk[-1:]
    return dest, slot_tok, first_blk, n_blk, n_act


def _even_mixer(lat, cx, nw, mod_lat, mod_ctx, e, w_in, conv_w, conv_b, dt_bias, a_log, d_skip, ssd_norm_w,
                q_norm_w, k_norm_w, rpb, w_out, ctx_out):
    def project(x, mod, tag):
        zx, qkv, dt = _even_proj(x, nw, mod[0], mod[1], w_in, e, q_norm_w[e], k_norm_w[e], "proj_even_" + tag)
        xbc = _xbc_conv(zx, conv_w[e], conv_b[e])
        return zx, xbc, dt[:, :2 * SSD_HEADS], qkv

    zx_c, xbc_c, dt_c, qkv_c = project(cx, mod_ctx, "ctx")
    zx_l, xbc_l, dt_l, qkv_l = project(lat, mod_lat, "lat")
    h0 = jnp.zeros((2, SSD_PAIRS, SSD_STATE, LANES), F32)
    y_c, h_ctx = _ssd(xbc_c, dt_c, dt_bias[e], a_log[e], h0)
    y_l, _ = _ssd(xbc_l, dt_l, dt_bias[e], a_log[e], h_ctx)
    yssd_l = _ssd_out(y_l, xbc_l, zx_l, d_skip[e], ssd_norm_w[e])
    yatt_l = _natten(qkv_l, qkv_c, rpb[e])
    lat = _matmul_res([yssd_l, yatt_l], w_out, e, lat, mod_lat[2], "out_even_lat")
    if ctx_out:
        yssd_c = _ssd_out(y_c, xbc_c, zx_c, d_skip[e], ssd_norm_w[e])
        yatt_c = _ctx_attn(qkv_c)
        cx = _matmul_res([yssd_c, yatt_c], w_out, e, cx, mod_ctx[2], "out_even_ctx")
    return lat, cx


def _odd_mixer(lat, cx, nw, mod_lat, mod_ctx, o, w_in, conv_w, w_out, ctx_out):
    def mix(x, mod, tag):
        u = _gated_conv_proj(x, nw, mod[0], mod[1], w_in, o, conv_w[o], "proj_odd_" + tag)
        return _matmul_res([u], w_out, o, x, mod[2], "out_odd_" + tag)

    lat = mix(lat, mod_lat, "lat")
    if ctx_out:
        cx = mix(cx, mod_ctx, "ctx")
    return lat, cx


def kernel(x, c, ctx, c_ctx, w_mod, b_mod, norm_mix_w, norm_ffn_w, ev_w_in, ev_conv_w, ev_conv_b, ev_dt_bias,
           ev_a_log, ev_d_skip, ev_ssd_norm_w, ev_q_norm_w, ev_k_norm_w, ev_rpb, ev_w_out, od_w_in, od_conv_w,
           od_w_out, moe_w_group, moe_w_router, moe_w_gate, moe_w_up, moe_w_down):
    bsz, n_lat, d = x.shape
    assert bsz == 1 and ctx.shape[0] == 1
    assert n_lat % GRID_W == 0 and n_lat // GRID_W >= NA_WIN_ROWS and n_lat % SSD_CHUNK == 0
    assert ctx.shape[1] % SSD_CHUNK == 0 and d % LANES == 0
    depth = w_mod.shape[0]
    lat, cx = x[0], ctx[0]
    mods = _modulation(jnp.stack([c[0], c_ctx], axis=1), w_mod, b_mod)
    mods = mods.reshape(depth * 2 * 6, d)
    ev_w_packed = _repack_w_in(ev_w_in)

    for l in range(depth):
        even = l % 2 == 0
        ctx_out = any(j % 2 == 0 for j in range(l + 1, depth))
        mod_lat = [_vec(mods, (l * 2 + 0) * 6 + i) for i in range(6)]
        mod_ctx = [_vec(mods, (l * 2 + 1) * 6 + i) for i in range(6)]
        nw = _vec(norm_mix_w, l)
        if even:
            lat, cx = _even_mixer(lat, cx, nw, mod_lat, mod_ctx, l // 2, ev_w_packed, ev_conv_w, ev_conv_b,
                                  ev_dt_bias, ev_a_log, ev_d_skip, ev_ssd_norm_w, ev_q_norm_w,
                                  ev_k_norm_w, ev_rpb, ev_w_out, ctx_out)
        else:
            lat, cx = _odd_mixer(lat, cx, nw, mod_lat, mod_ctx, l // 2, od_w_in, od_conv_w, od_w_out, ctx_out)

        nfw = _vec(norm_ffn_w, l)
        w_route = jnp.concatenate([moe_w_group[l], moe_w_router[l],
                                   jnp.zeros((d, LANES - N_GROUPS - N_EXPERTS), F32)], axis=1)
        m_tok, expert, gates = _router(lat, cx if ctx_out else None, nfw, mod_lat, mod_ctx, w_route)
        dest, slot_tok, first_blk, n_blk, n_act = _dispatch(expert)
        yb = _moe_ffn(m_tok, slot_tok, first_blk, n_blk, n_act, moe_w_gate, moe_w_up, moe_w_down, l)
        lat = _combine(lat, mod_lat[5], gates[:n_lat], yb, dest[:n_lat * TOP_K])
        if ctx_out:
            cx = _combine(cx, mod_ctx[5], gates[n_lat:], yb, dest[n_lat * TOP_K:])
    return lat[None]
```

```python
import functools

import jax
import jax.numpy as jnp
from jax import lax
from jax.experimental import pallas as pl
from jax.experimental.pallas import tpu as pltpu

F32 = jnp.float32
MXU_DTYPE = jnp.bfloat16
HI = lax.Precision.HIGHEST
EPS = 1e-6
NEG = -1e30

LANES = 128
SUBLANES = 8
VMEM_LIMIT = 48 * 1024 * 1024
ROW_DMA_THREADS = 2

SSD_HEADS = 16
SSD_HEAD_DIM = 64
SSD_D_INNER = SSD_HEADS * SSD_HEAD_DIM
SSD_STATE = 128
SSD_GROUPS = 2
SSD_CHUNK = 128
SSD_BC = SSD_GROUPS * SSD_STATE
SSD_XBC = SSD_D_INNER + 2 * SSD_BC
SSD_PAIRS = SSD_HEADS // 2
NA_HEADS = 16
NA_HEAD_DIM = 64
NA_D = NA_HEADS * NA_HEAD_DIM
NA_WIN_ROWS = 8
NA_WIN_COLS = 16
GRID_W = 64
N_GROUPS = 8
EXPERTS_PER_GROUP = 8
N_EXPERTS = N_GROUPS * EXPERTS_PER_GROUP
TOP_K = 2
D_FF_EXPERT = 384
MOE_BLOCK = 128
MOE_AHEAD = 2
MOE_SLOTS = MOE_AHEAD + 1
MOE_WEIGHT_AHEAD = 2
MOE_WEIGHT_SLOTS = MOE_WEIGHT_AHEAD + 1
S_ZX = SSD_D_INNER + SSD_XBC
S_DT = S_ZX + 2 * SSD_HEADS
QKVD = 3 * NA_D + LANES


def _pick(n, prefs):
    for p in prefs:
        if n % p == 0:
            return p
    return n


def _params(*sem):
    return pltpu.CompilerParams(dimension_semantics=sem, vmem_limit_bytes=VMEM_LIMIT)


def _vec(arr, idx):
    return arr.reshape(arr.shape[0], 1, arr.shape[-1]), idx


def _vec_spec(vec, width, col):
    idx = vec[1]
    return pl.BlockSpec((None, 1, width), lambda *g: (idx, 0, col(*g)))


def _mat_spec(layer, block, idx):
    return pl.BlockSpec((None,) + block, lambda *g: (layer,) + idx(*g))


def _zero(*g):
    return 0


def _silu(x):
    return x * (1.0 / (1.0 + jnp.exp(-x)))


def _softplus(x):
    return jnp.maximum(x, 0.0) + jnp.log1p(jnp.exp(-jnp.abs(x)))


def _dot(a, b, precision=None):
    return jnp.dot(a, b, preferred_element_type=F32, precision=precision)


def _dot_select(a, b, split_lhs):
    x = a if split_lhs else b
    hi = x.astype(MXU_DTYPE)
    rest = x - hi.astype(F32)
    mid = rest.astype(MXU_DTYPE)
    lo = (rest - mid.astype(F32)).astype(MXU_DTYPE)
    if split_lhs:
        return _dot(hi, b) + _dot(mid, b) + _dot(lo, b)
    return _dot(a, hi) + _dot(a, mid) + _dot(a, lo)


def _dot_split2(a, b):
    a_hi, b_hi = a.astype(MXU_DTYPE), b.astype(MXU_DTYPE)
    a_lo = (a - a_hi.astype(F32)).astype(MXU_DTYPE)
    b_lo = (b - b_hi.astype(F32)).astype(MXU_DTYPE)
    return _dot(a_hi, b_hi) + _dot(a_hi, b_lo) + _dot(a_lo, b_hi)


def _dot_nt(a, b):
    return lax.dot_general(a, b, (((1,), (1,)), ((), ())), preferred_element_type=F32)


def _norm_mod(x, nw, sh, sc):
    ms = jnp.mean(x * x, axis=-1, keepdims=True)
    n = x * lax.rsqrt(ms + EPS) * nw
    return n * (1.0 + sc) + sh


def _mod_kernel(ct_ref, w_ref, b_ref, o_ref):
    ct = ct_ref[...]
    s = _silu(ct)
    w = w_ref[...]
    r0 = jnp.sum(s[:, 0:1] * w, axis=0, keepdims=True)
    r1 = jnp.sum(s[:, 1:2] * w, axis=0, keepdims=True)
    o_ref[...] = jnp.concatenate([r0, r1], axis=0) + b_ref[...]


def _modulation(ct, w_mod, b_mod):
    depth, d, n = w_mod.shape
    tn = _pick(n, (512, 256, 128))
    return pl.pallas_call(
        _mod_kernel,
        grid=(depth, n // tn),
        in_specs=[pl.BlockSpec((d, 2), lambda l, j: (0, 0)),
                  pl.BlockSpec((None, d, tn), lambda l, j: (l, 0, j)),
                  pl.BlockSpec((None, 1, tn), lambda l, j: (l, 0, j))],
        out_specs=pl.BlockSpec((None, 2, tn), lambda l, j: (l, 0, j)),
        out_shape=jax.ShapeDtypeStruct((depth, 2, n), F32),
        compiler_params=_params("parallel", "parallel"),
        name="modulation",
    )(ct, w_mod, b_mod.reshape(depth, 1, n))


def _repack_w_in_kernel(w_ref, o_ref):
    o_ref[0:S_ZX, :] = w_ref[0:S_ZX, :].astype(o_ref.dtype)
    o_ref[S_ZX:S_ZX + 3 * NA_D, :] = w_ref[S_DT:S_DT + 3 * NA_D, :].astype(o_ref.dtype)
    o_ref[S_ZX + 3 * NA_D:S_ZX + 3 * NA_D + 2 * SSD_HEADS, :] = w_ref[S_ZX:S_DT, :].astype(o_ref.dtype)
    o_ref[S_ZX + 3 * NA_D + 2 * SSD_HEADS:, :] = jnp.zeros((LANES - 2 * SSD_HEADS, o_ref.shape[1]), o_ref.dtype)


def _repack_w_in(w_in):
    layers, d, n_in = w_in.shape
    tc = _pick(d, (512, 256, 128))
    n_out = S_ZX + QKVD
    return pl.pallas_call(
        _repack_w_in_kernel,
        grid=(layers, d // tc),
        in_specs=[pl.BlockSpec((None, n_in, tc), lambda l, i: (l, 0, i))],
        out_specs=pl.BlockSpec((None, n_out, tc), lambda l, i: (l, 0, i)),
        out_shape=jax.ShapeDtypeStruct((layers, n_out, d), MXU_DTYPE),
        compiler_params=_params("parallel", "parallel"),
        name="repack_w_in",
    )(jnp.swapaxes(w_in, 1, 2))


def _matmul_res_kernel(*refs, n_a):
    a_refs, w_refs = refs[:n_a], refs[n_a:2 * n_a]
    res_ref, g_ref, o_ref = refs[2 * n_a:]
    acc = None
    for a_ref, w_ref in zip(a_refs, w_refs):
        t = _dot(a_ref[...].astype(MXU_DTYPE), w_ref[...].astype(MXU_DTYPE))
        acc = t if acc is None else acc + t
    o_ref[...] = res_ref[...] + g_ref[...] * acc


def _matmul_res(a_list, w, layer, res, g, name):
    n_a = len(a_list)
    m, n = res.shape
    kp = w.shape[1] // n_a
    tm = _pick(m, (2048, 1024, 512, 256, 128))
    tn = _pick(n, (512, 256, 128))
    in_specs = [pl.BlockSpec((tm, kp), lambda i, j: (i, 0)) for _ in range(n_a)]
    in_specs += [_mat_spec(layer, (kp, tn), functools.partial(lambda i, j, p: (p, j), p=p)) for p in range(n_a)]
    in_specs += [pl.BlockSpec((tm, tn), lambda i, j: (i, j)), _vec_spec(g, tn, lambda i, j: j)]
    return pl.pallas_call(
        functools.partial(_matmul_res_kernel, n_a=n_a),
        grid=(m // tm, n // tn),
        in_specs=in_specs,
        out_specs=pl.BlockSpec((tm, tn), lambda i, j: (i, j)),
        out_shape=jax.ShapeDtypeStruct((m, n), F32),
        compiler_params=_params("parallel", "parallel"),
        name=name,
    )(*a_list, *([w] * n_a), res, g[0])


def _shifted(x, prev_ref, next_ref):
    i, n_i = pl.program_id(0), pl.num_programs(0)
    tl = x.shape[0]
    row = lax.broadcasted_iota(jnp.int32, x.shape, 0)
    prev_row = jnp.where(i == 0, 0.0, prev_ref[SUBLANES - 1:SUBLANES, :])
    next_row = jnp.where(i == n_i - 1, 0.0, next_ref[0:1, :])
    up = jnp.where(row == 0, prev_row, pltpu.roll(x, 1, 0))
    down = jnp.where(row == tl - 1, next_row, pltpu.roll(x, tl - 1, 0))
    return up, down


def _halo_specs(tl, tc, n_rows, col_off):
    nb = tl // SUBLANES
    last = n_rows // SUBLANES - 1
    return [pl.BlockSpec((tl, tc), lambda i, j: (i, j + col_off)),
            pl.BlockSpec((SUBLANES, tc), lambda i, j: (jnp.maximum(i * nb - 1, 0), j + col_off)),
            pl.BlockSpec((SUBLANES, tc), lambda i, j: (jnp.minimum((i + 1) * nb, last), j + col_off))]


def _xbc_conv_kernel(x_ref, prev_ref, next_ref, w_ref, b_ref, o_ref):
    x = x_ref[...]
    up, down = _shifted(x, prev_ref, next_ref)
    w = w_ref[...]
    o_ref[...] = _silu(w[0:1] * up + w[1:2] * x + w[2:3] * down + b_ref[...])


def _xbc_conv(zx, conv_w, conv_b):
    n_rows = zx.shape[0]
    tl = _pick(n_rows, (512, 256, 128))
    tc = 512
    off = SSD_D_INNER // tc
    return pl.pallas_call(
        _xbc_conv_kernel,
        grid=(n_rows // tl, SSD_XBC // tc),
        in_specs=_halo_specs(tl, tc, n_rows, off) + [pl.BlockSpec((3, tc), lambda i, j: (0, j)),
                                                     pl.BlockSpec((1, tc), lambda i, j: (0, j))],
        out_specs=pl.BlockSpec((tl, tc), lambda i, j: (i, j)),
        out_shape=jax.ShapeDtypeStruct((n_rows, SSD_XBC), F32),
        compiler_params=_params("parallel", "parallel"),
        name="xbc_conv",
    )(zx, zx, zx, conv_w, conv_b.reshape(1, -1))


def _gated_conv_proj_kernel(x_ref, xp_ref, xn_ref, nw_ref, sh_ref, sc_ref, wb_ref, wc_ref, wx_ref, cw_ref, o_ref,
                            a_ref, ah_ref):
    i, n_i = pl.program_id(0), pl.num_programs(0)

    @pl.when(pl.program_id(1) == 0)
    def _():
        nw, sh, sc = nw_ref[...], sh_ref[...], sc_ref[...]
        a_ref[...] = _norm_mod(x_ref[...], nw, sh, sc).astype(a_ref.dtype)
        ah_ref[0:SUBLANES, :] = _norm_mod(xp_ref[...], nw, sh, sc)
        ah_ref[SUBLANES:, :] = _norm_mod(xn_ref[...], nw, sh, sc)

    a, ah = a_ref[...], ah_ref[...].astype(a_ref.dtype)
    wc, wx = wc_ref[...].astype(a.dtype), wx_ref[...].astype(a.dtype)
    u = _dot(a, wc) * _dot(a, wx)
    uh = _dot(ah, wc) * _dot(ah, wx)
    tm = u.shape[0]
    row = lax.broadcasted_iota(jnp.int32, u.shape, 0)
    prev_row = jnp.where(i == 0, 0.0, uh[SUBLANES - 1:SUBLANES, :])
    next_row = jnp.where(i == n_i - 1, 0.0, uh[SUBLANES:SUBLANES + 1, :])
    up = jnp.where(row == 0, prev_row, pltpu.roll(u, 1, 0))
    down = jnp.where(row == tm - 1, next_row, pltpu.roll(u, tm - 1, 0))
    cw = cw_ref[...]
    conv = cw[0:1] * up + cw[1:2] * u + cw[2:3] * down
    o_ref[...] = (_dot(a, wb_ref[...].astype(a.dtype)) * conv).astype(o_ref.dtype)


def _gated_conv_proj(x, nw, sh, sc, w, layer, conv_w, name):
    m, d = x.shape
    c = w.shape[2] // 3
    tm = _pick(m, (1024, 512, 256, 128))
    tc = _pick(c, (256, 128))
    nb = c // tc
    blocks = tm // SUBLANES
    last = m // SUBLANES - 1
    wspec = lambda off: _mat_spec(layer, (d, tc), lambda i, j: (0, j + off))
    return pl.pallas_call(
        _gated_conv_proj_kernel,
        grid=(m // tm, nb),
        in_specs=[pl.BlockSpec((tm, d), lambda i, j: (i, 0)),
                  pl.BlockSpec((SUBLANES, d), lambda i, j: (jnp.maximum(i * blocks - 1, 0), 0)),
                  pl.BlockSpec((SUBLANES, d), lambda i, j: (jnp.minimum((i + 1) * blocks, last), 0)),
                  _vec_spec(nw, d, _zero), _vec_spec(sh, d, _zero), _vec_spec(sc, d, _zero),
                  wspec(0), wspec(nb), wspec(2 * nb),
                  pl.BlockSpec((3, tc), lambda i, j: (0, j))],
        out_specs=pl.BlockSpec((tm, tc), lambda i, j: (i, j)),
        out_shape=jax.ShapeDtypeStruct((m, c), MXU_DTYPE),
        scratch_shapes=[pltpu.VMEM((tm, d), MXU_DTYPE), pltpu.VMEM((2 * SUBLANES, d), F32)],
        compiler_params=_params("parallel", "arbitrary"),
        name=name,
    )(x, x, x, nw[0], sh[0], sc[0], w, w, w, conv_w)


def _ssd_kernel(xbc_ref, dt_ref, dtt_ref, dtb_ref, dtbt_ref, alog_ref, alogt_ref, h0_ref,
                y_ref, hout_ref, h_ref):
    d, c, n_c = pl.program_id(0), pl.program_id(1), pl.num_programs(1)
    q = SSD_CHUNK

    @pl.when(c == 0)
    def _():
        h_ref[...] = h0_ref[...]

    dt = _softplus(dt_ref[...] + dtb_ref[...])
    dtt = _softplus(dtt_ref[...] + dtbt_ref[...])
    a = dt * -jnp.exp(alog_ref[...])
    at = dtt * -jnp.exp(alogt_ref[...])
    row = lax.broadcasted_iota(jnp.int32, (q, q), 0)
    col = lax.broadcasted_iota(jnp.int32, (q, q), 1)
    sign = jnp.where(d == 0, 1, -1)
    mask = (row - col) * sign >= 0
    tri = mask.astype(MXU_DTYPE)
    tri_t = ((col - row) * sign >= 0).astype(MXU_DTYPE)
    cs = _dot_select(tri, a, split_lhs=False)
    cst = _dot_select(at, tri_t, split_lhs=True)
    tot = jnp.broadcast_to(jnp.sum(a, axis=0, keepdims=True), (SUBLANES, SSD_HEADS))
    expand = (lax.broadcasted_iota(jnp.int32, (SSD_HEADS, SSD_D_INNER), 1) // SSD_HEAD_DIM
              == lax.broadcasted_iota(jnp.int32, (SSD_HEADS, SSD_D_INNER), 0)).astype(MXU_DTYPE)
    dt_e = _dot_select(dt, expand, split_lhs=True)
    cs_e = _dot_select(cs, expand, split_lhs=True)
    tot_e = _dot_select(tot, expand, split_lhs=True)[0:1]

    xs = xbc_ref[:, 0:SSD_D_INNER]
    xdt = xs * dt_e
    xdt_m = xdt.astype(MXU_DTYPE)
    xw_m = (xdt * jnp.exp(tot_e - cs_e)).astype(MXU_DTYPE)
    e_e = jnp.exp(cs_e)
    dec_e = jnp.exp(tot_e)
    first = lax.broadcasted_iota(jnp.int32, (q, LANES), 1) < SSD_HEAD_DIM

    ppg = SSD_PAIRS // SSD_GROUPS
    for g in range(SSD_GROUPS):
        bg = xbc_ref[:, SSD_D_INNER + g * SSD_STATE:SSD_D_INNER + (g + 1) * SSD_STATE]
        cg = xbc_ref[:, SSD_D_INNER + SSD_BC + g * SSD_STATE:SSD_D_INNER + SSD_BC + (g + 1) * SSD_STATE]
        bg_m, cg_m = bg.astype(MXU_DTYPE), cg.astype(MXU_DTYPE)
        bgt_m = bg.T.astype(MXU_DTYPE)
        scores = _dot_nt(cg_m, bg_m)
        for pp in range(ppg):
            p = g * ppg + pp
            sl = slice(p * LANES, (p + 1) * LANES)
            ys = []
            for hh in range(2):
                h = 2 * p + hh
                diff = cs[:, h:h + 1] - cst[h:h + 1, :]
                decay = jnp.exp(jnp.where(mask, diff, -jnp.inf))
                ys.append(_dot((scores * decay).astype(MXU_DTYPE), xdt_m[:, sl]))
            hp = h_ref[p]
            y_off = _dot(cg_m, hp.astype(MXU_DTYPE)) * e_e[:, sl]
            y_ref[:, sl] = jnp.where(first, ys[0], ys[1]) + y_off
            h_ref[p] = dec_e[:, sl] * hp + _dot(bgt_m, xw_m[:, sl])

    @pl.when(c == n_c - 1)
    def _():
        hout_ref[...] = h_ref[...]


def _ssd(xbc, dt_raw, dt_bias, a_log, h0):
    n_rows = xbc.shape[0]
    q = SSD_CHUNK
    n_c = n_rows // q
    dt = dt_raw.reshape(n_rows, 2, SSD_HEADS).transpose(1, 0, 2)
    dtt = dt.transpose(0, 2, 1)

    def chunk(d, c):
        return jnp.where(d == 0, c, n_c - 1 - c)

    small = lambda shape: pl.BlockSpec((None,) + shape, lambda d, c: (d, 0, 0))
    return pl.pallas_call(
        _ssd_kernel,
        grid=(2, n_c),
        in_specs=[pl.BlockSpec((q, SSD_XBC), lambda d, c: (chunk(d, c), 0)),
                  pl.BlockSpec((None, q, SSD_HEADS), lambda d, c: (d, chunk(d, c), 0)),
                  pl.BlockSpec((None, SSD_HEADS, q), lambda d, c: (d, 0, chunk(d, c))),
                  small((1, SSD_HEADS)), small((SSD_HEADS, 1)), small((1, SSD_HEADS)), small((SSD_HEADS, 1)),
                  pl.BlockSpec((None, SSD_PAIRS, SSD_STATE, LANES), lambda d, c: (d, 0, 0, 0))],
        out_specs=[pl.BlockSpec((None, q, SSD_D_INNER), lambda d, c: (d, chunk(d, c), 0)),
                   pl.BlockSpec((None, SSD_PAIRS, SSD_STATE, LANES), lambda d, c: (d, 0, 0, 0))],
        out_shape=[jax.ShapeDtypeStruct((2, n_rows, SSD_D_INNER), F32),
                   jax.ShapeDtypeStruct((2, SSD_PAIRS, SSD_STATE, LANES), F32)],
        scratch_shapes=[pltpu.VMEM((SSD_PAIRS, SSD_STATE, LANES), F32)],
        compiler_params=_params("arbitrary", "arbitrary"),
        name="ssd_scan",
    )(xbc, dt, dtt, dt_bias[:, None, :], dt_bias[:, :, None], a_log[:, None, :], a_log[:, :, None], h0)


def _ssd_out_kernel(y_ref, xbc_ref, z_ref, dsk_ref, nw_ref, o_ref):
    y = y_ref[0] + y_ref[1] + dsk_ref[...] * xbc_ref[...]
    g = y * _silu(z_ref[...])
    ms = jnp.mean(g * g, axis=-1, keepdims=True)
    o_ref[...] = (g * lax.rsqrt(ms + EPS) * nw_ref[...]).astype(o_ref.dtype)


def _ssd_out(y2, xbc, zx, d_skip, norm_w):
    n_rows = xbc.shape[0]
    tl = _pick(n_rows, (512, 256, 128))
    w = SSD_D_INNER
    return pl.pallas_call(
        _ssd_out_kernel,
        grid=(n_rows // tl,),
        in_specs=[pl.BlockSpec((2, tl, w), lambda i: (0, i, 0)),
                  pl.BlockSpec((tl, w), lambda i: (i, 0)),
                  pl.BlockSpec((tl, w), lambda i: (i, 0)),
                  pl.BlockSpec((1, w), lambda i: (0, 0)),
                  pl.BlockSpec((1, w), lambda i: (0, 0))],
        out_specs=pl.BlockSpec((tl, w), lambda i: (i, 0)),
        out_shape=jax.ShapeDtypeStruct((n_rows, w), MXU_DTYPE),
        compiler_params=_params("parallel"),
        name="ssd_out",
    )(y2, xbc, zx, jnp.repeat(d_skip, SSD_HEAD_DIM)[None, :], norm_w[None, :])


def _head_mean_sq(x):
    blk = (lax.broadcasted_iota(jnp.int32, (LANES, LANES), 0) // NA_HEAD_DIM
           == lax.broadcasted_iota(jnp.int32, (LANES, LANES), 1) // NA_HEAD_DIM).astype(MXU_DTYPE)
    x2 = x * x
    hi = x2.astype(MXU_DTYPE)
    lo = (x2 - hi.astype(F32)).astype(MXU_DTYPE)
    return (_dot(hi, blk) + _dot(lo, blk)) * (1.0 / NA_HEAD_DIM)


def _even_proj_kernel(x_ref, nw_ref, sh_ref, sc_ref, w_ref, qw_ref, kw_ref, zx_ref, qkv_ref, dt_ref, a_ref,
                      *, zx_tiles, chunks):
    j = pl.program_id(1)

    @pl.when(j == 0)
    def _():
        a_ref[...] = _norm_mod(x_ref[...], nw_ref[...], sh_ref[...], sc_ref[...]).astype(a_ref.dtype)

    acc = _dot_nt(a_ref[...], w_ref[...].astype(a_ref.dtype))

    @pl.when(j < zx_tiles)
    def _():
        zx_ref[...] = acc

    @pl.when(j >= zx_tiles)
    def _():
        n_head_chunks = NA_D // LANES
        for c in range(chunks):
            chunk = (j - zx_tiles) * chunks + c
            x = acc[:, c * LANES:(c + 1) * LANES]
            w = jnp.where(chunk < n_head_chunks, qw_ref[...] * NA_HEAD_DIM ** -0.5, kw_ref[...])
            normed = x * lax.rsqrt(_head_mean_sq(x) + EPS) * w
            qkv_ref[:, c * LANES:(c + 1) * LANES] = jnp.where(chunk < 2 * n_head_chunks, normed, x).astype(qkv_ref.dtype)

    @pl.when(j == pl.num_programs(1) - 1)
    def _():
        dt_ref[...] = acc[:, (chunks - 1) * LANES:]


def _even_proj(x, nw, sh, sc, w_packed, layer, q_norm_w, k_norm_w, name):
    m, d = x.shape
    tm = _pick(m, (1024, 512, 256, 128))
    tn = 5 * LANES
    zx_tiles = S_ZX // tn
    wspec = pl.BlockSpec((1, LANES), lambda i, j: (0, 0))
    return pl.pallas_call(
        functools.partial(_even_proj_kernel, zx_tiles=zx_tiles, chunks=tn // LANES),
        grid=(m // tm, (S_ZX + QKVD) // tn),
        in_specs=[pl.BlockSpec((tm, d), lambda i, j: (i, 0)),
                  _vec_spec(nw, d, _zero), _vec_spec(sh, d, _zero), _vec_spec(sc, d, _zero),
                  _mat_spec(layer, (tn, d), lambda i, j: (j, 0)), wspec, wspec],
        out_specs=[pl.BlockSpec((tm, tn), lambda i, j: (i, jnp.minimum(j, zx_tiles - 1))),
                   pl.BlockSpec((tm, tn), lambda i, j: (i, jnp.maximum(j - zx_tiles, 0))),
                   pl.BlockSpec((tm, LANES), lambda i, j: (i, 0))],
        out_shape=[jax.ShapeDtypeStruct((m, S_ZX), F32), jax.ShapeDtypeStruct((m, QKVD), MXU_DTYPE),
                   jax.ShapeDtypeStruct((m, LANES), F32)],
        scratch_shapes=[pltpu.VMEM((tm, d), MXU_DTYPE)],
        compiler_params=_params("parallel", "arbitrary"),
        name=name,
    )(x, nw[0], sh[0], sc[0], w_packed, jnp.tile(q_norm_w, 2)[None, :], jnp.tile(k_norm_w, 2)[None, :])


def _attend(q2, parts):
    first = lax.broadcasted_iota(jnp.int32, q2.shape, 1) < NA_HEAD_DIM
    outs = []
    for head_mask in (first, jnp.logical_not(first)):
        qa = jnp.where(head_mask, q2, jnp.zeros_like(q2))
        scores = []
        for k, _, bias in parts:
            s = _dot_nt(qa, k)
            scores.append(s if bias is None else s + bias)
        m = functools.reduce(jnp.maximum, [jnp.max(s, axis=-1, keepdims=True) for s in scores])
        probs = [jnp.exp(s - m) for s in scores]
        denom = functools.reduce(jnp.add, [jnp.sum(p, axis=-1, keepdims=True) for p in probs])
        acc = functools.reduce(jnp.add, [_dot(p.astype(MXU_DTYPE), v) for p, (_, v, _) in zip(probs, parts)])
        outs.append(acc / denom)
    return jnp.where(first, outs[0], outs[1])


def _natten_kernel(q_ref, k_ref, v_ref, kc_ref, vc_ref, tbl_ref, o_ref, *, rb, rows):
    i = pl.program_id(1)
    kc, vc = kc_ref[...], vc_ref[...]
    n_win = NA_WIN_ROWS * GRID_W
    first = lax.broadcasted_iota(jnp.int32, (GRID_W, LANES), 1) < NA_HEAD_DIM
    for t in range(rb):
        r = i * rb + t
        start = jnp.clip(r - NA_WIN_ROWS // 2, 0, rows - NA_WIN_ROWS)
        dr0 = start - r + (NA_WIN_ROWS - 1)
        ks = pl.ds(pl.multiple_of(start * GRID_W, GRID_W), n_win)
        q2 = q_ref[t * GRID_W:(t + 1) * GRID_W, :]
        kw, vw = k_ref[ks, :], v_ref[ks, :]
        zero = jnp.zeros_like(q2)
        qs = jnp.concatenate([jnp.where(first, q2, zero), jnp.where(first, zero, q2)], axis=0)
        bias = jnp.concatenate([tbl_ref[dr0 + 2 * jj] for jj in range(NA_WIN_ROWS // 2)], axis=1)
        s_loc = _dot_nt(qs, kw) + bias
        s_ctx = _dot_nt(qs, kc)
        m = jnp.maximum(jnp.max(s_loc, axis=-1, keepdims=True), jnp.max(s_ctx, axis=-1, keepdims=True))
        p_loc, p_ctx = jnp.exp(s_loc - m), jnp.exp(s_ctx - m)
        denom = jnp.sum(p_loc, axis=-1, keepdims=True) + jnp.sum(p_ctx, axis=-1, keepdims=True)
        o = (_dot(p_loc.astype(MXU_DTYPE), vw) + _dot(p_ctx.astype(MXU_DTYPE), vc)) / denom
        o_ref[t * GRID_W:(t + 1) * GRID_W, :] = jnp.where(first, o[:GRID_W], o[GRID_W:]).astype(o_ref.dtype)


def _bias_table(rpb):
    col = jnp.arange(GRID_W)
    c0 = jnp.clip(col - NA_WIN_COLS // 2, 0, GRID_W - NA_WIN_COLS)
    col_in = (col[None, :] >= c0[:, None]) & (col[None, :] < c0[:, None] + NA_WIN_COLS)
    dc = jnp.clip(col[None, :] - col[:, None], 1 - NA_WIN_COLS, NA_WIN_COLS - 1) + (NA_WIN_COLS - 1)
    t = jnp.where(col_in, rpb.astype(F32)[:, :, dc], NEG)
    t = jnp.concatenate([t[:, :-1], t[:, 1:]], axis=-1)
    n_dr = 2 * NA_WIN_ROWS - 2
    t = t.reshape(NA_HEADS // 2, 2, n_dr, GRID_W, 2 * GRID_W).transpose(0, 2, 1, 3, 4)
    return t.reshape(NA_HEADS // 2, n_dr, 2 * GRID_W, 2 * GRID_W)


def _natten(qkv, qkv_ctx, rpb):
    n_rows = qkv.shape[0]
    rows = n_rows // GRID_W
    n_ctx = qkv_ctx.shape[0]
    rb = _pick(rows, (8, 4, 2, 1))
    n_win = NA_WIN_ROWS * GRID_W
    nb = NA_D // LANES
    seq = lambda n, off: pl.BlockSpec((n, LANES), lambda p, i: (0, p + off))
    tile = pl.BlockSpec((rb * GRID_W, LANES), lambda p, i: (i, p))
    return pl.pallas_call(
        functools.partial(_natten_kernel, rb=rb, rows=rows),
        grid=(NA_HEADS // 2, rows // rb),
        in_specs=[tile, seq(n_rows, nb), seq(n_rows, 2 * nb), seq(n_ctx, nb), seq(n_ctx, 2 * nb),
                  pl.BlockSpec((None, 2 * NA_WIN_ROWS - 2, 2 * GRID_W, 2 * GRID_W), lambda p, i: (p, 0, 0, 0))],
        out_specs=tile,
        out_shape=jax.ShapeDtypeStruct((n_rows, NA_D), MXU_DTYPE),
        compiler_params=_params("parallel", "arbitrary"),
        name="natten",
    )(qkv, qkv, qkv, qkv_ctx, qkv_ctx, _bias_table(rpb))


def _ctx_attn_kernel(q_ref, k_ref, v_ref, o_ref):
    o_ref[...] = _attend(q_ref[...], [(k_ref[...], v_ref[...], None)]).astype(o_ref.dtype)


def _ctx_attn(qkv_ctx):
    n_ctx = qkv_ctx.shape[0]
    nb = NA_D // LANES
    spec = lambda off: pl.BlockSpec((n_ctx, LANES), lambda p: (0, p + off))
    return pl.pallas_call(
        _ctx_attn_kernel,
        grid=(NA_HEADS // 2,),
        in_specs=[spec(0), spec(nb), spec(2 * nb)],
        out_specs=spec(0),
        out_shape=jax.ShapeDtypeStruct((n_ctx, NA_D), MXU_DTYPE),
        compiler_params=_params("parallel"),
        name="ctx_attn",
    )(qkv_ctx, qkv_ctx, qkv_ctx)


def _load_slabs(ref, n, s):
    return jnp.concatenate([ref[pl.ds(c, n, stride=s), :] for c in range(s)], axis=1)


def _store_slabs(ref, val):
    n = val.shape[0]
    s = val.shape[1] // LANES
    for c in range(s):
        ref[pl.ds(c, n, stride=s), :] = val[:, c * LANES:(c + 1) * LANES].astype(ref.dtype)

def _router_kernel(*refs, n_lat_tiles, with_ctx):
    if with_ctx:
        x_ref, cx_ref, nw_ref, sh_ref, sc_ref, shc_ref, scc_ref, wr_ref, m_ref, e_ref, g_ref = refs
        is_ctx = pl.program_id(0) >= n_lat_tiles
        x = jnp.where(is_ctx, cx_ref[...], x_ref[...])
        sh = jnp.where(is_ctx, shc_ref[...], sh_ref[...])
        sc = jnp.where(is_ctx, scc_ref[...], sc_ref[...])
    else:
        x_ref, nw_ref, sh_ref, sc_ref, wr_ref, m_ref, e_ref, g_ref = refs
        x, sh, sc = x_ref[...], sh_ref[...], sc_ref[...]
    m = _norm_mod(x, nw_ref[...], sh, sc)
    _store_slabs(m_ref, m)
    logits = _dot_split2(m, wr_ref[...])
    lane = lax.broadcasted_iota(jnp.int32, logits.shape, 1)
    big = jnp.int32(LANES)

    def top(vals):
        v = jnp.max(vals, axis=-1, keepdims=True)
        idx = jnp.min(jnp.where(vals == v, lane, big), axis=-1, keepdims=True)
        return v, idx

    gl = jnp.where(lane < N_GROUPS, logits, -jnp.inf)
    g_max, grp = top(gl)
    p_grp = 1.0 / jnp.sum(jnp.exp(gl - g_max), axis=-1, keepdims=True)
    e_lane = lane - N_GROUPS
    in_grp = (e_lane >= grp * EXPERTS_PER_GROUP) & (e_lane < (grp + 1) * EXPERTS_PER_GROUP)
    el = jnp.where(in_grp, logits, -jnp.inf)
    v1, i1 = top(el)
    v2, i2 = top(jnp.where(lane == i1, -jnp.inf, el))
    t = jnp.exp(v2 - v1)
    g1 = p_grp / (1.0 + t)
    g2 = p_grp * t / (1.0 + t)
    e_ref[...] = jnp.where(lane == 0, i1 - N_GROUPS, jnp.where(lane == 1, i2 - N_GROUPS, 0))
    g_ref[...] = jnp.where(lane == 0, g1, jnp.where(lane == 1, g2, 0.0))


def _router(lat, cx, nw, mod_lat, mod_ctx, w_route):
    n_lat, d = lat.shape
    with_ctx = cx is not None
    n_ctx = cx.shape[0] if with_ctx else 0
    tm = _pick(n_lat, (256, 128)) if with_ctx else _pick(n_lat, (512, 256, 128))
    assert n_ctx % tm == 0
    n_lat_tiles = n_lat // tm
    n_tok = n_lat + n_ctx
    lat_tile = pl.BlockSpec((tm, d), lambda i: (jnp.minimum(i, n_lat_tiles - 1), 0))
    vec = lambda v: _vec_spec(v, d, _zero)
    wide = pl.BlockSpec((tm, LANES), lambda i: (i, 0))
    w_spec = pl.BlockSpec((d, LANES), lambda i: (0, 0))
    if with_ctx:
        ctx_tile = pl.BlockSpec((tm, d), lambda i: (jnp.maximum(i - n_lat_tiles, 0), 0))
        in_specs = [lat_tile, ctx_tile, vec(nw), vec(mod_lat[3]), vec(mod_lat[4]), vec(mod_ctx[3]), vec(mod_ctx[4]),
                    w_spec]
        args = (lat, cx, nw[0], mod_lat[3][0], mod_lat[4][0], mod_ctx[3][0], mod_ctx[4][0], w_route)
    else:
        in_specs = [lat_tile, vec(nw), vec(mod_lat[3]), vec(mod_lat[4]), w_spec]
        args = (lat, nw[0], mod_lat[3][0], mod_lat[4][0], w_route)
    m, e, g = pl.pallas_call(
        functools.partial(_router_kernel, n_lat_tiles=n_lat_tiles, with_ctx=with_ctx),
        grid=(n_tok // tm,),
        in_specs=in_specs,
        out_specs=[pl.BlockSpec((tm * (d // LANES), LANES), lambda i: (i, 0)), wide, wide],
        out_shape=[jax.ShapeDtypeStruct((n_tok * (d // LANES), LANES), F32),
                   jax.ShapeDtypeStruct((n_tok, LANES), jnp.int32),
                   jax.ShapeDtypeStruct((n_tok, LANES), F32)],
        compiler_params=_params("parallel"),
        name="router",
    )(*args)
    return m, e[:, :TOP_K], g


def _moe_kernel(first_ref, nblk_ref, tok_ref, nact_ref, m_hbm, wg_hbm, wu_hbm, wd_hbm, yb_hbm,
                xbuf, ybuf, gsem, osem, wgf_ref, wuf_ref, wdf_ref, wsem, wgb_ref, wub_ref, wdb_ref, *, layer):
    e, n_e = pl.program_id(0), pl.num_programs(0)
    n_act = nact_ref[0]
    s = wgf_ref.shape[1] // LANES
    blk_rows = MOE_BLOCK * s
    n_blocks = yb_hbm.shape[0] // blk_rows

    def weight_copies(expert, ws):
        return [pltpu.make_async_copy(src.at[layer, expert], dst.at[ws], wsem.at[k, ws])
                for k, (src, dst) in enumerate(((wg_hbm, wgf_ref), (wu_hbm, wuf_ref), (wd_hbm, wdf_ref)))]

    @pl.when(e == 0)
    def _():
        for ahead in range(MOE_WEIGHT_AHEAD):
            for copy in weight_copies(ahead, ahead):
                copy.start()

    ws = e % MOE_WEIGHT_SLOTS
    for copy in weight_copies(e, ws):
        copy.wait()

    @pl.when(e + MOE_WEIGHT_AHEAD < n_e)
    def _():
        for copy in weight_copies(e + MOE_WEIGHT_AHEAD, (e + MOE_WEIGHT_AHEAD) % MOE_WEIGHT_SLOTS):
            copy.start()

    def gather_start(blk, slot):
        for r in range(MOE_BLOCK):
            tok = tok_ref[blk * MOE_BLOCK + r]
            pltpu.make_async_copy(m_hbm.at[pl.ds(pl.multiple_of(tok * s, s), s), :],
                                  xbuf.at[slot, pl.ds(r * s, s), :],
                                  gsem.at[slot]).start(priority=r % ROW_DMA_THREADS)

    def gather_wait(slot):
        pltpu.make_async_copy(m_hbm.at[pl.ds(0, blk_rows), :], xbuf.at[slot], gsem.at[slot]).wait()

    def out_copy(blk, slot):
        rows = pl.ds(pl.multiple_of(blk * blk_rows, blk_rows), blk_rows)
        return pltpu.make_async_copy(ybuf.at[slot], yb_hbm.at[rows, :], osem.at[slot])

    @pl.when(e == 0)
    def _():
        for ahead in range(MOE_AHEAD):
            gather_start(jnp.minimum(ahead, n_act - 1), ahead)

    @pl.when(nblk_ref[e] > 0)
    def _():
        wgb_ref[...] = wgf_ref[ws].astype(wgb_ref.dtype)
        wub_ref[...] = wuf_ref[ws].astype(wub_ref.dtype)
        wdb_ref[...] = wdf_ref[ws].astype(wdb_ref.dtype)

    def block(j, carry):
        b = first_ref[e] + j
        slot = b % MOE_SLOTS
        gather_wait(slot)

        @pl.when(b >= MOE_SLOTS)
        def _():
            out_copy(b - MOE_SLOTS, slot).wait()

        gather_start(jnp.minimum(b + MOE_AHEAD, n_act - 1), (b + MOE_AHEAD) % MOE_SLOTS)
        x = _load_slabs(xbuf.at[slot], MOE_BLOCK, s).astype(MXU_DTYPE)
        h = _silu(_dot(x, wgb_ref[...])) * _dot(x, wub_ref[...])
        _store_slabs(ybuf.at[slot], _dot(h.astype(MXU_DTYPE), wdb_ref[...]))
        out_copy(b, slot).start()
        return carry

    lax.fori_loop(0, nblk_ref[e], block, 0)

    @pl.when(e == n_e - 1)
    def _():
        for ahead in range(MOE_AHEAD):
            gather_wait((n_act + ahead) % MOE_SLOTS)
        for back in range(1, MOE_SLOTS + 1):
            @pl.when(n_act >= back)
            def _():
                out_copy(n_act - back, (n_act - back) % MOE_SLOTS).wait()

        ybuf[0] = jnp.zeros(ybuf.shape[1:], ybuf.dtype)

        def fill(b, carry):
            out_copy(b, 0).start()
            out_copy(b, 0).wait()
            return carry

        lax.fori_loop(n_act, n_blocks, fill, 0)


def _moe_ffn(m_tok, slot_tok, first_blk, n_blk, n_act, w_gate, w_up, w_down, layer):
    n_rows = slot_tok.shape[0]
    d, f = w_gate.shape[-2:]
    blk_rows = MOE_BLOCK * (d // LANES)
    hbm = pl.BlockSpec(memory_space=pl.ANY)
    grid_spec = pltpu.PrefetchScalarGridSpec(
        num_scalar_prefetch=4,
        grid=(N_EXPERTS,),
        in_specs=[hbm, hbm, hbm, hbm],
        out_specs=hbm,
        scratch_shapes=[pltpu.VMEM((MOE_SLOTS, blk_rows, LANES), F32), pltpu.VMEM((MOE_SLOTS, blk_rows, LANES), F32),
                        pltpu.SemaphoreType.DMA((MOE_SLOTS,)), pltpu.SemaphoreType.DMA((MOE_SLOTS,)),
                        pltpu.VMEM((MOE_WEIGHT_SLOTS, d, f), F32), pltpu.VMEM((MOE_WEIGHT_SLOTS, d, f), F32),
                        pltpu.VMEM((MOE_WEIGHT_SLOTS, f, d), F32), pltpu.SemaphoreType.DMA((3, MOE_WEIGHT_SLOTS)),
                        pltpu.VMEM((d, f), MXU_DTYPE), pltpu.VMEM((d, f), MXU_DTYPE), pltpu.VMEM((f, d), MXU_DTYPE)],
    )
    return pl.pallas_call(
        functools.partial(_moe_kernel, layer=layer),
        grid_spec=grid_spec,
        out_shape=jax.ShapeDtypeStruct((n_rows * (d // LANES), LANES), F32),
        compiler_params=_params("arbitrary"),
        name="moe_ffn",
    )(first_blk, n_blk, slot_tok, n_act, m_tok, w_gate, w_up, w_down)


def _combine_kernel(pos_ref, x_ref, g_ref, gt_ref, yb_hbm, o_ref, buf, sem, *, tm):
    i, n_i = pl.program_id(0), pl.num_programs(0)
    s = x_ref.shape[1] // LANES

    def gather_start(tile, slot):
        for j in range(tm):
            for k in range(TOP_K):
                p = pos_ref[(tile * tm + j) * TOP_K + k]
                pltpu.make_async_copy(yb_hbm.at[pl.ds(pl.multiple_of(p * s, s), s), :],
                                      buf.at[slot, k, pl.ds(j * s, s), :],
                                      sem.at[slot]).start(priority=(j * TOP_K + k) % ROW_DMA_THREADS)

    def gather_wait(slot):
        for k in range(TOP_K):
            pltpu.make_async_copy(yb_hbm.at[pl.ds(0, tm * s), :], buf.at[slot, k], sem.at[slot]).wait()

    @pl.when(i == 0)
    def _():
        for ahead in range(MOE_AHEAD):
            gather_start(jnp.minimum(ahead, n_i - 1), ahead)

    slot = i % MOE_SLOTS
    gather_wait(slot)
    gather_start(jnp.minimum(i + MOE_AHEAD, n_i - 1), (i + MOE_AHEAD) % MOE_SLOTS)
    gt = gt_ref[...]
    y0 = _load_slabs(buf.at[slot, 0], tm, s)
    y1 = _load_slabs(buf.at[slot, 1], tm, s)
    o_ref[...] = x_ref[...] + g_ref[...] * (gt[:, 0:1] * y0 + gt[:, 1:2] * y1)

    @pl.when(i == n_i - 1)
    def _():
        for ahead in range(1, MOE_AHEAD + 1):
            gather_wait((i + ahead) % MOE_SLOTS)


def _combine(x, g, gates, yb, pos):
    m, d = x.shape
    tm = MOE_BLOCK
    grid_spec = pltpu.PrefetchScalarGridSpec(
        num_scalar_prefetch=1,
        grid=(m // tm,),
        in_specs=[pl.BlockSpec((tm, d), lambda i, pos: (i, 0)),
                  _vec_spec(g, d, _zero),
                  pl.BlockSpec((tm, LANES), lambda i, pos: (i, 0)),
                  pl.BlockSpec(memory_space=pl.ANY)],
        out_specs=pl.BlockSpec((tm, d), lambda i, pos: (i, 0)),
        scratch_shapes=[pltpu.VMEM((MOE_SLOTS, TOP_K, tm * (d // LANES), LANES), F32),
                        pltpu.SemaphoreType.DMA((MOE_SLOTS,))],
    )
    return pl.pallas_call(
        functools.partial(_combine_kernel, tm=tm),
        grid_spec=grid_spec,
        out_shape=jax.ShapeDtypeStruct((m, d), F32),
        compiler_params=_params("arbitrary"),
        name="moe_combine",
    )(pos, x, g[0], gates, yb)


def _dispatch_kernel(e_ref, dest_ref, first_ref, nblk_ref, cnt_ref, base_ref):
    phase, i = pl.program_id(0), pl.program_id(1)
    tile = e_ref.shape[1]
    onehot = lax.broadcasted_iota(jnp.int32, (N_EXPERTS, tile), 0) == e_ref[...]

    @pl.when(jnp.logical_and(phase == 0, i == 0))
    def _():
        cnt_ref[...] = jnp.zeros_like(cnt_ref)

    @pl.when(phase == 0)
    def _():
        cnt_ref[...] += jnp.sum(onehot.astype(F32), axis=1, keepdims=True)

    @pl.when(jnp.logical_and(phase == 1, i == 0))
    def _():
        n_blk = jnp.right_shift(cnt_ref[...].astype(jnp.int32) + (MOE_BLOCK - 1),
                                MOE_BLOCK.bit_length() - 1)
        tri = (lax.broadcasted_iota(jnp.int32, (N_EXPERTS, N_EXPERTS), 0)
               >= lax.broadcasted_iota(jnp.int32, (N_EXPERTS, N_EXPERTS), 1)).astype(F32)
        blk_end = _dot(tri, n_blk.astype(F32), HI)
        first = blk_end.astype(jnp.int32) - n_blk
        nblk_ref[...] = n_blk
        first_ref[...] = first
        base_ref[...] = (first * MOE_BLOCK).astype(F32)

    @pl.when(phase == 1)
    def _():
        tri = (lax.broadcasted_iota(jnp.int32, (tile, tile), 0)
               <= lax.broadcasted_iota(jnp.int32, (tile, tile), 1)).astype(MXU_DTYPE)
        cum = _dot(onehot.astype(MXU_DTYPE), tri)
        slot = jnp.where(onehot, base_ref[:, 0:1] + cum - 1.0, 0.0)
        dest_ref[...] = jnp.sum(slot, axis=0, keepdims=True).astype(jnp.int32)
        base_ref[...] += cum[:, tile - 1:tile]


def _dispatch(expert):
    n_tok = expert.shape[0]
    n = n_tok * TOP_K
    n_blocks = -(-n // MOE_BLOCK) + N_EXPERTS
    n_rows = n_blocks * MOE_BLOCK
    tile = _pick(n, (512, 256, 128))
    per_expert = pl.BlockSpec((N_EXPERTS, LANES), lambda ph, i: (0, 0))
    dest, first, n_blk = pl.pallas_call(
        _dispatch_kernel,
        grid=(2, n // tile),
        in_specs=[pl.BlockSpec((1, tile), lambda ph, i: (0, i))],
        out_specs=[pl.BlockSpec((1, tile), lambda ph, i: (0, i * ph)), per_expert, per_expert],
        out_shape=[jax.ShapeDtypeStruct((1, n), jnp.int32),
                   jax.ShapeDtypeStruct((N_EXPERTS, LANES), jnp.int32),
                   jax.ShapeDtypeStruct((N_EXPERTS, LANES), jnp.int32)],
        scratch_shapes=[pltpu.VMEM((N_EXPERTS, LANES), F32), pltpu.VMEM((N_EXPERTS, LANES), F32)],
        compiler_params=_params("arbitrary", "arbitrary"),
        name="moe_dispatch",
    )(expert.reshape(1, n))
    dest, first_blk, n_blk = dest[0], first[:, 0], n_blk[:, 0]
    pair_tok = jnp.arange(n, dtype=jnp.int32) // TOP_K
    slot_tok = jnp.zeros((n_rows,), jnp.int32).at[dest].set(pair_tok)
    n_act = first_blk[-1:] + n_blk[-1:]
    return dest, slot_tok, first_blk, n_blk, n_act


def _even_mixer(lat, cx, nw, mod_lat, mod_ctx, e, w_in, conv_w, conv_b, dt_bias, a_log, d_skip, ssd_norm_w,
                q_norm_w, k_norm_w, rpb, w_out, ctx_out):
    def project(x, mod, tag):
        zx, qkv, dt = _even_proj(x, nw, mod[0], mod[1], w_in, e, q_norm_w[e], k_norm_w[e], "proj_even_" + tag)
        xbc = _xbc_conv(zx, conv_w[e], conv_b[e])
        return zx, xbc, dt[:, :2 * SSD_HEADS], qkv

    zx_c, xbc_c, dt_c, qkv_c = project(cx, mod_ctx, "ctx")
    zx_l, xbc_l, dt_l, qkv_l = project(lat, mod_lat, "lat")
    h0 = jnp.zeros((2, SSD_PAIRS, SSD_STATE, LANES), F32)
    y_c, h_ctx = _ssd(xbc_c, dt_c, dt_bias[e], a_log[e], h0)
    y_l, _ = _ssd(xbc_l, dt_l, dt_bias[e], a_log[e], h_ctx)
    yssd_l = _ssd_out(y_l, xbc_l, zx_l, d_skip[e], ssd_norm_w[e])
    yatt_l = _natten(qkv_l, qkv_c, rpb[e])
    lat = _matmul_res([yssd_l, yatt_l], w_out, e, lat, mod_lat[2], "out_even_lat")
    if ctx_out:
        yssd_c = _ssd_out(y_c, xbc_c, zx_c, d_skip[e], ssd_norm_w[e])
        yatt_c = _ctx_attn(qkv_c)
        cx = _matmul_res([yssd_c, yatt_c], w_out, e, cx, mod_ctx[2], "out_even_ctx")
    return lat, cx


def _odd_mixer(lat, cx, nw, mod_lat, mod_ctx, o, w_in, conv_w, w_out, ctx_out):
    def mix(x, mod, tag):
        u = _gated_conv_proj(x, nw, mod[0], mod[1], w_in, o, conv_w[o], "proj_odd_" + tag)
        return _matmul_res([u], w_out, o, x, mod[2], "out_odd_" + tag)

    lat = mix(lat, mod_lat, "lat")
    if ctx_out:
        cx = mix(cx, mod_ctx, "ctx")
    return lat, cx


def kernel(x, c, ctx, c_ctx, w_mod, b_mod, norm_mix_w, norm_ffn_w, ev_w_in, ev_conv_w, ev_conv_b, ev_dt_bias,
           ev_a_log, ev_d_skip, ev_ssd_norm_w, ev_q_norm_w, ev_k_norm_w, ev_rpb, ev_w_out, od_w_in, od_conv_w,
           od_w_out, moe_w_group, moe_w_router, moe_w_gate, moe_w_up, moe_w_down):
    bsz, n_lat, d = x.shape
    assert bsz == 1 and ctx.shape[0] == 1
    assert n_lat % GRID_W == 0 and n_lat // GRID_W >= NA_WIN_ROWS and n_lat % SSD_CHUNK == 0
    assert ctx.shape[1] % SSD_CHUNK == 0 and d % LANES == 0
    depth = w_mod.shape[0]
    lat, cx = x[0], ctx[0]
    mods = _modulation(jnp.stack([c[0], c_ctx], axis=1), w_mod, b_mod)
    mods = mods.reshape(depth * 2 * 6, d)
    ev_w_packed = _repack_w_in(ev_w_in)

    for l in range(depth):
        even = l % 2 == 0
        ctx_out = any(j % 2 == 0 for j in range(l + 1, depth))
        mod_lat = [_vec(mods, (l * 2 + 0) * 6 + i) for i in range(6)]
        mod_ctx = [_vec(mods, (l * 2 + 1) * 6 + i) for i in range(6)]
        nw = _vec(norm_mix_w, l)
        if even:
            lat, cx = _even_mixer(lat, cx, nw, mod_lat, mod_ctx, l // 2, ev_w_packed, ev_conv_w, ev_conv_b,
                                  ev_dt_bias, ev_a_log, ev_d_skip, ev_ssd_norm_w, ev_q_norm_w,
                                  ev_k_norm_w, ev_rpb, ev_w_out, ctx_out)
        else:
            lat, cx = _odd_mixer(lat, cx, nw, mod_lat, mod_ctx, l // 2, od_w_in, od_conv_w, od_w_out, ctx_out)

        nfw = _vec(norm_ffn_w, l)
        w_route = jnp.concatenate([moe_w_group[l], moe_w_router[l],
                                   jnp.zeros((d, LANES - N_GROUPS - N_EXPERTS), F32)], axis=1)
        m_tok, expert, gates = _router(lat, cx if ctx_out else None, nfw, mod_lat, mod_ctx, w_route)
        dest, slot_tok, first_blk, n_blk, n_act = _dispatch(expert)
        yb = _moe_ffn(m_tok, slot_tok, first_blk, n_blk, n_act, moe_w_gate, moe_w_up, moe_w_down, l)
        lat = _combine(lat, mod_lat[5], gates[:n_lat], yb, dest[:n_lat * TOP_K])
        if ctx_out:
            cx = _combine(cx, mod_ctx[5], gates[n_lat:], yb, dest[n_lat * TOP_K:])
    return lat[None]
```

```python
import functools

import jax
import jax.numpy as jnp
from jax import lax
from jax.experimental import pallas as pl
from jax.experimental.pallas import tpu as pltpu

F32 = jnp.float32
MXU_DTYPE = jnp.bfloat16
EPS = 1e-6
NEG = -1e30

LANES = 128
SUBLANES = 8
VMEM_LIMIT = 48 * 1024 * 1024
ROW_DMA_THREADS = 2

SSD_HEADS = 16
SSD_HEAD_DIM = 64
SSD_D_INNER = SSD_HEADS * SSD_HEAD_DIM
SSD_STATE = 128
SSD_GROUPS = 2
SSD_CHUNK = 128
SSD_BC = SSD_GROUPS * SSD_STATE
SSD_XBC = SSD_D_INNER + 2 * SSD_BC
SSD_PAIRS = SSD_HEADS // 2
NA_HEADS = 16
NA_HEAD_DIM = 64
NA_D = NA_HEADS * NA_HEAD_DIM
NA_WIN_ROWS = 8
NA_WIN_COLS = 16
GRID_W = 64
N_GROUPS = 8
EXPERTS_PER_GROUP = 8
N_EXPERTS = N_GROUPS * EXPERTS_PER_GROUP
TOP_K = 2
D_FF_EXPERT = 384
MOE_BLOCK = 128
MOE_AHEAD = 2
MOE_SLOTS = MOE_AHEAD + 1
MOE_WEIGHT_AHEAD = 2
MOE_WEIGHT_SLOTS = MOE_WEIGHT_AHEAD + 1
S_ZX = SSD_D_INNER + SSD_XBC
S_DT = S_ZX + 2 * SSD_HEADS
QKVD = 3 * NA_D + LANES


def _pick(n, prefs):
    for p in prefs:
        if n % p == 0:
            return p
    return n


def _params(*sem):
    return pltpu.CompilerParams(dimension_semantics=sem, vmem_limit_bytes=VMEM_LIMIT)


def _vec(arr, idx):
    return arr.reshape(arr.shape[0], 1, arr.shape[-1]), idx


def _vec_spec(vec, width, col):
    idx = vec[1]
    return pl.BlockSpec((None, 1, width), lambda *g: (idx, 0, col(*g)))


def _mat_spec(layer, block, idx):
    return pl.BlockSpec((None,) + block, lambda *g: (layer,) + idx(*g))


def _zero(*g):
    return 0


def _silu(x):
    return x * (1.0 / (1.0 + jnp.exp(-x)))


def _softplus(x):
    return jnp.maximum(x, 0.0) + jnp.log1p(jnp.exp(-jnp.abs(x)))


def _dot(a, b, precision=None):
    return jnp.dot(a, b, preferred_element_type=F32, precision=precision)


def _dot_select(a, b, split_lhs):
    x = a if split_lhs else b
    hi = x.astype(MXU_DTYPE)
    rest = x - hi.astype(F32)
    mid = rest.astype(MXU_DTYPE)
    lo = (rest - mid.astype(F32)).astype(MXU_DTYPE)
    if split_lhs:
        return _dot(hi, b) + _dot(mid, b) + _dot(lo, b)
    return _dot(a, hi) + _dot(a, mid) + _dot(a, lo)


def _dot_split2(a, b):
    a_hi, b_hi = a.astype(MXU_DTYPE), b.astype(MXU_DTYPE)
    a_lo = (a - a_hi.astype(F32)).astype(MXU_DTYPE)
    b_lo = (b - b_hi.astype(F32)).astype(MXU_DTYPE)
    return _dot(a_hi, b_hi) + _dot(a_hi, b_lo) + _dot(a_lo, b_hi)


def _dot_nt(a, b):
    return lax.dot_general(a, b, (((1,), (1,)), ((), ())), preferred_element_type=F32)


def _norm_mod(x, nw, sh, sc):
    ms = jnp.mean(x * x, axis=-1, keepdims=True)
    n = x * lax.rsqrt(ms + EPS) * nw
    return n * (1.0 + sc) + sh


def _mod_kernel(ct_ref, w_ref, b_ref, o_ref):
    ct = ct_ref[...]
    s = _silu(ct)
    w = w_ref[...]
    r0 = jnp.sum(s[:, 0:1] * w, axis=0, keepdims=True)
    r1 = jnp.sum(s[:, 1:2] * w, axis=0, keepdims=True)
    o_ref[...] = jnp.concatenate([r0, r1], axis=0) + b_ref[...]


def _modulation(ct, w_mod, b_mod):
    depth, d, n = w_mod.shape
    tn = _pick(n, (512, 256, 128))
    return pl.pallas_call(
        _mod_kernel,
        grid=(depth, n // tn),
        in_specs=[pl.BlockSpec((d, 2), lambda l, j: (0, 0)),
                  pl.BlockSpec((None, d, tn), lambda l, j: (l, 0, j)),
                  pl.BlockSpec((None, 1, tn), lambda l, j: (l, 0, j))],
        out_specs=pl.BlockSpec((None, 2, tn), lambda l, j: (l, 0, j)),
        out_shape=jax.ShapeDtypeStruct((depth, 2, n), F32),
        compiler_params=_params("parallel", "parallel"),
        name="modulation",
    )(ct, w_mod, b_mod.reshape(depth, 1, n))


def _repack_w_in_kernel(w_ref, o_ref):
    o_ref[0:S_ZX, :] = w_ref[0:S_ZX, :].astype(o_ref.dtype)
    o_ref[S_ZX:S_ZX + 3 * NA_D, :] = w_ref[S_DT:S_DT + 3 * NA_D, :].astype(o_ref.dtype)
    o_ref[S_ZX + 3 * NA_D:S_ZX + 3 * NA_D + 2 * SSD_HEADS, :] = w_ref[S_ZX:S_DT, :].astype(o_ref.dtype)
    o_ref[S_ZX + 3 * NA_D + 2 * SSD_HEADS:, :] = jnp.zeros((LANES - 2 * SSD_HEADS, o_ref.shape[1]), o_ref.dtype)


def _repack_w_in(w_in):
    layers, d, n_in = w_in.shape
    tc = _pick(d, (512, 256, 128))
    n_out = S_ZX + QKVD
    return pl.pallas_call(
        _repack_w_in_kernel,
        grid=(layers, d // tc),
        in_specs=[pl.BlockSpec((None, n_in, tc), lambda l, i: (l, 0, i))],
        out_specs=pl.BlockSpec((None, n_out, tc), lambda l, i: (l, 0, i)),
        out_shape=jax.ShapeDtypeStruct((layers, n_out, d), MXU_DTYPE),
        compiler_params=_params("parallel", "parallel"),
        name="repack_w_in",
    )(jnp.swapaxes(w_in, 1, 2))


def _matmul_res_kernel(*refs, n_a):
    a_refs, w_refs = refs[:n_a], refs[n_a:2 * n_a]
    res_ref, g_ref, o_ref = refs[2 * n_a:]
    acc = None
    for a_ref, w_ref in zip(a_refs, w_refs):
        t = _dot(a_ref[...].astype(MXU_DTYPE), w_ref[...].astype(MXU_DTYPE))
        acc = t if acc is None else acc + t
    o_ref[...] = res_ref[...] + g_ref[...] * acc


def _matmul_res(a_list, w, layer, res, g, name):
    n_a = len(a_list)
    m, n = res.shape
    kp = w.shape[1] // n_a
    tm = _pick(m, (2048, 1024, 512, 256, 128))
    tn = _pick(n, (512, 256, 128))
    in_specs = [pl.BlockSpec((tm, kp), lambda i, j: (i, 0)) for _ in range(n_a)]
    in_specs += [_mat_spec(layer, (kp, tn), functools.partial(lambda i, j, p: (p, j), p=p)) for p in range(n_a)]
    in_specs += [pl.BlockSpec((tm, tn), lambda i, j: (i, j)), _vec_spec(g, tn, lambda i, j: j)]
    return pl.pallas_call(
        functools.partial(_matmul_res_kernel, n_a=n_a),
        grid=(m // tm, n // tn),
        in_specs=in_specs,
        out_specs=pl.BlockSpec((tm, tn), lambda i, j: (i, j)),
        out_shape=jax.ShapeDtypeStruct((m, n), F32),
        compiler_params=_params("parallel", "parallel"),
        name=name,
    )(*a_list, *([w] * n_a), res, g[0])


def _shifted(x, prev_ref, next_ref):
    i, n_i = pl.program_id(0), pl.num_programs(0)
    tl = x.shape[0]
    row = lax.broadcasted_iota(jnp.int32, x.shape, 0)
    prev_row = jnp.where(i == 0, 0.0, prev_ref[SUBLANES - 1:SUBLANES, :])
    next_row = jnp.where(i == n_i - 1, 0.0, next_ref[0:1, :])
    up = jnp.where(row == 0, prev_row, pltpu.roll(x, 1, 0))
    down = jnp.where(row == tl - 1, next_row, pltpu.roll(x, tl - 1, 0))
    return up, down


def _halo_specs(tl, tc, n_rows, col_off):
    nb = tl // SUBLANES
    last = n_rows // SUBLANES - 1
    return [pl.BlockSpec((tl, tc), lambda i, j: (i, j + col_off)),
            pl.BlockSpec((SUBLANES, tc), lambda i, j: (jnp.maximum(i * nb - 1, 0), j + col_off)),
            pl.BlockSpec((SUBLANES, tc), lambda i, j: (jnp.minimum((i + 1) * nb, last), j + col_off))]


def _xbc_conv_kernel(x_ref, prev_ref, next_ref, w_ref, b_ref, o_ref):
    x = x_ref[...]
    up, down = _shifted(x, prev_ref, next_ref)
    w = w_ref[...]
    o_ref[...] = _silu(w[0:1] * up + w[1:2] * x + w[2:3] * down + b_ref[...])


def _xbc_conv(zx, conv_w, conv_b):
    n_rows = zx.shape[0]
    tl = _pick(n_rows, (512, 256, 128))
    tc = 512
    off = SSD_D_INNER // tc
    return pl.pallas_call(
        _xbc_conv_kernel,
        grid=(n_rows // tl, SSD_XBC // tc),
        in_specs=_halo_specs(tl, tc, n_rows, off) + [pl.BlockSpec((3, tc), lambda i, j: (0, j)),
                                                     pl.BlockSpec((1, tc), lambda i, j: (0, j))],
        out_specs=pl.BlockSpec((tl, tc), lambda i, j: (i, j)),
        out_shape=jax.ShapeDtypeStruct((n_rows, SSD_XBC), F32),
        compiler_params=_params("parallel", "parallel"),
        name="xbc_conv",
    )(zx, zx, zx, conv_w, conv_b.reshape(1, -1))


def _gated_conv_proj_kernel(x_ref, xp_ref, xn_ref, nw_ref, sh_ref, sc_ref, wb_ref, wc_ref, wx_ref, cw_ref, o_ref,
                            a_ref, ah_ref):
    i, n_i = pl.program_id(0), pl.num_programs(0)

    @pl.when(pl.program_id(1) == 0)
    def _():
        nw, sh, sc = nw_ref[...], sh_ref[...], sc_ref[...]
        a_ref[...] = _norm_mod(x_ref[...], nw, sh, sc).astype(a_ref.dtype)
        ah_ref[0:SUBLANES, :] = _norm_mod(xp_ref[...], nw, sh, sc)
        ah_ref[SUBLANES:, :] = _norm_mod(xn_ref[...], nw, sh, sc)

    a, ah = a_ref[...], ah_ref[...].astype(a_ref.dtype)
    wc, wx = wc_ref[...].astype(a.dtype), wx_ref[...].astype(a.dtype)
    u = _dot(a, wc) * _dot(a, wx)
    uh = _dot(ah, wc) * _dot(ah, wx)
    tm = u.shape[0]
    row = lax.broadcasted_iota(jnp.int32, u.shape, 0)
    prev_row = jnp.where(i == 0, 0.0, uh[SUBLANES - 1:SUBLANES, :])
    next_row = jnp.where(i == n_i - 1, 0.0, uh[SUBLANES:SUBLANES + 1, :])
    up = jnp.where(row == 0, prev_row, pltpu.roll(u, 1, 0))
    down = jnp.where(row == tm - 1, next_row, pltpu.roll(u, tm - 1, 0))
    cw = cw_ref[...]
    conv = cw[0:1] * up + cw[1:2] * u + cw[2:3] * down
    o_ref[...] = (_dot(a, wb_ref[...].astype(a.dtype)) * conv).astype(o_ref.dtype)


def _gated_conv_proj(x, nw, sh, sc, w, layer, conv_w, name):
    m, d = x.shape
    c = w.shape[2] // 3
    tm = _pick(m, (1024, 512, 256, 128))
    tc = _pick(c, (256, 128))
    nb = c // tc
    blocks = tm // SUBLANES
    last = m // SUBLANES - 1
    wspec = lambda off: _mat_spec(layer, (d, tc), lambda i, j: (0, j + off))
    return pl.pallas_call(
        _gated_conv_proj_kernel,
        grid=(m // tm, nb),
        in_specs=[pl.BlockSpec((tm, d), lambda i, j: (i, 0)),
                  pl.BlockSpec((SUBLANES, d), lambda i, j: (jnp.maximum(i * blocks - 1, 0), 0)),
                  pl.BlockSpec((SUBLANES, d), lambda i, j: (jnp.minimum((i + 1) * blocks, last), 0)),
                  _vec_spec(nw, d, _zero), _vec_spec(sh, d, _zero), _vec_spec(sc, d, _zero),
                  wspec(0), wspec(nb), wspec(2 * nb),
                  pl.BlockSpec((3, tc), lambda i, j: (0, j))],
        out_specs=pl.BlockSpec((tm, tc), lambda i, j: (i, j)),
        out_shape=jax.ShapeDtypeStruct((m, c), MXU_DTYPE),
        scratch_shapes=[pltpu.VMEM((tm, d), MXU_DTYPE), pltpu.VMEM((2 * SUBLANES, d), F32)],
        compiler_params=_params("parallel", "arbitrary"),
        name=name,
    )(x, x, x, nw[0], sh[0], sc[0], w, w, w, conv_w)


def _ssd_kernel(xbc_ref, dt_ref, dtt_ref, dtb_ref, dtbt_ref, alog_ref, alogt_ref, h0_ref,
                y_ref, hout_ref, h_ref):
    d, c, n_c = pl.program_id(0), pl.program_id(1), pl.num_programs(1)
    q = SSD_CHUNK

    @pl.when(c == 0)
    def _():
        h_ref[...] = h0_ref[...]

    dt = _softplus(dt_ref[...] + dtb_ref[...])
    dtt = _softplus(dtt_ref[...] + dtbt_ref[...])
    a = dt * -jnp.exp(alog_ref[...])
    at = dtt * -jnp.exp(alogt_ref[...])
    row = lax.broadcasted_iota(jnp.int32, (q, q), 0)
    col = lax.broadcasted_iota(jnp.int32, (q, q), 1)
    sign = jnp.where(d == 0, 1, -1)
    mask = (row - col) * sign >= 0
    tri = mask.astype(MXU_DTYPE)
    tri_t = ((col - row) * sign >= 0).astype(MXU_DTYPE)
    cs = _dot_select(tri, a, split_lhs=False)
    cst = _dot_select(at, tri_t, split_lhs=True)
    tot = jnp.broadcast_to(jnp.sum(a, axis=0, keepdims=True), (SUBLANES, SSD_HEADS))
    expand = (lax.broadcasted_iota(jnp.int32, (SSD_HEADS, SSD_D_INNER), 1) // SSD_HEAD_DIM
              == lax.broadcasted_iota(jnp.int32, (SSD_HEADS, SSD_D_INNER), 0)).astype(MXU_DTYPE)
    dt_e = _dot_select(dt, expand, split_lhs=True)
    cs_e = _dot_select(cs, expand, split_lhs=True)
    tot_e = _dot_select(tot, expand, split_lhs=True)[0:1]

    xs = xbc_ref[:, 0:SSD_D_INNER]
    xdt = xs * dt_e
    xdt_m = xdt.astype(MXU_DTYPE)
    xw_m = (xdt * jnp.exp(tot_e - cs_e)).astype(MXU_DTYPE)
    e_e = jnp.exp(cs_e)
    dec_e = jnp.exp(tot_e)
    first = lax.broadcasted_iota(jnp.int32, (q, LANES), 1) < SSD_HEAD_DIM

    ppg = SSD_PAIRS // SSD_GROUPS
    for g in range(SSD_GROUPS):
        bg = xbc_ref[:, SSD_D_INNER + g * SSD_STATE:SSD_D_INNER + (g + 1) * SSD_STATE]
        cg = xbc_ref[:, SSD_D_INNER + SSD_BC + g * SSD_STATE:SSD_D_INNER + SSD_BC + (g + 1) * SSD_STATE]
        bg_m, cg_m = bg.astype(MXU_DTYPE), cg.astype(MXU_DTYPE)
        bgt_m = bg.T.astype(MXU_DTYPE)
        scores = _dot_nt(cg_m, bg_m)
        for pp in range(ppg):
            p = g * ppg + pp
            sl = slice(p * LANES, (p + 1) * LANES)
            ys = []
            for hh in range(2):
                h = 2 * p + hh
                diff = cs[:, h:h + 1] - cst[h:h + 1, :]
                decay = jnp.exp(jnp.where(mask, diff, -jnp.inf))
                ys.append(_dot((scores * decay).astype(MXU_DTYPE), xdt_m[:, sl]))
            hp = h_ref[p]
            y_off = _dot(cg_m, hp.astype(MXU_DTYPE)) * e_e[:, sl]
            y_ref[:, sl] = jnp.where(first, ys[0], ys[1]) + y_off
            h_ref[p] = dec_e[:, sl] * hp + _dot(bgt_m, xw_m[:, sl])

    @pl.when(c == n_c - 1)
    def _():
        hout_ref[...] = h_ref[...]


def _ssd(xbc, dt_raw, dt_bias, a_log, h0):
    n_rows = xbc.shape[0]
    q = SSD_CHUNK
    n_c = n_rows // q
    dt = dt_raw.reshape(n_rows, 2, SSD_HEADS).transpose(1, 0, 2)
    dtt = dt.transpose(0, 2, 1)

    def chunk(d, c):
        return jnp.where(d == 0, c, n_c - 1 - c)

    small = lambda shape: pl.BlockSpec((None,) + shape, lambda d, c: (d, 0, 0))
    return pl.pallas_call(
        _ssd_kernel,
        grid=(2, n_c),
        in_specs=[pl.BlockSpec((q, SSD_XBC), lambda d, c: (chunk(d, c), 0)),
                  pl.BlockSpec((None, q, SSD_HEADS), lambda d, c: (d, chunk(d, c), 0)),
                  pl.BlockSpec((None, SSD_HEADS, q), lambda d, c: (d, 0, chunk(d, c))),
                  small((1, SSD_HEADS)), small((SSD_HEADS, 1)), small((1, SSD_HEADS)), small((SSD_HEADS, 1)),
                  pl.BlockSpec((None, SSD_PAIRS, SSD_STATE, LANES), lambda d, c: (d, 0, 0, 0))],
        out_specs=[pl.BlockSpec((None, q, SSD_D_INNER), lambda d, c: (d, chunk(d, c), 0)),
                   pl.BlockSpec((None, SSD_PAIRS, SSD_STATE, LANES), lambda d, c: (d, 0, 0, 0))],
        out_shape=[jax.ShapeDtypeStruct((2, n_rows, SSD_D_INNER), F32),
                   jax.ShapeDtypeStruct((2, SSD_PAIRS, SSD_STATE, LANES), F32)],
        scratch_shapes=[pltpu.VMEM((SSD_PAIRS, SSD_STATE, LANES), F32)],
        compiler_params=_params("arbitrary", "arbitrary"),
        name="ssd_scan",
    )(xbc, dt, dtt, dt_bias[:, None, :], dt_bias[:, :, None], a_log[:, None, :], a_log[:, :, None], h0)


def _ssd_out_kernel(y_ref, xbc_ref, z_ref, dsk_ref, nw_ref, o_ref):
    y = y_ref[0] + y_ref[1] + dsk_ref[...] * xbc_ref[...]
    g = y * _silu(z_ref[...])
    ms = jnp.mean(g * g, axis=-1, keepdims=True)
    o_ref[...] = (g * lax.rsqrt(ms + EPS) * nw_ref[...]).astype(o_ref.dtype)


def _ssd_out(y2, xbc, zx, d_skip, norm_w):
    n_rows = xbc.shape[0]
    tl = _pick(n_rows, (512, 256, 128))
    w = SSD_D_INNER
    return pl.pallas_call(
        _ssd_out_kernel,
        grid=(n_rows // tl,),
        in_specs=[pl.BlockSpec((2, tl, w), lambda i: (0, i, 0)),
                  pl.BlockSpec((tl, w), lambda i: (i, 0)),
                  pl.BlockSpec((tl, w), lambda i: (i, 0)),
                  pl.BlockSpec((1, w), lambda i: (0, 0)),
                  pl.BlockSpec((1, w), lambda i: (0, 0))],
        out_specs=pl.BlockSpec((tl, w), lambda i: (i, 0)),
        out_shape=jax.ShapeDtypeStruct((n_rows, w), MXU_DTYPE),
        compiler_params=_params("parallel"),
        name="ssd_out",
    )(y2, xbc, zx, jnp.repeat(d_skip, SSD_HEAD_DIM)[None, :], norm_w[None, :])


def _head_mean_sq(x):
    blk = (lax.broadcasted_iota(jnp.int32, (LANES, LANES), 0) // NA_HEAD_DIM
           == lax.broadcasted_iota(jnp.int32, (LANES, LANES), 1) // NA_HEAD_DIM).astype(MXU_DTYPE)
    x2 = x * x
    hi = x2.astype(MXU_DTYPE)
    lo = (x2 - hi.astype(F32)).astype(MXU_DTYPE)
    return (_dot(hi, blk) + _dot(lo, blk)) * (1.0 / NA_HEAD_DIM)


def _even_proj_kernel(x_ref, nw_ref, sh_ref, sc_ref, w_ref, qw_ref, kw_ref, zx_ref, qkv_ref, dt_ref, a_ref,
                      *, zx_tiles, chunks):
    j = pl.program_id(1)

    @pl.when(j == 0)
    def _():
        a_ref[...] = _norm_mod(x_ref[...], nw_ref[...], sh_ref[...], sc_ref[...]).astype(a_ref.dtype)

    acc = _dot_nt(a_ref[...], w_ref[...].astype(a_ref.dtype))

    @pl.when(j < zx_tiles)
    def _():
        zx_ref[...] = acc

    @pl.when(j >= zx_tiles)
    def _():
        n_head_chunks = NA_D // LANES
        for c in range(chunks):
            chunk = (j - zx_tiles) * chunks + c
            x = acc[:, c * LANES:(c + 1) * LANES]
            w = jnp.where(chunk < n_head_chunks, qw_ref[...] * NA_HEAD_DIM ** -0.5, kw_ref[...])
            normed = x * lax.rsqrt(_head_mean_sq(x) + EPS) * w
            qkv_ref[:, c * LANES:(c + 1) * LANES] = jnp.where(chunk < 2 * n_head_chunks, normed, x).astype(qkv_ref.dtype)

    @pl.when(j == pl.num_programs(1) - 1)
    def _():
        dt_ref[...] = acc[:, (chunks - 1) * LANES:]


def _even_proj(x, nw, sh, sc, w_packed, layer, q_norm_w, k_norm_w, name):
    m, d = x.shape
    tm = _pick(m, (1024, 512, 256, 128))
    tn = 5 * LANES
    zx_tiles = S_ZX // tn
    wspec = pl.BlockSpec((1, LANES), lambda i, j: (0, 0))
    return pl.pallas_call(
        functools.partial(_even_proj_kernel, zx_tiles=zx_tiles, chunks=tn // LANES),
        grid=(m // tm, (S_ZX + QKVD) // tn),
        in_specs=[pl.BlockSpec((tm, d), lambda i, j: (i, 0)),
                  _vec_spec(nw, d, _zero), _vec_spec(sh, d, _zero), _vec_spec(sc, d, _zero),
                  _mat_spec(layer, (tn, d), lambda i, j: (j, 0)), wspec, wspec],
        out_specs=[pl.BlockSpec((tm, tn), lambda i, j: (i, jnp.minimum(j, zx_tiles - 1))),
                   pl.BlockSpec((tm, tn), lambda i, j: (i, jnp.maximum(j - zx_tiles, 0))),
                   pl.BlockSpec((tm, LANES), lambda i, j: (i, 0))],
        out_shape=[jax.ShapeDtypeStruct((m, S_ZX), F32), jax.ShapeDtypeStruct((m, QKVD), MXU_DTYPE),
                   jax.ShapeDtypeStruct((m, LANES), F32)],
        scratch_shapes=[pltpu.VMEM((tm, d), MXU_DTYPE)],
        compiler_params=_params("parallel", "arbitrary"),
        name=name,
    )(x, nw[0], sh[0], sc[0], w_packed, jnp.tile(q_norm_w, 2)[None, :], jnp.tile(k_norm_w, 2)[None, :])


def _attend(q2, parts):
    first = lax.broadcasted_iota(jnp.int32, q2.shape, 1) < NA_HEAD_DIM
    outs = []
    for head_mask in (first, jnp.logical_not(first)):
        qa = jnp.where(head_mask, q2, jnp.zeros_like(q2))
        scores = []
        for k, _, bias in parts:
            s = _dot_nt(qa, k)
            scores.append(s if bias is None else s + bias)
        m = functools.reduce(jnp.maximum, [jnp.max(s, axis=-1, keepdims=True) for s in scores])
        probs = [jnp.exp(s - m) for s in scores]
        denom = functools.reduce(jnp.add, [jnp.sum(p, axis=-1, keepdims=True) for p in probs])
        acc = functools.reduce(jnp.add, [_dot(p.astype(MXU_DTYPE), v) for p, (_, v, _) in zip(probs, parts)])
        outs.append(acc / denom)
    return jnp.where(first, outs[0], outs[1])


def _natten_kernel(q_ref, k_ref, v_ref, kc_ref, vc_ref, tbl_ref, o_ref, *, rb, rows):
    i = pl.program_id(1)
    kc, vc = kc_ref[...], vc_ref[...]
    n_win = NA_WIN_ROWS * GRID_W
    first = lax.broadcasted_iota(jnp.int32, (GRID_W, LANES), 1) < NA_HEAD_DIM
    for t in range(rb):
        r = i * rb + t
        start = jnp.clip(r - NA_WIN_ROWS // 2, 0, rows - NA_WIN_ROWS)
        dr0 = start - r + (NA_WIN_ROWS - 1)
        ks = pl.ds(pl.multiple_of(start * GRID_W, GRID_W), n_win)
        q2 = q_ref[t * GRID_W:(t + 1) * GRID_W, :]
        kw, vw = k_ref[ks, :], v_ref[ks, :]
        zero = jnp.zeros_like(q2)
        qs = jnp.concatenate([jnp.where(first, q2, zero), jnp.where(first, zero, q2)], axis=0)
        bias = jnp.concatenate([tbl_ref[dr0 + 2 * jj] for jj in range(NA_WIN_ROWS // 2)], axis=1)
        s_loc = _dot_nt(qs, kw) + bias
        s_ctx = _dot_nt(qs, kc)
        m = jnp.maximum(jnp.max(s_loc, axis=-1, keepdims=True), jnp.max(s_ctx, axis=-1, keepdims=True))
        p_loc, p_ctx = jnp.exp(s_loc - m), jnp.exp(s_ctx - m)
        denom = jnp.sum(p_loc, axis=-1, keepdims=True) + jnp.sum(p_ctx, axis=-1, keepdims=True)
        o = (_dot(p_loc.astype(MXU_DTYPE), vw) + _dot(p_ctx.astype(MXU_DTYPE), vc)) / denom
        o_ref[t * GRID_W:(t + 1) * GRID_W, :] = jnp.where(first, o[:GRID_W], o[GRID_W:]).astype(o_ref.dtype)


def _bias_table(rpb):
    col = jnp.arange(GRID_W)
    c0 = jnp.clip(col - NA_WIN_COLS // 2, 0, GRID_W - NA_WIN_COLS)
    col_in = (col[None, :] >= c0[:, None]) & (col[None, :] < c0[:, None] + NA_WIN_COLS)
    dc = jnp.clip(col[None, :] - col[:, None], 1 - NA_WIN_COLS, NA_WIN_COLS - 1) + (NA_WIN_COLS - 1)
    t = jnp.where(col_in, rpb.astype(F32)[:, :, dc], NEG)
    t = jnp.concatenate([t[:, :-1], t[:, 1:]], axis=-1)
    n_dr = 2 * NA_WIN_ROWS - 2
    t = t.reshape(NA_HEADS // 2, 2, n_dr, GRID_W, 2 * GRID_W).transpose(0, 2, 1, 3, 4)
    return t.reshape(NA_HEADS // 2, n_dr, 2 * GRID_W, 2 * GRID_W)


def _natten(qkv, qkv_ctx, rpb):
    n_rows = qkv.shape[0]
    rows = n_rows // GRID_W
    n_ctx = qkv_ctx.shape[0]
    rb = _pick(rows, (8, 4, 2, 1))
    nb = NA_D // LANES
    seq = lambda n, off: pl.BlockSpec((n, LANES), lambda p, i: (0, p + off))
    tile = pl.BlockSpec((rb * GRID_W, LANES), lambda p, i: (i, p))
    return pl.pallas_call(
        functools.partial(_natten_kernel, rb=rb, rows=rows),
        grid=(NA_HEADS // 2, rows // rb),
        in_specs=[tile, seq(n_rows, nb), seq(n_rows, 2 * nb), seq(n_ctx, nb), seq(n_ctx, 2 * nb),
                  pl.BlockSpec((None, 2 * NA_WIN_ROWS - 2, 2 * GRID_W, 2 * GRID_W), lambda p, i: (p, 0, 0, 0))],
        out_specs=tile,
        out_shape=jax.ShapeDtypeStruct((n_rows, NA_D), MXU_DTYPE),
        compiler_params=_params("parallel", "arbitrary"),
        name="natten",
    )(qkv, qkv, qkv, qkv_ctx, qkv_ctx, _bias_table(rpb))


def _ctx_attn_kernel(q_ref, k_ref, v_ref, o_ref):
    o_ref[...] = _attend(q_ref[...], [(k_ref[...], v_ref[...], None)]).astype(o_ref.dtype)


def _ctx_attn(qkv_ctx):
    n_ctx = qkv_ctx.shape[0]
    nb = NA_D // LANES
    spec = lambda off: pl.BlockSpec((n_ctx, LANES), lambda p: (0, p + off))
    return pl.pallas_call(
        _ctx_attn_kernel,
        grid=(NA_HEADS // 2,),
        in_specs=[spec(0), spec(nb), spec(2 * nb)],
        out_specs=spec(0),
        out_shape=jax.ShapeDtypeStruct((n_ctx, NA_D), MXU_DTYPE),
        compiler_params=_params("parallel"),
        name="ctx_attn",
    )(qkv_ctx, qkv_ctx, qkv_ctx)


def _load_slabs(ref, n, s):
    return jnp.concatenate([ref[pl.ds(c, n, stride=s), :] for c in range(s)], axis=1)


def _store_slabs(ref, val):
    n = val.shape[0]
    s = val.shape[1] // LANES
    for c in range(s):
        ref[pl.ds(c, n, stride=s), :] = val[:, c * LANES:(c + 1) * LANES].astype(ref.dtype)

def _router_kernel(*refs, n_lat_tiles, with_ctx):
    if with_ctx:
        x_ref, cx_ref, nw_ref, sh_ref, sc_ref, shc_ref, scc_ref, wr_ref, m_ref, e_ref, g_ref = refs
        is_ctx = pl.program_id(0) >= n_lat_tiles
        x = jnp.where(is_ctx, cx_ref[...], x_ref[...])
        sh = jnp.where(is_ctx, shc_ref[...], sh_ref[...])
        sc = jnp.where(is_ctx, scc_ref[...], sc_ref[...])
    else:
        x_ref, nw_ref, sh_ref, sc_ref, wr_ref, m_ref, e_ref, g_ref = refs
        x, sh, sc = x_ref[...], sh_ref[...], sc_ref[...]
    m = _norm_mod(x, nw_ref[...], sh, sc)
    _store_slabs(m_ref, m)
    logits = _dot_split2(m, wr_ref[...])
    lane = lax.broadcasted_iota(jnp.int32, logits.shape, 1)
    big = jnp.int32(LANES)

    def top(vals):
        v = jnp.max(vals, axis=-1, keepdims=True)
        idx = jnp.min(jnp.where(vals == v, lane, big), axis=-1, keepdims=True)
        return v, idx

    gl = jnp.where(lane < N_GROUPS, logits, -jnp.inf)
    g_max, grp = top(gl)
    p_grp = 1.0 / jnp.sum(jnp.exp(gl - g_max), axis=-1, keepdims=True)
    e_lane = lane - N_GROUPS
    in_grp = (e_lane >= grp * EXPERTS_PER_GROUP) & (e_lane < (grp + 1) * EXPERTS_PER_GROUP)
    el = jnp.where(in_grp, logits, -jnp.inf)
    v1, i1 = top(el)
    v2, i2 = top(jnp.where(lane == i1, -jnp.inf, el))
    t = jnp.exp(v2 - v1)
    g1 = p_grp / (1.0 + t)
    g2 = p_grp * t / (1.0 + t)
    e_ref[...] = jnp.where(lane == 0, i1 - N_GROUPS, jnp.where(lane == 1, i2 - N_GROUPS, 0))
    g_ref[...] = jnp.where(lane == 0, g1, jnp.where(lane == 1, g2, 0.0))


def _router(lat, cx, nw, mod_lat, mod_ctx, w_route):
    n_lat, d = lat.shape
    with_ctx = cx is not None
    n_ctx = cx.shape[0] if with_ctx else 0
    tm = _pick(n_lat, (256, 128)) if with_ctx else _pick(n_lat, (512, 256, 128))
    assert n_ctx % tm == 0
    n_lat_tiles = n_lat // tm
    n_tok = n_lat + n_ctx
    lat_tile = pl.BlockSpec((tm, d), lambda i: (jnp.minimum(i, n_lat_tiles - 1), 0))
    vec = lambda v: _vec_spec(v, d, _zero)
    wide = pl.BlockSpec((tm, LANES), lambda i: (i, 0))
    w_spec = pl.BlockSpec((d, LANES), lambda i: (0, 0))
    if with_ctx:
        ctx_tile = pl.BlockSpec((tm, d), lambda i: (jnp.maximum(i - n_lat_tiles, 0), 0))
        in_specs = [lat_tile, ctx_tile, vec(nw), vec(mod_lat[3]), vec(mod_lat[4]), vec(mod_ctx[3]), vec(mod_ctx[4]),
                    w_spec]
        args = (lat, cx, nw[0], mod_lat[3][0], mod_lat[4][0], mod_ctx[3][0], mod_ctx[4][0], w_route)
    else:
        in_specs = [lat_tile, vec(nw), vec(mod_lat[3]), vec(mod_lat[4]), w_spec]
        args = (lat, nw[0], mod_lat[3][0], mod_lat[4][0], w_route)
    m, e, g = pl.pallas_call(
        functools.partial(_router_kernel, n_lat_tiles=n_lat_tiles, with_ctx=with_ctx),
        grid=(n_tok // tm,),
        in_specs=in_specs,
        out_specs=[pl.BlockSpec((tm * (d // LANES), LANES), lambda i: (i, 0)), wide, wide],
        out_shape=[jax.ShapeDtypeStruct((n_tok * (d // LANES), LANES), F32),
                   jax.ShapeDtypeStruct((n_tok, LANES), jnp.int32),
                   jax.ShapeDtypeStruct((n_tok, LANES), F32)],
        compiler_params=_params("parallel"),
        name="router",
    )(*args)
    return m, e[:, :TOP_K], g


def _moe_kernel(first_ref, nblk_ref, tok_ref, nact_ref, m_hbm, wg_hbm, wu_hbm, wd_hbm, yb_hbm,
                xbuf, ybuf, gsem, osem, wgf_ref, wuf_ref, wdf_ref, wsem, wgb_ref, wub_ref, wdb_ref, *, layer):
    e, n_e = pl.program_id(0), pl.num_programs(0)
    n_act = nact_ref[0]
    s = wgf_ref.shape[1] // LANES
    blk_rows = MOE_BLOCK * s
    n_blocks = yb_hbm.shape[0] // blk_rows

    def weight_copies(expert, ws):
        return [pltpu.make_async_copy(src.at[layer, expert], dst.at[ws], wsem.at[k, ws])
                for k, (src, dst) in enumerate(((wg_hbm, wgf_ref), (wu_hbm, wuf_ref), (wd_hbm, wdf_ref)))]

    @pl.when(e == 0)
    def _():
        for ahead in range(MOE_WEIGHT_AHEAD):
            for copy in weight_copies(ahead, ahead):
                copy.start()

    ws = e % MOE_WEIGHT_SLOTS
    for copy in weight_copies(e, ws):
        copy.wait()

    @pl.when(e + MOE_WEIGHT_AHEAD < n_e)
    def _():
        for copy in weight_copies(e + MOE_WEIGHT_AHEAD, (e + MOE_WEIGHT_AHEAD) % MOE_WEIGHT_SLOTS):
            copy.start()

    def gather_start(blk, slot):
        for r in range(MOE_BLOCK):
            tok = tok_ref[blk * MOE_BLOCK + r]
            pltpu.make_async_copy(m_hbm.at[pl.ds(pl.multiple_of(tok * s, s), s), :],
                                  xbuf.at[slot, pl.ds(r * s, s), :],
                                  gsem.at[slot]).start(priority=r % ROW_DMA_THREADS)

    def gather_wait(slot):
        pltpu.make_async_copy(m_hbm.at[pl.ds(0, blk_rows), :], xbuf.at[slot], gsem.at[slot]).wait()

    def out_copy(blk, slot):
        rows = pl.ds(pl.multiple_of(blk * blk_rows, blk_rows), blk_rows)
        return pltpu.make_async_copy(ybuf.at[slot], yb_hbm.at[rows, :], osem.at[slot])

    @pl.when(e == 0)
    def _():
        for ahead in range(MOE_AHEAD):
            gather_start(jnp.clip(n_act - 1, 0, ahead), ahead)

    @pl.when(nblk_ref[e] > 0)
    def _():
        wgb_ref[...] = wgf_ref[ws].astype(wgb_ref.dtype)
        wub_ref[...] = wuf_ref[ws].astype(wub_ref.dtype)
        wdb_ref[...] = wdf_ref[ws].astype(wdb_ref.dtype)

    def block(j, carry):
        b = first_ref[e] + j
        slot = b % MOE_SLOTS
        gather_wait(slot)

        @pl.when(b >= MOE_SLOTS)
        def _():
            out_copy(b - MOE_SLOTS, slot).wait()

        gather_start(jnp.clip(n_act - 1, 0, b + MOE_AHEAD), (b + MOE_AHEAD) % MOE_SLOTS)
        x = _load_slabs(xbuf.at[slot], MOE_BLOCK, s).astype(MXU_DTYPE)
        h = _silu(_dot(x, wgb_ref[...])) * _dot(x, wub_ref[...])
        _store_slabs(ybuf.at[slot], _dot(h.astype(MXU_DTYPE), wdb_ref[...]))
        out_copy(b, slot).start()
        return carry

    lax.fori_loop(0, nblk_ref[e], block, 0)

    @pl.when(e == n_e - 1)
    def _():
        for ahead in range(MOE_AHEAD):
            gather_wait((n_act + ahead) % MOE_SLOTS)
        for back in range(1, MOE_SLOTS + 1):
            @pl.when(n_act >= back)
            def _():
                out_copy(n_act - back, (n_act - back) % MOE_SLOTS).wait()

        ybuf[0] = jnp.zeros(ybuf.shape[1:], ybuf.dtype)

        def fill(b, carry):
            out_copy(b, 0).start()
            out_copy(b, 0).wait()
            return carry

        lax.fori_loop(n_act, n_blocks, fill, 0)


def _moe_ffn(m_tok, slot_tok, first_blk, n_blk, n_act, w_gate, w_up, w_down, layer):
    n_rows = slot_tok.shape[0]
    d, f = w_gate.shape[-2:]
    blk_rows = MOE_BLOCK * (d // LANES)
    hbm = pl.BlockSpec(memory_space=pl.ANY)
    grid_spec = pltpu.PrefetchScalarGridSpec(
        num_scalar_prefetch=4,
        grid=(N_EXPERTS,),
        in_specs=[hbm, hbm, hbm, hbm],
        out_specs=hbm,
        scratch_shapes=[pltpu.VMEM((MOE_SLOTS, blk_rows, LANES), F32), pltpu.VMEM((MOE_SLOTS, blk_rows, LANES), F32),
                        pltpu.SemaphoreType.DMA((MOE_SLOTS,)), pltpu.SemaphoreType.DMA((MOE_SLOTS,)),
                        pltpu.VMEM((MOE_WEIGHT_SLOTS, d, f), F32), pltpu.VMEM((MOE_WEIGHT_SLOTS, d, f), F32),
                        pltpu.VMEM((MOE_WEIGHT_SLOTS, f, d), F32), pltpu.SemaphoreType.DMA((3, MOE_WEIGHT_SLOTS)),
                        pltpu.VMEM((d, f), MXU_DTYPE), pltpu.VMEM((d, f), MXU_DTYPE), pltpu.VMEM((f, d), MXU_DTYPE)],
    )
    return pl.pallas_call(
        functools.partial(_moe_kernel, layer=layer),
        grid_spec=grid_spec,
        out_shape=jax.ShapeDtypeStruct((n_rows * (d // LANES), LANES), F32),
        compiler_params=_params("arbitrary"),
        name="moe_ffn",
    )(first_blk, n_blk, slot_tok, n_act, m_tok, w_gate, w_up, w_down)


def _combine_kernel(pos_ref, x_ref, g_ref, gt_ref, yb_hbm, o_ref, buf, sem, *, tm):
    i, n_i = pl.program_id(0), pl.num_programs(0)
    s = x_ref.shape[1] // LANES

    def gather_start(tile, slot):
        for j in range(tm):
            for k in range(TOP_K):
                p = pos_ref[(tile * tm + j) * TOP_K + k]
                pltpu.make_async_copy(yb_hbm.at[pl.ds(pl.multiple_of(p * s, s), s), :],
                                      buf.at[slot, k, pl.ds(j * s, s), :],
                                      sem.at[slot]).start(priority=(j * TOP_K + k) % ROW_DMA_THREADS)

    def gather_wait(slot):
        for k in range(TOP_K):
            pltpu.make_async_copy(yb_hbm.at[pl.ds(0, tm * s), :], buf.at[slot, k], sem.at[slot]).wait()

    @pl.when(i == 0)
    def _():
        for ahead in range(MOE_AHEAD):
            gather_start(jnp.minimum(ahead, n_i - 1), ahead)

    slot = i % MOE_SLOTS
    gather_wait(slot)
    gather_start(jnp.minimum(i + MOE_AHEAD, n_i - 1), (i + MOE_AHEAD) % MOE_SLOTS)
    gt = gt_ref[...]
    y0 = _load_slabs(buf.at[slot, 0], tm, s)
    y1 = _load_slabs(buf.at[slot, 1], tm, s)
    o_ref[...] = x_ref[...] + g_ref[...] * (gt[:, 0:1] * y0 + gt[:, 1:2] * y1)

    @pl.when(i == n_i - 1)
    def _():
        for ahead in range(1, MOE_AHEAD + 1):
            gather_wait((i + ahead) % MOE_SLOTS)


def _combine(x, g, gates, yb, pos):
    m, d = x.shape
    tm = MOE_BLOCK
    grid_spec = pltpu.PrefetchScalarGridSpec(
        num_scalar_prefetch=1,
        grid=(m // tm,),
        in_specs=[pl.BlockSpec((tm, d), lambda i, pos: (i, 0)),
                  _vec_spec(g, d, _zero),
                  pl.BlockSpec((tm, LANES), lambda i, pos: (i, 0)),
                  pl.BlockSpec(memory_space=pl.ANY)],
        out_specs=pl.BlockSpec((tm, d), lambda i, pos: (i, 0)),
        scratch_shapes=[pltpu.VMEM((MOE_SLOTS, TOP_K, tm * (d // LANES), LANES), F32),
                        pltpu.SemaphoreType.DMA((MOE_SLOTS,))],
    )
    return pl.pallas_call(
        functools.partial(_combine_kernel, tm=tm),
        grid_spec=grid_spec,
        out_shape=jax.ShapeDtypeStruct((m, d), F32),
        compiler_params=_params("arbitrary"),
        name="moe_combine",
    )(pos, x, g[0], gates, yb)


def _dispatch_kernel(e_ref, dest_ref, first_ref, nblk_ref, cnt_ref, base_ref):
    phase, i = pl.program_id(0), pl.program_id(1)
    tile = e_ref.shape[1]
    onehot = lax.broadcasted_iota(jnp.int32, (N_EXPERTS, tile), 0) == e_ref[...]

    @pl.when(jnp.logical_and(phase == 0, i == 0))
    def _():
        cnt_ref[...] = jnp.zeros_like(cnt_ref)

    @pl.when(phase == 0)
    def _():
        cnt_ref[...] += jnp.sum(onehot.astype(F32), axis=1, keepdims=True)

    @pl.when(jnp.logical_and(phase == 1, i == 0))
    def _():
        n_blk = jnp.right_shift(cnt_ref[...].astype(jnp.int32) + (MOE_BLOCK - 1),
                                MOE_BLOCK.bit_length() - 1)
        tri = (lax.broadcasted_iota(jnp.int32, (N_EXPERTS, N_EXPERTS), 0)
               >= lax.broadcasted_iota(jnp.int32, (N_EXPERTS, N_EXPERTS), 1)).astype(MXU_DTYPE)
        blk_end = _dot_select(tri, n_blk.astype(F32), split_lhs=False)
        first = blk_end.astype(jnp.int32) - n_blk
        nblk_ref[...] = n_blk
        first_ref[...] = first
        base_ref[...] = (first * MOE_BLOCK).astype(F32)

    @pl.when(phase == 1)
    def _():
        tri = (lax.broadcasted_iota(jnp.int32, (tile, tile), 0)
               <= lax.broadcasted_iota(jnp.int32, (tile, tile), 1)).astype(MXU_DTYPE)
        cum = _dot(onehot.astype(MXU_DTYPE), tri)
        slot = jnp.where(onehot, base_ref[:, 0:1] + cum - 1.0, 0.0)
        dest_ref[...] = jnp.sum(slot, axis=0, keepdims=True).astype(jnp.int32)
        base_ref[...] += cum[:, tile - 1:tile]


def _dispatch(expert):
    n_tok = expert.shape[0]
    n = n_tok * TOP_K
    n_blocks = -(-n // MOE_BLOCK) + N_EXPERTS
    n_rows = n_blocks * MOE_BLOCK
    tile = _pick(n, (512, 256, 128))
    per_expert = pl.BlockSpec((N_EXPERTS, LANES), lambda ph, i: (0, 0))
    dest, first, n_blk = pl.pallas_call(
        _dispatch_kernel,
        grid=(2, n // tile),
        in_specs=[pl.BlockSpec((1, tile), lambda ph, i: (0, i))],
        out_specs=[pl.BlockSpec((1, tile), lambda ph, i: (0, i * ph)), per_expert, per_expert],
        out_shape=[jax.ShapeDtypeStruct((1, n), jnp.int32),
                   jax.ShapeDtypeStruct((N_EXPERTS, LANES), jnp.int32),
                   jax.ShapeDtypeStruct((N_EXPERTS, LANES), jnp.int32)],
        scratch_shapes=[pltpu.VMEM((N_EXPERTS, LANES), F32), pltpu.VMEM((N_EXPERTS, LANES), F32)],
        compiler_params=_params("arbitrary", "arbitrary"),
        name="moe_dispatch",
    )(expert.reshape(1, n))
    dest, first_blk, n_blk = dest[0], first[:, 0], n_blk[:, 0]
    pair_tok = jnp.arange(n, dtype=jnp.int32) // TOP_K
    slot_tok = jnp.zeros((n_rows,), jnp.int32).at[dest].set(pair_tok)
    n_act = first_blk[-1:] + n_blk[-1:]
    return dest, slot_tok, first_blk, n_blk, n_act


def _even_mixer(lat, cx, nw, mod_lat, mod_ctx, e, w_in, conv_w, conv_b, dt_bias, a_log, d_skip, ssd_norm_w,
                q_norm_w, k_norm_w, rpb, w_out, ctx_out):
    def project(x, mod, tag):
        zx, qkv, dt = _even_proj(x, nw, mod[0], mod[1], w_in, e, q_norm_w[e], k_norm_w[e], "proj_even_" + tag)
        xbc = _xbc_conv(zx, conv_w[e], conv_b[e])
        return zx, xbc, dt[:, :2 * SSD_HEADS], qkv

    zx_c, xbc_c, dt_c, qkv_c = project(cx, mod_ctx, "ctx")
    zx_l, xbc_l, dt_l, qkv_l = project(lat, mod_lat, "lat")
    h0 = jnp.zeros((2, SSD_PAIRS, SSD_STATE, LANES), F32)
    y_c, h_ctx = _ssd(xbc_c, dt_c, dt_bias[e], a_log[e], h0)
    y_l, _ = _ssd(xbc_l, dt_l, dt_bias[e], a_log[e], h_ctx)
    yssd_l = _ssd_out(y_l, xbc_l, zx_l, d_skip[e], ssd_norm_w[e])
    yatt_l = _natten(qkv_l, qkv_c, rpb[e])
    lat = _matmul_res([yssd_l, yatt_l], w_out, e, lat, mod_lat[2], "out_even_lat")
    if ctx_out:
        yssd_c = _ssd_out(y_c, xbc_c, zx_c, d_skip[e], ssd_norm_w[e])
        yatt_c = _ctx_attn(qkv_c)
        cx = _matmul_res([yssd_c, yatt_c], w_out, e, cx, mod_ctx[2], "out_even_ctx")
    return lat, cx


def _odd_mixer(lat, cx, nw, mod_lat, mod_ctx, o, w_in, conv_w, w_out, ctx_out):
    def mix(x, mod, tag):
        u = _gated_conv_proj(x, nw, mod[0], mod[1], w_in, o, conv_w[o], "proj_odd_" + tag)
        return _matmul_res([u], w_out, o, x, mod[2], "out_odd_" + tag)

    lat = mix(lat, mod_lat, "lat")
    if ctx_out:
        cx = mix(cx, mod_ctx, "ctx")
    return lat, cx


def kernel(x, c, ctx, c_ctx, w_mod, b_mod, norm_mix_w, norm_ffn_w, ev_w_in, ev_conv_w, ev_conv_b, ev_dt_bias,
           ev_a_log, ev_d_skip, ev_ssd_norm_w, ev_q_norm_w, ev_k_norm_w, ev_rpb, ev_w_out, od_w_in, od_conv_w,
           od_w_out, moe_w_group, moe_w_router, moe_w_gate, moe_w_up, moe_w_down):
    bsz, n_lat, d = x.shape
    assert bsz == 1 and ctx.shape[0] == 1
    assert n_lat % GRID_W == 0 and n_lat // GRID_W >= NA_WIN_ROWS and n_lat % SSD_CHUNK == 0
    assert ctx.shape[1] % SSD_CHUNK == 0 and d % LANES == 0
    depth = w_mod.shape[0]
    lat, cx = x[0], ctx[0]
    mods = _modulation(jnp.stack([c[0], c_ctx], axis=1), w_mod, b_mod)
    mods = mods.reshape(depth * 2 * 6, d)
    ev_w_packed = _repack_w_in(ev_w_in)

    for l in range(depth):
        even = l % 2 == 0
        ctx_out = any(j % 2 == 0 for j in range(l + 1, depth))
        mod_lat = [_vec(mods, (l * 2 + 0) * 6 + i) for i in range(6)]
        mod_ctx = [_vec(mods, (l * 2 + 1) * 6 + i) for i in range(6)]
        nw = _vec(norm_mix_w, l)
        if even:
            lat, cx = _even_mixer(lat, cx, nw, mod_lat, mod_ctx, l // 2, ev_w_packed, ev_conv_w, ev_conv_b,
                                  ev_dt_bias, ev_a_log, ev_d_skip, ev_ssd_norm_w, ev_q_norm_w,
                                  ev_k_norm_w, ev_rpb, ev_w_out, ctx_out)
        else:
            lat, cx = _odd_mixer(lat, cx, nw, mod_lat, mod_ctx, l // 2, od_w_in, od_conv_w, od_w_out, ctx_out)

        nfw = _vec(norm_ffn_w, l)
        w_route = jnp.concatenate([moe_w_group[l], moe_w_router[l],
                                   jnp.zeros((d, LANES - N_GROUPS - N_EXPERTS), F32)], axis=1)
        m_tok, expert, gates = _router(lat, cx if ctx_out else None, nfw, mod_lat, mod_ctx, w_route)
        dest, slot_tok, first_blk, n_blk, n_act = _dispatch(expert)
        yb = _moe_ffn(m_tok, slot_tok, first_blk, n_blk, n_act, moe_w_gate, moe_w_up, moe_w_down, l)
        lat = _combine(lat, mod_lat[5], gates[:n_lat], yb, dest[:n_lat * TOP_K])
        if ctx_out:
            cx = _combine(cx, mod_ctx[5], gates[n_lat:], yb, dest[n_lat * TOP_K:])
    return lat[None]
```

```python
import functools

import jax
import jax.numpy as jnp
from jax import lax
from jax.experimental import pallas as pl
from jax.experimental.pallas import tpu as pltpu

F32 = jnp.float32
MXU_DTYPE = jnp.bfloat16
EPS = 1e-6
NEG = -1e30

LANES = 128
SUBLANES = 8
VMEM_LIMIT = 48 * 1024 * 1024
ROW_DMA_THREADS = 2

SSD_HEADS = 16
SSD_HEAD_DIM = 64
SSD_D_INNER = SSD_HEADS * SSD_HEAD_DIM
SSD_STATE = 128
SSD_GROUPS = 2
SSD_CHUNK = 128
SSD_BC = SSD_GROUPS * SSD_STATE
SSD_XBC = SSD_D_INNER + 2 * SSD_BC
SSD_PAIRS = SSD_HEADS // 2
NA_HEADS = 16
NA_HEAD_DIM = 64
NA_D = NA_HEADS * NA_HEAD_DIM
NA_WIN_ROWS = 8
NA_WIN_COLS = 16
GRID_W = 64
N_GROUPS = 8
EXPERTS_PER_GROUP = 8
N_EXPERTS = N_GROUPS * EXPERTS_PER_GROUP
TOP_K = 2
D_FF_EXPERT = 384
MOE_BLOCK = 128
NORM_CHUNK = 16
MOE_AHEAD = 2
MOE_SLOTS = MOE_AHEAD + 1
MOE_WEIGHT_AHEAD = 2
MOE_WEIGHT_SLOTS = MOE_WEIGHT_AHEAD + 1
S_ZX = SSD_D_INNER + SSD_XBC
S_DT = S_ZX + 2 * SSD_HEADS
QKVD = 3 * NA_D + LANES


def _pick(n, prefs):
    for p in prefs:
        if n % p == 0:
            return p
    return n


def _params(*sem):
    return pltpu.CompilerParams(dimension_semantics=sem, vmem_limit_bytes=VMEM_LIMIT)


def _vec(arr, idx):
    return arr.reshape(arr.shape[0], 1, arr.shape[-1]), idx


def _vec_spec(vec, width, col):
    idx = vec[1]
    return pl.BlockSpec((None, 1, width), lambda *g: (idx, 0, col(*g)))


def _mat_spec(layer, block, idx):
    return pl.BlockSpec((None,) + block, lambda *g: (layer,) + idx(*g))


def _zero(*g):
    return 0


def _silu(x):
    return x * (1.0 / (1.0 + jnp.exp(-x)))


def _softplus(x):
    return jnp.maximum(x, 0.0) + jnp.log1p(jnp.exp(-jnp.abs(x)))


def _dot(a, b, precision=None):
    return jnp.dot(a, b, preferred_element_type=F32, precision=precision)


def _dot_select(a, b, split_lhs):
    x = a if split_lhs else b
    hi = x.astype(MXU_DTYPE)
    rest = x - hi.astype(F32)
    mid = rest.astype(MXU_DTYPE)
    lo = (rest - mid.astype(F32)).astype(MXU_DTYPE)
    if split_lhs:
        return _dot(hi, b) + _dot(mid, b) + _dot(lo, b)
    return _dot(a, hi) + _dot(a, mid) + _dot(a, lo)


def _dot_split2(a, b):
    a_hi, b_hi = a.astype(MXU_DTYPE), b.astype(MXU_DTYPE)
    a_lo = (a - a_hi.astype(F32)).astype(MXU_DTYPE)
    b_lo = (b - b_hi.astype(F32)).astype(MXU_DTYPE)
    return _dot(a_hi, b_hi) + _dot(a_hi, b_lo) + _dot(a_lo, b_hi)


def _dot_nt(a, b):
    return lax.dot_general(a, b, (((1,), (1,)), ((), ())), preferred_element_type=F32)


def _norm_mod_rows(x_ref, a_ref, nw, sh, sc):
    def body(r, carry):
        rows = pl.ds(pl.multiple_of(r * NORM_CHUNK, NORM_CHUNK), NORM_CHUNK)
        a_ref[rows, :] = _norm_mod(x_ref[rows, :], nw, sh, sc).astype(a_ref.dtype)
        return carry

    lax.fori_loop(0, x_ref.shape[0] // NORM_CHUNK, body, 0, unroll=8)


def _norm_mod(x, nw, sh, sc):
    ms = jnp.mean(x * x, axis=-1, keepdims=True)
    n = x * lax.rsqrt(ms + EPS) * nw
    return n * (1.0 + sc) + sh


def _mod_kernel(ct_ref, w_ref, b_ref, o_ref):
    ct = ct_ref[...]
    s = _silu(ct)
    w = w_ref[...]
    r0 = jnp.sum(s[:, 0:1] * w, axis=0, keepdims=True)
    r1 = jnp.sum(s[:, 1:2] * w, axis=0, keepdims=True)
    o_ref[...] = jnp.concatenate([r0, r1], axis=0) + b_ref[...]


def _modulation(ct, w_mod, b_mod):
    depth, d, n = w_mod.shape
    tn = _pick(n, (512, 256, 128))
    return pl.pallas_call(
        _mod_kernel,
        grid=(depth, n // tn),
        in_specs=[pl.BlockSpec((d, 2), lambda l, j: (0, 0)),
                  pl.BlockSpec((None, d, tn), lambda l, j: (l, 0, j)),
                  pl.BlockSpec((None, 1, tn), lambda l, j: (l, 0, j))],
        out_specs=pl.BlockSpec((None, 2, tn), lambda l, j: (l, 0, j)),
        out_shape=jax.ShapeDtypeStruct((depth, 2, n), F32),
        compiler_params=_params("parallel", "parallel"),
        name="modulation",
    )(ct, w_mod, b_mod.reshape(depth, 1, n))


def _repack_w_in_kernel(w_ref, o_ref):
    o_ref[0:S_ZX, :] = w_ref[0:S_ZX, :].astype(o_ref.dtype)
    o_ref[S_ZX:S_ZX + 3 * NA_D, :] = w_ref[S_DT:S_DT + 3 * NA_D, :].astype(o_ref.dtype)
    o_ref[S_ZX + 3 * NA_D:S_ZX + 3 * NA_D + 2 * SSD_HEADS, :] = w_ref[S_ZX:S_DT, :].astype(o_ref.dtype)
    o_ref[S_ZX + 3 * NA_D + 2 * SSD_HEADS:, :] = jnp.zeros((LANES - 2 * SSD_HEADS, o_ref.shape[1]), o_ref.dtype)


def _repack_w_in(w_in):
    layers, d, n_in = w_in.shape
    tc = _pick(d, (512, 256, 128))
    n_out = S_ZX + QKVD
    return pl.pallas_call(
        _repack_w_in_kernel,
        grid=(layers, d // tc),
        in_specs=[pl.BlockSpec((None, n_in, tc), lambda l, i: (l, 0, i))],
        out_specs=pl.BlockSpec((None, n_out, tc), lambda l, i: (l, 0, i)),
        out_shape=jax.ShapeDtypeStruct((layers, n_out, d), MXU_DTYPE),
        compiler_params=_params("parallel", "parallel"),
        name="repack_w_in",
    )(jnp.swapaxes(w_in, 1, 2))


def _matmul_res_kernel(*refs, n_a):
    a_refs, w_refs = refs[:n_a], refs[n_a:2 * n_a]
    res_ref, g_ref, o_ref = refs[2 * n_a:]
    acc = None
    for a_ref, w_ref in zip(a_refs, w_refs):
        t = _dot(a_ref[...].astype(MXU_DTYPE), w_ref[...].astype(MXU_DTYPE))
        acc = t if acc is None else acc + t
    o_ref[...] = res_ref[...] + g_ref[...] * acc


def _matmul_res(a_list, w, layer, res, g, name):
    n_a = len(a_list)
    m, n = res.shape
    kp = w.shape[1] // n_a
    tm = _pick(m, (2048, 1024, 512, 256, 128))
    tn = _pick(n, (512, 256, 128))
    in_specs = [pl.BlockSpec((tm, kp), lambda i, j: (i, 0)) for _ in range(n_a)]
    in_specs += [_mat_spec(layer, (kp, tn), functools.partial(lambda i, j, p: (p, j), p=p)) for p in range(n_a)]
    in_specs += [pl.BlockSpec((tm, tn), lambda i, j: (i, j)), _vec_spec(g, tn, lambda i, j: j)]
    return pl.pallas_call(
        functools.partial(_matmul_res_kernel, n_a=n_a),
        grid=(m // tm, n // tn),
        in_specs=in_specs,
        out_specs=pl.BlockSpec((tm, tn), lambda i, j: (i, j)),
        out_shape=jax.ShapeDtypeStruct((m, n), F32),
        compiler_params=_params("parallel", "parallel"),
        name=name,
    )(*a_list, *([w] * n_a), res, g[0])


def _shifted(x, prev_ref, next_ref):
    i, n_i = pl.program_id(0), pl.num_programs(0)
    tl = x.shape[0]
    row = lax.broadcasted_iota(jnp.int32, x.shape, 0)
    prev_row = jnp.where(i == 0, 0.0, prev_ref[SUBLANES - 1:SUBLANES, :])
    next_row = jnp.where(i == n_i - 1, 0.0, next_ref[0:1, :])
    up = jnp.where(row == 0, prev_row, pltpu.roll(x, 1, 0))
    down = jnp.where(row == tl - 1, next_row, pltpu.roll(x, tl - 1, 0))
    return up, down


def _halo_specs(tl, tc, n_rows, col_off):
    nb = tl // SUBLANES
    last = n_rows // SUBLANES - 1
    return [pl.BlockSpec((tl, tc), lambda i, j: (i, j + col_off)),
            pl.BlockSpec((SUBLANES, tc), lambda i, j: (jnp.maximum(i * nb - 1, 0), j + col_off)),
            pl.BlockSpec((SUBLANES, tc), lambda i, j: (jnp.minimum((i + 1) * nb, last), j + col_off))]


def _xbc_conv_kernel(x_ref, prev_ref, next_ref, w_ref, b_ref, o_ref):
    x = x_ref[...]
    up, down = _shifted(x, prev_ref, next_ref)
    w = w_ref[...]
    o_ref[...] = _silu(w[0:1] * up + w[1:2] * x + w[2:3] * down + b_ref[...])


def _xbc_conv(zx, conv_w, conv_b):
    n_rows = zx.shape[0]
    tl = _pick(n_rows, (512, 256, 128))
    tc = 512
    off = SSD_D_INNER // tc
    return pl.pallas_call(
        _xbc_conv_kernel,
        grid=(n_rows // tl, SSD_XBC // tc),
        in_specs=_halo_specs(tl, tc, n_rows, off) + [pl.BlockSpec((3, tc), lambda i, j: (0, j)),
                                                     pl.BlockSpec((1, tc), lambda i, j: (0, j))],
        out_specs=pl.BlockSpec((tl, tc), lambda i, j: (i, j)),
        out_shape=jax.ShapeDtypeStruct((n_rows, SSD_XBC), F32),
        compiler_params=_params("parallel", "parallel"),
        name="xbc_conv",
    )(zx, zx, zx, conv_w, conv_b.reshape(1, -1))


def _gated_conv_proj_kernel(x_ref, xp_ref, xn_ref, nw_ref, sh_ref, sc_ref, wb_ref, wc_ref, wx_ref, cw_ref, o_ref,
                            a_ref, ah_ref):
    i, n_i = pl.program_id(0), pl.num_programs(0)

    @pl.when(pl.program_id(1) == 0)
    def _():
        nw, sh, sc = nw_ref[...], sh_ref[...], sc_ref[...]
        _norm_mod_rows(x_ref, a_ref, nw, sh, sc)
        ah_ref[0:SUBLANES, :] = _norm_mod(xp_ref[...], nw, sh, sc)
        ah_ref[SUBLANES:, :] = _norm_mod(xn_ref[...], nw, sh, sc)

    a, ah = a_ref[...], ah_ref[...].astype(a_ref.dtype)
    wc, wx = wc_ref[...].astype(a.dtype), wx_ref[...].astype(a.dtype)
    u = _dot(a, wc) * _dot(a, wx)
    uh = _dot(ah, wc) * _dot(ah, wx)
    tm = u.shape[0]
    row = lax.broadcasted_iota(jnp.int32, u.shape, 0)
    prev_row = jnp.where(i == 0, 0.0, uh[SUBLANES - 1:SUBLANES, :])
    next_row = jnp.where(i == n_i - 1, 0.0, uh[SUBLANES:SUBLANES + 1, :])
    up = jnp.where(row == 0, prev_row, pltpu.roll(u, 1, 0))
    down = jnp.where(row == tm - 1, next_row, pltpu.roll(u, tm - 1, 0))
    cw = cw_ref[...]
    conv = cw[0:1] * up + cw[1:2] * u + cw[2:3] * down
    o_ref[...] = (_dot(a, wb_ref[...].astype(a.dtype)) * conv).astype(o_ref.dtype)


def _gated_conv_proj(x, nw, sh, sc, w, layer, conv_w, name):
    m, d = x.shape
    c = w.shape[2] // 3
    tm = _pick(m, (1024, 512, 256, 128))
    tc = _pick(c, (256, 128))
    nb = c // tc
    blocks = tm // SUBLANES
    last = m // SUBLANES - 1
    wspec = lambda off: _mat_spec(layer, (d, tc), lambda i, j: (0, j + off))
    return pl.pallas_call(
        _gated_conv_proj_kernel,
        grid=(m // tm, nb),
        in_specs=[pl.BlockSpec((tm, d), lambda i, j: (i, 0)),
                  pl.BlockSpec((SUBLANES, d), lambda i, j: (jnp.maximum(i * blocks - 1, 0), 0)),
                  pl.BlockSpec((SUBLANES, d), lambda i, j: (jnp.minimum((i + 1) * blocks, last), 0)),
                  _vec_spec(nw, d, _zero), _vec_spec(sh, d, _zero), _vec_spec(sc, d, _zero),
                  wspec(0), wspec(nb), wspec(2 * nb),
                  pl.BlockSpec((3, tc), lambda i, j: (0, j))],
        out_specs=pl.BlockSpec((tm, tc), lambda i, j: (i, j)),
        out_shape=jax.ShapeDtypeStruct((m, c), MXU_DTYPE),
        scratch_shapes=[pltpu.VMEM((tm, d), MXU_DTYPE), pltpu.VMEM((2 * SUBLANES, d), F32)],
        compiler_params=_params("parallel", "arbitrary"),
        name=name,
    )(x, x, x, nw[0], sh[0], sc[0], w, w, w, conv_w)


def _ssd_kernel(xbc_ref, dt_ref, dtt_ref, dtb_ref, dtbt_ref, alog_ref, alogt_ref, h0_ref,
                y_ref, hout_ref, h_ref):
    d, c, n_c = pl.program_id(0), pl.program_id(1), pl.num_programs(1)
    q = SSD_CHUNK

    @pl.when(c == 0)
    def _():
        h_ref[...] = h0_ref[...]

    dt = _softplus(dt_ref[...] + dtb_ref[...])
    dtt = _softplus(dtt_ref[...] + dtbt_ref[...])
    a = dt * -jnp.exp(alog_ref[...])
    at = dtt * -jnp.exp(alogt_ref[...])
    row = lax.broadcasted_iota(jnp.int32, (q, q), 0)
    col = lax.broadcasted_iota(jnp.int32, (q, q), 1)
    sign = jnp.where(d == 0, 1, -1)
    mask = (row - col) * sign >= 0
    tri = mask.astype(MXU_DTYPE)
    tri_t = ((col - row) * sign >= 0).astype(MXU_DTYPE)
    cs = _dot_select(tri, a, split_lhs=False)
    cst = _dot_select(at, tri_t, split_lhs=True)
    tot = jnp.broadcast_to(jnp.sum(a, axis=0, keepdims=True), (SUBLANES, SSD_HEADS))
    expand = (lax.broadcasted_iota(jnp.int32, (SSD_HEADS, SSD_D_INNER), 1) // SSD_HEAD_DIM
              == lax.broadcasted_iota(jnp.int32, (SSD_HEADS, SSD_D_INNER), 0)).astype(MXU_DTYPE)
    dt_e = _dot_select(dt, expand, split_lhs=True)
    cs_e = _dot_select(cs, expand, split_lhs=True)
    tot_e = _dot_select(tot, expand, split_lhs=True)[0:1]

    xs = xbc_ref[:, 0:SSD_D_INNER]
    xdt = xs * dt_e
    xdt_m = xdt.astype(MXU_DTYPE)
    xw_m = (xdt * jnp.exp(tot_e - cs_e)).astype(MXU_DTYPE)
    e_e = jnp.exp(cs_e)
    dec_e = jnp.exp(tot_e)
    first = lax.broadcasted_iota(jnp.int32, (q, LANES), 1) < SSD_HEAD_DIM

    ppg = SSD_PAIRS // SSD_GROUPS
    for g in range(SSD_GROUPS):
        bg = xbc_ref[:, SSD_D_INNER + g * SSD_STATE:SSD_D_INNER + (g + 1) * SSD_STATE]
        cg = xbc_ref[:, SSD_D_INNER + SSD_BC + g * SSD_STATE:SSD_D_INNER + SSD_BC + (g + 1) * SSD_STATE]
        bg_m, cg_m = bg.astype(MXU_DTYPE), cg.astype(MXU_DTYPE)
        bgt_m = bg.T.astype(MXU_DTYPE)
        scores = _dot_nt(cg_m, bg_m)
        for pp in range(ppg):
            p = g * ppg + pp
            sl = slice(p * LANES, (p + 1) * LANES)
            ys = []
            for hh in range(2):
                h = 2 * p + hh
                diff = cs[:, h:h + 1] - cst[h:h + 1, :]
                decay = jnp.exp(jnp.where(mask, diff, -jnp.inf))
                ys.append(_dot((scores * decay).astype(MXU_DTYPE), xdt_m[:, sl]))
            hp = h_ref[p]
            y_off = _dot(cg_m, hp.astype(MXU_DTYPE)) * e_e[:, sl]
            y_ref[:, sl] = jnp.where(first, ys[0], ys[1]) + y_off
            h_ref[p] = dec_e[:, sl] * hp + _dot(bgt_m, xw_m[:, sl])

    @pl.when(c == n_c - 1)
    def _():
        hout_ref[...] = h_ref[...]


def _ssd(xbc, dt_raw, dt_bias, a_log, h0):
    n_rows = xbc.shape[0]
    q = SSD_CHUNK
    n_c = n_rows // q
    dt = dt_raw.reshape(n_rows, 2, SSD_HEADS).transpose(1, 0, 2)
    dtt = dt.transpose(0, 2, 1)

    def chunk(d, c):
        return jnp.where(d == 0, c, n_c - 1 - c)

    small = lambda shape: pl.BlockSpec((None,) + shape, lambda d, c: (d, 0, 0))
    return pl.pallas_call(
        _ssd_kernel,
        grid=(2, n_c),
        in_specs=[pl.BlockSpec((q, SSD_XBC), lambda d, c: (chunk(d, c), 0)),
                  pl.BlockSpec((None, q, SSD_HEADS), lambda d, c: (d, chunk(d, c), 0)),
                  pl.BlockSpec((None, SSD_HEADS, q), lambda d, c: (d, 0, chunk(d, c))),
                  small((1, SSD_HEADS)), small((SSD_HEADS, 1)), small((1, SSD_HEADS)), small((SSD_HEADS, 1)),
                  pl.BlockSpec((None, SSD_PAIRS, SSD_STATE, LANES), lambda d, c: (d, 0, 0, 0))],
        out_specs=[pl.BlockSpec((None, q, SSD_D_INNER), lambda d, c: (d, chunk(d, c), 0)),
                   pl.BlockSpec((None, SSD_PAIRS, SSD_STATE, LANES), lambda d, c: (d, 0, 0, 0))],
        out_shape=[jax.ShapeDtypeStruct((2, n_rows, SSD_D_INNER), F32),
                   jax.ShapeDtypeStruct((2, SSD_PAIRS, SSD_STATE, LANES), F32)],
        scratch_shapes=[pltpu.VMEM((SSD_PAIRS, SSD_STATE, LANES), F32)],
        compiler_params=_params("arbitrary", "arbitrary"),
        name="ssd_scan",
    )(xbc, dt, dtt, dt_bias[:, None, :], dt_bias[:, :, None], a_log[:, None, :], a_log[:, :, None], h0)


def _ssd_out_kernel(y_ref, xbc_ref, z_ref, dsk_ref, nw_ref, o_ref):
    y = y_ref[0] + y_ref[1] + dsk_ref[...] * xbc_ref[...]
    g = y * _silu(z_ref[...])
    ms = jnp.mean(g * g, axis=-1, keepdims=True)
    o_ref[...] = (g * lax.rsqrt(ms + EPS) * nw_ref[...]).astype(o_ref.dtype)


def _ssd_out(y2, xbc, zx, d_skip, norm_w):
    n_rows = xbc.shape[0]
    tl = _pick(n_rows, (512, 256, 128))
    w = SSD_D_INNER
    return pl.pallas_call(
        _ssd_out_kernel,
        grid=(n_rows // tl,),
        in_specs=[pl.BlockSpec((2, tl, w), lambda i: (0, i, 0)),
                  pl.BlockSpec((tl, w), lambda i: (i, 0)),
                  pl.BlockSpec((tl, w), lambda i: (i, 0)),
                  pl.BlockSpec((1, w), lambda i: (0, 0)),
                  pl.BlockSpec((1, w), lambda i: (0, 0))],
        out_specs=pl.BlockSpec((tl, w), lambda i: (i, 0)),
        out_shape=jax.ShapeDtypeStruct((n_rows, w), MXU_DTYPE),
        compiler_params=_params("parallel"),
        name="ssd_out",
    )(y2, xbc, zx, jnp.repeat(d_skip, SSD_HEAD_DIM)[None, :], norm_w[None, :])


def _head_mean_sq(x):
    blk = (lax.broadcasted_iota(jnp.int32, (LANES, LANES), 0) // NA_HEAD_DIM
           == lax.broadcasted_iota(jnp.int32, (LANES, LANES), 1) // NA_HEAD_DIM).astype(MXU_DTYPE)
    x2 = x * x
    hi = x2.astype(MXU_DTYPE)
    lo = (x2 - hi.astype(F32)).astype(MXU_DTYPE)
    return (_dot(hi, blk) + _dot(lo, blk)) * (1.0 / NA_HEAD_DIM)


def _even_proj_kernel(x_ref, nw_ref, sh_ref, sc_ref, w_ref, qw_ref, kw_ref, zx_ref, qkv_ref, dt_ref, a_ref,
                      *, zx_tiles, chunks):
    j = pl.program_id(1)

    @pl.when(j == 0)
    def _():
        _norm_mod_rows(x_ref, a_ref, nw_ref[...], sh_ref[...], sc_ref[...])

    acc = _dot_nt(a_ref[...], w_ref[...].astype(a_ref.dtype))

    @pl.when(j < zx_tiles)
    def _():
        zx_ref[...] = acc

    @pl.when(j >= zx_tiles)
    def _():
        n_head_chunks = NA_D // LANES
        for c in range(chunks):
            chunk = (j - zx_tiles) * chunks + c
            x = acc[:, c * LANES:(c + 1) * LANES]
            w = jnp.where(chunk < n_head_chunks, qw_ref[...] * NA_HEAD_DIM ** -0.5, kw_ref[...])
            normed = x * lax.rsqrt(_head_mean_sq(x) + EPS) * w
            qkv_ref[:, c * LANES:(c + 1) * LANES] = jnp.where(chunk < 2 * n_head_chunks, normed, x).astype(qkv_ref.dtype)

    @pl.when(j == pl.num_programs(1) - 1)
    def _():
        dt_ref[...] = acc[:, (chunks - 1) * LANES:]


def _even_proj(x, nw, sh, sc, w_packed, layer, q_norm_w, k_norm_w, name):
    m, d = x.shape
    tm = _pick(m, (1024, 512, 256, 128))
    tn = 5 * LANES
    zx_tiles = S_ZX // tn
    wspec = pl.BlockSpec((1, LANES), lambda i, j: (0, 0))
    return pl.pallas_call(
        functools.partial(_even_proj_kernel, zx_tiles=zx_tiles, chunks=tn // LANES),
        grid=(m // tm, (S_ZX + QKVD) // tn),
        in_specs=[pl.BlockSpec((tm, d), lambda i, j: (i, 0)),
                  _vec_spec(nw, d, _zero), _vec_spec(sh, d, _zero), _vec_spec(sc, d, _zero),
                  _mat_spec(layer, (tn, d), lambda i, j: (j, 0)), wspec, wspec],
        out_specs=[pl.BlockSpec((tm, tn), lambda i, j: (i, jnp.minimum(j, zx_tiles - 1))),
                   pl.BlockSpec((tm, tn), lambda i, j: (i, jnp.maximum(j - zx_tiles, 0))),
                   pl.BlockSpec((tm, LANES), lambda i, j: (i, 0))],
        out_shape=[jax.ShapeDtypeStruct((m, S_ZX), F32), jax.ShapeDtypeStruct((m, QKVD), MXU_DTYPE),
                   jax.ShapeDtypeStruct((m, LANES), F32)],
        scratch_shapes=[pltpu.VMEM((tm, d), MXU_DTYPE)],
        compiler_params=_params("parallel", "arbitrary"),
        name=name,
    )(x, nw[0], sh[0], sc[0], w_packed, jnp.tile(q_norm_w, 2)[None, :], jnp.tile(k_norm_w, 2)[None, :])


def _attend(q2, parts):
    first = lax.broadcasted_iota(jnp.int32, q2.shape, 1) < NA_HEAD_DIM
    outs = []
    for head_mask in (first, jnp.logical_not(first)):
        qa = jnp.where(head_mask, q2, jnp.zeros_like(q2))
        scores = []
        for k, _, bias in parts:
            s = _dot_nt(qa, k)
            scores.append(s if bias is None else s + bias)
        m = functools.reduce(jnp.maximum, [jnp.max(s, axis=-1, keepdims=True) for s in scores])
        probs = [jnp.exp(s - m) for s in scores]
        denom = functools.reduce(jnp.add, [jnp.sum(p, axis=-1, keepdims=True) for p in probs])
        acc = functools.reduce(jnp.add, [_dot(p.astype(MXU_DTYPE), v) for p, (_, v, _) in zip(probs, parts)])
        outs.append(acc / denom)
    return jnp.where(first, outs[0], outs[1])


def _natten_kernel(q_ref, k_ref, v_ref, kc_ref, vc_ref, tbl_ref, o_ref, *, rb, rows):
    i = pl.program_id(1)
    kc, vc = kc_ref[...], vc_ref[...]
    n_win = NA_WIN_ROWS * GRID_W
    first = lax.broadcasted_iota(jnp.int32, (GRID_W, LANES), 1) < NA_HEAD_DIM
    for t in range(rb):
        r = i * rb + t
        start = jnp.clip(r - NA_WIN_ROWS // 2, 0, rows - NA_WIN_ROWS)
        dr0 = start - r + (NA_WIN_ROWS - 1)
        ks = pl.ds(pl.multiple_of(start * GRID_W, GRID_W), n_win)
        q2 = q_ref[t * GRID_W:(t + 1) * GRID_W, :]
        kw, vw = k_ref[ks, :], v_ref[ks, :]
        zero = jnp.zeros_like(q2)
        qs = jnp.concatenate([jnp.where(first, q2, zero), jnp.where(first, zero, q2)], axis=0)
        bias = jnp.concatenate([tbl_ref[dr0 + 2 * jj] for jj in range(NA_WIN_ROWS // 2)], axis=1)
        s_loc = _dot_nt(qs, kw) + bias
        s_ctx = _dot_nt(qs, kc)
        m = jnp.maximum(jnp.max(s_loc, axis=-1, keepdims=True), jnp.max(s_ctx, axis=-1, keepdims=True))
        p_loc, p_ctx = jnp.exp(s_loc - m), jnp.exp(s_ctx - m)
        denom = jnp.sum(p_loc, axis=-1, keepdims=True) + jnp.sum(p_ctx, axis=-1, keepdims=True)
        o = (_dot(p_loc.astype(MXU_DTYPE), vw) + _dot(p_ctx.astype(MXU_DTYPE), vc)) / denom
        o_ref[t * GRID_W:(t + 1) * GRID_W, :] = jnp.where(first, o[:GRID_W], o[GRID_W:]).astype(o_ref.dtype)


def _bias_table(rpb):
    col = jnp.arange(GRID_W)
    c0 = jnp.clip(col - NA_WIN_COLS // 2, 0, GRID_W - NA_WIN_COLS)
    col_in = (col[None, :] >= c0[:, None]) & (col[None, :] < c0[:, None] + NA_WIN_COLS)
    dc = jnp.clip(col[None, :] - col[:, None], 1 - NA_WIN_COLS, NA_WIN_COLS - 1) + (NA_WIN_COLS - 1)
    t = jnp.where(col_in, rpb.astype(F32)[:, :, dc], NEG)
    t = jnp.concatenate([t[:, :-1], t[:, 1:]], axis=-1)
    n_dr = 2 * NA_WIN_ROWS - 2
    t = t.reshape(NA_HEADS // 2, 2, n_dr, GRID_W, 2 * GRID_W).transpose(0, 2, 1, 3, 4)
    return t.reshape(NA_HEADS // 2, n_dr, 2 * GRID_W, 2 * GRID_W)


def _natten(qkv, qkv_ctx, rpb):
    n_rows = qkv.shape[0]
    rows = n_rows // GRID_W
    n_ctx = qkv_ctx.shape[0]
    rb = _pick(rows, (8, 4, 2, 1))
    nb = NA_D // LANES
    seq = lambda n, off: pl.BlockSpec((n, LANES), lambda p, i: (0, p + off))
    tile = pl.BlockSpec((rb * GRID_W, LANES), lambda p, i: (i, p))
    return pl.pallas_call(
        functools.partial(_natten_kernel, rb=rb, rows=rows),
        grid=(NA_HEADS // 2, rows // rb),
        in_specs=[tile, seq(n_rows, nb), seq(n_rows, 2 * nb), seq(n_ctx, nb), seq(n_ctx, 2 * nb),
                  pl.BlockSpec((None, 2 * NA_WIN_ROWS - 2, 2 * GRID_W, 2 * GRID_W), lambda p, i: (p, 0, 0, 0))],
        out_specs=tile,
        out_shape=jax.ShapeDtypeStruct((n_rows, NA_D), MXU_DTYPE),
        compiler_params=_params("parallel", "arbitrary"),
        name="natten",
    )(qkv, qkv, qkv, qkv_ctx, qkv_ctx, _bias_table(rpb))


def _ctx_attn_kernel(q_ref, k_ref, v_ref, o_ref):
    o_ref[...] = _attend(q_ref[...], [(k_ref[...], v_ref[...], None)]).astype(o_ref.dtype)


def _ctx_attn(qkv_ctx):
    n_ctx = qkv_ctx.shape[0]
    nb = NA_D // LANES
    spec = lambda off: pl.BlockSpec((n_ctx, LANES), lambda p: (0, p + off))
    return pl.pallas_call(
        _ctx_attn_kernel,
        grid=(NA_HEADS // 2,),
        in_specs=[spec(0), spec(nb), spec(2 * nb)],
        out_specs=spec(0),
        out_shape=jax.ShapeDtypeStruct((n_ctx, NA_D), MXU_DTYPE),
        compiler_params=_params("parallel"),
        name="ctx_attn",
    )(qkv_ctx, qkv_ctx, qkv_ctx)


def _load_slabs(ref, n, s):
    return jnp.concatenate([ref[pl.ds(c, n, stride=s), :] for c in range(s)], axis=1)


def _store_slabs(ref, val):
    n = val.shape[0]
    s = val.shape[1] // LANES
    for c in range(s):
        ref[pl.ds(c, n, stride=s), :] = val[:, c * LANES:(c + 1) * LANES].astype(ref.dtype)

def _router_kernel(*refs, n_lat_tiles, with_ctx):
    if with_ctx:
        x_ref, cx_ref, nw_ref, sh_ref, sc_ref, shc_ref, scc_ref, wr_ref, m_ref, e_ref, g_ref = refs
        is_ctx = pl.program_id(0) >= n_lat_tiles
        x = jnp.where(is_ctx, cx_ref[...], x_ref[...])
        sh = jnp.where(is_ctx, shc_ref[...], sh_ref[...])
        sc = jnp.where(is_ctx, scc_ref[...], sc_ref[...])
    else:
        x_ref, nw_ref, sh_ref, sc_ref, wr_ref, m_ref, e_ref, g_ref = refs
        x, sh, sc = x_ref[...], sh_ref[...], sc_ref[...]
    m = _norm_mod(x, nw_ref[...], sh, sc)
    _store_slabs(m_ref, m)
    logits = _dot_split2(m, wr_ref[...])
    lane = lax.broadcasted_iota(jnp.int32, logits.shape, 1)
    big = jnp.int32(LANES)

    def top(vals):
        v = jnp.max(vals, axis=-1, keepdims=True)
        idx = jnp.min(jnp.where(vals == v, lane, big), axis=-1, keepdims=True)
        return v, idx

    gl = jnp.where(lane < N_GROUPS, logits, -jnp.inf)
    g_max, grp = top(gl)
    p_grp = 1.0 / jnp.sum(jnp.exp(gl - g_max), axis=-1, keepdims=True)
    e_lane = lane - N_GROUPS
    in_grp = (e_lane >= grp * EXPERTS_PER_GROUP) & (e_lane < (grp + 1) * EXPERTS_PER_GROUP)
    el = jnp.where(in_grp, logits, -jnp.inf)
    v1, i1 = top(el)
    v2, i2 = top(jnp.where(lane == i1, -jnp.inf, el))
    t = jnp.exp(v2 - v1)
    g1 = p_grp / (1.0 + t)
    g2 = p_grp * t / (1.0 + t)
    e_ref[...] = jnp.where(lane == 0, i1 - N_GROUPS, jnp.where(lane == 1, i2 - N_GROUPS, 0))
    g_ref[...] = jnp.where(lane == 0, g1, jnp.where(lane == 1, g2, 0.0))


def _router(lat, cx, nw, mod_lat, mod_ctx, w_route):
    n_lat, d = lat.shape
    with_ctx = cx is not None
    n_ctx = cx.shape[0] if with_ctx else 0
    tm = _pick(n_lat, (256, 128)) if with_ctx else _pick(n_lat, (512, 256, 128))
    assert n_ctx % tm == 0
    n_lat_tiles = n_lat // tm
    n_tok = n_lat + n_ctx
    lat_tile = pl.BlockSpec((tm, d), lambda i: (jnp.minimum(i, n_lat_tiles - 1), 0))
    vec = lambda v: _vec_spec(v, d, _zero)
    wide = pl.BlockSpec((tm, LANES), lambda i: (i, 0))
    w_spec = pl.BlockSpec((d, LANES), lambda i: (0, 0))
    if with_ctx:
        ctx_tile = pl.BlockSpec((tm, d), lambda i: (jnp.maximum(i - n_lat_tiles, 0), 0))
        in_specs = [lat_tile, ctx_tile, vec(nw), vec(mod_lat[3]), vec(mod_lat[4]), vec(mod_ctx[3]), vec(mod_ctx[4]),
                    w_spec]
        args = (lat, cx, nw[0], mod_lat[3][0], mod_lat[4][0], mod_ctx[3][0], mod_ctx[4][0], w_route)
    else:
        in_specs = [lat_tile, vec(nw), vec(mod_lat[3]), vec(mod_lat[4]), w_spec]
        args = (lat, nw[0], mod_lat[3][0], mod_lat[4][0], w_route)
    m, e, g = pl.pallas_call(
        functools.partial(_router_kernel, n_lat_tiles=n_lat_tiles, with_ctx=with_ctx),
        grid=(n_tok // tm,),
        in_specs=in_specs,
        out_specs=[pl.BlockSpec((tm * (d // LANES), LANES), lambda i: (i, 0)), wide, wide],
        out_shape=[jax.ShapeDtypeStruct((n_tok * (d // LANES), LANES), F32),
                   jax.ShapeDtypeStruct((n_tok, LANES), jnp.int32),
                   jax.ShapeDtypeStruct((n_tok, LANES), F32)],
        compiler_params=_params("parallel"),
        name="router",
    )(*args)
    return m, e[:, :TOP_K], g


def _moe_kernel(first_ref, nblk_ref, tok_ref, nact_ref, m_hbm, wg_hbm, wu_hbm, wd_hbm, yb_hbm,
                xbuf, ybuf, gsem, osem, wgf_ref, wuf_ref, wdf_ref, wsem, wgb_ref, wub_ref, wdb_ref, *, layer):
    e, n_e = pl.program_id(0), pl.num_programs(0)
    n_act = nact_ref[0]
    s = wgf_ref.shape[1] // LANES
    blk_rows = MOE_BLOCK * s
    n_blocks = yb_hbm.shape[0] // blk_rows

    def weight_copies(expert, ws):
        return [pltpu.make_async_copy(src.at[layer, expert], dst.at[ws], wsem.at[k, ws])
                for k, (src, dst) in enumerate(((wg_hbm, wgf_ref), (wu_hbm, wuf_ref), (wd_hbm, wdf_ref)))]

    @pl.when(e == 0)
    def _():
        for ahead in range(MOE_WEIGHT_AHEAD):
            for copy in weight_copies(ahead, ahead):
                copy.start()

    ws = e % MOE_WEIGHT_SLOTS
    for copy in weight_copies(e, ws):
        copy.wait()

    @pl.when(e + MOE_WEIGHT_AHEAD < n_e)
    def _():
        for copy in weight_copies(e + MOE_WEIGHT_AHEAD, (e + MOE_WEIGHT_AHEAD) % MOE_WEIGHT_SLOTS):
            copy.start()

    def gather_start(blk, slot):
        for r in range(MOE_BLOCK):
            tok = tok_ref[blk * MOE_BLOCK + r]
            pltpu.make_async_copy(m_hbm.at[pl.ds(pl.multiple_of(tok * s, s), s), :],
                                  xbuf.at[slot, pl.ds(r * s, s), :],
                                  gsem.at[slot]).start(priority=r % ROW_DMA_THREADS)

    def gather_wait(slot):
        pltpu.make_async_copy(m_hbm.at[pl.ds(0, blk_rows), :], xbuf.at[slot], gsem.at[slot]).wait()

    def out_copy(blk, slot):
        rows = pl.ds(pl.multiple_of(blk * blk_rows, blk_rows), blk_rows)
        return pltpu.make_async_copy(ybuf.at[slot], yb_hbm.at[rows, :], osem.at[slot])

    @pl.when(e == 0)
    def _():
        for ahead in range(MOE_AHEAD):
            gather_start(jnp.clip(n_act - 1, 0, ahead), ahead)

    @pl.when(nblk_ref[e] > 0)
    def _():
        wgb_ref[...] = wgf_ref[ws].astype(wgb_ref.dtype)
        wub_ref[...] = wuf_ref[ws].astype(wub_ref.dtype)
        wdb_ref[...] = wdf_ref[ws].astype(wdb_ref.dtype)

    def block(j, carry):
        b = first_ref[e] + j
        slot = b % MOE_SLOTS
        gather_wait(slot)

        @pl.when(b >= MOE_SLOTS)
        def _():
            out_copy(b - MOE_SLOTS, slot).wait()

        gather_start(jnp.clip(n_act - 1, 0, b + MOE_AHEAD), (b + MOE_AHEAD) % MOE_SLOTS)
        x = _load_slabs(xbuf.at[slot], MOE_BLOCK, s).astype(MXU_DTYPE)
        h = _silu(_dot(x, wgb_ref[...])) * _dot(x, wub_ref[...])
        _store_slabs(ybuf.at[slot], _dot(h.astype(MXU_DTYPE), wdb_ref[...]))
        out_copy(b, slot).start()
        return carry

    lax.fori_loop(0, nblk_ref[e], block, 0)

    @pl.when(e == n_e - 1)
    def _():
        for ahead in range(MOE_AHEAD):
            gather_wait((n_act + ahead) % MOE_SLOTS)
        for back in range(1, MOE_SLOTS + 1):
            @pl.when(n_act >= back)
            def _():
                out_copy(n_act - back, (n_act - back) % MOE_SLOTS).wait()

        ybuf[0] = jnp.zeros(ybuf.shape[1:], ybuf.dtype)

        def fill(b, carry):
            out_copy(b, 0).start()
            out_copy(b, 0).wait()
            return carry

        lax.fori_loop(n_act, n_blocks, fill, 0)


def _moe_ffn(m_tok, slot_tok, first_blk, n_blk, n_act, w_gate, w_up, w_down, layer):
    n_rows = slot_tok.shape[0]
    d, f = w_gate.shape[-2:]
    blk_rows = MOE_BLOCK * (d // LANES)
    hbm = pl.BlockSpec(memory_space=pl.ANY)
    grid_spec = pltpu.PrefetchScalarGridSpec(
        num_scalar_prefetch=4,
        grid=(N_EXPERTS,),
        in_specs=[hbm, hbm, hbm, hbm],
        out_specs=hbm,
        scratch_shapes=[pltpu.VMEM((MOE_SLOTS, blk_rows, LANES), F32), pltpu.VMEM((MOE_SLOTS, blk_rows, LANES), F32),
                        pltpu.SemaphoreType.DMA((MOE_SLOTS,)), pltpu.SemaphoreType.DMA((MOE_SLOTS,)),
                        pltpu.VMEM((MOE_WEIGHT_SLOTS, d, f), F32), pltpu.VMEM((MOE_WEIGHT_SLOTS, d, f), F32),
                        pltpu.VMEM((MOE_WEIGHT_SLOTS, f, d), F32), pltpu.SemaphoreType.DMA((3, MOE_WEIGHT_SLOTS)),
                        pltpu.VMEM((d, f), MXU_DTYPE), pltpu.VMEM((d, f), MXU_DTYPE), pltpu.VMEM((f, d), MXU_DTYPE)],
    )
    return pl.pallas_call(
        functools.partial(_moe_kernel, layer=layer),
        grid_spec=grid_spec,
        out_shape=jax.ShapeDtypeStruct((n_rows * (d // LANES), LANES), F32),
        compiler_params=_params("arbitrary"),
        name="moe_ffn",
    )(first_blk, n_blk, slot_tok, n_act, m_tok, w_gate, w_up, w_down)


def _combine_kernel(pos_ref, x_ref, g_ref, gt_ref, yb_hbm, o_ref, buf, sem, *, tm):
    i, n_i = pl.program_id(0), pl.num_programs(0)
    s = x_ref.shape[1] // LANES

    def gather_start(tile, slot):
        for j in range(tm):
            for k in range(TOP_K):
                p = pos_ref[(tile * tm + j) * TOP_K + k]
                pltpu.make_async_copy(yb_hbm.at[pl.ds(pl.multiple_of(p * s, s), s), :],
                                      buf.at[slot, k, pl.ds(j * s, s), :],
                                      sem.at[slot]).start(priority=(j * TOP_K + k) % ROW_DMA_THREADS)

    def gather_wait(slot):
        for k in range(TOP_K):
            pltpu.make_async_copy(yb_hbm.at[pl.ds(0, tm * s), :], buf.at[slot, k], sem.at[slot]).wait()

    @pl.when(i == 0)
    def _():
        for ahead in range(MOE_AHEAD):
            gather_start(jnp.minimum(ahead, n_i - 1), ahead)

    slot = i % MOE_SLOTS
    gather_wait(slot)
    gather_start(jnp.minimum(i + MOE_AHEAD, n_i - 1), (i + MOE_AHEAD) % MOE_SLOTS)
    gt = gt_ref[...]
    y0 = _load_slabs(buf.at[slot, 0], tm, s)
    y1 = _load_slabs(buf.at[slot, 1], tm, s)
    o_ref[...] = x_ref[...] + g_ref[...] * (gt[:, 0:1] * y0 + gt[:, 1:2] * y1)

    @pl.when(i == n_i - 1)
    def _():
        for ahead in range(1, MOE_AHEAD + 1):
            gather_wait((i + ahead) % MOE_SLOTS)


def _combine(x, g, gates, yb, pos):
    m, d = x.shape
    tm = MOE_BLOCK
    grid_spec = pltpu.PrefetchScalarGridSpec(
        num_scalar_prefetch=1,
        grid=(m // tm,),
        in_specs=[pl.BlockSpec((tm, d), lambda i, pos: (i, 0)),
                  _vec_spec(g, d, _zero),
                  pl.BlockSpec((tm, LANES), lambda i, pos: (i, 0)),
                  pl.BlockSpec(memory_space=pl.ANY)],
        out_specs=pl.BlockSpec((tm, d), lambda i, pos: (i, 0)),
        scratch_shapes=[pltpu.VMEM((MOE_SLOTS, TOP_K, tm * (d // LANES), LANES), F32),
                        pltpu.SemaphoreType.DMA((MOE_SLOTS,))],
    )
    return pl.pallas_call(
        functools.partial(_combine_kernel, tm=tm),
        grid_spec=grid_spec,
        out_shape=jax.ShapeDtypeStruct((m, d), F32),
        compiler_params=_params("arbitrary"),
        name="moe_combine",
    )(pos, x, g[0], gates, yb)


def _dispatch_kernel(e_ref, dest_ref, first_ref, nblk_ref, cnt_ref, base_ref):
    phase, i = pl.program_id(0), pl.program_id(1)
    tile = e_ref.shape[1]
    onehot = lax.broadcasted_iota(jnp.int32, (N_EXPERTS, tile), 0) == e_ref[...]

    @pl.when(jnp.logical_and(phase == 0, i == 0))
    def _():
        cnt_ref[...] = jnp.zeros_like(cnt_ref)

    @pl.when(phase == 0)
    def _():
        cnt_ref[...] += jnp.sum(onehot.astype(F32), axis=1, keepdims=True)

    @pl.when(jnp.logical_and(phase == 1, i == 0))
    def _():
        n_blk = jnp.right_shift(cnt_ref[...].astype(jnp.int32) + (MOE_BLOCK - 1),
                                MOE_BLOCK.bit_length() - 1)
        tri = (lax.broadcasted_iota(jnp.int32, (N_EXPERTS, N_EXPERTS), 0)
               >= lax.broadcasted_iota(jnp.int32, (N_EXPERTS, N_EXPERTS), 1)).astype(MXU_DTYPE)
        blk_end = _dot_select(tri, n_blk.astype(F32), split_lhs=False)
        first = blk_end.astype(jnp.int32) - n_blk
        nblk_ref[...] = n_blk
        first_ref[...] = first
        base_ref[...] = (first * MOE_BLOCK).astype(F32)

    @pl.when(phase == 1)
    def _():
        tri = (lax.broadcasted_iota(jnp.int32, (tile, tile), 0)
               <= lax.broadcasted_iota(jnp.int32, (tile, tile), 1)).astype(MXU_DTYPE)
        cum = _dot(onehot.astype(MXU_DTYPE), tri)
        slot = jnp.where(onehot, base_ref[:, 0:1] + cum - 1.0, 0.0)
        dest_ref[...] = jnp.sum(slot, axis=0, keepdims=True).astype(jnp.int32)
        base_ref[...] += cum[:, tile - 1:tile]


def _dispatch(expert):
    n_tok = expert.shape[0]
    n = n_tok * TOP_K
    n_blocks = -(-n // MOE_BLOCK) + N_EXPERTS
    n_rows = n_blocks * MOE_BLOCK
    tile = _pick(n, (512, 256, 128))
    per_expert = pl.BlockSpec((N_EXPERTS, LANES), lambda ph, i: (0, 0))
    dest, first, n_blk = pl.pallas_call(
        _dispatch_kernel,
        grid=(2, n // tile),
        in_specs=[pl.BlockSpec((1, tile), lambda ph, i: (0, i))],
        out_specs=[pl.BlockSpec((1, tile), lambda ph, i: (0, i * ph)), per_expert, per_expert],
        out_shape=[jax.ShapeDtypeStruct((1, n), jnp.int32),
                   jax.ShapeDtypeStruct((N_EXPERTS, LANES), jnp.int32),
                   jax.ShapeDtypeStruct((N_EXPERTS, LANES), jnp.int32)],
        scratch_shapes=[pltpu.VMEM((N_EXPERTS, LANES), F32), pltpu.VMEM((N_EXPERTS, LANES), F32)],
        compiler_params=_params("arbitrary", "arbitrary"),
        name="moe_dispatch",
    )(expert.reshape(1, n))
    dest, first_blk, n_blk = dest[0], first[:, 0], n_blk[:, 0]
    pair_tok = jnp.arange(n, dtype=jnp.int32) // TOP_K
    slot_tok = jnp.zeros((n_rows,), jnp.int32).at[dest].set(pair_tok)
    n_act = first_blk[-1:] + n_blk[-1:]
    return dest, slot_tok, first_blk, n_blk, n_act


def _even_mixer(lat, cx, nw, mod_lat, mod_ctx, e, w_in, conv_w, conv_b, dt_bias, a_log, d_skip, ssd_norm_w,
                q_norm_w, k_norm_w, rpb, w_out, ctx_out):
    def project(x, mod, tag):
        zx, qkv, dt = _even_proj(x, nw, mod[0], mod[1], w_in, e, q_norm_w[e], k_norm_w[e], "proj_even_" + tag)
        xbc = _xbc_conv(zx, conv_w[e], conv_b[e])
        return zx, xbc, dt[:, :2 * SSD_HEADS], qkv

    zx_c, xbc_c, dt_c, qkv_c = project(cx, mod_ctx, "ctx")
    zx_l, xbc_l, dt_l, qkv_l = project(lat, mod_lat, "lat")
    h0 = jnp.zeros((2, SSD_PAIRS, SSD_STATE, LANES), F32)
    y_c, h_ctx = _ssd(xbc_c, dt_c, dt_bias[e], a_log[e], h0)
    y_l, _ = _ssd(xbc_l, dt_l, dt_bias[e], a_log[e], h_ctx)
    yssd_l = _ssd_out(y_l, xbc_l, zx_l, d_skip[e], ssd_norm_w[e])
    yatt_l = _natten(qkv_l, qkv_c, rpb[e])
    lat = _matmul_res([yssd_l, yatt_l], w_out, e, lat, mod_lat[2], "out_even_lat")
    if ctx_out:
        yssd_c = _ssd_out(y_c, xbc_c, zx_c, d_skip[e], ssd_norm_w[e])
        yatt_c = _ctx_attn(qkv_c)
        cx = _matmul_res([yssd_c, yatt_c], w_out, e, cx, mod_ctx[2], "out_even_ctx")
    return lat, cx


def _odd_mixer(lat, cx, nw, mod_lat, mod_ctx, o, w_in, conv_w, w_out, ctx_out):
    def mix(x, mod, tag):
        u = _gated_conv_proj(x, nw, mod[0], mod[1], w_in, o, conv_w[o], "proj_odd_" + tag)
        return _matmul_res([u], w_out, o, x, mod[2], "out_odd_" + tag)

    lat = mix(lat, mod_lat, "lat")
    if ctx_out:
        cx = mix(cx, mod_ctx, "ctx")
    return lat, cx


def kernel(x, c, ctx, c_ctx, w_mod, b_mod, norm_mix_w, norm_ffn_w, ev_w_in, ev_conv_w, ev_conv_b, ev_dt_bias,
           ev_a_log, ev_d_skip, ev_ssd_norm_w, ev_q_norm_w, ev_k_norm_w, ev_rpb, ev_w_out, od_w_in, od_conv_w,
           od_w_out, moe_w_group, moe_w_router, moe_w_gate, moe_w_up, moe_w_down):
    bsz, n_lat, d = x.shape
    assert bsz == 1 and ctx.shape[0] == 1
    assert n_lat % GRID_W == 0 and n_lat // GRID_W >= NA_WIN_ROWS and n_lat % SSD_CHUNK == 0
    assert ctx.shape[1] % SSD_CHUNK == 0 and d % LANES == 0
    depth = w_mod.shape[0]
    lat, cx = x[0], ctx[0]
    mods = _modulation(jnp.stack([c[0], c_ctx], axis=1), w_mod, b_mod)
    mods = mods.reshape(depth * 2 * 6, d)
    ev_w_packed = _repack_w_in(ev_w_in)

    for l in range(depth):
        even = l % 2 == 0
        ctx_out = any(j % 2 == 0 for j in range(l + 1, depth))
        mod_lat = [_vec(mods, (l * 2 + 0) * 6 + i) for i in range(6)]
        mod_ctx = [_vec(mods, (l * 2 + 1) * 6 + i) for i in range(6)]
        nw = _vec(norm_mix_w, l)
        if even:
            lat, cx = _even_mixer(lat, cx, nw, mod_lat, mod_ctx, l // 2, ev_w_packed, ev_conv_w, ev_conv_b,
                                  ev_dt_bias, ev_a_log, ev_d_skip, ev_ssd_norm_w, ev_q_norm_w,
                                  ev_k_norm_w, ev_rpb, ev_w_out, ctx_out)
        else:
            lat, cx = _odd_mixer(lat, cx, nw, mod_lat, mod_ctx, l // 2, od_w_in, od_conv_w, od_w_out, ctx_out)

        nfw = _vec(norm_ffn_w, l)
        w_route = jnp.concatenate([moe_w_group[l], moe_w_router[l],
                                   jnp.zeros((d, LANES - N_GROUPS - N_EXPERTS), F32)], axis=1)
        m_tok, expert, gates = _router(lat, cx if ctx_out else None, nfw, mod_lat, mod_ctx, w_route)
        dest, slot_tok, first_blk, n_blk, n_act = _dispatch(expert)
        yb = _moe_ffn(m_tok, slot_tok, first_blk, n_blk, n_act, moe_w_gate, moe_w_up, moe_w_down, l)
        lat = _combine(lat, mod_lat[5], gates[:n_lat], yb, dest[:n_lat * TOP_K])
        if ctx_out:
            cx = _combine(cx, mod_ctx[5], gates[n_lat:], yb, dest[n_lat * TOP_K:])
    return lat[None]
```
